```python
import jax
import jax.numpy as jnp
from jax import lax
import numpy as np

D_MODEL = 1024
BATCH = 1
SEQ = 16384
DEPTH = 2

HGRN_HEADS = 4
HGRN_DIM = 128
HGRN_WIDTH = HGRN_HEADS * HGRN_DIM
RET_HEADS = 4
RET_DIM = 128
RET_WIDTH = RET_HEADS * RET_DIM
SWA_Q_HEADS = 8
SWA_KV_HEADS = 2
SWA_DIM = 64
SWA_Q_WIDTH = SWA_Q_HEADS * SWA_DIM
SWA_KV_WIDTH = SWA_KV_HEADS * SWA_DIM
WINDOW = 128
SWA_BLOCK = 128
N_BRANCHES = 3
CHUNK = 64
ROPE_THETA = 10000.0
N_MEM = 256
X_HEADS = 4
X_DIM = D_MODEL // X_HEADS
PEER_HEADS = 8
PEER_N_KEYS = 128
PEER_N_EXPERTS = PEER_N_KEYS * PEER_N_KEYS
PEER_TOPK = 16
PEER_QDIM = 256
PEER_HALF = PEER_QDIM // 2
PEER_TOK_BLOCK = 128
EPS = 1e-6
NEG_BIG = -1e30
TINY = 1e-30

IN_SIZES = (HGRN_WIDTH,) * 4 + (RET_WIDTH,) * 4 + (SWA_Q_WIDTH, SWA_KV_WIDTH, SWA_KV_WIDTH, N_BRANCHES * D_MODEL)
IN_COLS = int(sum(IN_SIZES))
IN_SPLITS = tuple(int(c) for c in np.cumsum(IN_SIZES)[:-1])

kernel_name = 'hybrid_hgrn2_retnet_swa_peer'


def rmsnorm(x, g):
    xf = x.astype(jnp.float32)
    y = xf * lax.rsqrt(jnp.mean(xf * xf, axis=-1, keepdims=True) + EPS)
    if g is not None:
        y = y * g.astype(jnp.float32)
    return y.astype(x.dtype)


def rope(a, pos, inv_freq):
    ang = pos.astype(jnp.float32)[:, :, None, None] * inv_freq
    cos, sin = jnp.cos(ang), jnp.sin(ang)
    a1, a2 = jnp.split(a.astype(jnp.float32), 2, axis=-1)
    return jnp.concatenate([a1 * cos - a2 * sin, a2 * cos + a1 * sin], axis=-1).astype(a.dtype)


def split_heads(a, n):
    return a.reshape(a.shape[0], a.shape[1], n, -1)


def bhtd(a):
    return jnp.swapaxes(a, 1, 2)


def to_chunks(a):
    b, h, t, d = a.shape
    return jnp.moveaxis(a.reshape(b, h, t // CHUNK, CHUNK, d), 2, 0)


def from_chunks(a):
    n, b, h, c, d = a.shape
    return jnp.moveaxis(a, 0, 2).reshape(b, h, n * c, d)


def hgrn2_chunkwise(q, k, v, log_f):
    f32 = jnp.float32
    b, h, t, dk = q.shape
    dv = v.shape[-1]
    causal = jnp.tril(jnp.ones((CHUNK, CHUNK), dtype=bool))[:, :, None]

    def step(state, inp):
        qc, kc, vc, gc = inp
        cum = jnp.cumsum(gc, axis=2)
        o_inter = jnp.einsum('bhtk,bhkv->bhtv', qc * jnp.exp(cum), state)
        rel = jnp.where(causal, cum[:, :, :, None, :] - cum[:, :, None, :, :], NEG_BIG)
        scores = jnp.einsum('bhtk,bhsk,bhtsk->bhts', qc, kc, jnp.exp(rel))
        o = o_inter + jnp.einsum('bhts,bhsv->bhtv', scores, vc)
        last = cum[:, :, -1:, :]
        state = jnp.exp(last[:, :, 0, :, None]) * state + jnp.einsum('bhsk,bhsv->bhkv', kc * jnp.exp(last - cum), vc)
        return state, o

    s0 = jnp.zeros((b, h, dk, dv), f32)
    xs = tuple(to_chunks(a.astype(f32)) for a in (q, k, v, log_f))
    _, o = lax.scan(step, s0, xs)
    return from_chunks(o).astype(q.dtype)


def retention_chunkwise(q, k, v, log_gamma):
    f32 = jnp.float32
    b, h, t, dk = q.shape
    dv = v.shape[-1]
    idx = jnp.arange(CHUNK, dtype=f32)
    rel = idx[:, None] - idx[None, :]
    decay = jnp.exp(jnp.where(rel >= 0, log_gamma[:, None, None] * rel, NEG_BIG))
    q_decay = jnp.exp(log_gamma[:, None] * (idx + 1.0))[:, :, None]
    k_decay = jnp.exp(log_gamma[:, None] * (CHUNK - 1.0 - idx))[:, :, None]
    c_decay = jnp.exp(log_gamma * CHUNK)[:, None, None]

    def step(state, inp):
        qc, kc, vc = inp
        o_inter = jnp.einsum('bhtk,bhkv->bhtv', qc, state) * q_decay
        scores = jnp.einsum('bhtk,bhsk->bhts', qc, kc) * decay
        o = o_inter + jnp.einsum('bhts,bhsv->bhtv', scores, vc)
        state = c_decay * state + jnp.einsum('bhsk,bhsv->bhkv', kc * k_decay, vc)
        return state, o

    s0 = jnp.zeros((b, h, dk, dv), f32)
    xs = tuple(to_chunks(a.astype(f32)) for a in (q, k, v))
    _, o = lax.scan(step, s0, xs)
    return from_chunks(o).astype(q.dtype)


def swa_sink_attention(q, k, v, sinks):
    f32 = jnp.float32
    b, hq, t, dh = q.shape
    hkv = k.shape[1]
    g = hq // hkv
    nb = t // SWA_BLOCK
    qb = q.reshape(b, hkv, g, nb, SWA_BLOCK, dh)

    def band(a):
        ab = a.reshape(b, hkv, nb, SWA_BLOCK, dh)
        prev = jnp.pad(ab, ((0, 0), (0, 0), (1, 0), (0, 0), (0, 0)))[:, :, :-1]
        return jnp.concatenate([prev, ab], axis=3)

    kw, vw = band(k), band(v)
    s = jnp.einsum('bhgnqd,bhnkd->bhgnqk', qb, kw).astype(f32) * (dh ** -0.5)
    qi = jnp.arange(SWA_BLOCK)[:, None] + SWA_BLOCK
    ki = jnp.arange(2 * SWA_BLOCK)[None, :]
    rel = qi - ki
    in_window = (rel >= 0) & (rel < WINDOW)
    has_prev = (jnp.arange(nb) > 0)[:, None, None] | (ki >= SWA_BLOCK)[None]
    s = jnp.where(in_window[None] & has_prev, s, NEG_BIG)
    sink = sinks.astype(f32).reshape(1, hkv, g, 1, 1, 1)
    m = jnp.maximum(jnp.max(s, axis=-1, keepdims=True), sink)
    p = jnp.exp(s - m)
    denom = jnp.sum(p, axis=-1, keepdims=True) + jnp.exp(sink - m)
    o = jnp.einsum('bhgnqk,bhnkd->bhgnqd', p / denom, vw.astype(f32))
    return o.reshape(b, hq, t, dh).astype(q.dtype)


def memory_cross_attention(h, m, wq, wk, wv, wo):
    b, t, d = h.shape
    nm = m.shape[1]
    q = (h @ wq).reshape(b, t, X_HEADS, X_DIM)
    k = (m @ wk).reshape(b, nm, X_HEADS, X_DIM)
    v = (m @ wv).reshape(b, nm, X_HEADS, X_DIM)
    s = jnp.einsum('bthd,bmhd->bhtm', q, k).astype(jnp.float32) * (X_DIM ** -0.5)
    p = jax.nn.softmax(s, axis=-1)
    o = jnp.einsum('bhtm,bmhd->bthd', p, v.astype(jnp.float32)).reshape(b, t, d).astype(h.dtype)
    return o @ wo


def peer_ffn(h, wq, sub_keys, u_tab, v_tab):
    f32 = jnp.float32
    b, t, d = h.shape
    q = (h @ wq).reshape(b, t, PEER_HEADS, 2, PEER_HALF)
    s = jnp.einsum('bthpd,phnd->bthpn', q, sub_keys).astype(f32)
    sv, si = lax.top_k(s, PEER_TOPK)
    cand = (sv[..., 0, :, None] + sv[..., 1, None, :]).reshape(b, t, PEER_HEADS, PEER_TOPK * PEER_TOPK)
    cv, ci = lax.top_k(cand, PEER_TOPK)
    e1 = jnp.take_along_axis(si[..., 0, :], ci // PEER_TOPK, axis=-1)
    e2 = jnp.take_along_axis(si[..., 1, :], ci % PEER_TOPK, axis=-1)
    n_blk = (b * t) // PEER_TOK_BLOCK
    n_sel = PEER_HEADS * PEER_TOPK
    idx = (e1 * PEER_N_KEYS + e2).reshape(n_blk, PEER_TOK_BLOCK, n_sel)
    gate = jax.nn.softmax(cv, axis=-1).reshape(n_blk, PEER_TOK_BLOCK, n_sel)
    hb = h.reshape(n_blk, PEER_TOK_BLOCK, d)

    def block(args):
        hh, ii, gg = args
        act = jax.nn.gelu(jnp.einsum('td,ted->te', hh, u_tab[ii]).astype(f32), approximate=False)
        return jnp.einsum('te,ted->td', gg * act, v_tab[ii].astype(f32)).astype(hh.dtype)

    return lax.map(block, (hb, idx, gate)).reshape(b, t, d)


def setup_inputs(seed: int = 0) -> dict:
    key = jax.random.key(seed)
    ks = jax.random.split(key, 24)
    f32 = jnp.float32
    nrm = lambda k, shape, scale: jax.random.normal(k, shape, f32) * scale
    gain = lambda k, shape: 1.0 + 0.02 * jax.random.normal(k, shape, f32)
    offset = jax.random.randint(ks[2], (BATCH, 1), 0, 4096, dtype=jnp.int32)
    positions = offset + jnp.arange(SEQ, dtype=jnp.int32)[None, :]
    return {
        'x': nrm(ks[0], (BATCH, SEQ, D_MODEL), 1.0),
        'mem': nrm(ks[1], (BATCH, N_MEM, D_MODEL), 1.0),
        'positions': positions,
        'ln_mix': gain(ks[3], (DEPTH, D_MODEL)),
        'w_in': nrm(ks[4], (DEPTH, D_MODEL, IN_COLS), D_MODEL ** -0.5),
        'lb_param': nrm(ks[5], (DEPTH, HGRN_WIDTH), 1.0),
        'hgrn_norm': gain(ks[6], (DEPTH, HGRN_WIDTH)),
        'swa_sinks': nrm(ks[7], (DEPTH, SWA_Q_HEADS), 0.5),
        'w_br_hgrn': nrm(ks[8], (DEPTH, HGRN_WIDTH, D_MODEL), HGRN_WIDTH ** -0.5),
        'w_br_ret': nrm(ks[9], (DEPTH, RET_WIDTH, D_MODEL), RET_WIDTH ** -0.5),
        'w_br_swa': nrm(ks[10], (DEPTH, SWA_Q_WIDTH, D_MODEL), SWA_Q_WIDTH ** -0.5),
        'w_out': nrm(ks[11], (DEPTH, D_MODEL, D_MODEL), D_MODEL ** -0.5),
        'ln_xq': gain(ks[12], (DEPTH, D_MODEL)),
        'ln_xkv': gain(ks[13], (DEPTH, D_MODEL)),
        'w_xq': nrm(ks[14], (DEPTH, D_MODEL, D_MODEL), D_MODEL ** -0.5),
        'w_xk': nrm(ks[15], (DEPTH, D_MODEL, D_MODEL), D_MODEL ** -0.5),
        'w_xv': nrm(ks[16], (DEPTH, D_MODEL, D_MODEL), D_MODEL ** -0.5),
        'w_xo': nrm(ks[17], (DEPTH, D_MODEL, D_MODEL), D_MODEL ** -0.5),
        'ln_ffn': gain(ks[18], (DEPTH, D_MODEL)),
        'peer_wq': nrm(ks[19], (DEPTH, D_MODEL, PEER_HEADS * PEER_QDIM), D_MODEL ** -0.5),
        'peer_keys': nrm(ks[20], (DEPTH, 2, PEER_HEADS, PEER_N_KEYS, PEER_HALF), PEER_HALF ** -0.5),
        'peer_u': nrm(ks[21], (DEPTH, PEER_N_EXPERTS, D_MODEL), D_MODEL ** -0.5),
        'peer_v': nrm(ks[22], (DEPTH, PEER_N_EXPERTS, D_MODEL), PEER_HEADS ** -0.5),
        'ln_final': gain(ks[23], (D_MODEL,)),
    }


def reference(x, mem, positions, ln_mix, w_in, lb_param, hgrn_norm, swa_sinks, w_br_hgrn, w_br_ret, w_br_swa, w_out, ln_xq, ln_xkv, w_xq, w_xk, w_xv, w_xo, ln_ffn, peer_wq, peer_keys, peer_u, peer_v, ln_final):
    f32 = jnp.float32
    b, t, d = x.shape
    swa_inv_freq = ROPE_THETA ** (-jnp.arange(0, SWA_DIM, 2, dtype=f32) / SWA_DIM)
    ret_inv_freq = ROPE_THETA ** (-jnp.linspace(0.0, 1.0, RET_DIM // 2, dtype=f32))
    log_gamma = jnp.log(1.0 - 2.0 ** (-5.0 - jnp.arange(RET_HEADS, dtype=f32)))
    lb_sm = jax.nn.softmax(lb_param.astype(f32), axis=0)
    lower = jnp.cumsum(lb_sm, axis=0) - lb_sm[0]

    for l in range(DEPTH):
        h = rmsnorm(x, ln_mix[l])
        (hq, hf, hi, hg, rq, rk, rv, rg, sq, sk, sv, gate_logits) = jnp.split(h @ w_in[l], IN_SPLITS, axis=-1)

        lb = lower[l]
        z = hf.astype(f32)
        f_gate = lb + (1.0 - lb) * jax.nn.sigmoid(z)
        log_f = jnp.log(jnp.maximum(f_gate, TINY))
        k_in = (1.0 - lb) * jax.nn.sigmoid(-z)
        qa = bhtd(split_heads(jax.nn.silu(hq), HGRN_HEADS)) * (HGRN_DIM ** -0.5)
        oa = hgrn2_chunkwise(qa, bhtd(split_heads(k_in, HGRN_HEADS)), bhtd(split_heads(hi, HGRN_HEADS)), bhtd(split_heads(log_f, HGRN_HEADS)))
        oa = rmsnorm(bhtd(oa).reshape(b, t, HGRN_WIDTH), hgrn_norm[l]) * jax.nn.silu(hg)
        y_a = oa @ w_br_hgrn[l]

        qb = bhtd(rope(split_heads(rq, RET_HEADS), positions, ret_inv_freq))
        kb = bhtd(rope(split_heads(rk, RET_HEADS), positions, ret_inv_freq)) * (RET_DIM ** -0.5)
        ob = retention_chunkwise(qb, kb, bhtd(split_heads(rv, RET_HEADS)), log_gamma)
        ob = rmsnorm(bhtd(ob), None).reshape(b, t, RET_WIDTH) * jax.nn.silu(rg)
        y_b = ob @ w_br_ret[l]

        qc = bhtd(rope(split_heads(sq, SWA_Q_HEADS), positions, swa_inv_freq))
        kc = bhtd(rope(split_heads(sk, SWA_KV_HEADS), positions, swa_inv_freq))
        oc = swa_sink_attention(qc, kc, bhtd(split_heads(sv, SWA_KV_HEADS)), swa_sinks[l])
        y_c = bhtd(oc).reshape(b, t, SWA_Q_WIDTH) @ w_br_swa[l]

        gates = jax.nn.sigmoid(gate_logits.reshape(b, t, N_BRANCHES, d))
        mixed = gates[:, :, 0] * y_a + gates[:, :, 1] * y_b + gates[:, :, 2] * y_c
        x = x + mixed @ w_out[l]

        x = x + memory_cross_attention(rmsnorm(x, ln_xq[l]), rmsnorm(mem, ln_xkv[l]), w_xq[l], w_xk[l], w_xv[l], w_xo[l])

        x = x + peer_ffn(rmsnorm(x, ln_ffn[l]), peer_wq[l], peer_keys[l], peer_u[l], peer_v[l])

    return rmsnorm(x, ln_final)
```

```python
import functools
import math

import numpy as np
import jax
import jax.numpy as jnp
from jax import lax
from jax.experimental import pallas as pl
from jax.experimental.pallas import tpu as pltpu

F32 = jnp.float32
BF16 = jnp.bfloat16

D_MODEL = 1024
HEAD_DIM = 128
N_HEADS = 4
SWA_Q_HEADS = 8
SWA_KV_HEADS = 2
SWA_DIM = 64
SWA_BLOCK = 128
ROPE_THETA = 10000.0
N_MEM = 256
X_HEADS = 4
X_DIM = D_MODEL // X_HEADS
PEER_HEADS = 8
PEER_KEYS = 128
PEER_TOPK = 16
PEER_HALF = 128
N_EXPERTS = PEER_KEYS * PEER_KEYS
EPS = 1e-6
NEG_BIG = -1e30
TINY = 1e-30

LANE = 128
HG_CHUNK = 64
HG_SUB = 16
RET_CHUNK = 256
VMEM_LIMIT = 56 * 1024 * 1024

CB_GATE = 0
CB_HQ, CB_HF, CB_HI, CB_HG = 24, 28, 32, 36
CB_RQ, CB_RQR, CB_RK, CB_RKR, CB_RV, CB_RG = 40, 44, 48, 52, 56, 60
CB_SQ, CB_SQR, CB_SK, CB_SKR, CB_SV = 64, 72, 80, 82, 84
N_CB = 88
PROJ_COLS = N_CB * LANE


def _cparams(sem):
    return pltpu.CompilerParams(dimension_semantics=sem, vmem_limit_bytes=VMEM_LIMIT)


def _dot(a, b):
    return jnp.dot(a, b, preferred_element_type=F32)


def _dot_nt(a, b):
    return lax.dot_general(a, b, (((1,), (1,)), ((), ())), preferred_element_type=F32)


def _dot_tn(a, b):
    return lax.dot_general(a, b, (((0,), (0,)), ((), ())), preferred_element_type=F32)


def _split2(x):
    hi = x.astype(BF16)
    lo = (x - hi.astype(F32)).astype(BF16)
    return hi, lo


def _split3(x):
    hi = x.astype(BF16)
    r = x - hi.astype(F32)
    mid = r.astype(BF16)
    lo = (r - mid.astype(F32)).astype(BF16)
    return hi, mid, lo


def _sigmoid(x):
    return 1.0 / (1.0 + jnp.exp(-x))


def _rms(x):
    return x * lax.rsqrt(jnp.mean(x * x, axis=-1, keepdims=True) + EPS)


def _rope_kernel(pos_ref, fr_ref, fs_ref, cr_ref, sr_ref, cs_ref, ss_ref):
    pos = pos_ref[...].astype(F32)
    ang_r = pos * fr_ref[...]
    cr_ref[...] = jnp.cos(ang_r)
    sr_ref[...] = jnp.sin(ang_r)
    ang_s = pos * fs_ref[...]
    cs_ref[...] = jnp.cos(ang_s)
    ss_ref[...] = jnp.sin(ang_s)


def _rope_tables(positions):
    t = positions.shape[1]
    tt = min(t, 1024)
    ret_f = ROPE_THETA ** (-jnp.linspace(0.0, 1.0, HEAD_DIM // 2, dtype=F32))
    swa_f = ROPE_THETA ** (-jnp.arange(0, SWA_DIM, 2, dtype=F32) / SWA_DIM)
    fr = jnp.concatenate([ret_f, ret_f])[None, :]
    fs = jnp.concatenate([swa_f, swa_f, jnp.zeros((LANE - SWA_DIM,), F32)])[None, :]
    pos = positions.reshape(t, 1)
    tab = jax.ShapeDtypeStruct((t, LANE), F32)
    row = pl.BlockSpec((tt, LANE), lambda i: (i, 0))
    vec = pl.BlockSpec((1, LANE), lambda i: (0, 0))
    return pl.pallas_call(
        _rope_kernel,
        grid=(t // tt,),
        in_specs=[pl.BlockSpec((tt, 1), lambda i: (i, 0)), vec, vec],
        out_specs=[row, row, row, row],
        out_shape=[tab, tab, tab, tab],
        compiler_params=_cparams(("parallel",)),
        name="rope_tables",
    )(pos, fr, fs)


def _proj_kernel(x_ref, g_ref, w_ref, o_ref, h_scr):
    @pl.when(pl.program_id(1) == 0)
    def _():
        h_scr[...] = (_rms(x_ref[...]) * g_ref[...]).astype(BF16)

    o_ref[...] = _dot(h_scr[...], w_ref[...])


def _proj(x, g, w):
    t = x.shape[0]
    n = w.shape[1]
    tm = min(t, 1024)
    tn = 1024
    return pl.pallas_call(
        _proj_kernel,
        grid=(t // tm, n // tn),
        in_specs=[pl.BlockSpec((tm, D_MODEL), lambda i, j: (i, 0)),
                  pl.BlockSpec((1, D_MODEL), lambda i, j: (0, 0)),
                  pl.BlockSpec((D_MODEL, tn), lambda i, j: (0, j))],
        out_specs=pl.BlockSpec((tm, tn), lambda i, j: (i, j)),
        out_shape=jax.ShapeDtypeStruct((t, n), F32),
        scratch_shapes=[pltpu.VMEM((tm, D_MODEL), BF16)],
        compiler_params=_cparams(("parallel", "arbitrary")),
        name="in_proj",
    )(x, g, w)


def _relay_w_in(w_in):
    d = w_in.shape[0]
    sizes = (512,) * 8 + (512, 128, 128, 3 * D_MODEL)
    offs = np.concatenate([[0], np.cumsum(sizes)])
    hq, hf, hi, hg, rq, rk, rv, rg, sq, sk, sv, gates = [
        w_in[:, int(offs[i]):int(offs[i + 1])] for i in range(12)]

    def rot(w, nh, dh):
        w = w.reshape(d, nh, dh)
        return jnp.concatenate([-w[..., dh // 2:], w[..., :dh // 2]], axis=-1).reshape(d, nh * dh)

    def pad_heads(w, nh):
        w = w.reshape(d, nh, SWA_DIM)
        return jnp.pad(w, ((0, 0), (0, 0), (0, LANE - SWA_DIM))).reshape(d, nh * LANE)

    cols = [gates, hq, hf, hi, hg,
            rq, rot(rq, N_HEADS, HEAD_DIM), rk, rot(rk, N_HEADS, HEAD_DIM), rv, rg,
            pad_heads(sq, SWA_Q_HEADS), pad_heads(rot(sq, SWA_Q_HEADS, SWA_DIM), SWA_Q_HEADS),
            pad_heads(sk, SWA_KV_HEADS), pad_heads(rot(sk, SWA_KV_HEADS, SWA_DIM), SWA_KV_HEADS),
            pad_heads(sv, SWA_KV_HEADS),
            jnp.zeros((d, PROJ_COLS - (CB_SV + SWA_KV_HEADS) * LANE), w_in.dtype)]
    return jnp.concatenate(cols, axis=1).astype(BF16)


def _hgrn_kernel(hq_ref, hf_ref, hi_ref, lb_ref, o_ref, st_ref, *, n_chunks):
    @pl.when(pl.program_id(0) == 0)
    def _():
        st_ref[...] = jnp.zeros_like(st_ref)

    c = HG_CHUNK
    nsub = c // HG_SUB
    row = lax.broadcasted_iota(jnp.int32, (c, c), 0)
    col = lax.broadcasted_iota(jnp.int32, (c, c), 1)
    tri = (col <= row).astype(BF16)
    row_blk = row // HG_SUB
    col_blk = col // HG_SUB
    row_in = row % HG_SUB
    col_in = col % HG_SUB
    rsub = lax.broadcasted_iota(jnp.int32, (c, HEAD_DIM), 0) // HG_SUB
    scale = HEAD_DIM ** -0.5

    def chunk_body(ci, carry):
        r0 = pl.multiple_of(ci * c, c)
        for h in range(N_HEADS):
            lanes = slice(h * HEAD_DIM, (h + 1) * HEAD_DIM)
            xq = hq_ref[pl.ds(r0, c), lanes]
            z = hf_ref[pl.ds(r0, c), lanes]
            v = hi_ref[pl.ds(r0, c), lanes]
            lb = lb_ref[:, lanes]
            omlb = 1.0 - lb
            q = xq * _sigmoid(xq) * scale
            f = lb + omlb * _sigmoid(z)
            logf = jnp.log(jnp.maximum(f, TINY))
            kk = omlb * _sigmoid(-z)
            l1, l2, l3 = _split3(logf)
            cum = _dot(tri, l1) + _dot(tri, l2) + _dot(tri, l3)
            ends = [cum[(j + 1) * HG_SUB - 1:(j + 1) * HG_SUB, :] for j in range(nsub)]
            bsel = ends[nsub - 1]
            for j in range(nsub - 2, -1, -1):
                bsel = jnp.where(rsub <= j, ends[j], bsel)
            kp = kk * jnp.exp(bsel - cum)
            kp_b = kp.astype(BF16)
            scores = jnp.zeros((c, c), F32)
            for j in range(nsub - 1):
                qj = (q * jnp.exp(jnp.minimum(cum - ends[j], 0.0))).astype(BF16)
                sj = _dot_nt(qj, kp_b)
                scores = jnp.where((col_blk == j) & (row_blk > j), sj, scores)
            q3 = q.reshape(nsub, HG_SUB, HEAD_DIM)
            k3 = kk.reshape(nsub, HG_SUB, HEAD_DIM)
            c3 = cum.reshape(nsub, HG_SUB, HEAD_DIM)
            for s in range(HG_SUB):
                krow = k3[:, s:s + 1, :]
                crow = c3[:, s:s + 1, :]
                p = q3 * krow * jnp.exp(jnp.minimum(c3 - crow, 0.0))
                colsum = jnp.sum(p.reshape(c, HEAD_DIM), axis=-1, keepdims=True)
                hit = (col_blk == row_blk) & (col_in == s) & (row_in >= s)
                scores = jnp.where(hit, colsum, scores)
            st = st_ref[h]
            qe = (q * jnp.exp(cum)).astype(BF16)
            o = _dot_nt(qe, st.astype(BF16)) + _dot(scores.astype(BF16), v.astype(BF16))
            o_ref[pl.ds(r0, c), lanes] = o
            last = ends[nsub - 1]
            kl = (kp * jnp.exp(last - bsel)).astype(BF16)
            st_ref[h] = jnp.exp(last) * st + _dot_tn(v.astype(BF16), kl)
        return carry

    lax.fori_loop(0, n_chunks, chunk_body, 0)


def _hgrn(proj, lower):
    t = proj.shape[0]
    tb = min(t, 512)
    w = N_HEADS * HEAD_DIM

    def cols(cb):
        return pl.BlockSpec((tb, w), lambda i: (i, cb // N_HEADS))

    return pl.pallas_call(
        functools.partial(_hgrn_kernel, n_chunks=tb // HG_CHUNK),
        grid=(t // tb,),
        in_specs=[cols(CB_HQ), cols(CB_HF), cols(CB_HI),
                  pl.BlockSpec((1, w), lambda i: (0, 0))],
        out_specs=pl.BlockSpec((tb, w), lambda i: (i, 0)),
        out_shape=jax.ShapeDtypeStruct((t, w), F32),
        scratch_shapes=[pltpu.VMEM((N_HEADS, HEAD_DIM, HEAD_DIM), F32)],
        compiler_params=_cparams(("arbitrary",)),
        name="hgrn2",
    )(proj, proj, proj, lower)


def _ret_kernel(q_ref, qr_ref, k_ref, kr_ref, v_ref, cos_ref, sin_ref,
                dec_ref, qd_ref, kd_ref, cd_ref, o_ref, s_ref):
    @pl.when(pl.program_id(0) == 0)
    def _():
        s_ref[...] = jnp.zeros_like(s_ref)

    cos = cos_ref[...]
    sin = sin_ref[...]
    scale = HEAD_DIM ** -0.5
    for h in range(N_HEADS):
        lanes = slice(h * HEAD_DIM, (h + 1) * HEAD_DIM)
        q = q_ref[:, lanes] * cos + qr_ref[:, lanes] * sin
        k = (k_ref[:, lanes] * cos + kr_ref[:, lanes] * sin) * scale
        v = v_ref[:, lanes].astype(BF16)
        qb = q.astype(BF16)
        s = s_ref[h]
        scores = _dot_nt(qb, k.astype(BF16)) * dec_ref[h]
        o = _dot(qb, s.astype(BF16)) * qd_ref[h] + _dot(scores.astype(BF16), v)
        o_ref[:, lanes] = o
        s_ref[h] = cd_ref[h] * s + _dot_tn((k * kd_ref[h]).astype(BF16), v)


def _ret(proj, cos_r, sin_r):
    t = proj.shape[0]
    c = min(t, RET_CHUNK)
    w = N_HEADS * HEAD_DIM
    log_gamma = jnp.log(1.0 - 2.0 ** (-5.0 - jnp.arange(N_HEADS, dtype=F32)))
    idx = jnp.arange(c, dtype=F32)
    rel = idx[:, None] - idx[None, :]
    decay = jnp.exp(jnp.where(rel >= 0, log_gamma[:, None, None] * rel, NEG_BIG))
    ones = jnp.ones((1, 1, HEAD_DIM), F32)
    q_decay = jnp.exp(log_gamma[:, None] * (idx + 1.0))[:, :, None] * ones
    k_decay = jnp.exp(log_gamma[:, None] * (c - 1.0 - idx))[:, :, None] * ones
    c_decay = jnp.exp(log_gamma * c)[:, None, None] * ones

    def cols(cb):
        return pl.BlockSpec((c, w), lambda i: (i, cb // N_HEADS))

    def const(shape):
        return pl.BlockSpec(shape, lambda i: (0, 0, 0))

    tab = pl.BlockSpec((c, LANE), lambda i: (i, 0))
    return pl.pallas_call(
        _ret_kernel,
        grid=(t // c,),
        in_specs=[cols(CB_RQ), cols(CB_RQR), cols(CB_RK), cols(CB_RKR), cols(CB_RV), tab, tab,
                  const((N_HEADS, c, c)), const((N_HEADS, c, HEAD_DIM)),
                  const((N_HEADS, c, HEAD_DIM)), const((N_HEADS, 1, HEAD_DIM))],
        out_specs=pl.BlockSpec((c, w), lambda i: (i, 0)),
        out_shape=jax.ShapeDtypeStruct((t, w), F32),
        scratch_shapes=[pltpu.VMEM((N_HEADS, HEAD_DIM, HEAD_DIM), F32)],
        compiler_params=_cparams(("arbitrary",)),
        name="retention",
    )(proj, proj, proj, proj, proj, cos_r, sin_r, decay, q_decay, k_decay, c_decay)


def _swa_kernel(sink_ref, q_ref, qr_ref, kc_ref, krc_ref, vc_ref, kp_ref, krp_ref, vp_ref,
                cc_ref, sc_ref, cp_ref, sp_ref, o_ref):
    b = SWA_BLOCK
    blk = pl.program_id(0)
    cos_c, sin_c = cc_ref[...], sc_ref[...]
    cos_p, sin_p = cp_ref[...], sp_ref[...]
    qi = lax.broadcasted_iota(jnp.int32, (b, 2 * b), 0) + b
    ki = lax.broadcasted_iota(jnp.int32, (b, 2 * b), 1)
    rel = qi - ki
    keep = (rel >= 0) & (rel < b) & ((blk > 0) | (ki >= b))
    group = SWA_Q_HEADS // SWA_KV_HEADS
    for g in range(SWA_KV_HEADS):
        kl = slice(g * LANE, (g + 1) * LANE)
        k_cur = kc_ref[:, kl] * cos_c + krc_ref[:, kl] * sin_c
        k_prev = kp_ref[:, kl] * cos_p + krp_ref[:, kl] * sin_p
        kw = jnp.concatenate([k_prev, k_cur], axis=0).astype(BF16)
        vw = jnp.concatenate([vp_ref[:, kl], vc_ref[:, kl]], axis=0).astype(BF16)
        for j in range(group):
            h = g * group + j
            ql = slice(h * LANE, (h + 1) * LANE)
            q = (q_ref[:, ql] * cos_c + qr_ref[:, ql] * sin_c).astype(BF16)
            s = _dot_nt(q, kw) * (SWA_DIM ** -0.5)
            s = jnp.where(keep, s, NEG_BIG)
            sink = sink_ref[h]
            m = jnp.maximum(jnp.max(s, axis=-1, keepdims=True), sink)
            p = jnp.exp(s - m)
            denom = jnp.sum(p, axis=-1, keepdims=True) + jnp.exp(sink - m)
            o_ref[:, ql] = _dot((p / denom).astype(BF16), vw)


def _swa(proj, cos_s, sin_s, sinks):
    t = proj.shape[0]
    b = SWA_BLOCK
    qw = SWA_Q_HEADS * LANE
    kw = SWA_KV_HEADS * LANE

    def cur(cb, width):
        return pl.BlockSpec((b, width), lambda i: (i, cb * LANE // width))

    def prev(cb, width):
        return pl.BlockSpec((b, width), lambda i: (jnp.maximum(i - 1, 0), cb * LANE // width))

    tab_c = pl.BlockSpec((b, LANE), lambda i: (i, 0))
    tab_p = pl.BlockSpec((b, LANE), lambda i: (jnp.maximum(i - 1, 0), 0))
    return pl.pallas_call(
        _swa_kernel,
        grid=(t // b,),
        in_specs=[pl.BlockSpec(memory_space=pltpu.SMEM),
                  cur(CB_SQ, qw), cur(CB_SQR, qw),
                  cur(CB_SK, kw), cur(CB_SKR, kw), cur(CB_SV, kw),
                  prev(CB_SK, kw), prev(CB_SKR, kw), prev(CB_SV, kw),
                  tab_c, tab_c, tab_p, tab_p],
        out_specs=pl.BlockSpec((b, qw), lambda i: (i, 0)),
        out_shape=jax.ShapeDtypeStruct((t, qw), F32),
        compiler_params=_cparams(("parallel",)),
        name="swa",
    )(sinks, proj, proj, proj, proj, proj, proj, proj, proj, cos_s, sin_s, cos_s, sin_s)


def _merge_kernel(x_ref, oa_ref, hg_ref, ob_ref, rg_ref, oc_ref, ga_ref, gb_ref, gc_ref,
                  hn_ref, wa_ref, wb_ref, wc_ref, wo_ref, o_ref):
    hg = hg_ref[...]
    a = _rms(oa_ref[...]) * hn_ref[...] * (hg * _sigmoid(hg))
    ya = _dot(a.astype(BF16), wa_ref[...])
    rg = rg_ref[...]
    ob = ob_ref[...]
    parts = []
    for h in range(N_HEADS):
        lanes = slice(h * HEAD_DIM, (h + 1) * HEAD_DIM)
        parts.append(_rms(ob[:, lanes]))
    bn = jnp.concatenate(parts, axis=-1) * (rg * _sigmoid(rg))
    yb = _dot(bn.astype(BF16), wb_ref[...])
    yc = _dot(oc_ref[...].astype(BF16), wc_ref[...])
    mixed = _sigmoid(ga_ref[...]) * ya + _sigmoid(gb_ref[...]) * yb + _sigmoid(gc_ref[...]) * yc
    o_ref[...] = x_ref[...] + _dot(mixed.astype(BF16), wo_ref[...])


def _merge(x, proj, oa, ob, oc, hn, wa, wb, wc, wo):
    t = x.shape[0]
    tt = min(t, 512)
    w = N_HEADS * HEAD_DIM
    qw = SWA_Q_HEADS * LANE

    def rows(width, cb=0):
        return pl.BlockSpec((tt, width), lambda i: (i, cb * LANE // width))

    def full(shape):
        return pl.BlockSpec(shape, lambda i: (0, 0))

    return pl.pallas_call(
        _merge_kernel,
        grid=(t // tt,),
        in_specs=[rows(D_MODEL), rows(w), rows(w, CB_HG), rows(w), rows(w, CB_RG), rows(qw),
                  rows(D_MODEL, CB_GATE), rows(D_MODEL, CB_GATE + 8), rows(D_MODEL, CB_GATE + 16),
                  full((1, w)), full((w, D_MODEL)), full((w, D_MODEL)), full((qw, D_MODEL)),
                  full((D_MODEL, D_MODEL))],
        out_specs=rows(D_MODEL),
        out_shape=jax.ShapeDtypeStruct((t, D_MODEL), F32),
        compiler_params=_cparams(("parallel",)),
        name="merge",
    )(x, oa, proj, ob, proj, oc, proj, proj, proj, hn, wa, wb, wc, wo)


def _memkv_kernel(m_ref, g_ref, wk_ref, wv_ref, k_ref, v_ref):
    h = (_rms(m_ref[...]) * g_ref[...]).astype(BF16)
    k_ref[...] = _dot(h, wk_ref[...]).astype(BF16)
    v_ref[...] = _dot(h, wv_ref[...]).astype(BF16)


def _memkv(mem, g, wk, wv):
    nm = mem.shape[0]
    out = jax.ShapeDtypeStruct((nm, D_MODEL), BF16)
    return pl.pallas_call(
        _memkv_kernel,
        out_shape=[out, out],
        compiler_params=pltpu.CompilerParams(vmem_limit_bytes=VMEM_LIMIT),
        name="mem_kv",
    )(mem, g, wk, wv)


def _xattn_kernel(x_ref, g_ref, wq_ref, k_ref, v_ref, wo_ref, o_ref):
    x = x_ref[...]
    h = (_rms(x) * g_ref[...]).astype(BF16)
    q = _dot(h, wq_ref[...])
    outs = []
    for hh in range(X_HEADS):
        lanes = slice(hh * X_DIM, (hh + 1) * X_DIM)
        s = _dot_nt(q[:, lanes].astype(BF16), k_ref[:, lanes]) * (X_DIM ** -0.5)
        m = jnp.max(s, axis=-1, keepdims=True)
        p = jnp.exp(s - m)
        p = p / jnp.sum(p, axis=-1, keepdims=True)
        outs.append(_dot(p.astype(BF16), v_ref[:, lanes]))
    o = jnp.concatenate(outs, axis=-1)
    o_ref[...] = x + _dot(o.astype(BF16), wo_ref[...])


def _xattn(x, g, wq, k, v, wo):
    t = x.shape[0]
    tt = min(t, 512)
    nm = k.shape[0]

    def full(shape):
        return pl.BlockSpec(shape, lambda i: (0, 0))

    rows = pl.BlockSpec((tt, D_MODEL), lambda i: (i, 0))
    return pl.pallas_call(
        _xattn_kernel,
        grid=(t // tt,),
        in_specs=[rows, full((1, D_MODEL)), full((D_MODEL, D_MODEL)),
                  full((nm, D_MODEL)), full((nm, D_MODEL)), full((D_MODEL, D_MODEL))],
        out_specs=rows,
        out_shape=jax.ShapeDtypeStruct((t, D_MODEL), F32),
        compiler_params=_cparams(("parallel",)),
        name="xattn",
    )(x, g, wq, k, v, wo)


_CAND_ROWS = [PEER_TOPK // (k1 + 1) for k1 in range(8)]


def _top_values(s, n):
    vals = []
    for _ in range(n):
        m = jnp.max(s, axis=0, keepdims=True)
        vals.append(m)
        s = jnp.where(s >= m, -jnp.inf, s)
    return vals


def _stack_rows(rows, n):
    tt = rows[0].shape[1]
    r = lax.broadcasted_iota(jnp.int32, (n, tt), 0)
    out = jnp.zeros((n, tt), F32)
    for k in range(n):
        out = jnp.where(r == k, rows[k], out)
    return out


def _router_kernel(x_ref, g_ref, wh_ref, wl_ref, kh_ref, kl_ref,
                   hb_ref, a_ref, b_ref, r_ref, l_ref, qt_scr):
    h = _rms(x_ref[...]) * g_ref[...]
    h_hi, h_lo = _split2(h)
    hb_ref[...] = h_hi
    wh = wh_ref[...]
    qt_scr[...] = _dot_nt(wh, h_hi) + _dot_nt(wl_ref[...], h_hi) + _dot_nt(wh, h_lo)
    tt = x_ref.shape[0]
    k = PEER_TOPK
    r8 = lax.broadcasted_iota(jnp.int32, (8, tt), 0)

    def head_body(hh, carry):
        scores = []
        for p in range(2):
            r0 = pl.multiple_of(hh * (2 * PEER_HALF) + p * PEER_HALF, PEER_HALF)
            q_hi, q_lo = _split2(qt_scr[pl.ds(r0, PEER_HALF), :])
            kh = kh_ref[p * PEER_HEADS + hh]
            scores.append(_dot(kh, q_hi) + _dot(kl_ref[p * PEER_HEADS + hh], q_hi) + _dot(kh, q_lo))
        s1, s2 = scores
        v1 = _top_values(s1, k)
        v2 = _top_values(s2, k)
        sv1 = _stack_rows(v1, k)
        sv2 = _stack_rows(v2, k)
        groups = [v1[0] + sv2]
        for k1 in range(1, 8):
            groups.append(jnp.where(r8 < _CAND_ROWS[k1], v1[k1] + sv2[0:8], -jnp.inf))
        groups.append(sv1[8:16] + v2[0])
        cand = jnp.concatenate(groups, axis=0)
        tau = _top_values(cand, k)[k - 1]
        m0 = v1[0] + v2[0]
        z = jnp.sum(jnp.where(cand >= tau, jnp.exp(cand - m0), 0.0), axis=0, keepdims=True)
        a_ref[hh] = jnp.exp(s1 - v1[0])
        b_ref[hh] = jnp.exp(s2 - v2[0]) / z
        rank2 = jnp.zeros_like(s2)
        count1 = jnp.zeros_like(s1)
        for kk in range(k):
            rank2 = rank2 + (s2 < v2[kk]).astype(F32)
            count1 = count1 + (s1 + v2[kk] >= tau).astype(F32)
        r_ref[hh] = rank2
        l_ref[hh] = count1
        return carry

    lax.fori_loop(0, PEER_HEADS, head_body, 0)


def _router(x, g, wq_hi, wq_lo, k_hi, k_lo):
    t = x.shape[0]
    tt = min(t, 256)
    nq = wq_hi.shape[0]
    sel = jax.ShapeDtypeStruct((PEER_HEADS, PEER_KEYS, t), F32)
    sel_spec = pl.BlockSpec((PEER_HEADS, PEER_KEYS, tt), lambda i: (0, 0, i))
    return pl.pallas_call(
        _router_kernel,
        grid=(t // tt,),
        in_specs=[pl.BlockSpec((tt, D_MODEL), lambda i: (i, 0)),
                  pl.BlockSpec((1, D_MODEL), lambda i: (0, 0)),
                  pl.BlockSpec((nq, D_MODEL), lambda i: (0, 0)),
                  pl.BlockSpec((nq, D_MODEL), lambda i: (0, 0)),
                  pl.BlockSpec((2 * PEER_HEADS, PEER_KEYS, PEER_HALF), lambda i: (0, 0, 0)),
                  pl.BlockSpec((2 * PEER_HEADS, PEER_KEYS, PEER_HALF), lambda i: (0, 0, 0))],
        out_specs=[pl.BlockSpec((tt, D_MODEL), lambda i: (i, 0)), sel_spec, sel_spec, sel_spec, sel_spec],
        out_shape=[jax.ShapeDtypeStruct((t, D_MODEL), BF16), sel, sel, sel, sel],
        scratch_shapes=[pltpu.VMEM((nq, tt), F32)],
        compiler_params=_cparams(("parallel",)),
        name="peer_router",
    )(x, g, wq_hi, wq_lo, k_hi, k_lo)


def _peer_kernel(x_ref, hb_ref, a_ref, l_ref, b_ref, r_ref, u_ref, vt_ref, gf_ref, o_ref,
                 acc_ref, ga_ref, *, final_norm):
    e = pl.program_id(1)

    @pl.when(e == 0)
    def _():
        acc_ref[...] = jnp.zeros_like(acc_ref)

    act = _dot_nt(u_ref[...], hb_ref[...])
    act = 0.5 * act * (1.0 + lax.erf(act * (2.0 ** -0.5)))
    n_i1 = u_ref.shape[0] // PEER_KEYS
    for j in range(n_i1):
        gate = jnp.zeros((PEER_KEYS, act.shape[1]), F32)
        for hh in range(PEER_HEADS):
            arow = a_ref[hh, j:j + 1, :]
            lrow = l_ref[hh, j:j + 1, :]
            gate = gate + arow * jnp.where(r_ref[hh] < lrow, b_ref[hh], 0.0)
        rows = slice(j * PEER_KEYS, (j + 1) * PEER_KEYS)
        ga_ref[rows, :] = (gate * act[rows, :]).astype(BF16)
    acc_ref[...] += _dot(vt_ref[...], ga_ref[...])

    @pl.when(e == pl.num_programs(1) - 1)
    def _():
        y = x_ref[...] + acc_ref[...].T
        if final_norm:
            y = _rms(y) * gf_ref[...]
        o_ref[...] = y


def _peer(x, hb, a, l, b, r, u, vt, gf, final_norm):
    t = x.shape[0]
    tt = min(t, 512)
    eb = 8 * PEER_KEYS
    rows = pl.BlockSpec((tt, D_MODEL), lambda i, e: (i, 0))
    i1_rows = pl.BlockSpec((PEER_HEADS, eb // PEER_KEYS, tt), lambda i, e: (0, e, i))
    all_rows = pl.BlockSpec((PEER_HEADS, PEER_KEYS, tt), lambda i, e: (0, 0, i))
    return pl.pallas_call(
        functools.partial(_peer_kernel, final_norm=final_norm),
        grid=(t // tt, N_EXPERTS // eb),
        in_specs=[rows, rows, i1_rows, i1_rows, all_rows, all_rows,
                  pl.BlockSpec((eb, D_MODEL), lambda i, e: (e, 0)),
                  pl.BlockSpec((D_MODEL, eb), lambda i, e: (0, e)),
                  pl.BlockSpec((1, D_MODEL), lambda i, e: (0, 0))],
        out_specs=rows,
        out_shape=jax.ShapeDtypeStruct((t, D_MODEL), F32),
        scratch_shapes=[pltpu.VMEM((D_MODEL, tt), F32), pltpu.VMEM((eb, tt), BF16)],
        compiler_params=_cparams(("parallel", "arbitrary")),
        name="peer_dense",
    )(x, hb, a, l, b, r, u, vt, gf)


def kernel(x, mem, positions, ln_mix, w_in, lb_param, hgrn_norm, swa_sinks, w_br_hgrn, w_br_ret,
           w_br_swa, w_out, ln_xq, ln_xkv, w_xq, w_xk, w_xv, w_xo, ln_ffn, peer_wq, peer_keys,
           peer_u, peer_v, ln_final):
    b, t, d = x.shape
    assert b == 1 and d == D_MODEL
    depth = w_in.shape[0]
    xs = x.reshape(t, d)
    ms = mem.reshape(mem.shape[1], d)
    cos_r, sin_r, cos_s, sin_s = _rope_tables(positions)
    lb_sm = jax.nn.softmax(lb_param.astype(F32), axis=0)
    lower = jnp.cumsum(lb_sm, axis=0) - lb_sm[0]

    for l in range(depth):
        proj = _proj(xs, ln_mix[l][None, :], _relay_w_in(w_in[l]))
        oa = _hgrn(proj, lower[l][None, :])
        ob = _ret(proj, cos_r, sin_r)
        oc = _swa(proj, cos_s, sin_s, swa_sinks[l])
        wc = jnp.pad(w_br_swa[l].reshape(SWA_Q_HEADS, SWA_DIM, d),
                     ((0, 0), (0, LANE - SWA_DIM), (0, 0))).reshape(SWA_Q_HEADS * LANE, d)
        xs = _merge(xs, proj, oa, ob, oc, hgrn_norm[l][None, :], w_br_hgrn[l].astype(BF16),
                    w_br_ret[l].astype(BF16), wc.astype(BF16), w_out[l].astype(BF16))

        km, vm = _memkv(ms, ln_xkv[l][None, :], w_xk[l].astype(BF16), w_xv[l].astype(BF16))
        xs = _xattn(xs, ln_xq[l][None, :], w_xq[l].astype(BF16), km, vm, w_xo[l].astype(BF16))

        wq_hi, wq_lo = _split2(peer_wq[l].T)
        k_hi, k_lo = _split2(peer_keys[l].reshape(2 * PEER_HEADS, PEER_KEYS, PEER_HALF))
        hb, a, bz, r2, cnt = _router(xs, ln_ffn[l][None, :], wq_hi, wq_lo, k_hi, k_lo)
        xs = _peer(xs, hb, a, cnt, bz, r2, peer_u[l].astype(BF16), peer_v[l].T.astype(BF16),
                   ln_final[None, :], final_norm=(l == depth - 1))

    return xs.reshape(b, t, d)
```

```python
import functools
import math

import numpy as np
import jax
import jax.numpy as jnp
from jax import lax
from jax.experimental import pallas as pl
from jax.experimental.pallas import tpu as pltpu

F32 = jnp.float32
BF16 = jnp.bfloat16

D_MODEL = 1024
HEAD_DIM = 128
N_HEADS = 4
SWA_Q_HEADS = 8
SWA_KV_HEADS = 2
SWA_DIM = 64
SWA_BLOCK = 128
ROPE_THETA = 10000.0
N_MEM = 256
X_HEADS = 4
X_DIM = D_MODEL // X_HEADS
PEER_HEADS = 8
PEER_KEYS = 128
PEER_TOPK = 16
PEER_HALF = 128
N_EXPERTS = PEER_KEYS * PEER_KEYS
EPS = 1e-6
NEG_BIG = -1e30
TINY = 1e-30

LANE = 128
HG_CHUNK = 64
HG_SUB = 16
RET_CHUNK = 256
PEER_TOKEN_CHUNK = 512
VMEM_LIMIT = 56 * 1024 * 1024

CB_GATE = 0
CB_HQ, CB_HF, CB_HI, CB_HG = 24, 28, 32, 36
CB_RQ, CB_RQR, CB_RK, CB_RKR, CB_RV, CB_RG = 40, 44, 48, 52, 56, 60
CB_SQ, CB_SQR, CB_SK, CB_SKR, CB_SV = 64, 72, 80, 82, 84
N_CB = 88
PROJ_COLS = N_CB * LANE


def _cparams(sem):
    return pltpu.CompilerParams(dimension_semantics=sem, vmem_limit_bytes=VMEM_LIMIT)


def _dot(a, b):
    return jnp.dot(a, b, preferred_element_type=F32)


def _dot_nt(a, b):
    return lax.dot_general(a, b, (((1,), (1,)), ((), ())), preferred_element_type=F32)


def _dot_tn(a, b):
    return lax.dot_general(a, b, (((0,), (0,)), ((), ())), preferred_element_type=F32)


def _split2(x):
    hi = x.astype(BF16)
    lo = (x - hi.astype(F32)).astype(BF16)
    return hi, lo


def _split3(x):
    hi = x.astype(BF16)
    r = x - hi.astype(F32)
    mid = r.astype(BF16)
    lo = (r - mid.astype(F32)).astype(BF16)
    return hi, mid, lo


def _sigmoid(x):
    return 1.0 / (1.0 + jnp.exp(-x))


def _rms(x):
    return x * lax.rsqrt(jnp.mean(x * x, axis=-1, keepdims=True) + EPS)


def _rope_kernel(pos_ref, fr_ref, fs_ref, cr_ref, sr_ref, cs_ref, ss_ref):
    pos = pos_ref[...].astype(F32)
    ang_r = pos * fr_ref[...]
    cr_ref[...] = jnp.cos(ang_r)
    sr_ref[...] = jnp.sin(ang_r)
    ang_s = pos * fs_ref[...]
    cs_ref[...] = jnp.cos(ang_s)
    ss_ref[...] = jnp.sin(ang_s)


def _rope_tables(positions):
    t = positions.shape[1]
    tt = min(t, 1024)
    ret_f = ROPE_THETA ** (-jnp.linspace(0.0, 1.0, HEAD_DIM // 2, dtype=F32))
    swa_f = ROPE_THETA ** (-jnp.arange(0, SWA_DIM, 2, dtype=F32) / SWA_DIM)
    fr = jnp.concatenate([ret_f, ret_f])[None, :]
    fs = jnp.concatenate([swa_f, swa_f, jnp.zeros((LANE - SWA_DIM,), F32)])[None, :]
    pos = positions.reshape(t, 1)
    tab = jax.ShapeDtypeStruct((t, LANE), F32)
    row = pl.BlockSpec((tt, LANE), lambda i: (i, 0))
    vec = pl.BlockSpec((1, LANE), lambda i: (0, 0))
    return pl.pallas_call(
        _rope_kernel,
        grid=(t // tt,),
        in_specs=[pl.BlockSpec((tt, 1), lambda i: (i, 0)), vec, vec],
        out_specs=[row, row, row, row],
        out_shape=[tab, tab, tab, tab],
        compiler_params=_cparams(("parallel",)),
        name="rope_tables",
    )(pos, fr, fs)


def _proj_kernel(x_ref, g_ref, w_ref, o_ref, h_scr):
    @pl.when(pl.program_id(1) == 0)
    def _():
        h_scr[...] = (_rms(x_ref[...]) * g_ref[...]).astype(BF16)

    o_ref[...] = _dot(h_scr[...], w_ref[...])


def _proj(x, g, w):
    t = x.shape[0]
    n = w.shape[1]
    tm = min(t, 1024)
    tn = 1024
    return pl.pallas_call(
        _proj_kernel,
        grid=(t // tm, n // tn),
        in_specs=[pl.BlockSpec((tm, D_MODEL), lambda i, j: (i, 0)),
                  pl.BlockSpec((1, D_MODEL), lambda i, j: (0, 0)),
                  pl.BlockSpec((D_MODEL, tn), lambda i, j: (0, j))],
        out_specs=pl.BlockSpec((tm, tn), lambda i, j: (i, j)),
        out_shape=jax.ShapeDtypeStruct((t, n), F32),
        scratch_shapes=[pltpu.VMEM((tm, D_MODEL), BF16)],
        compiler_params=_cparams(("parallel", "arbitrary")),
        name="in_proj",
    )(x, g, w)


def _relay_w_in(w_in):
    d = w_in.shape[0]
    sizes = (512,) * 8 + (512, 128, 128, 3 * D_MODEL)
    offs = np.concatenate([[0], np.cumsum(sizes)])
    hq, hf, hi, hg, rq, rk, rv, rg, sq, sk, sv, gates = [
        w_in[:, int(offs[i]):int(offs[i + 1])] for i in range(12)]

    def rot(w, nh, dh):
        w = w.reshape(d, nh, dh)
        return jnp.concatenate([-w[..., dh // 2:], w[..., :dh // 2]], axis=-1).reshape(d, nh * dh)

    def pad_heads(w, nh):
        w = w.reshape(d, nh, SWA_DIM)
        return jnp.pad(w, ((0, 0), (0, 0), (0, LANE - SWA_DIM))).reshape(d, nh * LANE)

    cols = [gates, hq, hf, hi, hg,
            rq, rot(rq, N_HEADS, HEAD_DIM), rk, rot(rk, N_HEADS, HEAD_DIM), rv, rg,
            pad_heads(sq, SWA_Q_HEADS), pad_heads(rot(sq, SWA_Q_HEADS, SWA_DIM), SWA_Q_HEADS),
            pad_heads(sk, SWA_KV_HEADS), pad_heads(rot(sk, SWA_KV_HEADS, SWA_DIM), SWA_KV_HEADS),
            pad_heads(sv, SWA_KV_HEADS),
            jnp.zeros((d, PROJ_COLS - (CB_SV + SWA_KV_HEADS) * LANE), w_in.dtype)]
    return jnp.concatenate(cols, axis=1).astype(BF16)


def _hgrn_kernel(hq_ref, hf_ref, hi_ref, lb_ref, o_ref, st_ref, *, n_chunks):
    @pl.when(pl.program_id(0) == 0)
    def _():
        st_ref[...] = jnp.zeros_like(st_ref)

    c = HG_CHUNK
    nsub = c // HG_SUB
    row = lax.broadcasted_iota(jnp.int32, (c, c), 0)
    col = lax.broadcasted_iota(jnp.int32, (c, c), 1)
    tri = (col <= row).astype(BF16)
    row_blk = row // HG_SUB
    col_blk = col // HG_SUB
    row_in = row % HG_SUB
    col_in = col % HG_SUB
    rsub = lax.broadcasted_iota(jnp.int32, (c, HEAD_DIM), 0) // HG_SUB
    scale = HEAD_DIM ** -0.5

    def chunk_body(ci, carry):
        r0 = pl.multiple_of(ci * c, c)
        for h in range(N_HEADS):
            lanes = slice(h * HEAD_DIM, (h + 1) * HEAD_DIM)
            xq = hq_ref[pl.ds(r0, c), lanes]
            z = hf_ref[pl.ds(r0, c), lanes]
            v = hi_ref[pl.ds(r0, c), lanes]
            lb = lb_ref[:, lanes]
            omlb = 1.0 - lb
            q = xq * _sigmoid(xq) * scale
            f = lb + omlb * _sigmoid(z)
            logf = jnp.log(jnp.maximum(f, TINY))
            kk = omlb * _sigmoid(-z)
            l1, l2, l3 = _split3(logf)
            cum = _dot(tri, l1) + _dot(tri, l2) + _dot(tri, l3)
            ends = [cum[(j + 1) * HG_SUB - 1:(j + 1) * HG_SUB, :] for j in range(nsub)]
            bsel = ends[nsub - 1]
            for j in range(nsub - 2, -1, -1):
                bsel = jnp.where(rsub <= j, ends[j], bsel)
            kp = kk * jnp.exp(bsel - cum)
            kp_b = kp.astype(BF16)
            scores = jnp.zeros((c, c), F32)
            for j in range(nsub - 1):
                qj = (q * jnp.exp(jnp.minimum(cum - ends[j], 0.0))).astype(BF16)
                sj = _dot_nt(qj, kp_b)
                scores = jnp.where((col_blk == j) & (row_blk > j), sj, scores)
            q3 = q.reshape(nsub, HG_SUB, HEAD_DIM)
            k3 = kk.reshape(nsub, HG_SUB, HEAD_DIM)
            c3 = cum.reshape(nsub, HG_SUB, HEAD_DIM)
            for s in range(HG_SUB):
                krow = k3[:, s:s + 1, :]
                crow = c3[:, s:s + 1, :]
                p = q3 * krow * jnp.exp(jnp.minimum(c3 - crow, 0.0))
                colsum = jnp.sum(p.reshape(c, HEAD_DIM), axis=-1, keepdims=True)
                hit = (col_blk == row_blk) & (col_in == s) & (row_in >= s)
                scores = jnp.where(hit, colsum, scores)
            st = st_ref[h]
            qe = (q * jnp.exp(cum)).astype(BF16)
            o = _dot_nt(qe, st.astype(BF16)) + _dot(scores.astype(BF16), v.astype(BF16))
            o_ref[pl.ds(r0, c), lanes] = o
            last = ends[nsub - 1]
            kl = (kp * jnp.exp(last - bsel)).astype(BF16)
            st_ref[h] = jnp.exp(last) * st + _dot_tn(v.astype(BF16), kl)
        return carry

    lax.fori_loop(0, n_chunks, chunk_body, 0)


def _hgrn(proj, lower):
    t = proj.shape[0]
    tb = min(t, 512)
    w = N_HEADS * HEAD_DIM

    def cols(cb):
        return pl.BlockSpec((tb, w), lambda i: (i, cb // N_HEADS))

    return pl.pallas_call(
        functools.partial(_hgrn_kernel, n_chunks=tb // HG_CHUNK),
        grid=(t // tb,),
        in_specs=[cols(CB_HQ), cols(CB_HF), cols(CB_HI),
                  pl.BlockSpec((1, w), lambda i: (0, 0))],
        out_specs=pl.BlockSpec((tb, w), lambda i: (i, 0)),
        out_shape=jax.ShapeDtypeStruct((t, w), F32),
        scratch_shapes=[pltpu.VMEM((N_HEADS, HEAD_DIM, HEAD_DIM), F32)],
        compiler_params=_cparams(("arbitrary",)),
        name="hgrn2",
    )(proj, proj, proj, lower)


def _ret_kernel(q_ref, qr_ref, k_ref, kr_ref, v_ref, cos_ref, sin_ref,
                dec_ref, qd_ref, kd_ref, cd_ref, o_ref, s_ref):
    @pl.when(pl.program_id(0) == 0)
    def _():
        s_ref[...] = jnp.zeros_like(s_ref)

    cos = cos_ref[...]
    sin = sin_ref[...]
    scale = HEAD_DIM ** -0.5
    for h in range(N_HEADS):
        lanes = slice(h * HEAD_DIM, (h + 1) * HEAD_DIM)
        q = q_ref[:, lanes] * cos + qr_ref[:, lanes] * sin
        k = (k_ref[:, lanes] * cos + kr_ref[:, lanes] * sin) * scale
        v = v_ref[:, lanes].astype(BF16)
        qb = q.astype(BF16)
        s = s_ref[h]
        scores = _dot_nt(qb, k.astype(BF16)) * dec_ref[h]
        o = _dot(qb, s.astype(BF16)) * qd_ref[h] + _dot(scores.astype(BF16), v)
        o_ref[:, lanes] = o
        s_ref[h] = cd_ref[h] * s + _dot_tn((k * kd_ref[h]).astype(BF16), v)


def _ret(proj, cos_r, sin_r):
    t = proj.shape[0]
    c = min(t, RET_CHUNK)
    w = N_HEADS * HEAD_DIM
    log_gamma = jnp.log(1.0 - 2.0 ** (-5.0 - jnp.arange(N_HEADS, dtype=F32)))
    idx = jnp.arange(c, dtype=F32)
    rel = idx[:, None] - idx[None, :]
    decay = jnp.exp(jnp.where(rel >= 0, log_gamma[:, None, None] * rel, NEG_BIG))
    ones = jnp.ones((1, 1, HEAD_DIM), F32)
    q_decay = jnp.exp(log_gamma[:, None] * (idx + 1.0))[:, :, None] * ones
    k_decay = jnp.exp(log_gamma[:, None] * (c - 1.0 - idx))[:, :, None] * ones
    c_decay = jnp.exp(log_gamma * c)[:, None, None] * ones

    def cols(cb):
        return pl.BlockSpec((c, w), lambda i: (i, cb // N_HEADS))

    def const(shape):
        return pl.BlockSpec(shape, lambda i: (0, 0, 0))

    tab = pl.BlockSpec((c, LANE), lambda i: (i, 0))
    return pl.pallas_call(
        _ret_kernel,
        grid=(t // c,),
        in_specs=[cols(CB_RQ), cols(CB_RQR), cols(CB_RK), cols(CB_RKR), cols(CB_RV), tab, tab,
                  const((N_HEADS, c, c)), const((N_HEADS, c, HEAD_DIM)),
                  const((N_HEADS, c, HEAD_DIM)), const((N_HEADS, 1, HEAD_DIM))],
        out_specs=pl.BlockSpec((c, w), lambda i: (i, 0)),
        out_shape=jax.ShapeDtypeStruct((t, w), F32),
        scratch_shapes=[pltpu.VMEM((N_HEADS, HEAD_DIM, HEAD_DIM), F32)],
        compiler_params=_cparams(("arbitrary",)),
        name="retention",
    )(proj, proj, proj, proj, proj, cos_r, sin_r, decay, q_decay, k_decay, c_decay)


def _swa_kernel(sink_ref, q_ref, qr_ref, kc_ref, krc_ref, vc_ref, kp_ref, krp_ref, vp_ref,
                cc_ref, sc_ref, cp_ref, sp_ref, o_ref):
    b = SWA_BLOCK
    blk = pl.program_id(0)
    cos_c, sin_c = cc_ref[...], sc_ref[...]
    cos_p, sin_p = cp_ref[...], sp_ref[...]
    qi = lax.broadcasted_iota(jnp.int32, (b, 2 * b), 0) + b
    ki = lax.broadcasted_iota(jnp.int32, (b, 2 * b), 1)
    rel = qi - ki
    keep = (rel >= 0) & (rel < b) & ((blk > 0) | (ki >= b))
    group = SWA_Q_HEADS // SWA_KV_HEADS
    for g in range(SWA_KV_HEADS):
        kl = slice(g * LANE, (g + 1) * LANE)
        k_cur = kc_ref[:, kl] * cos_c + krc_ref[:, kl] * sin_c
        k_prev = kp_ref[:, kl] * cos_p + krp_ref[:, kl] * sin_p
        kw = jnp.concatenate([k_prev, k_cur], axis=0).astype(BF16)
        vw = jnp.concatenate([vp_ref[:, kl], vc_ref[:, kl]], axis=0).astype(BF16)
        for j in range(group):
            h = g * group + j
            ql = slice(h * LANE, (h + 1) * LANE)
            q = (q_ref[:, ql] * cos_c + qr_ref[:, ql] * sin_c).astype(BF16)
            s = _dot_nt(q, kw) * (SWA_DIM ** -0.5)
            s = jnp.where(keep, s, NEG_BIG)
            sink = sink_ref[h]
            m = jnp.maximum(jnp.max(s, axis=-1, keepdims=True), sink)
            p = jnp.exp(s - m)
            denom = jnp.sum(p, axis=-1, keepdims=True) + jnp.exp(sink - m)
            o_ref[:, ql] = _dot((p / denom).astype(BF16), vw)


def _swa(proj, cos_s, sin_s, sinks):
    t = proj.shape[0]
    b = SWA_BLOCK
    qw = SWA_Q_HEADS * LANE
    kw = SWA_KV_HEADS * LANE

    def cur(cb, width):
        return pl.BlockSpec((b, width), lambda i: (i, cb * LANE // width))

    def prev(cb, width):
        return pl.BlockSpec((b, width), lambda i: (jnp.maximum(i - 1, 0), cb * LANE // width))

    tab_c = pl.BlockSpec((b, LANE), lambda i: (i, 0))
    tab_p = pl.BlockSpec((b, LANE), lambda i: (jnp.maximum(i - 1, 0), 0))
    return pl.pallas_call(
        _swa_kernel,
        grid=(t // b,),
        in_specs=[pl.BlockSpec(memory_space=pltpu.SMEM),
                  cur(CB_SQ, qw), cur(CB_SQR, qw),
                  cur(CB_SK, kw), cur(CB_SKR, kw), cur(CB_SV, kw),
                  prev(CB_SK, kw), prev(CB_SKR, kw), prev(CB_SV, kw),
                  tab_c, tab_c, tab_p, tab_p],
        out_specs=pl.BlockSpec((b, qw), lambda i: (i, 0)),
        out_shape=jax.ShapeDtypeStruct((t, qw), F32),
        compiler_params=_cparams(("parallel",)),
        name="swa",
    )(sinks, proj, proj, proj, proj, proj, proj, proj, proj, cos_s, sin_s, cos_s, sin_s)


def _merge_kernel(x_ref, oa_ref, hg_ref, ob_ref, rg_ref, oc_ref, ga_ref, gb_ref, gc_ref,
                  hn_ref, wa_ref, wb_ref, wc_ref, wo_ref, o_ref):
    hg = hg_ref[...]
    a = _rms(oa_ref[...]) * hn_ref[...] * (hg * _sigmoid(hg))
    ya = _dot(a.astype(BF16), wa_ref[...])
    rg = rg_ref[...]
    ob = ob_ref[...]
    parts = []
    for h in range(N_HEADS):
        lanes = slice(h * HEAD_DIM, (h + 1) * HEAD_DIM)
        parts.append(_rms(ob[:, lanes]))
    bn = jnp.concatenate(parts, axis=-1) * (rg * _sigmoid(rg))
    yb = _dot(bn.astype(BF16), wb_ref[...])
    yc = _dot(oc_ref[...].astype(BF16), wc_ref[...])
    mixed = _sigmoid(ga_ref[...]) * ya + _sigmoid(gb_ref[...]) * yb + _sigmoid(gc_ref[...]) * yc
    o_ref[...] = x_ref[...] + _dot(mixed.astype(BF16), wo_ref[...])


def _merge(x, proj, oa, ob, oc, hn, wa, wb, wc, wo):
    t = x.shape[0]
    tt = min(t, 512)
    w = N_HEADS * HEAD_DIM
    qw = SWA_Q_HEADS * LANE

    def rows(width, cb=0):
        return pl.BlockSpec((tt, width), lambda i: (i, cb * LANE // width))

    def full(shape):
        return pl.BlockSpec(shape, lambda i: (0, 0))

    return pl.pallas_call(
        _merge_kernel,
        grid=(t // tt,),
        in_specs=[rows(D_MODEL), rows(w), rows(w, CB_HG), rows(w), rows(w, CB_RG), rows(qw),
                  rows(D_MODEL, CB_GATE), rows(D_MODEL, CB_GATE + 8), rows(D_MODEL, CB_GATE + 16),
                  full((1, w)), full((w, D_MODEL)), full((w, D_MODEL)), full((qw, D_MODEL)),
                  full((D_MODEL, D_MODEL))],
        out_specs=rows(D_MODEL),
        out_shape=jax.ShapeDtypeStruct((t, D_MODEL), F32),
        compiler_params=_cparams(("parallel",)),
        name="merge",
    )(x, oa, proj, ob, proj, oc, proj, proj, proj, hn, wa, wb, wc, wo)


def _memkv_kernel(m_ref, g_ref, wk_ref, wv_ref, k_ref, v_ref):
    h = (_rms(m_ref[...]) * g_ref[...]).astype(BF16)
    k_ref[...] = _dot(h, wk_ref[...]).astype(BF16)
    v_ref[...] = _dot(h, wv_ref[...]).astype(BF16)


def _memkv(mem, g, wk, wv):
    nm = mem.shape[0]
    out = jax.ShapeDtypeStruct((nm, D_MODEL), BF16)
    return pl.pallas_call(
        _memkv_kernel,
        out_shape=[out, out],
        compiler_params=pltpu.CompilerParams(vmem_limit_bytes=VMEM_LIMIT),
        name="mem_kv",
    )(mem, g, wk, wv)


def _xattn_kernel(x_ref, g_ref, wq_ref, k_ref, v_ref, wo_ref, o_ref):
    x = x_ref[...]
    h = (_rms(x) * g_ref[...]).astype(BF16)
    q = _dot(h, wq_ref[...])
    outs = []
    for hh in range(X_HEADS):
        lanes = slice(hh * X_DIM, (hh + 1) * X_DIM)
        s = _dot_nt(q[:, lanes].astype(BF16), k_ref[:, lanes]) * (X_DIM ** -0.5)
        m = jnp.max(s, axis=-1, keepdims=True)
        p = jnp.exp(s - m)
        p = p / jnp.sum(p, axis=-1, keepdims=True)
        outs.append(_dot(p.astype(BF16), v_ref[:, lanes]))
    o = jnp.concatenate(outs, axis=-1)
    o_ref[...] = x + _dot(o.astype(BF16), wo_ref[...])


def _xattn(x, g, wq, k, v, wo):
    t = x.shape[0]
    tt = min(t, 512)
    nm = k.shape[0]

    def full(shape):
        return pl.BlockSpec(shape, lambda i: (0, 0))

    rows = pl.BlockSpec((tt, D_MODEL), lambda i: (i, 0))
    return pl.pallas_call(
        _xattn_kernel,
        grid=(t // tt,),
        in_specs=[rows, full((1, D_MODEL)), full((D_MODEL, D_MODEL)),
                  full((nm, D_MODEL)), full((nm, D_MODEL)), full((D_MODEL, D_MODEL))],
        out_specs=rows,
        out_shape=jax.ShapeDtypeStruct((t, D_MODEL), F32),
        compiler_params=_cparams(("parallel",)),
        name="xattn",
    )(x, g, wq, k, v, wo)


_CAND_ROWS = [PEER_TOPK // (k1 + 1) for k1 in range(8)]


def _top_values(s, n, with_rank=False):
    vals = []
    rank = jnp.full(s.shape, float(n), F32)
    for i in range(n):
        m = jnp.max(s, axis=0, keepdims=True)
        vals.append(m)
        hit = s >= m
        if with_rank:
            rank = jnp.where(hit, float(i), rank)
        s = jnp.where(hit, -jnp.inf, s)
    return (vals, rank) if with_rank else vals


def _bf16_pair_bits(x):
    u = lax.bitcast_convert_type(x.astype(BF16).astype(F32), jnp.uint32)
    return u | (u >> 16)


def _stack_rows(rows, n):
    tt = rows[0].shape[1]
    r = lax.broadcasted_iota(jnp.int32, (n, tt), 0)
    out = jnp.zeros((n, tt), F32)
    for k in range(n):
        out = jnp.where(r == k, rows[k], out)
    return out


def _router_kernel(x_ref, g_ref, wh_ref, wl_ref, kh_ref, kl_ref,
                   hb_ref, a_ref, b_ref, r_ref, l_ref, qt_scr):
    h = _rms(x_ref[...]) * g_ref[...]
    h_hi, h_lo = _split2(h)
    hb_ref[...] = h.T.astype(BF16)
    wh = wh_ref[...]
    qt_scr[...] = _dot_nt(wh, h_hi) + _dot_nt(wl_ref[...], h_hi) + _dot_nt(wh, h_lo)
    tt = x_ref.shape[0]
    k = PEER_TOPK
    r8 = lax.broadcasted_iota(jnp.int32, (8, tt), 0)

    def head_body(hh, carry):
        scores = []
        for p in range(2):
            r0 = pl.multiple_of(hh * (2 * PEER_HALF) + p * PEER_HALF, PEER_HALF)
            q_hi, q_lo = _split2(qt_scr[pl.ds(r0, PEER_HALF), :])
            kh = kh_ref[p * PEER_HEADS + hh]
            scores.append(_dot(kh, q_hi) + _dot(kl_ref[p * PEER_HEADS + hh], q_hi) + _dot(kh, q_lo))
        s1, s2 = scores
        v1, rank1 = _top_values(s1, k, with_rank=True)
        v2, rank2 = _top_values(s2, k, with_rank=True)
        sv1 = _stack_rows(v1, k)
        sv2 = _stack_rows(v2, k)
        groups = [v1[0] + sv2]
        for k1 in range(1, 8):
            groups.append(jnp.where(r8 < _CAND_ROWS[k1], v1[k1] + sv2[0:8], -jnp.inf))
        groups.append(sv1[8:16] + v2[0])
        cand = jnp.concatenate(groups, axis=0)
        tau = _top_values(cand, k)[k - 1]
        m0 = v1[0] + v2[0]
        z = jnp.sum(jnp.where(cand >= tau, jnp.exp(cand - m0), 0.0), axis=0, keepdims=True)
        a_ref[hh] = _bf16_pair_bits(jnp.exp(s1 - v1[0]))
        b_ref[hh] = (jnp.exp(s2 - v2[0]) / z).astype(BF16)
        count1 = jnp.zeros_like(s1)
        for k1 in range(k):
            n_k1 = jnp.sum((v1[k1] + sv2 >= tau).astype(F32), axis=0, keepdims=True)
            count1 = jnp.where(rank1 == float(k1), n_k1, count1)
        r_ref[hh] = rank2.astype(BF16)
        l_ref[hh] = _bf16_pair_bits(count1)
        return carry

    lax.fori_loop(0, PEER_HEADS, head_body, 0)


def _router(x, g, wq_hi, wq_lo, k_hi, k_lo):
    t = x.shape[0]
    tt = min(t, 256)
    nq = wq_hi.shape[0]
    sel = jax.ShapeDtypeStruct((PEER_HEADS, PEER_KEYS, t), jnp.uint32)
    sel_b = jax.ShapeDtypeStruct((PEER_HEADS, PEER_KEYS, t), BF16)
    sel_spec = pl.BlockSpec((PEER_HEADS, PEER_KEYS, tt), lambda i: (0, 0, i))
    return pl.pallas_call(
        _router_kernel,
        grid=(t // tt,),
        in_specs=[pl.BlockSpec((tt, D_MODEL), lambda i: (i, 0)),
                  pl.BlockSpec((1, D_MODEL), lambda i: (0, 0)),
                  pl.BlockSpec((nq, D_MODEL), lambda i: (0, 0)),
                  pl.BlockSpec((nq, D_MODEL), lambda i: (0, 0)),
                  pl.BlockSpec((2 * PEER_HEADS, PEER_KEYS, PEER_HALF), lambda i: (0, 0, 0)),
                  pl.BlockSpec((2 * PEER_HEADS, PEER_KEYS, PEER_HALF), lambda i: (0, 0, 0))],
        out_specs=[pl.BlockSpec((D_MODEL, tt), lambda i: (0, i)), sel_spec, sel_spec, sel_spec, sel_spec],
        out_shape=[jax.ShapeDtypeStruct((D_MODEL, t), BF16), sel, sel_b, sel_b, sel],
        scratch_shapes=[pltpu.VMEM((nq, tt), F32)],
        compiler_params=_cparams(("parallel",)),
        name="peer_router",
    )(x, g, wq_hi, wq_lo, k_hi, k_lo)


def _peer_kernel(x_ref, hb_ref, a_ref, l_ref, b_ref, r_ref, u_ref, vt_ref, gf_ref, o_ref,
                 acc_ref, ga_ref, *, final_norm):
    e = pl.program_id(1)

    @pl.when(e == 0)
    def _():
        acc_ref[...] = jnp.zeros_like(acc_ref)

    u_blk = pltpu.bitcast(u_ref[...], BF16)
    vt_blk = pltpu.bitcast(vt_ref[...], BF16)
    n_i1 = u_blk.shape[0] // PEER_KEYS
    sub = 16
    tc = PEER_TOKEN_CHUNK
    shape3 = (PEER_KEYS // sub, sub, tc)
    zero3 = jnp.zeros(shape3, BF16)
    n_chunks = hb_ref.shape[1] // tc
    for c in range(n_chunks):
        lanes = slice(c * tc, (c + 1) * tc)
        for j in range(n_i1):
            gate = zero3
            for hh in range(PEER_HEADS):
                arow = pltpu.bitcast(jnp.broadcast_to(a_ref[hh, j:j + 1, lanes], (8, tc)), BF16)
                lrow = pltpu.bitcast(jnp.broadcast_to(l_ref[hh, j:j + 1, lanes], (8, tc)), BF16)
                b3 = b_ref[hh, :, lanes].reshape(shape3)
                r3 = r_ref[hh, :, lanes].reshape(shape3)
                gate = gate + arow[None] * jnp.where(r3 < lrow[None], b3, zero3)
            ga_ref[j * PEER_KEYS:(j + 1) * PEER_KEYS, lanes] = gate.reshape(PEER_KEYS, tc)
    for c in range(n_chunks):
        lanes = slice(c * tc, (c + 1) * tc)
        pre = _dot(u_blk, hb_ref[:, lanes])
        act = pre + pre * lax.erf(pre)
        acc_ref[:, lanes] += _dot(vt_blk, ga_ref[:, lanes] * act.astype(BF16))

    @pl.when(e == pl.num_programs(1) - 1)
    def _():
        y = x_ref[...] + acc_ref[...].T
        if final_norm:
            y = _rms(y) * gf_ref[...]
        o_ref[...] = y


def _pack_row_pairs(w):
    n2, m = w.shape
    pairs = jnp.swapaxes(w.astype(BF16).reshape(n2 // 2, 2, m), 1, 2)
    return lax.bitcast_convert_type(pairs, jnp.uint32)


def _peer(x, hb, a, l, b, r, u, vt, gf, final_norm):
    t = x.shape[0]
    tt = min(t, 512)
    eb = 16 * PEER_KEYS
    rows = pl.BlockSpec((tt, D_MODEL), lambda i, e: (i, 0))
    i1_rows = pl.BlockSpec((PEER_HEADS, eb // PEER_KEYS, tt), lambda i, e: (0, e, i))
    all_rows = pl.BlockSpec((PEER_HEADS, PEER_KEYS, tt), lambda i, e: (0, 0, i))
    return pl.pallas_call(
        functools.partial(_peer_kernel, final_norm=final_norm),
        grid=(t // tt, N_EXPERTS // eb),
        in_specs=[rows, pl.BlockSpec((D_MODEL, tt), lambda i, e: (0, i)),
                  i1_rows, i1_rows, all_rows, all_rows,
                  pl.BlockSpec((eb // 2, D_MODEL), lambda i, e: (e, 0)),
                  pl.BlockSpec((D_MODEL // 2, eb), lambda i, e: (0, e)),
                  pl.BlockSpec((1, D_MODEL), lambda i, e: (0, 0))],
        out_specs=rows,
        out_shape=jax.ShapeDtypeStruct((t, D_MODEL), F32),
        scratch_shapes=[pltpu.VMEM((D_MODEL, tt), F32), pltpu.VMEM((eb, tt), BF16)],
        compiler_params=_cparams(("parallel", "arbitrary")),
        name="peer_dense",
    )(x, hb, a, l, b, r, u, vt, gf)


def kernel(x, mem, positions, ln_mix, w_in, lb_param, hgrn_norm, swa_sinks, w_br_hgrn, w_br_ret,
           w_br_swa, w_out, ln_xq, ln_xkv, w_xq, w_xk, w_xv, w_xo, ln_ffn, peer_wq, peer_keys,
           peer_u, peer_v, ln_final):
    b, t, d = x.shape
    assert b == 1 and d == D_MODEL
    depth = w_in.shape[0]
    xs = x.reshape(t, d)
    ms = mem.reshape(mem.shape[1], d)
    cos_r, sin_r, cos_s, sin_s = _rope_tables(positions)
    lb_sm = jax.nn.softmax(lb_param.astype(F32), axis=0)
    lower = jnp.cumsum(lb_sm, axis=0) - lb_sm[0]

    for l in range(depth):
        proj = _proj(xs, ln_mix[l][None, :], _relay_w_in(w_in[l]))
        oa = _hgrn(proj, lower[l][None, :])
        ob = _ret(proj, cos_r, sin_r)
        oc = _swa(proj, cos_s, sin_s, swa_sinks[l])
        wc = jnp.pad(w_br_swa[l].reshape(SWA_Q_HEADS, SWA_DIM, d),
                     ((0, 0), (0, LANE - SWA_DIM), (0, 0))).reshape(SWA_Q_HEADS * LANE, d)
        xs = _merge(xs, proj, oa, ob, oc, hgrn_norm[l][None, :], w_br_hgrn[l].astype(BF16),
                    w_br_ret[l].astype(BF16), wc.astype(BF16), w_out[l].astype(BF16))

        km, vm = _memkv(ms, ln_xkv[l][None, :], w_xk[l].astype(BF16), w_xv[l].astype(BF16))
        xs = _xattn(xs, ln_xq[l][None, :], w_xq[l].astype(BF16), km, vm, w_xo[l].astype(BF16))

        wq_hi, wq_lo = _split2(peer_wq[l].T)
        k_hi, k_lo = _split2(peer_keys[l].reshape(2 * PEER_HEADS, PEER_KEYS, PEER_HALF))
        hb, a, bz, r2, cnt = _router(xs, ln_ffn[l][None, :], wq_hi, wq_lo, k_hi, k_lo)
        c = 2.0 ** -0.5
        xs = _peer(xs, hb, a, cnt, bz, r2, _pack_row_pairs(peer_u[l] * c), _pack_row_pairs(peer_v[l].T * c),
                   ln_final[None, :], final_norm=(l == depth - 1))

    return xs.reshape(b, t, d)
```

```python
import functools
import math

import numpy as np
import jax
import jax.numpy as jnp
from jax import lax
from jax.experimental import pallas as pl
from jax.experimental.pallas import tpu as pltpu

F32 = jnp.float32
BF16 = jnp.bfloat16

D_MODEL = 1024
HEAD_DIM = 128
N_HEADS = 4
SWA_Q_HEADS = 8
SWA_KV_HEADS = 2
SWA_DIM = 64
SWA_BLOCK = 128
ROPE_THETA = 10000.0
N_MEM = 256
X_HEADS = 4
X_DIM = D_MODEL // X_HEADS
PEER_HEADS = 8
PEER_KEYS = 128
PEER_TOPK = 16
PEER_HALF = 128
N_EXPERTS = PEER_KEYS * PEER_KEYS
EPS = 1e-6
NEG_BIG = -1e30
TINY = 1e-30

LANE = 128
HG_CHUNK = 64
HG_SUB = 16
RET_CHUNK = 256
PEER_TOKEN_CHUNK = 512
VMEM_LIMIT = 56 * 1024 * 1024

CB_GATE = 0
CB_HQ, CB_HF, CB_HI, CB_HG = 24, 28, 32, 36
CB_RQ, CB_RQR, CB_RK, CB_RKR, CB_RV, CB_RG = 40, 44, 48, 52, 56, 60
CB_SQ, CB_SQR, CB_SK, CB_SKR, CB_SV = 64, 72, 80, 82, 84
N_CB = 88
PROJ_COLS = N_CB * LANE


def _cparams(sem):
    return pltpu.CompilerParams(dimension_semantics=sem, vmem_limit_bytes=VMEM_LIMIT)


def _dot(a, b):
    return jnp.dot(a, b, preferred_element_type=F32)


def _dot_nt(a, b):
    return lax.dot_general(a, b, (((1,), (1,)), ((), ())), preferred_element_type=F32)


def _dot_tn(a, b):
    return lax.dot_general(a, b, (((0,), (0,)), ((), ())), preferred_element_type=F32)


def _split2(x):
    hi = x.astype(BF16)
    lo = (x - hi.astype(F32)).astype(BF16)
    return hi, lo


def _split3(x):
    hi = x.astype(BF16)
    r = x - hi.astype(F32)
    mid = r.astype(BF16)
    lo = (r - mid.astype(F32)).astype(BF16)
    return hi, mid, lo


def _sigmoid(x):
    return 1.0 / (1.0 + jnp.exp(-x))


def _rms(x):
    return x * lax.rsqrt(jnp.mean(x * x, axis=-1, keepdims=True) + EPS)


def _rope_kernel(pos_ref, fr_ref, fs_ref, cr_ref, sr_ref, cs_ref, ss_ref):
    pos = pos_ref[...].astype(F32)
    ang_r = pos * fr_ref[...]
    cr_ref[...] = jnp.cos(ang_r)
    sr_ref[...] = jnp.sin(ang_r)
    ang_s = pos * fs_ref[...]
    cs_ref[...] = jnp.cos(ang_s)
    ss_ref[...] = jnp.sin(ang_s)


def _rope_tables(positions):
    t = positions.shape[1]
    tt = min(t, 1024)
    ret_f = ROPE_THETA ** (-jnp.linspace(0.0, 1.0, HEAD_DIM // 2, dtype=F32))
    swa_f = ROPE_THETA ** (-jnp.arange(0, SWA_DIM, 2, dtype=F32) / SWA_DIM)
    fr = jnp.concatenate([ret_f, ret_f])[None, :]
    fs = jnp.concatenate([swa_f, swa_f, jnp.zeros((LANE - SWA_DIM,), F32)])[None, :]
    pos = positions.reshape(t, 1)
    tab = jax.ShapeDtypeStruct((t, LANE), F32)
    row = pl.BlockSpec((tt, LANE), lambda i: (i, 0))
    vec = pl.BlockSpec((1, LANE), lambda i: (0, 0))
    return pl.pallas_call(
        _rope_kernel,
        grid=(t // tt,),
        in_specs=[pl.BlockSpec((tt, 1), lambda i: (i, 0)), vec, vec],
        out_specs=[row, row, row, row],
        out_shape=[tab, tab, tab, tab],
        compiler_params=_cparams(("parallel",)),
        name="rope_tables",
    )(pos, fr, fs)


def _proj_kernel(x_ref, g_ref, w_ref, o_ref, h_scr):
    @pl.when(pl.program_id(1) == 0)
    def _():
        h_scr[...] = (_rms(x_ref[...]) * g_ref[...]).astype(BF16)

    o_ref[...] = _dot(h_scr[...], w_ref[...])


def _proj(x, g, w):
    t = x.shape[0]
    n = w.shape[1]
    tm = min(t, 1024)
    tn = 1024
    return pl.pallas_call(
        _proj_kernel,
        grid=(t // tm, n // tn),
        in_specs=[pl.BlockSpec((tm, D_MODEL), lambda i, j: (i, 0)),
                  pl.BlockSpec((1, D_MODEL), lambda i, j: (0, 0)),
                  pl.BlockSpec((D_MODEL, tn), lambda i, j: (0, j))],
        out_specs=pl.BlockSpec((tm, tn), lambda i, j: (i, j)),
        out_shape=jax.ShapeDtypeStruct((t, n), F32),
        scratch_shapes=[pltpu.VMEM((tm, D_MODEL), BF16)],
        compiler_params=_cparams(("parallel", "arbitrary")),
        name="in_proj",
    )(x, g, w)


def _relay_w_in(w_in):
    d = w_in.shape[0]
    sizes = (512,) * 8 + (512, 128, 128, 3 * D_MODEL)
    offs = np.concatenate([[0], np.cumsum(sizes)])
    hq, hf, hi, hg, rq, rk, rv, rg, sq, sk, sv, gates = [
        w_in[:, int(offs[i]):int(offs[i + 1])] for i in range(12)]

    def rot(w, nh, dh):
        w = w.reshape(d, nh, dh)
        return jnp.concatenate([-w[..., dh // 2:], w[..., :dh // 2]], axis=-1).reshape(d, nh * dh)

    def pad_heads(w, nh):
        w = w.reshape(d, nh, SWA_DIM)
        return jnp.pad(w, ((0, 0), (0, 0), (0, LANE - SWA_DIM))).reshape(d, nh * LANE)

    cols = [gates, hq, hf, hi, hg,
            rq, rot(rq, N_HEADS, HEAD_DIM), rk, rot(rk, N_HEADS, HEAD_DIM), rv, rg,
            pad_heads(sq, SWA_Q_HEADS), pad_heads(rot(sq, SWA_Q_HEADS, SWA_DIM), SWA_Q_HEADS),
            pad_heads(sk, SWA_KV_HEADS), pad_heads(rot(sk, SWA_KV_HEADS, SWA_DIM), SWA_KV_HEADS),
            pad_heads(sv, SWA_KV_HEADS),
            jnp.zeros((d, PROJ_COLS - (CB_SV + SWA_KV_HEADS) * LANE), w_in.dtype)]
    return jnp.concatenate(cols, axis=1).astype(BF16)


def _hgrn_kernel(hq_ref, hf_ref, hi_ref, lb_ref, o_ref, st_ref, *, n_chunks):
    @pl.when(pl.program_id(0) == 0)
    def _():
        st_ref[...] = jnp.zeros_like(st_ref)

    c = HG_CHUNK
    nsub = c // HG_SUB
    row = lax.broadcasted_iota(jnp.int32, (c, c), 0)
    col = lax.broadcasted_iota(jnp.int32, (c, c), 1)
    tri = (col <= row).astype(BF16)
    row_blk = row // HG_SUB
    col_blk = col // HG_SUB
    row_in = row % HG_SUB
    col_in = col % HG_SUB
    rsub = lax.broadcasted_iota(jnp.int32, (c, HEAD_DIM), 0) // HG_SUB
    scale = HEAD_DIM ** -0.5

    def chunk_body(ci, carry):
        r0 = pl.multiple_of(ci * c, c)
        for h in range(N_HEADS):
            lanes = slice(h * HEAD_DIM, (h + 1) * HEAD_DIM)
            xq = hq_ref[pl.ds(r0, c), lanes]
            z = hf_ref[pl.ds(r0, c), lanes]
            v = hi_ref[pl.ds(r0, c), lanes]
            lb = lb_ref[:, lanes]
            omlb = 1.0 - lb
            q = xq * _sigmoid(xq) * scale
            f = lb + omlb * _sigmoid(z)
            logf = jnp.log(jnp.maximum(f, TINY))
            kk = omlb * _sigmoid(-z)
            l1, l2, l3 = _split3(logf)
            cum = _dot(tri, l1) + _dot(tri, l2) + _dot(tri, l3)
            ends = [cum[(j + 1) * HG_SUB - 1:(j + 1) * HG_SUB, :] for j in range(nsub)]
            bsel = ends[nsub - 1]
            for j in range(nsub - 2, -1, -1):
                bsel = jnp.where(rsub <= j, ends[j], bsel)
            kp = kk * jnp.exp(bsel - cum)
            kp_b = kp.astype(BF16)
            scores = jnp.zeros((c, c), F32)
            for j in range(nsub - 1):
                qj = (q * jnp.exp(jnp.minimum(cum - ends[j], 0.0))).astype(BF16)
                sj = _dot_nt(qj, kp_b)
                scores = jnp.where((col_blk == j) & (row_blk > j), sj, scores)
            q3 = q.reshape(nsub, HG_SUB, HEAD_DIM)
            k3 = kk.reshape(nsub, HG_SUB, HEAD_DIM)
            c3 = cum.reshape(nsub, HG_SUB, HEAD_DIM)
            for s in range(HG_SUB):
                krow = k3[:, s:s + 1, :]
                crow = c3[:, s:s + 1, :]
                p = q3 * krow * jnp.exp(jnp.minimum(c3 - crow, 0.0))
                colsum = jnp.sum(p.reshape(c, HEAD_DIM), axis=-1, keepdims=True)
                hit = (col_blk == row_blk) & (col_in == s) & (row_in >= s)
                scores = jnp.where(hit, colsum, scores)
            st = st_ref[h]
            qe = (q * jnp.exp(cum)).astype(BF16)
            o = _dot_nt(qe, st.astype(BF16)) + _dot(scores.astype(BF16), v.astype(BF16))
            o_ref[pl.ds(r0, c), lanes] = o
            last = ends[nsub - 1]
            kl = (kp * jnp.exp(last - bsel)).astype(BF16)
            st_ref[h] = jnp.exp(last) * st + _dot_tn(v.astype(BF16), kl)
        return carry

    lax.fori_loop(0, n_chunks, chunk_body, 0)


def _hgrn(proj, lower):
    t = proj.shape[0]
    tb = min(t, 512)
    w = N_HEADS * HEAD_DIM

    def cols(cb):
        return pl.BlockSpec((tb, w), lambda i: (i, cb // N_HEADS))

    return pl.pallas_call(
        functools.partial(_hgrn_kernel, n_chunks=tb // HG_CHUNK),
        grid=(t // tb,),
        in_specs=[cols(CB_HQ), cols(CB_HF), cols(CB_HI),
                  pl.BlockSpec((1, w), lambda i: (0, 0))],
        out_specs=pl.BlockSpec((tb, w), lambda i: (i, 0)),
        out_shape=jax.ShapeDtypeStruct((t, w), F32),
        scratch_shapes=[pltpu.VMEM((N_HEADS, HEAD_DIM, HEAD_DIM), F32)],
        compiler_params=_cparams(("arbitrary",)),
        name="hgrn2",
    )(proj, proj, proj, lower)


def _ret_kernel(q_ref, qr_ref, k_ref, kr_ref, v_ref, cos_ref, sin_ref,
                dec_ref, qd_ref, kd_ref, cd_ref, o_ref, s_ref):
    @pl.when(pl.program_id(0) == 0)
    def _():
        s_ref[...] = jnp.zeros_like(s_ref)

    cos = cos_ref[...]
    sin = sin_ref[...]
    scale = HEAD_DIM ** -0.5
    for h in range(N_HEADS):
        lanes = slice(h * HEAD_DIM, (h + 1) * HEAD_DIM)
        q = q_ref[:, lanes] * cos + qr_ref[:, lanes] * sin
        k = (k_ref[:, lanes] * cos + kr_ref[:, lanes] * sin) * scale
        v = v_ref[:, lanes].astype(BF16)
        qb = q.astype(BF16)
        s = s_ref[h]
        scores = _dot_nt(qb, k.astype(BF16)) * dec_ref[h]
        o = _dot(qb, s.astype(BF16)) * qd_ref[h] + _dot(scores.astype(BF16), v)
        o_ref[:, lanes] = o
        s_ref[h] = cd_ref[h] * s + _dot_tn((k * kd_ref[h]).astype(BF16), v)


def _ret(proj, cos_r, sin_r):
    t = proj.shape[0]
    c = min(t, RET_CHUNK)
    w = N_HEADS * HEAD_DIM
    log_gamma = jnp.log(1.0 - 2.0 ** (-5.0 - jnp.arange(N_HEADS, dtype=F32)))
    idx = jnp.arange(c, dtype=F32)
    rel = idx[:, None] - idx[None, :]
    decay = jnp.exp(jnp.where(rel >= 0, log_gamma[:, None, None] * rel, NEG_BIG))
    ones = jnp.ones((1, 1, HEAD_DIM), F32)
    q_decay = jnp.exp(log_gamma[:, None] * (idx + 1.0))[:, :, None] * ones
    k_decay = jnp.exp(log_gamma[:, None] * (c - 1.0 - idx))[:, :, None] * ones
    c_decay = jnp.exp(log_gamma * c)[:, None, None] * ones

    def cols(cb):
        return pl.BlockSpec((c, w), lambda i: (i, cb // N_HEADS))

    def const(shape):
        return pl.BlockSpec(shape, lambda i: (0, 0, 0))

    tab = pl.BlockSpec((c, LANE), lambda i: (i, 0))
    return pl.pallas_call(
        _ret_kernel,
        grid=(t // c,),
        in_specs=[cols(CB_RQ), cols(CB_RQR), cols(CB_RK), cols(CB_RKR), cols(CB_RV), tab, tab,
                  const((N_HEADS, c, c)), const((N_HEADS, c, HEAD_DIM)),
                  const((N_HEADS, c, HEAD_DIM)), const((N_HEADS, 1, HEAD_DIM))],
        out_specs=pl.BlockSpec((c, w), lambda i: (i, 0)),
        out_shape=jax.ShapeDtypeStruct((t, w), F32),
        scratch_shapes=[pltpu.VMEM((N_HEADS, HEAD_DIM, HEAD_DIM), F32)],
        compiler_params=_cparams(("arbitrary",)),
        name="retention",
    )(proj, proj, proj, proj, proj, cos_r, sin_r, decay, q_decay, k_decay, c_decay)


def _swa_kernel(sink_ref, q_ref, qr_ref, kc_ref, krc_ref, vc_ref, kp_ref, krp_ref, vp_ref,
                cc_ref, sc_ref, cp_ref, sp_ref, o_ref):
    b = SWA_BLOCK
    blk = pl.program_id(0)
    cos_c, sin_c = cc_ref[...], sc_ref[...]
    cos_p, sin_p = cp_ref[...], sp_ref[...]
    qi = lax.broadcasted_iota(jnp.int32, (b, 2 * b), 0) + b
    ki = lax.broadcasted_iota(jnp.int32, (b, 2 * b), 1)
    rel = qi - ki
    keep = (rel >= 0) & (rel < b) & ((blk > 0) | (ki >= b))
    group = SWA_Q_HEADS // SWA_KV_HEADS
    for g in range(SWA_KV_HEADS):
        kl = slice(g * LANE, (g + 1) * LANE)
        k_cur = kc_ref[:, kl] * cos_c + krc_ref[:, kl] * sin_c
        k_prev = kp_ref[:, kl] * cos_p + krp_ref[:, kl] * sin_p
        kw = jnp.concatenate([k_prev, k_cur], axis=0).astype(BF16)
        vw = jnp.concatenate([vp_ref[:, kl], vc_ref[:, kl]], axis=0).astype(BF16)
        for j in range(group):
            h = g * group + j
            ql = slice(h * LANE, (h + 1) * LANE)
            q = (q_ref[:, ql] * cos_c + qr_ref[:, ql] * sin_c).astype(BF16)
            s = _dot_nt(q, kw) * (SWA_DIM ** -0.5)
            s = jnp.where(keep, s, NEG_BIG)
            sink = sink_ref[h]
            m = jnp.maximum(jnp.max(s, axis=-1, keepdims=True), sink)
            p = jnp.exp(s - m)
            denom = jnp.sum(p, axis=-1, keepdims=True) + jnp.exp(sink - m)
            o_ref[:, ql] = _dot((p / denom).astype(BF16), vw)


def _swa(proj, cos_s, sin_s, sinks):
    t = proj.shape[0]
    b = SWA_BLOCK
    qw = SWA_Q_HEADS * LANE
    kw = SWA_KV_HEADS * LANE

    def cur(cb, width):
        return pl.BlockSpec((b, width), lambda i: (i, cb * LANE // width))

    def prev(cb, width):
        return pl.BlockSpec((b, width), lambda i: (jnp.maximum(i - 1, 0), cb * LANE // width))

    tab_c = pl.BlockSpec((b, LANE), lambda i: (i, 0))
    tab_p = pl.BlockSpec((b, LANE), lambda i: (jnp.maximum(i - 1, 0), 0))
    return pl.pallas_call(
        _swa_kernel,
        grid=(t // b,),
        in_specs=[pl.BlockSpec(memory_space=pltpu.SMEM),
                  cur(CB_SQ, qw), cur(CB_SQR, qw),
                  cur(CB_SK, kw), cur(CB_SKR, kw), cur(CB_SV, kw),
                  prev(CB_SK, kw), prev(CB_SKR, kw), prev(CB_SV, kw),
                  tab_c, tab_c, tab_p, tab_p],
        out_specs=pl.BlockSpec((b, qw), lambda i: (i, 0)),
        out_shape=jax.ShapeDtypeStruct((t, qw), F32),
        compiler_params=_cparams(("parallel",)),
        name="swa",
    )(sinks, proj, proj, proj, proj, proj, proj, proj, proj, cos_s, sin_s, cos_s, sin_s)


def _merge_kernel(x_ref, oa_ref, hg_ref, ob_ref, rg_ref, oc_ref, ga_ref, gb_ref, gc_ref,
                  hn_ref, wa_ref, wb_ref, wc_ref, wo_ref, o_ref):
    hg = hg_ref[...]
    a = _rms(oa_ref[...]) * hn_ref[...] * (hg * _sigmoid(hg))
    ya = _dot(a.astype(BF16), wa_ref[...])
    rg = rg_ref[...]
    ob = ob_ref[...]
    parts = []
    for h in range(N_HEADS):
        lanes = slice(h * HEAD_DIM, (h + 1) * HEAD_DIM)
        parts.append(_rms(ob[:, lanes]))
    bn = jnp.concatenate(parts, axis=-1) * (rg * _sigmoid(rg))
    yb = _dot(bn.astype(BF16), wb_ref[...])
    yc = _dot(oc_ref[...].astype(BF16), wc_ref[...])
    mixed = _sigmoid(ga_ref[...]) * ya + _sigmoid(gb_ref[...]) * yb + _sigmoid(gc_ref[...]) * yc
    o_ref[...] = x_ref[...] + _dot(mixed.astype(BF16), wo_ref[...])


def _merge(x, proj, oa, ob, oc, hn, wa, wb, wc, wo):
    t = x.shape[0]
    tt = min(t, 512)
    w = N_HEADS * HEAD_DIM
    qw = SWA_Q_HEADS * LANE

    def rows(width, cb=0):
        return pl.BlockSpec((tt, width), lambda i: (i, cb * LANE // width))

    def full(shape):
        return pl.BlockSpec(shape, lambda i: (0, 0))

    return pl.pallas_call(
        _merge_kernel,
        grid=(t // tt,),
        in_specs=[rows(D_MODEL), rows(w), rows(w, CB_HG), rows(w), rows(w, CB_RG), rows(qw),
                  rows(D_MODEL, CB_GATE), rows(D_MODEL, CB_GATE + 8), rows(D_MODEL, CB_GATE + 16),
                  full((1, w)), full((w, D_MODEL)), full((w, D_MODEL)), full((qw, D_MODEL)),
                  full((D_MODEL, D_MODEL))],
        out_specs=rows(D_MODEL),
        out_shape=jax.ShapeDtypeStruct((t, D_MODEL), F32),
        compiler_params=_cparams(("parallel",)),
        name="merge",
    )(x, oa, proj, ob, proj, oc, proj, proj, proj, hn, wa, wb, wc, wo)


def _memkv_kernel(m_ref, g_ref, wk_ref, wv_ref, k_ref, v_ref):
    h = (_rms(m_ref[...]) * g_ref[...]).astype(BF16)
    k_ref[...] = _dot(h, wk_ref[...]).astype(BF16)
    v_ref[...] = _dot(h, wv_ref[...]).astype(BF16)


def _memkv(mem, g, wk, wv):
    nm = mem.shape[0]
    out = jax.ShapeDtypeStruct((nm, D_MODEL), BF16)
    return pl.pallas_call(
        _memkv_kernel,
        out_shape=[out, out],
        compiler_params=pltpu.CompilerParams(vmem_limit_bytes=VMEM_LIMIT),
        name="mem_kv",
    )(mem, g, wk, wv)


def _xattn_kernel(x_ref, g_ref, wq_ref, k_ref, v_ref, wo_ref, o_ref):
    x = x_ref[...]
    h = (_rms(x) * g_ref[...]).astype(BF16)
    q = _dot(h, wq_ref[...])
    outs = []
    for hh in range(X_HEADS):
        lanes = slice(hh * X_DIM, (hh + 1) * X_DIM)
        s = _dot_nt(q[:, lanes].astype(BF16), k_ref[:, lanes]) * (X_DIM ** -0.5)
        m = jnp.max(s, axis=-1, keepdims=True)
        p = jnp.exp(s - m)
        p = p / jnp.sum(p, axis=-1, keepdims=True)
        outs.append(_dot(p.astype(BF16), v_ref[:, lanes]))
    o = jnp.concatenate(outs, axis=-1)
    o_ref[...] = x + _dot(o.astype(BF16), wo_ref[...])


def _xattn(x, g, wq, k, v, wo):
    t = x.shape[0]
    tt = min(t, 512)
    nm = k.shape[0]

    def full(shape):
        return pl.BlockSpec(shape, lambda i: (0, 0))

    rows = pl.BlockSpec((tt, D_MODEL), lambda i: (i, 0))
    return pl.pallas_call(
        _xattn_kernel,
        grid=(t // tt,),
        in_specs=[rows, full((1, D_MODEL)), full((D_MODEL, D_MODEL)),
                  full((nm, D_MODEL)), full((nm, D_MODEL)), full((D_MODEL, D_MODEL))],
        out_specs=rows,
        out_shape=jax.ShapeDtypeStruct((t, D_MODEL), F32),
        compiler_params=_cparams(("parallel",)),
        name="xattn",
    )(x, g, wq, k, v, wo)


_CAND_ROWS = [PEER_TOPK // (k1 + 1) for k1 in range(8)]


def _top_values(s, n, with_rank=False):
    vals = []
    rank = jnp.full(s.shape, float(n), F32)
    for i in range(n):
        m = jnp.max(s, axis=0, keepdims=True)
        vals.append(m)
        hit = s >= m
        if with_rank:
            rank = jnp.where(hit, float(i), rank)
        s = jnp.where(hit, -jnp.inf, s)
    return (vals, rank) if with_rank else vals


def _bf16_pair_bits(x):
    u = lax.bitcast_convert_type(x.astype(BF16).astype(F32), jnp.uint32)
    return u | (u >> 16)


def _stack_rows(rows, n):
    tt = rows[0].shape[1]
    r = lax.broadcasted_iota(jnp.int32, (n, tt), 0)
    out = jnp.zeros((n, tt), F32)
    for k in range(n):
        out = jnp.where(r == k, rows[k], out)
    return out


def _router_kernel(x_ref, g_ref, wh_ref, wl_ref, kh_ref, kl_ref,
                   hb_ref, a_ref, b_ref, r_ref, l_ref, qt_scr):
    h = _rms(x_ref[...]) * g_ref[...]
    h_hi, h_lo = _split2(h)
    hb_ref[...] = h.T.astype(BF16)
    wh = wh_ref[...]
    qt_scr[...] = _dot_nt(wh, h_hi) + _dot_nt(wl_ref[...], h_hi) + _dot_nt(wh, h_lo)
    tt = x_ref.shape[0]
    k = PEER_TOPK
    r8 = lax.broadcasted_iota(jnp.int32, (8, tt), 0)

    def head_body(hh, carry):
        scores = []
        for p in range(2):
            r0 = pl.multiple_of(hh * (2 * PEER_HALF) + p * PEER_HALF, PEER_HALF)
            q_hi, q_lo = _split2(qt_scr[pl.ds(r0, PEER_HALF), :])
            kh = kh_ref[p * PEER_HEADS + hh]
            scores.append(_dot(kh, q_hi) + _dot(kl_ref[p * PEER_HEADS + hh], q_hi) + _dot(kh, q_lo))
        s1, s2 = scores
        v1, rank1 = _top_values(s1, k, with_rank=True)
        v2, rank2 = _top_values(s2, k, with_rank=True)
        sv1 = _stack_rows(v1, k)
        sv2 = _stack_rows(v2, k)
        groups = [v1[0] + sv2]
        for k1 in range(1, 8):
            groups.append(jnp.where(r8 < _CAND_ROWS[k1], v1[k1] + sv2[0:8], -jnp.inf))
        groups.append(sv1[8:16] + v2[0])
        cand = jnp.concatenate(groups, axis=0)
        tau = _top_values(cand, k)[k - 1]
        m0 = v1[0] + v2[0]
        z = jnp.sum(jnp.where(cand >= tau, jnp.exp(cand - m0), 0.0), axis=0, keepdims=True)
        a_ref[hh] = _bf16_pair_bits(jnp.exp(s1 - v1[0]))
        b_ref[hh] = (jnp.exp(s2 - v2[0]) / z).astype(BF16)
        count1 = jnp.zeros_like(s1)
        for k1 in range(k):
            n_k1 = jnp.sum((v1[k1] + sv2 >= tau).astype(F32), axis=0, keepdims=True)
            count1 = jnp.where(rank1 == float(k1), n_k1, count1)
        r_ref[hh] = rank2.astype(BF16)
        l_ref[hh] = _bf16_pair_bits(count1)
        return carry

    lax.fori_loop(0, PEER_HEADS, head_body, 0)


def _router(x, g, wq_hi, wq_lo, k_hi, k_lo):
    t = x.shape[0]
    tt = min(t, 256)
    nq = wq_hi.shape[0]
    sel = jax.ShapeDtypeStruct((PEER_HEADS, PEER_KEYS, t), jnp.uint32)
    sel_b = jax.ShapeDtypeStruct((PEER_HEADS, PEER_KEYS, t), BF16)
    sel_spec = pl.BlockSpec((PEER_HEADS, PEER_KEYS, tt), lambda i: (0, 0, i))
    return pl.pallas_call(
        _router_kernel,
        grid=(t // tt,),
        in_specs=[pl.BlockSpec((tt, D_MODEL), lambda i: (i, 0)),
                  pl.BlockSpec((1, D_MODEL), lambda i: (0, 0)),
                  pl.BlockSpec((nq, D_MODEL), lambda i: (0, 0)),
                  pl.BlockSpec((nq, D_MODEL), lambda i: (0, 0)),
                  pl.BlockSpec((2 * PEER_HEADS, PEER_KEYS, PEER_HALF), lambda i: (0, 0, 0)),
                  pl.BlockSpec((2 * PEER_HEADS, PEER_KEYS, PEER_HALF), lambda i: (0, 0, 0))],
        out_specs=[pl.BlockSpec((D_MODEL, tt), lambda i: (0, i)), sel_spec, sel_spec, sel_spec, sel_spec],
        out_shape=[jax.ShapeDtypeStruct((D_MODEL, t), BF16), sel, sel_b, sel_b, sel],
        scratch_shapes=[pltpu.VMEM((nq, tt), F32)],
        compiler_params=_cparams(("parallel",)),
        name="peer_router",
    )(x, g, wq_hi, wq_lo, k_hi, k_lo)


def _peer_kernel(x_ref, hb_ref, a_ref, l_ref, b_ref, r_ref, u_ref, vt_ref, gf_ref, o_ref,
                 acc_ref, ga_ref, *, final_norm):
    e = pl.program_id(1)

    @pl.when(e == 0)
    def _():
        acc_ref[...] = jnp.zeros_like(acc_ref)

    u_blk = pltpu.bitcast(u_ref[...], BF16)
    vt_blk = pltpu.bitcast(vt_ref[...], BF16)
    n_i1 = u_blk.shape[0] // PEER_KEYS
    sub = 16
    tc = PEER_TOKEN_CHUNK
    shape3 = (PEER_KEYS // sub, sub, tc)
    zero3 = jnp.zeros(shape3, BF16)
    per_piece = 4
    piece = per_piece * PEER_KEYS
    total = None
    for p in range(n_i1 // per_piece):
        for j in range(p * per_piece, (p + 1) * per_piece):
            gate = zero3
            for hh in range(PEER_HEADS):
                arow = pltpu.bitcast(jnp.broadcast_to(a_ref[hh, j:j + 1, :], (8, tc)), BF16)
                lrow = pltpu.bitcast(jnp.broadcast_to(l_ref[hh, j:j + 1, :], (8, tc)), BF16)
                b3 = b_ref[hh].reshape(shape3)
                r3 = r_ref[hh].reshape(shape3)
                gate = gate + arow[None] * jnp.where(r3 < lrow[None], b3, zero3)
            ga_ref[j * PEER_KEYS:(j + 1) * PEER_KEYS, :] = gate.reshape(PEER_KEYS, tc)
        rows = slice(p * piece, (p + 1) * piece)
        pre = _dot(u_blk[rows, :], hb_ref[...])
        act = pre + pre * lax.erf(pre)
        part = _dot(vt_blk[:, rows], ga_ref[rows, :] * act.astype(BF16))
        total = part if total is None else total + part
    acc_ref[...] += total

    @pl.when(e == pl.num_programs(1) - 1)
    def _():
        y = x_ref[...] + acc_ref[...].T
        if final_norm:
            y = _rms(y) * gf_ref[...]
        o_ref[...] = y


def _pack_kernel(w_ref, o_ref, *, scale, transpose):
    w = w_ref[...]
    if transpose:
        w = w.T
    o_ref[...] = pltpu.bitcast((w * scale).astype(BF16), jnp.uint32)


def _pack_table(w, scale, transpose):
    rows, cols = w.shape
    if transpose:
        blk = 512
        in_spec = pl.BlockSpec((blk, cols), lambda i: (i, 0))
        out_spec = pl.BlockSpec((cols // 2, blk), lambda i: (0, i))
        out_shape = jax.ShapeDtypeStruct((cols // 2, rows), jnp.uint32)
    else:
        blk = 1024
        in_spec = pl.BlockSpec((blk, cols), lambda i: (i, 0))
        out_spec = pl.BlockSpec((blk // 2, cols), lambda i: (i, 0))
        out_shape = jax.ShapeDtypeStruct((rows // 2, cols), jnp.uint32)
    return pl.pallas_call(
        functools.partial(_pack_kernel, scale=scale, transpose=transpose),
        grid=(rows // blk,),
        in_specs=[in_spec],
        out_specs=out_spec,
        out_shape=out_shape,
        compiler_params=_cparams(("parallel",)),
        name="pack_table_t" if transpose else "pack_table",
    )(w)


def _peer(x, hb, a, l, b, r, u, vt, gf, final_norm):
    t = x.shape[0]
    tt = min(t, 512)
    eb = 16 * PEER_KEYS
    rows = pl.BlockSpec((tt, D_MODEL), lambda i, e: (i, 0))
    i1_rows = pl.BlockSpec((PEER_HEADS, eb // PEER_KEYS, tt), lambda i, e: (0, e, i))
    all_rows = pl.BlockSpec((PEER_HEADS, PEER_KEYS, tt), lambda i, e: (0, 0, i))
    return pl.pallas_call(
        functools.partial(_peer_kernel, final_norm=final_norm),
        grid=(t // tt, N_EXPERTS // eb),
        in_specs=[rows, pl.BlockSpec((D_MODEL, tt), lambda i, e: (0, i)),
                  i1_rows, i1_rows, all_rows, all_rows,
                  pl.BlockSpec((eb // 2, D_MODEL), lambda i, e: (e, 0)),
                  pl.BlockSpec((D_MODEL // 2, eb), lambda i, e: (0, e)),
                  pl.BlockSpec((1, D_MODEL), lambda i, e: (0, 0))],
        out_specs=rows,
        out_shape=jax.ShapeDtypeStruct((t, D_MODEL), F32),
        scratch_shapes=[pltpu.VMEM((D_MODEL, tt), F32), pltpu.VMEM((eb, tt), BF16)],
        compiler_params=_cparams(("parallel", "arbitrary")),
        name="peer_dense",
    )(x, hb, a, l, b, r, u, vt, gf)


def kernel(x, mem, positions, ln_mix, w_in, lb_param, hgrn_norm, swa_sinks, w_br_hgrn, w_br_ret,
           w_br_swa, w_out, ln_xq, ln_xkv, w_xq, w_xk, w_xv, w_xo, ln_ffn, peer_wq, peer_keys,
           peer_u, peer_v, ln_final):
    b, t, d = x.shape
    assert b == 1 and d == D_MODEL
    depth = w_in.shape[0]
    xs = x.reshape(t, d)
    ms = mem.reshape(mem.shape[1], d)
    cos_r, sin_r, cos_s, sin_s = _rope_tables(positions)
    lb_sm = jax.nn.softmax(lb_param.astype(F32), axis=0)
    lower = jnp.cumsum(lb_sm, axis=0) - lb_sm[0]

    for l in range(depth):
        proj = _proj(xs, ln_mix[l][None, :], _relay_w_in(w_in[l]))
        oa = _hgrn(proj, lower[l][None, :])
        ob = _ret(proj, cos_r, sin_r)
        oc = _swa(proj, cos_s, sin_s, swa_sinks[l])
        wc = jnp.pad(w_br_swa[l].reshape(SWA_Q_HEADS, SWA_DIM, d),
                     ((0, 0), (0, LANE - SWA_DIM), (0, 0))).reshape(SWA_Q_HEADS * LANE, d)
        xs = _merge(xs, proj, oa, ob, oc, hgrn_norm[l][None, :], w_br_hgrn[l].astype(BF16),
                    w_br_ret[l].astype(BF16), wc.astype(BF16), w_out[l].astype(BF16))

        km, vm = _memkv(ms, ln_xkv[l][None, :], w_xk[l].astype(BF16), w_xv[l].astype(BF16))
        xs = _xattn(xs, ln_xq[l][None, :], w_xq[l].astype(BF16), km, vm, w_xo[l].astype(BF16))

        wq_hi, wq_lo = _split2(peer_wq[l].T)
        k_hi, k_lo = _split2(peer_keys[l].reshape(2 * PEER_HEADS, PEER_KEYS, PEER_HALF))
        hb, a, bz, r2, cnt = _router(xs, ln_ffn[l][None, :], wq_hi, wq_lo, k_hi, k_lo)
        c = 2.0 ** -0.5
        xs = _peer(xs, hb, a, cnt, bz, r2, _pack_table(peer_u[l], c, False), _pack_table(peer_v[l], c, True),
                   ln_final[None, :], final_norm=(l == depth - 1))

    return xs.reshape(b, t, d)
```

```python
import functools
import math

import numpy as np
import jax
import jax.numpy as jnp
from jax import lax
from jax.experimental import pallas as pl
from jax.experimental.pallas import tpu as pltpu

F32 = jnp.float32
BF16 = jnp.bfloat16

D_MODEL = 1024
HEAD_DIM = 128
N_HEADS = 4
SWA_Q_HEADS = 8
SWA_KV_HEADS = 2
SWA_DIM = 64
SWA_BLOCK = 128
ROPE_THETA = 10000.0
N_MEM = 256
X_HEADS = 4
X_DIM = D_MODEL // X_HEADS
PEER_HEADS = 8
PEER_KEYS = 128
PEER_TOPK = 16
PEER_HALF = 128
N_EXPERTS = PEER_KEYS * PEER_KEYS
EPS = 1e-6
NEG_BIG = -1e30
TINY = 1e-30

LANE = 128
HG_CHUNK = 64
HG_SUB = 16
RET_CHUNK = 256
PEER_TOKEN_CHUNK = 512
VMEM_LIMIT = 56 * 1024 * 1024

CB_GATE = 0
CB_HQ, CB_HF, CB_HI, CB_HG = 24, 28, 32, 36
CB_RQ, CB_RQR, CB_RK, CB_RKR, CB_RV, CB_RG = 40, 44, 48, 52, 56, 60
CB_SQ, CB_SQR, CB_SK, CB_SKR, CB_SV = 64, 72, 80, 82, 84
N_CB = 88
PROJ_COLS = N_CB * LANE


def _cparams(sem):
    return pltpu.CompilerParams(dimension_semantics=sem, vmem_limit_bytes=VMEM_LIMIT)


def _dot(a, b):
    return jnp.dot(a, b, preferred_element_type=F32)


def _dot_nt(a, b):
    return lax.dot_general(a, b, (((1,), (1,)), ((), ())), preferred_element_type=F32)


def _dot_tn(a, b):
    return lax.dot_general(a, b, (((0,), (0,)), ((), ())), preferred_element_type=F32)


def _split2(x):
    hi = x.astype(BF16)
    lo = (x - hi.astype(F32)).astype(BF16)
    return hi, lo


def _split3(x):
    hi = x.astype(BF16)
    r = x - hi.astype(F32)
    mid = r.astype(BF16)
    lo = (r - mid.astype(F32)).astype(BF16)
    return hi, mid, lo


def _sigmoid(x):
    return 1.0 / (1.0 + jnp.exp(-x))


def _rms(x):
    return x * lax.rsqrt(jnp.mean(x * x, axis=-1, keepdims=True) + EPS)


def _rope_kernel(pos_ref, fr_ref, fs_ref, cr_ref, sr_ref, cs_ref, ss_ref):
    pos = pos_ref[...].astype(F32)
    ang_r = pos * fr_ref[...]
    cr_ref[...] = jnp.cos(ang_r)
    sr_ref[...] = jnp.sin(ang_r)
    ang_s = pos * fs_ref[...]
    cs_ref[...] = jnp.cos(ang_s)
    ss_ref[...] = jnp.sin(ang_s)


def _rope_tables(positions):
    t = positions.shape[1]
    tt = min(t, 1024)
    ret_f = ROPE_THETA ** (-jnp.linspace(0.0, 1.0, HEAD_DIM // 2, dtype=F32))
    swa_f = ROPE_THETA ** (-jnp.arange(0, SWA_DIM, 2, dtype=F32) / SWA_DIM)
    fr = jnp.concatenate([ret_f, ret_f])[None, :]
    fs = jnp.concatenate([swa_f, swa_f, jnp.zeros((LANE - SWA_DIM,), F32)])[None, :]
    pos = positions.reshape(t, 1)
    tab = jax.ShapeDtypeStruct((t, LANE), F32)
    row = pl.BlockSpec((tt, LANE), lambda i: (i, 0))
    vec = pl.BlockSpec((1, LANE), lambda i: (0, 0))
    return pl.pallas_call(
        _rope_kernel,
        grid=(t // tt,),
        in_specs=[pl.BlockSpec((tt, 1), lambda i: (i, 0)), vec, vec],
        out_specs=[row, row, row, row],
        out_shape=[tab, tab, tab, tab],
        compiler_params=_cparams(("parallel",)),
        name="rope_tables",
    )(pos, fr, fs)


def _proj_kernel(x_ref, g_ref, w_ref, o_ref, h_scr):
    @pl.when(pl.program_id(1) == 0)
    def _():
        h_scr[...] = (_rms(x_ref[...]) * g_ref[...]).astype(BF16)

    o_ref[...] = _dot(h_scr[...], w_ref[...])


def _proj(x, g, w):
    t = x.shape[0]
    n = w.shape[1]
    tm = min(t, 1024)
    tn = 1024
    return pl.pallas_call(
        _proj_kernel,
        grid=(t // tm, n // tn),
        in_specs=[pl.BlockSpec((tm, D_MODEL), lambda i, j: (i, 0)),
                  pl.BlockSpec((1, D_MODEL), lambda i, j: (0, 0)),
                  pl.BlockSpec((D_MODEL, tn), lambda i, j: (0, j))],
        out_specs=pl.BlockSpec((tm, tn), lambda i, j: (i, j)),
        out_shape=jax.ShapeDtypeStruct((t, n), F32),
        scratch_shapes=[pltpu.VMEM((tm, D_MODEL), BF16)],
        compiler_params=_cparams(("parallel", "arbitrary")),
        name="in_proj",
    )(x, g, w)


def _relay_w_in(w_in):
    d = w_in.shape[0]
    sizes = (512,) * 8 + (512, 128, 128, 3 * D_MODEL)
    offs = np.concatenate([[0], np.cumsum(sizes)])
    hq, hf, hi, hg, rq, rk, rv, rg, sq, sk, sv, gates = [
        w_in[:, int(offs[i]):int(offs[i + 1])] for i in range(12)]

    def rot(w, nh, dh):
        w = w.reshape(d, nh, dh)
        return jnp.concatenate([-w[..., dh // 2:], w[..., :dh // 2]], axis=-1).reshape(d, nh * dh)

    def pad_heads(w, nh):
        w = w.reshape(d, nh, SWA_DIM)
        return jnp.pad(w, ((0, 0), (0, 0), (0, LANE - SWA_DIM))).reshape(d, nh * LANE)

    cols = [gates, hq, hf, hi, hg,
            rq, rot(rq, N_HEADS, HEAD_DIM), rk, rot(rk, N_HEADS, HEAD_DIM), rv, rg,
            pad_heads(sq, SWA_Q_HEADS), pad_heads(rot(sq, SWA_Q_HEADS, SWA_DIM), SWA_Q_HEADS),
            pad_heads(sk, SWA_KV_HEADS), pad_heads(rot(sk, SWA_KV_HEADS, SWA_DIM), SWA_KV_HEADS),
            pad_heads(sv, SWA_KV_HEADS),
            jnp.zeros((d, PROJ_COLS - (CB_SV + SWA_KV_HEADS) * LANE), w_in.dtype)]
    return jnp.concatenate(cols, axis=1).astype(BF16)


def _hgrn_kernel(hq_ref, hf_ref, hi_ref, lb_ref, o_ref, st_ref, *, n_chunks):
    @pl.when(pl.program_id(0) == 0)
    def _():
        st_ref[...] = jnp.zeros_like(st_ref)

    c = HG_CHUNK
    nsub = c // HG_SUB
    row = lax.broadcasted_iota(jnp.int32, (c, c), 0)
    col = lax.broadcasted_iota(jnp.int32, (c, c), 1)
    tri = (col <= row).astype(BF16)
    row_blk = row // HG_SUB
    col_blk = col // HG_SUB
    row_in = row % HG_SUB
    col_in = col % HG_SUB
    rsub = lax.broadcasted_iota(jnp.int32, (c, HEAD_DIM), 0) // HG_SUB
    scale = HEAD_DIM ** -0.5

    def chunk_body(ci, carry):
        r0 = pl.multiple_of(ci * c, c)
        for h in range(N_HEADS):
            lanes = slice(h * HEAD_DIM, (h + 1) * HEAD_DIM)
            xq = hq_ref[pl.ds(r0, c), lanes]
            z = hf_ref[pl.ds(r0, c), lanes]
            v = hi_ref[pl.ds(r0, c), lanes]
            lb = lb_ref[:, lanes]
            omlb = 1.0 - lb
            q = xq * _sigmoid(xq) * scale
            f = lb + omlb * _sigmoid(z)
            logf = jnp.log(jnp.maximum(f, TINY))
            kk = omlb * _sigmoid(-z)
            l1, l2, l3 = _split3(logf)
            cum = _dot(tri, l1) + _dot(tri, l2) + _dot(tri, l3)
            ends = [cum[(j + 1) * HG_SUB - 1:(j + 1) * HG_SUB, :] for j in range(nsub)]
            bsel = ends[nsub - 1]
            for j in range(nsub - 2, -1, -1):
                bsel = jnp.where(rsub <= j, ends[j], bsel)
            kp = kk * jnp.exp(bsel - cum)
            kp_b = kp.astype(BF16)
            scores = jnp.zeros((c, c), F32)
            for j in range(nsub - 1):
                qj = (q * jnp.exp(jnp.minimum(cum - ends[j], 0.0))).astype(BF16)
                sj = _dot_nt(qj, kp_b)
                scores = jnp.where((col_blk == j) & (row_blk > j), sj, scores)
            q3 = q.reshape(nsub, HG_SUB, HEAD_DIM)
            k3 = kk.reshape(nsub, HG_SUB, HEAD_DIM)
            c3 = cum.reshape(nsub, HG_SUB, HEAD_DIM)
            for s in range(HG_SUB):
                krow = k3[:, s:s + 1, :]
                crow = c3[:, s:s + 1, :]
                p = q3 * krow * jnp.exp(jnp.minimum(c3 - crow, 0.0))
                colsum = jnp.sum(p.reshape(c, HEAD_DIM), axis=-1, keepdims=True)
                hit = (col_blk == row_blk) & (col_in == s) & (row_in >= s)
                scores = jnp.where(hit, colsum, scores)
            st = st_ref[h]
            qe = (q * jnp.exp(cum)).astype(BF16)
            o = _dot_nt(qe, st.astype(BF16)) + _dot(scores.astype(BF16), v.astype(BF16))
            o_ref[pl.ds(r0, c), lanes] = o
            last = ends[nsub - 1]
            kl = (kp * jnp.exp(last - bsel)).astype(BF16)
            st_ref[h] = jnp.exp(last) * st + _dot_tn(v.astype(BF16), kl)
        return carry

    lax.fori_loop(0, n_chunks, chunk_body, 0)


def _hgrn(proj, lower):
    t = proj.shape[0]
    tb = min(t, 512)
    w = N_HEADS * HEAD_DIM

    def cols(cb):
        return pl.BlockSpec((tb, w), lambda i: (i, cb // N_HEADS))

    return pl.pallas_call(
        functools.partial(_hgrn_kernel, n_chunks=tb // HG_CHUNK),
        grid=(t // tb,),
        in_specs=[cols(CB_HQ), cols(CB_HF), cols(CB_HI),
                  pl.BlockSpec((1, w), lambda i: (0, 0))],
        out_specs=pl.BlockSpec((tb, w), lambda i: (i, 0)),
        out_shape=jax.ShapeDtypeStruct((t, w), F32),
        scratch_shapes=[pltpu.VMEM((N_HEADS, HEAD_DIM, HEAD_DIM), F32)],
        compiler_params=_cparams(("arbitrary",)),
        name="hgrn2",
    )(proj, proj, proj, lower)


def _ret_kernel(q_ref, qr_ref, k_ref, kr_ref, v_ref, cos_ref, sin_ref,
                dec_ref, qd_ref, kd_ref, cd_ref, o_ref, s_ref):
    @pl.when(pl.program_id(0) == 0)
    def _():
        s_ref[...] = jnp.zeros_like(s_ref)

    cos = cos_ref[...]
    sin = sin_ref[...]
    scale = HEAD_DIM ** -0.5
    for h in range(N_HEADS):
        lanes = slice(h * HEAD_DIM, (h + 1) * HEAD_DIM)
        q = q_ref[:, lanes] * cos + qr_ref[:, lanes] * sin
        k = (k_ref[:, lanes] * cos + kr_ref[:, lanes] * sin) * scale
        v = v_ref[:, lanes].astype(BF16)
        qb = q.astype(BF16)
        s = s_ref[h]
        scores = _dot_nt(qb, k.astype(BF16)) * dec_ref[h]
        o = _dot(qb, s.astype(BF16)) * qd_ref[h] + _dot(scores.astype(BF16), v)
        o_ref[:, lanes] = o
        s_ref[h] = cd_ref[h] * s + _dot_tn((k * kd_ref[h]).astype(BF16), v)


def _ret(proj, cos_r, sin_r):
    t = proj.shape[0]
    c = min(t, RET_CHUNK)
    w = N_HEADS * HEAD_DIM
    log_gamma = jnp.log(1.0 - 2.0 ** (-5.0 - jnp.arange(N_HEADS, dtype=F32)))
    idx = jnp.arange(c, dtype=F32)
    rel = idx[:, None] - idx[None, :]
    decay = jnp.exp(jnp.where(rel >= 0, log_gamma[:, None, None] * rel, NEG_BIG))
    ones = jnp.ones((1, 1, HEAD_DIM), F32)
    q_decay = jnp.exp(log_gamma[:, None] * (idx + 1.0))[:, :, None] * ones
    k_decay = jnp.exp(log_gamma[:, None] * (c - 1.0 - idx))[:, :, None] * ones
    c_decay = jnp.exp(log_gamma * c)[:, None, None] * ones

    def cols(cb):
        return pl.BlockSpec((c, w), lambda i: (i, cb // N_HEADS))

    def const(shape):
        return pl.BlockSpec(shape, lambda i: (0, 0, 0))

    tab = pl.BlockSpec((c, LANE), lambda i: (i, 0))
    return pl.pallas_call(
        _ret_kernel,
        grid=(t // c,),
        in_specs=[cols(CB_RQ), cols(CB_RQR), cols(CB_RK), cols(CB_RKR), cols(CB_RV), tab, tab,
                  const((N_HEADS, c, c)), const((N_HEADS, c, HEAD_DIM)),
                  const((N_HEADS, c, HEAD_DIM)), const((N_HEADS, 1, HEAD_DIM))],
        out_specs=pl.BlockSpec((c, w), lambda i: (i, 0)),
        out_shape=jax.ShapeDtypeStruct((t, w), F32),
        scratch_shapes=[pltpu.VMEM((N_HEADS, HEAD_DIM, HEAD_DIM), F32)],
        compiler_params=_cparams(("arbitrary",)),
        name="retention",
    )(proj, proj, proj, proj, proj, cos_r, sin_r, decay, q_decay, k_decay, c_decay)


def _swa_kernel(sink_ref, q_ref, qr_ref, kc_ref, krc_ref, vc_ref, kp_ref, krp_ref, vp_ref,
                cc_ref, sc_ref, cp_ref, sp_ref, o_ref):
    b = SWA_BLOCK
    blk = pl.program_id(0)
    cos_c, sin_c = cc_ref[...], sc_ref[...]
    cos_p, sin_p = cp_ref[...], sp_ref[...]
    qi = lax.broadcasted_iota(jnp.int32, (b, 2 * b), 0) + b
    ki = lax.broadcasted_iota(jnp.int32, (b, 2 * b), 1)
    rel = qi - ki
    keep = (rel >= 0) & (rel < b) & ((blk > 0) | (ki >= b))
    group = SWA_Q_HEADS // SWA_KV_HEADS
    for g in range(SWA_KV_HEADS):
        kl = slice(g * LANE, (g + 1) * LANE)
        k_cur = kc_ref[:, kl] * cos_c + krc_ref[:, kl] * sin_c
        k_prev = kp_ref[:, kl] * cos_p + krp_ref[:, kl] * sin_p
        kw = jnp.concatenate([k_prev, k_cur], axis=0).astype(BF16)
        vw = jnp.concatenate([vp_ref[:, kl], vc_ref[:, kl]], axis=0).astype(BF16)
        for j in range(group):
            h = g * group + j
            ql = slice(h * LANE, (h + 1) * LANE)
            q = (q_ref[:, ql] * cos_c + qr_ref[:, ql] * sin_c).astype(BF16)
            s = _dot_nt(q, kw) * (SWA_DIM ** -0.5)
            s = jnp.where(keep, s, NEG_BIG)
            sink = sink_ref[h]
            m = jnp.maximum(jnp.max(s, axis=-1, keepdims=True), sink)
            p = jnp.exp(s - m)
            denom = jnp.sum(p, axis=-1, keepdims=True) + jnp.exp(sink - m)
            o_ref[:, ql] = _dot((p / denom).astype(BF16), vw)


def _swa(proj, cos_s, sin_s, sinks):
    t = proj.shape[0]
    b = SWA_BLOCK
    qw = SWA_Q_HEADS * LANE
    kw = SWA_KV_HEADS * LANE

    def cur(cb, width):
        return pl.BlockSpec((b, width), lambda i: (i, cb * LANE // width))

    def prev(cb, width):
        return pl.BlockSpec((b, width), lambda i: (jnp.maximum(i - 1, 0), cb * LANE // width))

    tab_c = pl.BlockSpec((b, LANE), lambda i: (i, 0))
    tab_p = pl.BlockSpec((b, LANE), lambda i: (jnp.maximum(i - 1, 0), 0))
    return pl.pallas_call(
        _swa_kernel,
        grid=(t // b,),
        in_specs=[pl.BlockSpec(memory_space=pltpu.SMEM),
                  cur(CB_SQ, qw), cur(CB_SQR, qw),
                  cur(CB_SK, kw), cur(CB_SKR, kw), cur(CB_SV, kw),
                  prev(CB_SK, kw), prev(CB_SKR, kw), prev(CB_SV, kw),
                  tab_c, tab_c, tab_p, tab_p],
        out_specs=pl.BlockSpec((b, qw), lambda i: (i, 0)),
        out_shape=jax.ShapeDtypeStruct((t, qw), F32),
        compiler_params=_cparams(("parallel",)),
        name="swa",
    )(sinks, proj, proj, proj, proj, proj, proj, proj, proj, cos_s, sin_s, cos_s, sin_s)


def _merge_kernel(x_ref, oa_ref, hg_ref, ob_ref, rg_ref, oc_ref, ga_ref, gb_ref, gc_ref,
                  hn_ref, wa_ref, wb_ref, wc_ref, wo_ref, o_ref):
    hg = hg_ref[...]
    a = _rms(oa_ref[...]) * hn_ref[...] * (hg * _sigmoid(hg))
    ya = _dot(a.astype(BF16), wa_ref[...])
    rg = rg_ref[...]
    ob = ob_ref[...]
    parts = []
    for h in range(N_HEADS):
        lanes = slice(h * HEAD_DIM, (h + 1) * HEAD_DIM)
        parts.append(_rms(ob[:, lanes]))
    bn = jnp.concatenate(parts, axis=-1) * (rg * _sigmoid(rg))
    yb = _dot(bn.astype(BF16), wb_ref[...])
    yc = _dot(oc_ref[...].astype(BF16), wc_ref[...])
    mixed = _sigmoid(ga_ref[...]) * ya + _sigmoid(gb_ref[...]) * yb + _sigmoid(gc_ref[...]) * yc
    o_ref[...] = x_ref[...] + _dot(mixed.astype(BF16), wo_ref[...])


def _merge(x, proj, oa, ob, oc, hn, wa, wb, wc, wo):
    t = x.shape[0]
    tt = min(t, 512)
    w = N_HEADS * HEAD_DIM
    qw = SWA_Q_HEADS * LANE

    def rows(width, cb=0):
        return pl.BlockSpec((tt, width), lambda i: (i, cb * LANE // width))

    def full(shape):
        return pl.BlockSpec(shape, lambda i: (0, 0))

    return pl.pallas_call(
        _merge_kernel,
        grid=(t // tt,),
        in_specs=[rows(D_MODEL), rows(w), rows(w, CB_HG), rows(w), rows(w, CB_RG), rows(qw),
                  rows(D_MODEL, CB_GATE), rows(D_MODEL, CB_GATE + 8), rows(D_MODEL, CB_GATE + 16),
                  full((1, w)), full((w, D_MODEL)), full((w, D_MODEL)), full((qw, D_MODEL)),
                  full((D_MODEL, D_MODEL))],
        out_specs=rows(D_MODEL),
        out_shape=jax.ShapeDtypeStruct((t, D_MODEL), F32),
        compiler_params=_cparams(("parallel",)),
        name="merge",
    )(x, oa, proj, ob, proj, oc, proj, proj, proj, hn, wa, wb, wc, wo)


def _memkv_kernel(m_ref, g_ref, wk_ref, wv_ref, k_ref, v_ref):
    h = (_rms(m_ref[...]) * g_ref[...]).astype(BF16)
    k_ref[...] = _dot(h, wk_ref[...]).astype(BF16)
    v_ref[...] = _dot(h, wv_ref[...]).astype(BF16)


def _memkv(mem, g, wk, wv):
    nm = mem.shape[0]
    out = jax.ShapeDtypeStruct((nm, D_MODEL), BF16)
    return pl.pallas_call(
        _memkv_kernel,
        out_shape=[out, out],
        compiler_params=pltpu.CompilerParams(vmem_limit_bytes=VMEM_LIMIT),
        name="mem_kv",
    )(mem, g, wk, wv)


def _xattn_kernel(x_ref, g_ref, wq_ref, k_ref, v_ref, wo_ref, o_ref):
    x = x_ref[...]
    h = (_rms(x) * g_ref[...]).astype(BF16)
    q = _dot(h, wq_ref[...])
    outs = []
    for hh in range(X_HEADS):
        lanes = slice(hh * X_DIM, (hh + 1) * X_DIM)
        s = _dot_nt(q[:, lanes].astype(BF16), k_ref[:, lanes]) * (X_DIM ** -0.5)
        m = jnp.max(s, axis=-1, keepdims=True)
        p = jnp.exp(s - m)
        p = p / jnp.sum(p, axis=-1, keepdims=True)
        outs.append(_dot(p.astype(BF16), v_ref[:, lanes]))
    o = jnp.concatenate(outs, axis=-1)
    o_ref[...] = x + _dot(o.astype(BF16), wo_ref[...])


def _xattn(x, g, wq, k, v, wo):
    t = x.shape[0]
    tt = min(t, 512)
    nm = k.shape[0]

    def full(shape):
        return pl.BlockSpec(shape, lambda i: (0, 0))

    rows = pl.BlockSpec((tt, D_MODEL), lambda i: (i, 0))
    return pl.pallas_call(
        _xattn_kernel,
        grid=(t // tt,),
        in_specs=[rows, full((1, D_MODEL)), full((D_MODEL, D_MODEL)),
                  full((nm, D_MODEL)), full((nm, D_MODEL)), full((D_MODEL, D_MODEL))],
        out_specs=rows,
        out_shape=jax.ShapeDtypeStruct((t, D_MODEL), F32),
        compiler_params=_cparams(("parallel",)),
        name="xattn",
    )(x, g, wq, k, v, wo)


_CAND_ROWS = [PEER_TOPK // (k1 + 1) for k1 in range(8)]


def _top_values(s, n, with_rank=False):
    vals = []
    rank = jnp.full(s.shape, float(n), F32)
    for i in range(n):
        m = jnp.max(s, axis=0, keepdims=True)
        vals.append(m)
        hit = s >= m
        if with_rank:
            rank = jnp.where(hit, float(i), rank)
        s = jnp.where(hit, -jnp.inf, s)
    return (vals, rank) if with_rank else vals


def _bf16_pair_bits(x):
    u = lax.bitcast_convert_type(x.astype(BF16).astype(F32), jnp.uint32)
    return u | (u >> 16)


def _stack_rows(rows, n):
    tt = rows[0].shape[1]
    r = lax.broadcasted_iota(jnp.int32, (n, tt), 0)
    out = jnp.zeros((n, tt), F32)
    for k in range(n):
        out = jnp.where(r == k, rows[k], out)
    return out


def _router_kernel(x_ref, g_ref, wh_ref, wl_ref, kh_ref, kl_ref,
                   hb_ref, a_ref, b_ref, r_ref, l_ref, qt_scr):
    h = _rms(x_ref[...]) * g_ref[...]
    h_hi, h_lo = _split2(h)
    hb_ref[...] = h.T.astype(BF16)
    wh = wh_ref[...]
    qt_scr[...] = _dot_nt(wh, h_hi) + _dot_nt(wl_ref[...], h_hi) + _dot_nt(wh, h_lo)
    tt = x_ref.shape[0]
    k = PEER_TOPK
    r8 = lax.broadcasted_iota(jnp.int32, (8, tt), 0)

    def head_body(hh, carry):
        scores = []
        for p in range(2):
            r0 = pl.multiple_of(hh * (2 * PEER_HALF) + p * PEER_HALF, PEER_HALF)
            q_hi, q_lo = _split2(qt_scr[pl.ds(r0, PEER_HALF), :])
            kh = kh_ref[p * PEER_HEADS + hh]
            scores.append(_dot(kh, q_hi) + _dot(kl_ref[p * PEER_HEADS + hh], q_hi) + _dot(kh, q_lo))
        s1, s2 = scores
        v1, rank1 = _top_values(s1, k, with_rank=True)
        v2, rank2 = _top_values(s2, k, with_rank=True)
        sv1 = _stack_rows(v1, k)
        sv2 = _stack_rows(v2, k)
        groups = [v1[0] + sv2]
        for k1 in range(1, 8):
            groups.append(jnp.where(r8 < _CAND_ROWS[k1], v1[k1] + sv2[0:8], -jnp.inf))
        groups.append(sv1[8:16] + v2[0])
        cand = jnp.concatenate(groups, axis=0)
        tau = _top_values(cand, k)[k - 1]
        m0 = v1[0] + v2[0]
        z = jnp.sum(jnp.where(cand >= tau, jnp.exp(cand - m0), 0.0), axis=0, keepdims=True)
        a_ref[hh] = _bf16_pair_bits(jnp.exp(s1 - v1[0]))
        b_ref[hh] = (jnp.exp(s2 - v2[0]) / z).astype(BF16)
        count1 = jnp.zeros_like(s1)
        for k1 in range(k):
            n_k1 = jnp.sum((v1[k1] + sv2 >= tau).astype(F32), axis=0, keepdims=True)
            count1 = jnp.where(rank1 == float(k1), n_k1, count1)
        r_ref[hh] = rank2.astype(BF16)
        l_ref[hh] = _bf16_pair_bits(count1)
        return carry

    lax.fori_loop(0, PEER_HEADS, head_body, 0)


def _router(x, g, wq_hi, wq_lo, k_hi, k_lo):
    t = x.shape[0]
    tt = min(t, 256)
    nq = wq_hi.shape[0]
    sel = jax.ShapeDtypeStruct((PEER_HEADS, PEER_KEYS, t), jnp.uint32)
    sel_b = jax.ShapeDtypeStruct((PEER_HEADS, PEER_KEYS, t), BF16)
    sel_spec = pl.BlockSpec((PEER_HEADS, PEER_KEYS, tt), lambda i: (0, 0, i))
    return pl.pallas_call(
        _router_kernel,
        grid=(t // tt,),
        in_specs=[pl.BlockSpec((tt, D_MODEL), lambda i: (i, 0)),
                  pl.BlockSpec((1, D_MODEL), lambda i: (0, 0)),
                  pl.BlockSpec((nq, D_MODEL), lambda i: (0, 0)),
                  pl.BlockSpec((nq, D_MODEL), lambda i: (0, 0)),
                  pl.BlockSpec((2 * PEER_HEADS, PEER_KEYS, PEER_HALF), lambda i: (0, 0, 0)),
                  pl.BlockSpec((2 * PEER_HEADS, PEER_KEYS, PEER_HALF), lambda i: (0, 0, 0))],
        out_specs=[pl.BlockSpec((D_MODEL, tt), lambda i: (0, i)), sel_spec, sel_spec, sel_spec, sel_spec],
        out_shape=[jax.ShapeDtypeStruct((D_MODEL, t), BF16), sel, sel_b, sel_b, sel],
        scratch_shapes=[pltpu.VMEM((nq, tt), F32)],
        compiler_params=_cparams(("parallel",)),
        name="peer_router",
    )(x, g, wq_hi, wq_lo, k_hi, k_lo)


def _peer_kernel(x_ref, hb_ref, a_ref, l_ref, b_ref, r_ref, u_ref, vt_ref, gf_ref, o_ref,
                 acc_ref, ga_ref, *, final_norm):
    e = pl.program_id(1)

    @pl.when(e == 0)
    def _():
        acc_ref[...] = jnp.zeros_like(acc_ref)

    u_blk = pltpu.bitcast(u_ref[...], BF16)
    vt_blk = pltpu.bitcast(vt_ref[...], BF16)
    n_i1 = u_blk.shape[0] // PEER_KEYS
    sub = 16
    tt = hb_ref.shape[1]
    shape3 = (PEER_KEYS // sub, sub, tt)
    zero3 = jnp.zeros(shape3, BF16)
    per_piece = 4
    piece = per_piece * PEER_KEYS
    for p in range(n_i1 // per_piece):
        for j in range(p * per_piece, (p + 1) * per_piece):
            gate = zero3
            for hh in range(PEER_HEADS):
                arow = pltpu.bitcast(jnp.broadcast_to(a_ref[hh, j:j + 1, :], (8, tt)), BF16)
                lrow = pltpu.bitcast(jnp.broadcast_to(l_ref[hh, j:j + 1, :], (8, tt)), BF16)
                b3 = b_ref[hh].reshape(shape3)
                r3 = r_ref[hh].reshape(shape3)
                gate = gate + arow[None] * jnp.where(r3 < lrow[None], b3, zero3)
            ga_ref[j * PEER_KEYS:(j + 1) * PEER_KEYS, :] = gate.reshape(PEER_KEYS, tt)
        rows = slice(p * piece, (p + 1) * piece)
        pre = _dot(u_blk[rows, :], hb_ref[...])
        act = pre + pre * lax.erf(pre)
        ga_ref[rows, :] = ga_ref[rows, :] * act.astype(BF16)
    acc_ref[...] += _dot(vt_blk, ga_ref[...])

    @pl.when(e == pl.num_programs(1) - 1)
    def _():
        y = x_ref[...] + acc_ref[...].T
        if final_norm:
            y = _rms(y) * gf_ref[...]
        o_ref[...] = y


def _pack_kernel(w_ref, o_ref, *, scale, transpose):
    w = w_ref[...]
    if transpose:
        w = w.T
    o_ref[...] = pltpu.bitcast((w * scale).astype(BF16), jnp.uint32)


def _pack_table(w, scale, transpose):
    rows, cols = w.shape
    if transpose:
        blk = 512
        in_spec = pl.BlockSpec((blk, cols), lambda i: (i, 0))
        out_spec = pl.BlockSpec((cols // 2, blk), lambda i: (0, i))
        out_shape = jax.ShapeDtypeStruct((cols // 2, rows), jnp.uint32)
    else:
        blk = 1024
        in_spec = pl.BlockSpec((blk, cols), lambda i: (i, 0))
        out_spec = pl.BlockSpec((blk // 2, cols), lambda i: (i, 0))
        out_shape = jax.ShapeDtypeStruct((rows // 2, cols), jnp.uint32)
    return pl.pallas_call(
        functools.partial(_pack_kernel, scale=scale, transpose=transpose),
        grid=(rows // blk,),
        in_specs=[in_spec],
        out_specs=out_spec,
        out_shape=out_shape,
        compiler_params=_cparams(("parallel",)),
        name="pack_table_t" if transpose else "pack_table",
    )(w)


def _peer(x, hb, a, l, b, r, u, vt, gf, final_norm):
    t = x.shape[0]
    tt = min(t, 512)
    eb = 16 * PEER_KEYS
    rows = pl.BlockSpec((tt, D_MODEL), lambda i, e: (i, 0))
    i1_rows = pl.BlockSpec((PEER_HEADS, eb // PEER_KEYS, tt), lambda i, e: (0, e, i))
    all_rows = pl.BlockSpec((PEER_HEADS, PEER_KEYS, tt), lambda i, e: (0, 0, i))
    return pl.pallas_call(
        functools.partial(_peer_kernel, final_norm=final_norm),
        grid=(t // tt, N_EXPERTS // eb),
        in_specs=[rows, pl.BlockSpec((D_MODEL, tt), lambda i, e: (0, i)),
                  i1_rows, i1_rows, all_rows, all_rows,
                  pl.BlockSpec((eb // 2, D_MODEL), lambda i, e: (e, 0)),
                  pl.BlockSpec((D_MODEL // 2, eb), lambda i, e: (0, e)),
                  pl.BlockSpec((1, D_MODEL), lambda i, e: (0, 0))],
        out_specs=rows,
        out_shape=jax.ShapeDtypeStruct((t, D_MODEL), F32),
        scratch_shapes=[pltpu.VMEM((D_MODEL, tt), F32), pltpu.VMEM((eb, tt), BF16)],
        compiler_params=_cparams(("parallel", "arbitrary")),
        name="peer_dense",
    )(x, hb, a, l, b, r, u, vt, gf)


def kernel(x, mem, positions, ln_mix, w_in, lb_param, hgrn_norm, swa_sinks, w_br_hgrn, w_br_ret,
           w_br_swa, w_out, ln_xq, ln_xkv, w_xq, w_xk, w_xv, w_xo, ln_ffn, peer_wq, peer_keys,
           peer_u, peer_v, ln_final):
    b, t, d = x.shape
    assert b == 1 and d == D_MODEL
    depth = w_in.shape[0]
    xs = x.reshape(t, d)
    ms = mem.reshape(mem.shape[1], d)
    cos_r, sin_r, cos_s, sin_s = _rope_tables(positions)
    lb_sm = jax.nn.softmax(lb_param.astype(F32), axis=0)
    lower = jnp.cumsum(lb_sm, axis=0) - lb_sm[0]

    for l in range(depth):
        proj = _proj(xs, ln_mix[l][None, :], _relay_w_in(w_in[l]))
        oa = _hgrn(proj, lower[l][None, :])
        ob = _ret(proj, cos_r, sin_r)
        oc = _swa(proj, cos_s, sin_s, swa_sinks[l])
        wc = jnp.pad(w_br_swa[l].reshape(SWA_Q_HEADS, SWA_DIM, d),
                     ((0, 0), (0, LANE - SWA_DIM), (0, 0))).reshape(SWA_Q_HEADS * LANE, d)
        xs = _merge(xs, proj, oa, ob, oc, hgrn_norm[l][None, :], w_br_hgrn[l].astype(BF16),
                    w_br_ret[l].astype(BF16), wc.astype(BF16), w_out[l].astype(BF16))

        km, vm = _memkv(ms, ln_xkv[l][None, :], w_xk[l].astype(BF16), w_xv[l].astype(BF16))
        xs = _xattn(xs, ln_xq[l][None, :], w_xq[l].astype(BF16), km, vm, w_xo[l].astype(BF16))

        wq_hi, wq_lo = _split2(peer_wq[l].T)
        k_hi, k_lo = _split2(peer_keys[l].reshape(2 * PEER_HEADS, PEER_KEYS, PEER_HALF))
        hb, a, bz, r2, cnt = _router(xs, ln_ffn[l][None, :], wq_hi, wq_lo, k_hi, k_lo)
        c = 2.0 ** -0.5
        xs = _peer(xs, hb, a, cnt, bz, r2, _pack_table(peer_u[l], c, False), _pack_table(peer_v[l], c, True),
                   ln_final[None, :], final_norm=(l == depth - 1))

    return xs.reshape(b, t, d)
```

```python
import functools
import math

import numpy as np
import jax
import jax.numpy as jnp
from jax import lax
from jax.experimental import pallas as pl
from jax.experimental.pallas import tpu as pltpu

F32 = jnp.float32
BF16 = jnp.bfloat16

D_MODEL = 1024
HEAD_DIM = 128
N_HEADS = 4
SWA_Q_HEADS = 8
SWA_KV_HEADS = 2
SWA_DIM = 64
SWA_BLOCK = 128
ROPE_THETA = 10000.0
N_MEM = 256
X_HEADS = 4
X_DIM = D_MODEL // X_HEADS
PEER_HEADS = 8
PEER_KEYS = 128
PEER_TOPK = 16
PEER_HALF = 128
N_EXPERTS = PEER_KEYS * PEER_KEYS
EPS = 1e-6
NEG_BIG = -1e30
TINY = 1e-30

LANE = 128
HG_CHUNK = 64
HG_SUB = 16
RET_CHUNK = 256
PEER_PIECE_KEYS = 4
VMEM_LIMIT = 56 * 1024 * 1024

CB_GATE = 0
CB_HQ, CB_HF, CB_HI, CB_HG = 24, 28, 32, 36
CB_RQ, CB_RQR, CB_RK, CB_RKR, CB_RV, CB_RG = 40, 44, 48, 52, 56, 60
CB_SQ, CB_SQR, CB_SK, CB_SKR, CB_SV = 64, 72, 80, 82, 84
N_CB = 88
PROJ_COLS = N_CB * LANE


def _cparams(sem):
    return pltpu.CompilerParams(dimension_semantics=sem, vmem_limit_bytes=VMEM_LIMIT)


def _dot(a, b):
    return jnp.dot(a, b, preferred_element_type=F32)


def _dot_nt(a, b):
    return lax.dot_general(a, b, (((1,), (1,)), ((), ())), preferred_element_type=F32)


def _dot_tn(a, b):
    return lax.dot_general(a, b, (((0,), (0,)), ((), ())), preferred_element_type=F32)


def _split2(x):
    hi = x.astype(BF16)
    lo = (x - hi.astype(F32)).astype(BF16)
    return hi, lo


def _split3(x):
    hi = x.astype(BF16)
    r = x - hi.astype(F32)
    mid = r.astype(BF16)
    lo = (r - mid.astype(F32)).astype(BF16)
    return hi, mid, lo


def _sigmoid(x):
    return 1.0 / (1.0 + jnp.exp(-x))


def _rms(x):
    return x * lax.rsqrt(jnp.mean(x * x, axis=-1, keepdims=True) + EPS)


def _rope_kernel(pos_ref, fr_ref, fs_ref, cr_ref, sr_ref, cs_ref, ss_ref):
    pos = pos_ref[...].astype(F32)
    ang_r = pos * fr_ref[...]
    cr_ref[...] = jnp.cos(ang_r)
    sr_ref[...] = jnp.sin(ang_r)
    ang_s = pos * fs_ref[...]
    cs_ref[...] = jnp.cos(ang_s)
    ss_ref[...] = jnp.sin(ang_s)


def _rope_tables(positions):
    t = positions.shape[1]
    tt = min(t, 1024)
    ret_f = ROPE_THETA ** (-jnp.linspace(0.0, 1.0, HEAD_DIM // 2, dtype=F32))
    swa_f = ROPE_THETA ** (-jnp.arange(0, SWA_DIM, 2, dtype=F32) / SWA_DIM)
    fr = jnp.concatenate([ret_f, ret_f])[None, :]
    fs = jnp.concatenate([swa_f, swa_f, jnp.zeros((LANE - SWA_DIM,), F32)])[None, :]
    pos = positions.reshape(t, 1)
    tab = jax.ShapeDtypeStruct((t, LANE), F32)
    row = pl.BlockSpec((tt, LANE), lambda i: (i, 0))
    vec = pl.BlockSpec((1, LANE), lambda i: (0, 0))
    return pl.pallas_call(
        _rope_kernel,
        grid=(t // tt,),
        in_specs=[pl.BlockSpec((tt, 1), lambda i: (i, 0)), vec, vec],
        out_specs=[row, row, row, row],
        out_shape=[tab, tab, tab, tab],
        compiler_params=_cparams(("parallel",)),
        name="rope_tables",
    )(pos, fr, fs)


def _proj_kernel(x_ref, g_ref, w_ref, o_ref, h_scr):
    @pl.when(pl.program_id(1) == 0)
    def _():
        h_scr[...] = (_rms(x_ref[...]) * g_ref[...]).astype(BF16)

    o_ref[...] = _dot(h_scr[...], w_ref[...])


def _proj(x, g, w):
    t = x.shape[0]
    n = w.shape[1]
    tm = min(t, 1024)
    tn = 1024
    return pl.pallas_call(
        _proj_kernel,
        grid=(t // tm, n // tn),
        in_specs=[pl.BlockSpec((tm, D_MODEL), lambda i, j: (i, 0)),
                  pl.BlockSpec((1, D_MODEL), lambda i, j: (0, 0)),
                  pl.BlockSpec((D_MODEL, tn), lambda i, j: (0, j))],
        out_specs=pl.BlockSpec((tm, tn), lambda i, j: (i, j)),
        out_shape=jax.ShapeDtypeStruct((t, n), F32),
        scratch_shapes=[pltpu.VMEM((tm, D_MODEL), BF16)],
        compiler_params=_cparams(("parallel", "arbitrary")),
        name="in_proj",
    )(x, g, w)


def _relay_w_in(w_in):
    d = w_in.shape[0]
    sizes = (512,) * 8 + (512, 128, 128, 3 * D_MODEL)
    offs = np.concatenate([[0], np.cumsum(sizes)])
    hq, hf, hi, hg, rq, rk, rv, rg, sq, sk, sv, gates = [
        w_in[:, int(offs[i]):int(offs[i + 1])] for i in range(12)]

    def rot(w, nh, dh):
        w = w.reshape(d, nh, dh)
        return jnp.concatenate([-w[..., dh // 2:], w[..., :dh // 2]], axis=-1).reshape(d, nh * dh)

    def pad_heads(w, nh):
        w = w.reshape(d, nh, SWA_DIM)
        return jnp.pad(w, ((0, 0), (0, 0), (0, LANE - SWA_DIM))).reshape(d, nh * LANE)

    cols = [gates, hq, hf, hi, hg,
            rq, rot(rq, N_HEADS, HEAD_DIM), rk, rot(rk, N_HEADS, HEAD_DIM), rv, rg,
            pad_heads(sq, SWA_Q_HEADS), pad_heads(rot(sq, SWA_Q_HEADS, SWA_DIM), SWA_Q_HEADS),
            pad_heads(sk, SWA_KV_HEADS), pad_heads(rot(sk, SWA_KV_HEADS, SWA_DIM), SWA_KV_HEADS),
            pad_heads(sv, SWA_KV_HEADS),
            jnp.zeros((d, PROJ_COLS - (CB_SV + SWA_KV_HEADS) * LANE), w_in.dtype)]
    return jnp.concatenate(cols, axis=1).astype(BF16)


def _hgrn_kernel(hq_ref, hf_ref, hi_ref, lb_ref, o_ref, st_ref, *, n_chunks):
    @pl.when(pl.program_id(0) == 0)
    def _():
        st_ref[...] = jnp.zeros_like(st_ref)

    c = HG_CHUNK
    nsub = c // HG_SUB
    row = lax.broadcasted_iota(jnp.int32, (c, c), 0)
    col = lax.broadcasted_iota(jnp.int32, (c, c), 1)
    tri = (col <= row).astype(BF16)
    row_blk = row // HG_SUB
    col_blk = col // HG_SUB
    row_in = row % HG_SUB
    col_in = col % HG_SUB
    rsub = lax.broadcasted_iota(jnp.int32, (c, HEAD_DIM), 0) // HG_SUB
    scale = HEAD_DIM ** -0.5

    def chunk_body(ci, carry):
        r0 = pl.multiple_of(ci * c, c)
        for h in range(N_HEADS):
            lanes = slice(h * HEAD_DIM, (h + 1) * HEAD_DIM)
            xq = hq_ref[pl.ds(r0, c), lanes]
            z = hf_ref[pl.ds(r0, c), lanes]
            v = hi_ref[pl.ds(r0, c), lanes]
            lb = lb_ref[:, lanes]
            omlb = 1.0 - lb
            q = xq * _sigmoid(xq) * scale
            f = lb + omlb * _sigmoid(z)
            logf = jnp.log(jnp.maximum(f, TINY))
            kk = omlb * _sigmoid(-z)
            l1, l2, l3 = _split3(logf)
            cum = _dot(tri, l1) + _dot(tri, l2) + _dot(tri, l3)
            ends = [cum[(j + 1) * HG_SUB - 1:(j + 1) * HG_SUB, :] for j in range(nsub)]
            bsel = ends[nsub - 1]
            for j in range(nsub - 2, -1, -1):
                bsel = jnp.where(rsub <= j, ends[j], bsel)
            kp = kk * jnp.exp(bsel - cum)
            kp_b = kp.astype(BF16)
            scores = jnp.zeros((c, c), F32)
            for j in range(nsub - 1):
                qj = (q * jnp.exp(jnp.minimum(cum - ends[j], 0.0))).astype(BF16)
                sj = _dot_nt(qj, kp_b)
                scores = jnp.where((col_blk == j) & (row_blk > j), sj, scores)
            q3 = q.reshape(nsub, HG_SUB, HEAD_DIM)
            k3 = kk.reshape(nsub, HG_SUB, HEAD_DIM)
            c3 = cum.reshape(nsub, HG_SUB, HEAD_DIM)
            for s in range(HG_SUB):
                krow = k3[:, s:s + 1, :]
                crow = c3[:, s:s + 1, :]
                p = q3 * krow * jnp.exp(jnp.minimum(c3 - crow, 0.0))
                colsum = jnp.sum(p.reshape(c, HEAD_DIM), axis=-1, keepdims=True)
                hit = (col_blk == row_blk) & (col_in == s) & (row_in >= s)
                scores = jnp.where(hit, colsum, scores)
            st = st_ref[h]
            qe = (q * jnp.exp(cum)).astype(BF16)
            o = _dot_nt(qe, st.astype(BF16)) + _dot(scores.astype(BF16), v.astype(BF16))
            o_ref[pl.ds(r0, c), lanes] = o
            last = ends[nsub - 1]
            kl = (kp * jnp.exp(last - bsel)).astype(BF16)
            st_ref[h] = jnp.exp(last) * st + _dot_tn(v.astype(BF16), kl)
        return carry

    lax.fori_loop(0, n_chunks, chunk_body, 0)


def _hgrn(proj, lower):
    t = proj.shape[0]
    tb = min(t, 512)
    w = N_HEADS * HEAD_DIM

    def cols(cb):
        return pl.BlockSpec((tb, w), lambda i: (i, cb // N_HEADS))

    return pl.pallas_call(
        functools.partial(_hgrn_kernel, n_chunks=tb // HG_CHUNK),
        grid=(t // tb,),
        in_specs=[cols(CB_HQ), cols(CB_HF), cols(CB_HI),
                  pl.BlockSpec((1, w), lambda i: (0, 0))],
        out_specs=pl.BlockSpec((tb, w), lambda i: (i, 0)),
        out_shape=jax.ShapeDtypeStruct((t, w), F32),
        scratch_shapes=[pltpu.VMEM((N_HEADS, HEAD_DIM, HEAD_DIM), F32)],
        compiler_params=_cparams(("arbitrary",)),
        name="hgrn2",
    )(proj, proj, proj, lower)


def _ret_kernel(q_ref, qr_ref, k_ref, kr_ref, v_ref, cos_ref, sin_ref,
                dec_ref, qd_ref, kd_ref, cd_ref, o_ref, s_ref):
    @pl.when(pl.program_id(0) == 0)
    def _():
        s_ref[...] = jnp.zeros_like(s_ref)

    cos = cos_ref[...]
    sin = sin_ref[...]
    scale = HEAD_DIM ** -0.5
    for h in range(N_HEADS):
        lanes = slice(h * HEAD_DIM, (h + 1) * HEAD_DIM)
        q = q_ref[:, lanes] * cos + qr_ref[:, lanes] * sin
        k = (k_ref[:, lanes] * cos + kr_ref[:, lanes] * sin) * scale
        v = v_ref[:, lanes].astype(BF16)
        qb = q.astype(BF16)
        s = s_ref[h]
        scores = _dot_nt(qb, k.astype(BF16)) * dec_ref[h]
        o = _dot(qb, s.astype(BF16)) * qd_ref[h] + _dot(scores.astype(BF16), v)
        o_ref[:, lanes] = o
        s_ref[h] = cd_ref[h] * s + _dot_tn((k * kd_ref[h]).astype(BF16), v)


def _ret(proj, cos_r, sin_r):
    t = proj.shape[0]
    c = min(t, RET_CHUNK)
    w = N_HEADS * HEAD_DIM
    log_gamma = jnp.log(1.0 - 2.0 ** (-5.0 - jnp.arange(N_HEADS, dtype=F32)))
    idx = jnp.arange(c, dtype=F32)
    rel = idx[:, None] - idx[None, :]
    decay = jnp.exp(jnp.where(rel >= 0, log_gamma[:, None, None] * rel, NEG_BIG))
    ones = jnp.ones((1, 1, HEAD_DIM), F32)
    q_decay = jnp.exp(log_gamma[:, None] * (idx + 1.0))[:, :, None] * ones
    k_decay = jnp.exp(log_gamma[:, None] * (c - 1.0 - idx))[:, :, None] * ones
    c_decay = jnp.exp(log_gamma * c)[:, None, None] * ones

    def cols(cb):
        return pl.BlockSpec((c, w), lambda i: (i, cb // N_HEADS))

    def const(shape):
        return pl.BlockSpec(shape, lambda i: (0, 0, 0))

    tab = pl.BlockSpec((c, LANE), lambda i: (i, 0))
    return pl.pallas_call(
        _ret_kernel,
        grid=(t // c,),
        in_specs=[cols(CB_RQ), cols(CB_RQR), cols(CB_RK), cols(CB_RKR), cols(CB_RV), tab, tab,
                  const((N_HEADS, c, c)), const((N_HEADS, c, HEAD_DIM)),
                  const((N_HEADS, c, HEAD_DIM)), const((N_HEADS, 1, HEAD_DIM))],
        out_specs=pl.BlockSpec((c, w), lambda i: (i, 0)),
        out_shape=jax.ShapeDtypeStruct((t, w), F32),
        scratch_shapes=[pltpu.VMEM((N_HEADS, HEAD_DIM, HEAD_DIM), F32)],
        compiler_params=_cparams(("arbitrary",)),
        name="retention",
    )(proj, proj, proj, proj, proj, cos_r, sin_r, decay, q_decay, k_decay, c_decay)


def _swa_kernel(sink_ref, q_ref, qr_ref, kc_ref, krc_ref, vc_ref, kp_ref, krp_ref, vp_ref,
                cc_ref, sc_ref, cp_ref, sp_ref, o_ref):
    b = SWA_BLOCK
    blk = pl.program_id(0)
    cos_c, sin_c = cc_ref[...], sc_ref[...]
    cos_p, sin_p = cp_ref[...], sp_ref[...]
    qi = lax.broadcasted_iota(jnp.int32, (b, 2 * b), 0) + b
    ki = lax.broadcasted_iota(jnp.int32, (b, 2 * b), 1)
    rel = qi - ki
    keep = (rel >= 0) & (rel < b) & ((blk > 0) | (ki >= b))
    group = SWA_Q_HEADS // SWA_KV_HEADS
    for g in range(SWA_KV_HEADS):
        kl = slice(g * LANE, (g + 1) * LANE)
        k_cur = kc_ref[:, kl] * cos_c + krc_ref[:, kl] * sin_c
        k_prev = kp_ref[:, kl] * cos_p + krp_ref[:, kl] * sin_p
        kw = jnp.concatenate([k_prev, k_cur], axis=0).astype(BF16)
        vw = jnp.concatenate([vp_ref[:, kl], vc_ref[:, kl]], axis=0).astype(BF16)
        for j in range(group):
            h = g * group + j
            ql = slice(h * LANE, (h + 1) * LANE)
            q = (q_ref[:, ql] * cos_c + qr_ref[:, ql] * sin_c).astype(BF16)
            s = _dot_nt(q, kw) * (SWA_DIM ** -0.5)
            s = jnp.where(keep, s, NEG_BIG)
            sink = sink_ref[h]
            m = jnp.maximum(jnp.max(s, axis=-1, keepdims=True), sink)
            p = jnp.exp(s - m)
            denom = jnp.sum(p, axis=-1, keepdims=True) + jnp.exp(sink - m)
            o_ref[:, ql] = _dot((p / denom).astype(BF16), vw)


def _swa(proj, cos_s, sin_s, sinks):
    t = proj.shape[0]
    b = SWA_BLOCK
    qw = SWA_Q_HEADS * LANE
    kw = SWA_KV_HEADS * LANE

    def cur(cb, width):
        return pl.BlockSpec((b, width), lambda i: (i, cb * LANE // width))

    def prev(cb, width):
        return pl.BlockSpec((b, width), lambda i: (jnp.maximum(i - 1, 0), cb * LANE // width))

    tab_c = pl.BlockSpec((b, LANE), lambda i: (i, 0))
    tab_p = pl.BlockSpec((b, LANE), lambda i: (jnp.maximum(i - 1, 0), 0))
    return pl.pallas_call(
        _swa_kernel,
        grid=(t // b,),
        in_specs=[pl.BlockSpec(memory_space=pltpu.SMEM),
                  cur(CB_SQ, qw), cur(CB_SQR, qw),
                  cur(CB_SK, kw), cur(CB_SKR, kw), cur(CB_SV, kw),
                  prev(CB_SK, kw), prev(CB_SKR, kw), prev(CB_SV, kw),
                  tab_c, tab_c, tab_p, tab_p],
        out_specs=pl.BlockSpec((b, qw), lambda i: (i, 0)),
        out_shape=jax.ShapeDtypeStruct((t, qw), F32),
        compiler_params=_cparams(("parallel",)),
        name="swa",
    )(sinks, proj, proj, proj, proj, proj, proj, proj, proj, cos_s, sin_s, cos_s, sin_s)


def _merge_kernel(x_ref, oa_ref, hg_ref, ob_ref, rg_ref, oc_ref, ga_ref, gb_ref, gc_ref,
                  hn_ref, wa_ref, wb_ref, wc_ref, wo_ref, o_ref):
    hg = hg_ref[...]
    a = _rms(oa_ref[...]) * hn_ref[...] * (hg * _sigmoid(hg))
    ya = _dot(a.astype(BF16), wa_ref[...])
    rg = rg_ref[...]
    ob = ob_ref[...]
    parts = []
    for h in range(N_HEADS):
        lanes = slice(h * HEAD_DIM, (h + 1) * HEAD_DIM)
        parts.append(_rms(ob[:, lanes]))
    bn = jnp.concatenate(parts, axis=-1) * (rg * _sigmoid(rg))
    yb = _dot(bn.astype(BF16), wb_ref[...])
    yc = _dot(oc_ref[...].astype(BF16), wc_ref[...])
    mixed = _sigmoid(ga_ref[...]) * ya + _sigmoid(gb_ref[...]) * yb + _sigmoid(gc_ref[...]) * yc
    o_ref[...] = x_ref[...] + _dot(mixed.astype(BF16), wo_ref[...])


def _merge(x, proj, oa, ob, oc, hn, wa, wb, wc, wo):
    t = x.shape[0]
    tt = min(t, 512)
    w = N_HEADS * HEAD_DIM
    qw = SWA_Q_HEADS * LANE

    def rows(width, cb=0):
        return pl.BlockSpec((tt, width), lambda i: (i, cb * LANE // width))

    def full(shape):
        return pl.BlockSpec(shape, lambda i: (0, 0))

    return pl.pallas_call(
        _merge_kernel,
        grid=(t // tt,),
        in_specs=[rows(D_MODEL), rows(w), rows(w, CB_HG), rows(w), rows(w, CB_RG), rows(qw),
                  rows(D_MODEL, CB_GATE), rows(D_MODEL, CB_GATE + 8), rows(D_MODEL, CB_GATE + 16),
                  full((1, w)), full((w, D_MODEL)), full((w, D_MODEL)), full((qw, D_MODEL)),
                  full((D_MODEL, D_MODEL))],
        out_specs=rows(D_MODEL),
        out_shape=jax.ShapeDtypeStruct((t, D_MODEL), F32),
        compiler_params=_cparams(("parallel",)),
        name="merge",
    )(x, oa, proj, ob, proj, oc, proj, proj, proj, hn, wa, wb, wc, wo)


def _memkv_kernel(m_ref, g_ref, wk_ref, wv_ref, k_ref, v_ref):
    h = (_rms(m_ref[...]) * g_ref[...]).astype(BF16)
    k_ref[...] = _dot(h, wk_ref[...]).astype(BF16)
    v_ref[...] = _dot(h, wv_ref[...]).astype(BF16)


def _memkv(mem, g, wk, wv):
    nm = mem.shape[0]
    out = jax.ShapeDtypeStruct((nm, D_MODEL), BF16)
    return pl.pallas_call(
        _memkv_kernel,
        out_shape=[out, out],
        compiler_params=pltpu.CompilerParams(vmem_limit_bytes=VMEM_LIMIT),
        name="mem_kv",
    )(mem, g, wk, wv)


def _xattn_kernel(x_ref, g_ref, wq_ref, k_ref, v_ref, wo_ref, o_ref):
    x = x_ref[...]
    h = (_rms(x) * g_ref[...]).astype(BF16)
    q = _dot(h, wq_ref[...])
    outs = []
    for hh in range(X_HEADS):
        lanes = slice(hh * X_DIM, (hh + 1) * X_DIM)
        s = _dot_nt(q[:, lanes].astype(BF16), k_ref[:, lanes]) * (X_DIM ** -0.5)
        m = jnp.max(s, axis=-1, keepdims=True)
        p = jnp.exp(s - m)
        p = p / jnp.sum(p, axis=-1, keepdims=True)
        outs.append(_dot(p.astype(BF16), v_ref[:, lanes]))
    o = jnp.concatenate(outs, axis=-1)
    o_ref[...] = x + _dot(o.astype(BF16), wo_ref[...])


def _xattn(x, g, wq, k, v, wo):
    t = x.shape[0]
    tt = min(t, 512)
    nm = k.shape[0]

    def full(shape):
        return pl.BlockSpec(shape, lambda i: (0, 0))

    rows = pl.BlockSpec((tt, D_MODEL), lambda i: (i, 0))
    return pl.pallas_call(
        _xattn_kernel,
        grid=(t // tt,),
        in_specs=[rows, full((1, D_MODEL)), full((D_MODEL, D_MODEL)),
                  full((nm, D_MODEL)), full((nm, D_MODEL)), full((D_MODEL, D_MODEL))],
        out_specs=rows,
        out_shape=jax.ShapeDtypeStruct((t, D_MODEL), F32),
        compiler_params=_cparams(("parallel",)),
        name="xattn",
    )(x, g, wq, k, v, wo)


_CAND_ROWS = [PEER_TOPK // (k1 + 1) for k1 in range(8)]


def _top_values(s, n, with_rank=False):
    vals = []
    rank = jnp.full(s.shape, float(n), F32)
    for i in range(n):
        m = jnp.max(s, axis=0, keepdims=True)
        vals.append(m)
        hit = s >= m
        if with_rank:
            rank = jnp.where(hit, float(i), rank)
        s = jnp.where(hit, -jnp.inf, s)
    return (vals, rank) if with_rank else vals


def _bf16_pair_bits(x):
    u = lax.bitcast_convert_type(x.astype(BF16).astype(F32), jnp.uint32)
    return u | (u >> 16)


def _stack_rows(rows, n):
    tt = rows[0].shape[1]
    r = lax.broadcasted_iota(jnp.int32, (n, tt), 0)
    out = jnp.zeros((n, tt), F32)
    for k in range(n):
        out = jnp.where(r == k, rows[k], out)
    return out


def _router_kernel(x_ref, g_ref, wq_ref, kh_ref, kl_ref,
                   hb_ref, a_ref, b_ref, r_ref, l_ref, qt_scr):
    h = _rms(x_ref[...]) * g_ref[...]
    hb_ref[...] = h.T.astype(BF16)
    qt_scr[...] = _dot_nt(wq_ref[...], h.astype(BF16))
    tt = x_ref.shape[0]
    k = PEER_TOPK
    r8 = lax.broadcasted_iota(jnp.int32, (8, tt), 0)

    def head_body(hh, carry):
        scores = []
        for p in range(2):
            r0 = pl.multiple_of(hh * (2 * PEER_HALF) + p * PEER_HALF, PEER_HALF)
            q_hi, q_lo = _split2(qt_scr[pl.ds(r0, PEER_HALF), :])
            kh = kh_ref[p * PEER_HEADS + hh]
            scores.append(_dot(kh, q_hi) + _dot(kl_ref[p * PEER_HEADS + hh], q_hi) + _dot(kh, q_lo))
        s1, s2 = scores
        v1, rank1 = _top_values(s1, k, with_rank=True)
        v2, rank2 = _top_values(s2, k, with_rank=True)
        sv1 = _stack_rows(v1, k)
        sv2 = _stack_rows(v2, k)
        groups = [v1[0] + sv2]
        for k1 in range(1, 8):
            groups.append(jnp.where(r8 < _CAND_ROWS[k1], v1[k1] + sv2[0:8], -jnp.inf))
        groups.append(sv1[8:16] + v2[0])
        cand = jnp.concatenate(groups, axis=0)
        tau = _top_values(cand, k)[k - 1]
        m0 = v1[0] + v2[0]
        z = jnp.sum(jnp.where(cand >= tau, jnp.exp(cand - m0), 0.0), axis=0, keepdims=True)
        a_ref[hh] = _bf16_pair_bits(jnp.exp(s1 - v1[0]))
        b_ref[hh] = (jnp.exp(s2 - v2[0]) / z).astype(BF16)
        count1 = jnp.zeros_like(s1)
        for k1 in range(k):
            n_k1 = jnp.sum((v1[k1] + sv2 >= tau).astype(F32), axis=0, keepdims=True)
            count1 = jnp.where(rank1 == float(k1), n_k1, count1)
        r_ref[hh] = rank2.astype(BF16)
        l_ref[hh] = _bf16_pair_bits(count1)
        return carry

    lax.fori_loop(0, PEER_HEADS, head_body, 0)


def _router(x, g, wq_t, k_hi, k_lo):
    t = x.shape[0]
    tt = min(t, 256)
    nq = wq_t.shape[0]
    sel = jax.ShapeDtypeStruct((PEER_HEADS, PEER_KEYS, t), jnp.uint32)
    sel_b = jax.ShapeDtypeStruct((PEER_HEADS, PEER_KEYS, t), BF16)
    sel_spec = pl.BlockSpec((PEER_HEADS, PEER_KEYS, tt), lambda i: (0, 0, i))
    return pl.pallas_call(
        _router_kernel,
        grid=(t // tt,),
        in_specs=[pl.BlockSpec((tt, D_MODEL), lambda i: (i, 0)),
                  pl.BlockSpec((1, D_MODEL), lambda i: (0, 0)),
                  pl.BlockSpec((nq, D_MODEL), lambda i: (0, 0)),
                  pl.BlockSpec((2 * PEER_HEADS, PEER_KEYS, PEER_HALF), lambda i: (0, 0, 0)),
                  pl.BlockSpec((2 * PEER_HEADS, PEER_KEYS, PEER_HALF), lambda i: (0, 0, 0))],
        out_specs=[pl.BlockSpec((D_MODEL, tt), lambda i: (0, i)), sel_spec, sel_spec, sel_spec, sel_spec],
        out_shape=[jax.ShapeDtypeStruct((D_MODEL, t), BF16), sel, sel_b, sel_b, sel],
        scratch_shapes=[pltpu.VMEM((nq, tt), F32)],
        compiler_params=_cparams(("parallel",)),
        name="peer_router",
    )(x, g, wq_t, k_hi, k_lo)


def _peer_kernel(x_ref, hb_ref, a_ref, l_ref, b_ref, r_ref, u_ref, vt_ref, gf_ref, o_ref,
                 acc_ref, ga_ref, *, final_norm):
    e = pl.program_id(1)

    @pl.when(e == 0)
    def _():
        acc_ref[...] = jnp.zeros_like(acc_ref)

    u_blk = pltpu.bitcast(u_ref[...], BF16)
    vt_blk = pltpu.bitcast(vt_ref[...], BF16)
    n_i1 = u_blk.shape[0] // PEER_KEYS
    sub = 16
    tt = hb_ref.shape[1]
    shape3 = (PEER_KEYS // sub, sub, tt)
    zero3 = jnp.zeros(shape3, BF16)
    piece = PEER_PIECE_KEYS * PEER_KEYS
    for p in range(n_i1 // PEER_PIECE_KEYS):
        for j in range(p * PEER_PIECE_KEYS, (p + 1) * PEER_PIECE_KEYS):
            gate = zero3
            for hh in range(PEER_HEADS):
                arow = pltpu.bitcast(jnp.broadcast_to(a_ref[hh, j:j + 1, :], (8, tt)), BF16)
                lrow = pltpu.bitcast(jnp.broadcast_to(l_ref[hh, j:j + 1, :], (8, tt)), BF16)
                b3 = b_ref[hh].reshape(shape3)
                r3 = r_ref[hh].reshape(shape3)
                gate = gate + arow[None] * jnp.where(r3 < lrow[None], b3, zero3)
            ga_ref[j * PEER_KEYS:(j + 1) * PEER_KEYS, :] = gate.reshape(PEER_KEYS, tt)
        rows = slice(p * piece, (p + 1) * piece)
        pre = _dot(u_blk[rows, :], hb_ref[...])
        act = pre + pre * lax.erf(pre)
        ga_ref[rows, :] = ga_ref[rows, :] * act.astype(BF16)
    acc_ref[...] += _dot(vt_blk, ga_ref[...])

    @pl.when(e == pl.num_programs(1) - 1)
    def _():
        y = x_ref[...] + acc_ref[...].T
        if final_norm:
            y = _rms(y) * gf_ref[...]
        o_ref[...] = y


def _pack_kernel(w_ref, o_ref, *, scale, transpose):
    w = w_ref[...]
    if transpose:
        w = w.T
    o_ref[...] = pltpu.bitcast((w * scale).astype(BF16), jnp.uint32)


def _pack_table(w, scale, transpose):
    rows, cols = w.shape
    if transpose:
        blk = 512
        in_spec = pl.BlockSpec((blk, cols), lambda i: (i, 0))
        out_spec = pl.BlockSpec((cols // 2, blk), lambda i: (0, i))
        out_shape = jax.ShapeDtypeStruct((cols // 2, rows), jnp.uint32)
    else:
        blk = 1024
        in_spec = pl.BlockSpec((blk, cols), lambda i: (i, 0))
        out_spec = pl.BlockSpec((blk // 2, cols), lambda i: (i, 0))
        out_shape = jax.ShapeDtypeStruct((rows // 2, cols), jnp.uint32)
    return pl.pallas_call(
        functools.partial(_pack_kernel, scale=scale, transpose=transpose),
        grid=(rows // blk,),
        in_specs=[in_spec],
        out_specs=out_spec,
        out_shape=out_shape,
        compiler_params=_cparams(("parallel",)),
        name="pack_table_t" if transpose else "pack_table",
    )(w)


def _peer(x, hb, a, l, b, r, u, vt, gf, final_norm):
    t = x.shape[0]
    tt = min(t, 512)
    eb = 16 * PEER_KEYS
    rows = pl.BlockSpec((tt, D_MODEL), lambda i, e: (i, 0))
    i1_rows = pl.BlockSpec((PEER_HEADS, eb // PEER_KEYS, tt), lambda i, e: (0, e, i))
    all_rows = pl.BlockSpec((PEER_HEADS, PEER_KEYS, tt), lambda i, e: (0, 0, i))
    return pl.pallas_call(
        functools.partial(_peer_kernel, final_norm=final_norm),
        grid=(t // tt, N_EXPERTS // eb),
        in_specs=[rows, pl.BlockSpec((D_MODEL, tt), lambda i, e: (0, i)),
                  i1_rows, i1_rows, all_rows, all_rows,
                  pl.BlockSpec((eb // 2, D_MODEL), lambda i, e: (e, 0)),
                  pl.BlockSpec((D_MODEL // 2, eb), lambda i, e: (0, e)),
                  pl.BlockSpec((1, D_MODEL), lambda i, e: (0, 0))],
        out_specs=rows,
        out_shape=jax.ShapeDtypeStruct((t, D_MODEL), F32),
        scratch_shapes=[pltpu.VMEM((D_MODEL, tt), F32), pltpu.VMEM((eb, tt), BF16)],
        compiler_params=_cparams(("parallel", "arbitrary")),
        name="peer_dense",
    )(x, hb, a, l, b, r, u, vt, gf)


def kernel(x, mem, positions, ln_mix, w_in, lb_param, hgrn_norm, swa_sinks, w_br_hgrn, w_br_ret,
           w_br_swa, w_out, ln_xq, ln_xkv, w_xq, w_xk, w_xv, w_xo, ln_ffn, peer_wq, peer_keys,
           peer_u, peer_v, ln_final):
    b, t, d = x.shape
    assert b == 1 and d == D_MODEL
    depth = w_in.shape[0]
    xs = x.reshape(t, d)
    ms = mem.reshape(mem.shape[1], d)
    cos_r, sin_r, cos_s, sin_s = _rope_tables(positions)
    lb_sm = jax.nn.softmax(lb_param.astype(F32), axis=0)
    lower = jnp.cumsum(lb_sm, axis=0) - lb_sm[0]

    for l in range(depth):
        proj = _proj(xs, ln_mix[l][None, :], _relay_w_in(w_in[l]))
        oa = _hgrn(proj, lower[l][None, :])
        ob = _ret(proj, cos_r, sin_r)
        oc = _swa(proj, cos_s, sin_s, swa_sinks[l])
        wc = jnp.pad(w_br_swa[l].reshape(SWA_Q_HEADS, SWA_DIM, d),
                     ((0, 0), (0, LANE - SWA_DIM), (0, 0))).reshape(SWA_Q_HEADS * LANE, d)
        xs = _merge(xs, proj, oa, ob, oc, hgrn_norm[l][None, :], w_br_hgrn[l].astype(BF16),
                    w_br_ret[l].astype(BF16), wc.astype(BF16), w_out[l].astype(BF16))

        km, vm = _memkv(ms, ln_xkv[l][None, :], w_xk[l].astype(BF16), w_xv[l].astype(BF16))
        xs = _xattn(xs, ln_xq[l][None, :], w_xq[l].astype(BF16), km, vm, w_xo[l].astype(BF16))

        k_hi, k_lo = _split2(peer_keys[l].reshape(2 * PEER_HEADS, PEER_KEYS, PEER_HALF))
        hb, a, bz, r2, cnt = _router(xs, ln_ffn[l][None, :], peer_wq[l].T.astype(BF16), k_hi, k_lo)
        c = 2.0 ** -0.5
        xs = _peer(xs, hb, a, cnt, bz, r2, _pack_table(peer_u[l], c, False), _pack_table(peer_v[l], c, True),
                   ln_final[None, :], final_norm=(l == depth - 1))

    return xs.reshape(b, t, d)
```

```python
import functools
import math

import numpy as np
import jax
import jax.numpy as jnp
from jax import lax
from jax.experimental import pallas as pl
from jax.experimental.pallas import tpu as pltpu

F32 = jnp.float32
BF16 = jnp.bfloat16

D_MODEL = 1024
HEAD_DIM = 128
N_HEADS = 4
SWA_Q_HEADS = 8
SWA_KV_HEADS = 2
SWA_DIM = 64
SWA_BLOCK = 128
ROPE_THETA = 10000.0
N_MEM = 256
X_HEADS = 4
X_DIM = D_MODEL // X_HEADS
PEER_HEADS = 8
PEER_KEYS = 128
PEER_TOPK = 16
PEER_HALF = 128
N_EXPERTS = PEER_KEYS * PEER_KEYS
EPS = 1e-6
NEG_BIG = -1e30
TINY = 1e-30

LANE = 128
HG_CHUNK = 64
RET_CHUNK = 256
PEER_PIECE_KEYS = 4
VMEM_LIMIT = 56 * 1024 * 1024

CB_GATE = 0
CB_HQ, CB_HF, CB_HI, CB_HG = 24, 28, 32, 36
CB_RQ, CB_RQR, CB_RK, CB_RKR, CB_RV, CB_RG = 40, 44, 48, 52, 56, 60
CB_SQ, CB_SQR, CB_SK, CB_SKR, CB_SV = 64, 72, 80, 82, 84
N_CB = 88
PROJ_COLS = N_CB * LANE


def _cparams(sem):
    return pltpu.CompilerParams(dimension_semantics=sem, vmem_limit_bytes=VMEM_LIMIT)


def _dot(a, b):
    return jnp.dot(a, b, preferred_element_type=F32)


def _dot_nt(a, b):
    return lax.dot_general(a, b, (((1,), (1,)), ((), ())), preferred_element_type=F32)


def _dot_tn(a, b):
    return lax.dot_general(a, b, (((0,), (0,)), ((), ())), preferred_element_type=F32)


def _split2(x):
    hi = x.astype(BF16)
    lo = (x - hi.astype(F32)).astype(BF16)
    return hi, lo


def _split3(x):
    hi = x.astype(BF16)
    r = x - hi.astype(F32)
    mid = r.astype(BF16)
    lo = (r - mid.astype(F32)).astype(BF16)
    return hi, mid, lo


def _sigmoid(x):
    return 1.0 / (1.0 + jnp.exp(-x))


def _rms(x):
    return x * lax.rsqrt(jnp.mean(x * x, axis=-1, keepdims=True) + EPS)


def _rope_kernel(pos_ref, fr_ref, fs_ref, cr_ref, sr_ref, cs_ref, ss_ref):
    pos = pos_ref[...].astype(F32)
    ang_r = pos * fr_ref[...]
    cr_ref[...] = jnp.cos(ang_r)
    sr_ref[...] = jnp.sin(ang_r)
    ang_s = pos * fs_ref[...]
    cs_ref[...] = jnp.cos(ang_s)
    ss_ref[...] = jnp.sin(ang_s)


def _rope_tables(positions):
    t = positions.shape[1]
    tt = min(t, 1024)
    ret_f = ROPE_THETA ** (-jnp.linspace(0.0, 1.0, HEAD_DIM // 2, dtype=F32))
    swa_f = ROPE_THETA ** (-jnp.arange(0, SWA_DIM, 2, dtype=F32) / SWA_DIM)
    fr = jnp.concatenate([ret_f, ret_f])[None, :]
    fs = jnp.concatenate([swa_f, swa_f, jnp.zeros((LANE - SWA_DIM,), F32)])[None, :]
    pos = positions.reshape(t, 1)
    tab = jax.ShapeDtypeStruct((t, LANE), F32)
    row = pl.BlockSpec((tt, LANE), lambda i: (i, 0))
    vec = pl.BlockSpec((1, LANE), lambda i: (0, 0))
    return pl.pallas_call(
        _rope_kernel,
        grid=(t // tt,),
        in_specs=[pl.BlockSpec((tt, 1), lambda i: (i, 0)), vec, vec],
        out_specs=[row, row, row, row],
        out_shape=[tab, tab, tab, tab],
        compiler_params=_cparams(("parallel",)),
        name="rope_tables",
    )(pos, fr, fs)


def _proj_kernel(x_ref, g_ref, w_ref, o_ref, h_scr):
    @pl.when(pl.program_id(1) == 0)
    def _():
        h_scr[...] = (_rms(x_ref[...]) * g_ref[...]).astype(BF16)

    o_ref[...] = _dot(h_scr[...], w_ref[...])


def _proj(x, g, w):
    t = x.shape[0]
    n = w.shape[1]
    tm = min(t, 1024)
    tn = 1024
    return pl.pallas_call(
        _proj_kernel,
        grid=(t // tm, n // tn),
        in_specs=[pl.BlockSpec((tm, D_MODEL), lambda i, j: (i, 0)),
                  pl.BlockSpec((1, D_MODEL), lambda i, j: (0, 0)),
                  pl.BlockSpec((D_MODEL, tn), lambda i, j: (0, j))],
        out_specs=pl.BlockSpec((tm, tn), lambda i, j: (i, j)),
        out_shape=jax.ShapeDtypeStruct((t, n), F32),
        scratch_shapes=[pltpu.VMEM((tm, D_MODEL), BF16)],
        compiler_params=_cparams(("parallel", "arbitrary")),
        name="in_proj",
    )(x, g, w)


def _relay_w_in(w_in):
    d = w_in.shape[0]
    sizes = (512,) * 8 + (512, 128, 128, 3 * D_MODEL)
    offs = np.concatenate([[0], np.cumsum(sizes)])
    hq, hf, hi, hg, rq, rk, rv, rg, sq, sk, sv, gates = [
        w_in[:, int(offs[i]):int(offs[i + 1])] for i in range(12)]

    def rot(w, nh, dh):
        w = w.reshape(d, nh, dh)
        return jnp.concatenate([-w[..., dh // 2:], w[..., :dh // 2]], axis=-1).reshape(d, nh * dh)

    def pad_heads(w, nh):
        w = w.reshape(d, nh, SWA_DIM)
        return jnp.pad(w, ((0, 0), (0, 0), (0, LANE - SWA_DIM))).reshape(d, nh * LANE)

    cols = [gates, hq, hf, hi, hg,
            rq, rot(rq, N_HEADS, HEAD_DIM), rk, rot(rk, N_HEADS, HEAD_DIM), rv, rg,
            pad_heads(sq, SWA_Q_HEADS), pad_heads(rot(sq, SWA_Q_HEADS, SWA_DIM), SWA_Q_HEADS),
            pad_heads(sk, SWA_KV_HEADS), pad_heads(rot(sk, SWA_KV_HEADS, SWA_DIM), SWA_KV_HEADS),
            pad_heads(sv, SWA_KV_HEADS),
            jnp.zeros((d, PROJ_COLS - (CB_SV + SWA_KV_HEADS) * LANE), w_in.dtype)]
    return jnp.concatenate(cols, axis=1).astype(BF16)


def _hgrn_kernel(hq_ref, hf_ref, hi_ref, lb_ref, o_ref, st_ref, *, n_chunks):
    @pl.when(pl.program_id(0) == 0)
    def _():
        st_ref[...] = jnp.zeros_like(st_ref)

    c = HG_CHUNK
    row = lax.broadcasted_iota(jnp.int32, (c, c), 0)
    col = lax.broadcasted_iota(jnp.int32, (c, c), 1)
    diag = row == col
    halves = [c >> (i + 1) for i in range(c.bit_length() - 1)]
    pair_mask = {b: (row // (2 * b) == col // (2 * b)) & (row % (2 * b) >= b) & (col % (2 * b) < b)
                 for b in halves}
    rw = lax.broadcasted_iota(jnp.int32, (c, HEAD_DIM), 0)
    r3 = lax.broadcasted_iota(jnp.int32, (c // 8, 8, HEAD_DIM), 1)
    scale = HEAD_DIM ** -0.5
    log2e = math.log2(math.e)

    def boundary(cum2, b):
        if b >= 8:
            ends = [cum2[s + b - 1:s + b, :] for s in range(0, c, 2 * b)]
            out = ends[-1]
            for i in range(len(ends) - 2, -1, -1):
                out = jnp.where(rw < (i + 1) * 2 * b, ends[i], out)
            return out
        c3 = cum2.reshape(c // 8, 8, HEAD_DIM)
        if b == 4:
            return jnp.broadcast_to(c3[:, 3:4, :], c3.shape).reshape(c, HEAD_DIM)
        if b == 2:
            lo = jnp.broadcast_to(c3[:, 1:2, :], c3.shape)
            hi = jnp.broadcast_to(c3[:, 5:6, :], c3.shape)
            return jnp.where(r3 < 4, lo, hi).reshape(c, HEAD_DIM)
        prev = pltpu.roll(cum2, 1, 0)
        return jnp.where(rw % 2 == 1, prev, cum2)

    def chunk_body(ci, carry):
        r0 = pl.multiple_of(ci * c, c)
        for h in range(N_HEADS):
            lanes = slice(h * HEAD_DIM, (h + 1) * HEAD_DIM)
            xq = hq_ref[pl.ds(r0, c), lanes]
            z = hf_ref[pl.ds(r0, c), lanes]
            v = hi_ref[pl.ds(r0, c), lanes]
            lb = lb_ref[:, lanes]
            omlb = 1.0 - lb
            q = xq * _sigmoid(xq) * scale
            f = lb + omlb * _sigmoid(z)
            logf = jnp.log(jnp.maximum(f, TINY))
            kk = omlb * _sigmoid(-z)
            cum2 = logf * log2e
            step = 1
            while step < c:
                cum2 = cum2 + jnp.where(rw >= step, pltpu.roll(cum2, step, 0), 0.0)
                step *= 2
            scores = jnp.where(diag, jnp.sum(q * kk, axis=-1, keepdims=True), 0.0)
            for b in halves:
                ref = boundary(cum2, b)
                qs = (q * jnp.exp2(jnp.minimum(cum2 - ref, 0.0))).astype(BF16)
                ks = (kk * jnp.exp2(jnp.minimum(ref - cum2, 0.0))).astype(BF16)
                scores = jnp.where(pair_mask[b], _dot_nt(qs, ks), scores)
            st = st_ref[h]
            qe = (q * jnp.exp2(cum2)).astype(BF16)
            o = _dot_nt(qe, st.astype(BF16)) + _dot(scores.astype(BF16), v.astype(BF16))
            o_ref[pl.ds(r0, c), lanes] = o
            last = cum2[c - 1:c, :]
            kl = (kk * jnp.exp2(last - cum2)).astype(BF16)
            st_ref[h] = jnp.exp2(last) * st + _dot_tn(v.astype(BF16), kl)
        return carry

    lax.fori_loop(0, n_chunks, chunk_body, 0, unroll=4)


def _hgrn(proj, lower):
    t = proj.shape[0]
    tb = min(t, 512)
    w = N_HEADS * HEAD_DIM

    def cols(cb):
        return pl.BlockSpec((tb, w), lambda i: (i, cb // N_HEADS))

    return pl.pallas_call(
        functools.partial(_hgrn_kernel, n_chunks=tb // HG_CHUNK),
        grid=(t // tb,),
        in_specs=[cols(CB_HQ), cols(CB_HF), cols(CB_HI),
                  pl.BlockSpec((1, w), lambda i: (0, 0))],
        out_specs=pl.BlockSpec((tb, w), lambda i: (i, 0)),
        out_shape=jax.ShapeDtypeStruct((t, w), F32),
        scratch_shapes=[pltpu.VMEM((N_HEADS, HEAD_DIM, HEAD_DIM), F32)],
        compiler_params=_cparams(("arbitrary",)),
        name="hgrn2",
    )(proj, proj, proj, lower)


def _ret_kernel(q_ref, qr_ref, k_ref, kr_ref, v_ref, cos_ref, sin_ref,
                dec_ref, qd_ref, kd_ref, cd_ref, o_ref, s_ref):
    @pl.when(pl.program_id(0) == 0)
    def _():
        s_ref[...] = jnp.zeros_like(s_ref)

    cos = cos_ref[...]
    sin = sin_ref[...]
    scale = HEAD_DIM ** -0.5
    for h in range(N_HEADS):
        lanes = slice(h * HEAD_DIM, (h + 1) * HEAD_DIM)
        q = q_ref[:, lanes] * cos + qr_ref[:, lanes] * sin
        k = (k_ref[:, lanes] * cos + kr_ref[:, lanes] * sin) * scale
        v = v_ref[:, lanes].astype(BF16)
        qb = q.astype(BF16)
        s = s_ref[h]
        scores = _dot_nt(qb, k.astype(BF16)) * dec_ref[h]
        o = _dot(qb, s.astype(BF16)) * qd_ref[h] + _dot(scores.astype(BF16), v)
        o_ref[:, lanes] = o
        s_ref[h] = cd_ref[h] * s + _dot_tn((k * kd_ref[h]).astype(BF16), v)


def _ret(proj, cos_r, sin_r):
    t = proj.shape[0]
    c = min(t, RET_CHUNK)
    w = N_HEADS * HEAD_DIM
    log_gamma = jnp.log(1.0 - 2.0 ** (-5.0 - jnp.arange(N_HEADS, dtype=F32)))
    idx = jnp.arange(c, dtype=F32)
    rel = idx[:, None] - idx[None, :]
    decay = jnp.exp(jnp.where(rel >= 0, log_gamma[:, None, None] * rel, NEG_BIG))
    ones = jnp.ones((1, 1, HEAD_DIM), F32)
    q_decay = jnp.exp(log_gamma[:, None] * (idx + 1.0))[:, :, None] * ones
    k_decay = jnp.exp(log_gamma[:, None] * (c - 1.0 - idx))[:, :, None] * ones
    c_decay = jnp.exp(log_gamma * c)[:, None, None] * ones

    def cols(cb):
        return pl.BlockSpec((c, w), lambda i: (i, cb // N_HEADS))

    def const(shape):
        return pl.BlockSpec(shape, lambda i: (0, 0, 0))

    tab = pl.BlockSpec((c, LANE), lambda i: (i, 0))
    return pl.pallas_call(
        _ret_kernel,
        grid=(t // c,),
        in_specs=[cols(CB_RQ), cols(CB_RQR), cols(CB_RK), cols(CB_RKR), cols(CB_RV), tab, tab,
                  const((N_HEADS, c, c)), const((N_HEADS, c, HEAD_DIM)),
                  const((N_HEADS, c, HEAD_DIM)), const((N_HEADS, 1, HEAD_DIM))],
        out_specs=pl.BlockSpec((c, w), lambda i: (i, 0)),
        out_shape=jax.ShapeDtypeStruct((t, w), F32),
        scratch_shapes=[pltpu.VMEM((N_HEADS, HEAD_DIM, HEAD_DIM), F32)],
        compiler_params=_cparams(("arbitrary",)),
        name="retention",
    )(proj, proj, proj, proj, proj, cos_r, sin_r, decay, q_decay, k_decay, c_decay)


def _swa_kernel(sink_ref, q_ref, qr_ref, kc_ref, krc_ref, vc_ref, kp_ref, krp_ref, vp_ref,
                cc_ref, sc_ref, cp_ref, sp_ref, o_ref):
    b = SWA_BLOCK
    blk = pl.program_id(0)
    cos_c, sin_c = cc_ref[...], sc_ref[...]
    cos_p, sin_p = cp_ref[...], sp_ref[...]
    qi = lax.broadcasted_iota(jnp.int32, (b, 2 * b), 0) + b
    ki = lax.broadcasted_iota(jnp.int32, (b, 2 * b), 1)
    rel = qi - ki
    keep = (rel >= 0) & (rel < b) & ((blk > 0) | (ki >= b))
    group = SWA_Q_HEADS // SWA_KV_HEADS
    for g in range(SWA_KV_HEADS):
        kl = slice(g * LANE, (g + 1) * LANE)
        k_cur = kc_ref[:, kl] * cos_c + krc_ref[:, kl] * sin_c
        k_prev = kp_ref[:, kl] * cos_p + krp_ref[:, kl] * sin_p
        kw = jnp.concatenate([k_prev, k_cur], axis=0).astype(BF16)
        vw = jnp.concatenate([vp_ref[:, kl], vc_ref[:, kl]], axis=0).astype(BF16)
        for j in range(group):
            h = g * group + j
            ql = slice(h * LANE, (h + 1) * LANE)
            q = (q_ref[:, ql] * cos_c + qr_ref[:, ql] * sin_c).astype(BF16)
            s = _dot_nt(q, kw) * (SWA_DIM ** -0.5)
            s = jnp.where(keep, s, NEG_BIG)
            sink = sink_ref[h]
            m = jnp.maximum(jnp.max(s, axis=-1, keepdims=True), sink)
            p = jnp.exp(s - m)
            denom = jnp.sum(p, axis=-1, keepdims=True) + jnp.exp(sink - m)
            o_ref[:, ql] = _dot((p / denom).astype(BF16), vw)


def _swa(proj, cos_s, sin_s, sinks):
    t = proj.shape[0]
    b = SWA_BLOCK
    qw = SWA_Q_HEADS * LANE
    kw = SWA_KV_HEADS * LANE

    def cur(cb, width):
        return pl.BlockSpec((b, width), lambda i: (i, cb * LANE // width))

    def prev(cb, width):
        return pl.BlockSpec((b, width), lambda i: (jnp.maximum(i - 1, 0), cb * LANE // width))

    tab_c = pl.BlockSpec((b, LANE), lambda i: (i, 0))
    tab_p = pl.BlockSpec((b, LANE), lambda i: (jnp.maximum(i - 1, 0), 0))
    return pl.pallas_call(
        _swa_kernel,
        grid=(t // b,),
        in_specs=[pl.BlockSpec(memory_space=pltpu.SMEM),
                  cur(CB_SQ, qw), cur(CB_SQR, qw),
                  cur(CB_SK, kw), cur(CB_SKR, kw), cur(CB_SV, kw),
                  prev(CB_SK, kw), prev(CB_SKR, kw), prev(CB_SV, kw),
                  tab_c, tab_c, tab_p, tab_p],
        out_specs=pl.BlockSpec((b, qw), lambda i: (i, 0)),
        out_shape=jax.ShapeDtypeStruct((t, qw), F32),
        compiler_params=_cparams(("parallel",)),
        name="swa",
    )(sinks, proj, proj, proj, proj, proj, proj, proj, proj, cos_s, sin_s, cos_s, sin_s)


def _merge_kernel(x_ref, oa_ref, hg_ref, ob_ref, rg_ref, oc_ref, ga_ref, gb_ref, gc_ref,
                  hn_ref, wa_ref, wb_ref, wc_ref, wo_ref, o_ref):
    hg = hg_ref[...]
    a = _rms(oa_ref[...]) * hn_ref[...] * (hg * _sigmoid(hg))
    ya = _dot(a.astype(BF16), wa_ref[...])
    rg = rg_ref[...]
    ob = ob_ref[...]
    parts = []
    for h in range(N_HEADS):
        lanes = slice(h * HEAD_DIM, (h + 1) * HEAD_DIM)
        parts.append(_rms(ob[:, lanes]))
    bn = jnp.concatenate(parts, axis=-1) * (rg * _sigmoid(rg))
    yb = _dot(bn.astype(BF16), wb_ref[...])
    yc = _dot(oc_ref[...].astype(BF16), wc_ref[...])
    mixed = _sigmoid(ga_ref[...]) * ya + _sigmoid(gb_ref[...]) * yb + _sigmoid(gc_ref[...]) * yc
    o_ref[...] = x_ref[...] + _dot(mixed.astype(BF16), wo_ref[...])


def _merge(x, proj, oa, ob, oc, hn, wa, wb, wc, wo):
    t = x.shape[0]
    tt = min(t, 512)
    w = N_HEADS * HEAD_DIM
    qw = SWA_Q_HEADS * LANE

    def rows(width, cb=0):
        return pl.BlockSpec((tt, width), lambda i: (i, cb * LANE // width))

    def full(shape):
        return pl.BlockSpec(shape, lambda i: (0, 0))

    return pl.pallas_call(
        _merge_kernel,
        grid=(t // tt,),
        in_specs=[rows(D_MODEL), rows(w), rows(w, CB_HG), rows(w), rows(w, CB_RG), rows(qw),
                  rows(D_MODEL, CB_GATE), rows(D_MODEL, CB_GATE + 8), rows(D_MODEL, CB_GATE + 16),
                  full((1, w)), full((w, D_MODEL)), full((w, D_MODEL)), full((qw, D_MODEL)),
                  full((D_MODEL, D_MODEL))],
        out_specs=rows(D_MODEL),
        out_shape=jax.ShapeDtypeStruct((t, D_MODEL), F32),
        compiler_params=_cparams(("parallel",)),
        name="merge",
    )(x, oa, proj, ob, proj, oc, proj, proj, proj, hn, wa, wb, wc, wo)


def _memkv_kernel(m_ref, g_ref, wk_ref, wv_ref, k_ref, v_ref):
    h = (_rms(m_ref[...]) * g_ref[...]).astype(BF16)
    k_ref[...] = _dot(h, wk_ref[...]).astype(BF16)
    v_ref[...] = _dot(h, wv_ref[...]).astype(BF16)


def _memkv(mem, g, wk, wv):
    nm = mem.shape[0]
    out = jax.ShapeDtypeStruct((nm, D_MODEL), BF16)
    return pl.pallas_call(
        _memkv_kernel,
        out_shape=[out, out],
        compiler_params=pltpu.CompilerParams(vmem_limit_bytes=VMEM_LIMIT),
        name="mem_kv",
    )(mem, g, wk, wv)


def _xattn_kernel(x_ref, g_ref, wq_ref, k_ref, v_ref, wo_ref, o_ref):
    x = x_ref[...]
    h = (_rms(x) * g_ref[...]).astype(BF16)
    q = _dot(h, wq_ref[...])
    outs = []
    for hh in range(X_HEADS):
        lanes = slice(hh * X_DIM, (hh + 1) * X_DIM)
        s = _dot_nt(q[:, lanes].astype(BF16), k_ref[:, lanes]) * (X_DIM ** -0.5)
        m = jnp.max(s, axis=-1, keepdims=True)
        p = jnp.exp(s - m)
        p = p / jnp.sum(p, axis=-1, keepdims=True)
        outs.append(_dot(p.astype(BF16), v_ref[:, lanes]))
    o = jnp.concatenate(outs, axis=-1)
    o_ref[...] = x + _dot(o.astype(BF16), wo_ref[...])


def _xattn(x, g, wq, k, v, wo):
    t = x.shape[0]
    tt = min(t, 512)
    nm = k.shape[0]

    def full(shape):
        return pl.BlockSpec(shape, lambda i: (0, 0))

    rows = pl.BlockSpec((tt, D_MODEL), lambda i: (i, 0))
    return pl.pallas_call(
        _xattn_kernel,
        grid=(t // tt,),
        in_specs=[rows, full((1, D_MODEL)), full((D_MODEL, D_MODEL)),
                  full((nm, D_MODEL)), full((nm, D_MODEL)), full((D_MODEL, D_MODEL))],
        out_specs=rows,
        out_shape=jax.ShapeDtypeStruct((t, D_MODEL), F32),
        compiler_params=_cparams(("parallel",)),
        name="xattn",
    )(x, g, wq, k, v, wo)


_CAND_ROWS = [PEER_TOPK // (k1 + 1) for k1 in range(8)]


def _top_values(s, n, with_rank=False):
    vals = []
    rank = jnp.full(s.shape, float(n), F32)
    for i in range(n):
        m = jnp.max(s, axis=0, keepdims=True)
        vals.append(m)
        hit = s >= m
        if with_rank:
            rank = jnp.where(hit, float(i), rank)
        s = jnp.where(hit, -jnp.inf, s)
    return (vals, rank) if with_rank else vals


def _bf16_pair_bits(x):
    u = lax.bitcast_convert_type(x.astype(BF16).astype(F32), jnp.uint32)
    return u | (u >> 16)


def _stack_rows(rows, n):
    tt = rows[0].shape[1]
    r = lax.broadcasted_iota(jnp.int32, (n, tt), 0)
    out = jnp.zeros((n, tt), F32)
    for k in range(n):
        out = jnp.where(r == k, rows[k], out)
    return out


def _router_kernel(x_ref, g_ref, wq_ref, kh_ref, kl_ref,
                   hb_ref, a_ref, b_ref, r_ref, l_ref, qt_scr):
    h = _rms(x_ref[...]) * g_ref[...]
    hb_ref[...] = h.T.astype(BF16)
    qt_scr[...] = _dot_nt(wq_ref[...], h.astype(BF16))
    tt = x_ref.shape[0]
    k = PEER_TOPK
    r8 = lax.broadcasted_iota(jnp.int32, (8, tt), 0)

    def head_body(hh, carry):
        scores = []
        for p in range(2):
            r0 = pl.multiple_of(hh * (2 * PEER_HALF) + p * PEER_HALF, PEER_HALF)
            q_hi, q_lo = _split2(qt_scr[pl.ds(r0, PEER_HALF), :])
            kh = kh_ref[p * PEER_HEADS + hh]
            scores.append(_dot(kh, q_hi) + _dot(kl_ref[p * PEER_HEADS + hh], q_hi) + _dot(kh, q_lo))
        s1, s2 = scores
        v1, rank1 = _top_values(s1, k, with_rank=True)
        v2, rank2 = _top_values(s2, k, with_rank=True)
        sv1 = _stack_rows(v1, k)
        sv2 = _stack_rows(v2, k)
        groups = [v1[0] + sv2]
        for k1 in range(1, 8):
            groups.append(jnp.where(r8 < _CAND_ROWS[k1], v1[k1] + sv2[0:8], -jnp.inf))
        groups.append(sv1[8:16] + v2[0])
        cand = jnp.concatenate(groups, axis=0)
        tau = _top_values(cand, k)[k - 1]
        m0 = v1[0] + v2[0]
        z = jnp.sum(jnp.where(cand >= tau, jnp.exp(cand - m0), 0.0), axis=0, keepdims=True)
        a_ref[hh] = _bf16_pair_bits(jnp.exp(s1 - v1[0]))
        b_ref[hh] = (jnp.exp(s2 - v2[0]) / z).astype(BF16)
        count1 = jnp.zeros_like(s1)
        for k1 in range(k):
            n_k1 = jnp.sum((v1[k1] + sv2 >= tau).astype(F32), axis=0, keepdims=True)
            count1 = jnp.where(rank1 == float(k1), n_k1, count1)
        r_ref[hh] = rank2.astype(BF16)
        l_ref[hh] = _bf16_pair_bits(count1)
        return carry

    lax.fori_loop(0, PEER_HEADS, head_body, 0)


def _router(x, g, wq_t, k_hi, k_lo):
    t = x.shape[0]
    tt = min(t, 512)
    nq = wq_t.shape[0]
    sel = jax.ShapeDtypeStruct((PEER_HEADS, PEER_KEYS, t), jnp.uint32)
    sel_b = jax.ShapeDtypeStruct((PEER_HEADS, PEER_KEYS, t), BF16)
    sel_spec = pl.BlockSpec((PEER_HEADS, PEER_KEYS, tt), lambda i: (0, 0, i))
    return pl.pallas_call(
        _router_kernel,
        grid=(t // tt,),
        in_specs=[pl.BlockSpec((tt, D_MODEL), lambda i: (i, 0)),
                  pl.BlockSpec((1, D_MODEL), lambda i: (0, 0)),
                  pl.BlockSpec((nq, D_MODEL), lambda i: (0, 0)),
                  pl.BlockSpec((2 * PEER_HEADS, PEER_KEYS, PEER_HALF), lambda i: (0, 0, 0)),
                  pl.BlockSpec((2 * PEER_HEADS, PEER_KEYS, PEER_HALF), lambda i: (0, 0, 0))],
        out_specs=[pl.BlockSpec((D_MODEL, tt), lambda i: (0, i)), sel_spec, sel_spec, sel_spec, sel_spec],
        out_shape=[jax.ShapeDtypeStruct((D_MODEL, t), BF16), sel, sel_b, sel_b, sel],
        scratch_shapes=[pltpu.VMEM((nq, tt), F32)],
        compiler_params=_cparams(("parallel",)),
        name="peer_router",
    )(x, g, wq_t, k_hi, k_lo)


def _peer_kernel(x_ref, hb_ref, a_ref, l_ref, b_ref, r_ref, u_ref, vt_ref, gf_ref, o_ref,
                 acc_ref, ga_ref, *, final_norm):
    e = pl.program_id(1)

    @pl.when(e == 0)
    def _():
        acc_ref[...] = jnp.zeros_like(acc_ref)

    u_blk = pltpu.bitcast(u_ref[...], BF16)
    vt_blk = pltpu.bitcast(vt_ref[...], BF16)
    n_i1 = u_blk.shape[0] // PEER_KEYS
    sub = 16
    tt = hb_ref.shape[1]
    shape3 = (PEER_KEYS // sub, sub, tt)
    zero3 = jnp.zeros(shape3, BF16)
    piece = PEER_PIECE_KEYS * PEER_KEYS
    for p in range(n_i1 // PEER_PIECE_KEYS):
        for j in range(p * PEER_PIECE_KEYS, (p + 1) * PEER_PIECE_KEYS):
            gate = zero3
            for hh in range(PEER_HEADS):
                arow = pltpu.bitcast(jnp.broadcast_to(a_ref[hh, j:j + 1, :], (8, tt)), BF16)
                lrow = pltpu.bitcast(jnp.broadcast_to(l_ref[hh, j:j + 1, :], (8, tt)), BF16)
                b3 = b_ref[hh].reshape(shape3)
                r3 = r_ref[hh].reshape(shape3)
                gate = gate + arow[None] * jnp.where(r3 < lrow[None], b3, zero3)
            ga_ref[j * PEER_KEYS:(j + 1) * PEER_KEYS, :] = gate.reshape(PEER_KEYS, tt)
        rows = slice(p * piece, (p + 1) * piece)
        pre = _dot(u_blk[rows, :], hb_ref[...])
        act = pre + pre * lax.erf(pre)
        ga_ref[rows, :] = ga_ref[rows, :] * act.astype(BF16)
    acc_ref[...] += _dot(vt_blk, ga_ref[...])

    @pl.when(e == pl.num_programs(1) - 1)
    def _():
        y = x_ref[...] + acc_ref[...].T
        if final_norm:
            y = _rms(y) * gf_ref[...]
        o_ref[...] = y


def _pack_kernel(w_ref, o_ref, *, scale, transpose):
    w = w_ref[...]
    if transpose:
        w = w.T
    o_ref[...] = pltpu.bitcast((w * scale).astype(BF16), jnp.uint32)


def _pack_table(w, scale, transpose):
    rows, cols = w.shape
    if transpose:
        blk = 512
        in_spec = pl.BlockSpec((blk, cols), lambda i: (i, 0))
        out_spec = pl.BlockSpec((cols // 2, blk), lambda i: (0, i))
        out_shape = jax.ShapeDtypeStruct((cols // 2, rows), jnp.uint32)
    else:
        blk = 1024
        in_spec = pl.BlockSpec((blk, cols), lambda i: (i, 0))
        out_spec = pl.BlockSpec((blk // 2, cols), lambda i: (i, 0))
        out_shape = jax.ShapeDtypeStruct((rows // 2, cols), jnp.uint32)
    return pl.pallas_call(
        functools.partial(_pack_kernel, scale=scale, transpose=transpose),
        grid=(rows // blk,),
        in_specs=[in_spec],
        out_specs=out_spec,
        out_shape=out_shape,
        compiler_params=_cparams(("parallel",)),
        name="pack_table_t" if transpose else "pack_table",
    )(w)


def _peer(x, hb, a, l, b, r, u, vt, gf, final_norm):
    t = x.shape[0]
    tt = min(t, 512)
    eb = 16 * PEER_KEYS
    rows = pl.BlockSpec((tt, D_MODEL), lambda i, e: (i, 0))
    i1_rows = pl.BlockSpec((PEER_HEADS, eb // PEER_KEYS, tt), lambda i, e: (0, e, i))
    all_rows = pl.BlockSpec((PEER_HEADS, PEER_KEYS, tt), lambda i, e: (0, 0, i))
    return pl.pallas_call(
        functools.partial(_peer_kernel, final_norm=final_norm),
        grid=(t // tt, N_EXPERTS // eb),
        in_specs=[rows, pl.BlockSpec((D_MODEL, tt), lambda i, e: (0, i)),
                  i1_rows, i1_rows, all_rows, all_rows,
                  pl.BlockSpec((eb // 2, D_MODEL), lambda i, e: (e, 0)),
                  pl.BlockSpec((D_MODEL // 2, eb), lambda i, e: (0, e)),
                  pl.BlockSpec((1, D_MODEL), lambda i, e: (0, 0))],
        out_specs=rows,
        out_shape=jax.ShapeDtypeStruct((t, D_MODEL), F32),
        scratch_shapes=[pltpu.VMEM((D_MODEL, tt), F32), pltpu.VMEM((eb, tt), BF16)],
        compiler_params=_cparams(("parallel", "arbitrary")),
        name="peer_dense",
    )(x, hb, a, l, b, r, u, vt, gf)


def kernel(x, mem, positions, ln_mix, w_in, lb_param, hgrn_norm, swa_sinks, w_br_hgrn, w_br_ret,
           w_br_swa, w_out, ln_xq, ln_xkv, w_xq, w_xk, w_xv, w_xo, ln_ffn, peer_wq, peer_keys,
           peer_u, peer_v, ln_final):
    b, t, d = x.shape
    assert b == 1 and d == D_MODEL
    depth = w_in.shape[0]
    xs = x.reshape(t, d)
    ms = mem.reshape(mem.shape[1], d)
    cos_r, sin_r, cos_s, sin_s = _rope_tables(positions)
    lb_sm = jax.nn.softmax(lb_param.astype(F32), axis=0)
    lower = jnp.cumsum(lb_sm, axis=0) - lb_sm[0]

    for l in range(depth):
        proj = _proj(xs, ln_mix[l][None, :], _relay_w_in(w_in[l]))
        oa = _hgrn(proj, lower[l][None, :])
        ob = _ret(proj, cos_r, sin_r)
        oc = _swa(proj, cos_s, sin_s, swa_sinks[l])
        wc = jnp.pad(w_br_swa[l].reshape(SWA_Q_HEADS, SWA_DIM, d),
                     ((0, 0), (0, LANE - SWA_DIM), (0, 0))).reshape(SWA_Q_HEADS * LANE, d)
        xs = _merge(xs, proj, oa, ob, oc, hgrn_norm[l][None, :], w_br_hgrn[l].astype(BF16),
                    w_br_ret[l].astype(BF16), wc.astype(BF16), w_out[l].astype(BF16))

        km, vm = _memkv(ms, ln_xkv[l][None, :], w_xk[l].astype(BF16), w_xv[l].astype(BF16))
        xs = _xattn(xs, ln_xq[l][None, :], w_xq[l].astype(BF16), km, vm, w_xo[l].astype(BF16))

        k_hi, k_lo = _split2(peer_keys[l].reshape(2 * PEER_HEADS, PEER_KEYS, PEER_HALF))
        hb, a, bz, r2, cnt = _router(xs, ln_ffn[l][None, :], peer_wq[l].T.astype(BF16), k_hi, k_lo)
        c = 2.0 ** -0.5
        xs = _peer(xs, hb, a, cnt, bz, r2, _pack_table(peer_u[l], c, False), _pack_table(peer_v[l], c, True),
                   ln_final[None, :], final_norm=(l == depth - 1))

    return xs.reshape(b, t, d)
```

```python
import functools
import math

import numpy as np
import jax
import jax.numpy as jnp
from jax import lax
from jax.experimental import pallas as pl
from jax.experimental.pallas import tpu as pltpu

F32 = jnp.float32
BF16 = jnp.bfloat16

D_MODEL = 1024
HEAD_DIM = 128
N_HEADS = 4
SWA_Q_HEADS = 8
SWA_KV_HEADS = 2
SWA_DIM = 64
SWA_BLOCK = 128
ROPE_THETA = 10000.0
N_MEM = 256
X_HEADS = 4
X_DIM = D_MODEL // X_HEADS
PEER_HEADS = 8
PEER_KEYS = 128
PEER_TOPK = 16
PEER_HALF = 128
N_EXPERTS = PEER_KEYS * PEER_KEYS
EPS = 1e-6
NEG_BIG = -1e30
TINY = 1e-30

LANE = 128
HG_CHUNK = 64
RET_CHUNK = 256
PEER_PIECE_KEYS = 4
VMEM_LIMIT = 56 * 1024 * 1024

CB_GATE = 0
CB_HQ, CB_HF, CB_HI, CB_HG = 24, 28, 32, 36
CB_RQ, CB_RQR, CB_RK, CB_RKR, CB_RV, CB_RG = 40, 44, 48, 52, 56, 60
CB_SQ, CB_SQR, CB_SK, CB_SKR, CB_SV = 64, 72, 80, 82, 84
N_CB = 88
PROJ_COLS = N_CB * LANE


def _cparams(sem):
    return pltpu.CompilerParams(dimension_semantics=sem, vmem_limit_bytes=VMEM_LIMIT)


def _dot(a, b):
    return jnp.dot(a, b, preferred_element_type=F32)


def _dot_nt(a, b):
    return lax.dot_general(a, b, (((1,), (1,)), ((), ())), preferred_element_type=F32)


def _dot_tn(a, b):
    return lax.dot_general(a, b, (((0,), (0,)), ((), ())), preferred_element_type=F32)


def _split2(x):
    hi = x.astype(BF16)
    lo = (x - hi.astype(F32)).astype(BF16)
    return hi, lo


def _split3(x):
    hi = x.astype(BF16)
    r = x - hi.astype(F32)
    mid = r.astype(BF16)
    lo = (r - mid.astype(F32)).astype(BF16)
    return hi, mid, lo


def _sigmoid(x):
    return 1.0 / (1.0 + jnp.exp(-x))


def _rms(x):
    return x * lax.rsqrt(jnp.mean(x * x, axis=-1, keepdims=True) + EPS)


def _rope_kernel(pos_ref, fr_ref, fs_ref, cr_ref, sr_ref, cs_ref, ss_ref):
    pos = pos_ref[...].astype(F32)
    ang_r = pos * fr_ref[...]
    cr_ref[...] = jnp.cos(ang_r)
    sr_ref[...] = jnp.sin(ang_r)
    ang_s = pos * fs_ref[...]
    cs_ref[...] = jnp.cos(ang_s)
    ss_ref[...] = jnp.sin(ang_s)


def _rope_tables(positions):
    t = positions.shape[1]
    tt = min(t, 1024)
    ret_f = ROPE_THETA ** (-jnp.linspace(0.0, 1.0, HEAD_DIM // 2, dtype=F32))
    swa_f = ROPE_THETA ** (-jnp.arange(0, SWA_DIM, 2, dtype=F32) / SWA_DIM)
    fr = jnp.concatenate([ret_f, ret_f])[None, :]
    fs = jnp.concatenate([swa_f, swa_f, jnp.zeros((LANE - SWA_DIM,), F32)])[None, :]
    pos = positions.reshape(t, 1)
    tab = jax.ShapeDtypeStruct((t, LANE), F32)
    row = pl.BlockSpec((tt, LANE), lambda i: (i, 0))
    vec = pl.BlockSpec((1, LANE), lambda i: (0, 0))
    return pl.pallas_call(
        _rope_kernel,
        grid=(t // tt,),
        in_specs=[pl.BlockSpec((tt, 1), lambda i: (i, 0)), vec, vec],
        out_specs=[row, row, row, row],
        out_shape=[tab, tab, tab, tab],
        compiler_params=_cparams(("parallel",)),
        name="rope_tables",
    )(pos, fr, fs)


PROJ_TN = 1024


def _proj_kernel(x_ref, g_ref, w_ref, o_ref, f_ref, h_scr):
    @pl.when(pl.program_id(1) == 0)
    def _():
        h_scr[...] = (_rms(x_ref[...]) * g_ref[...]).astype(BF16)

    y = _dot(h_scr[...], w_ref[...])
    o_ref[...] = y.astype(BF16)

    @pl.when(pl.program_id(1) == CB_HF * LANE // PROJ_TN)
    def _():
        lo = CB_HF * LANE % PROJ_TN
        f_ref[...] = y[:, lo:lo + N_HEADS * HEAD_DIM]


def _proj(x, g, w):
    t = x.shape[0]
    n = w.shape[1]
    tm = min(t, 1024)
    tn = PROJ_TN
    wf = N_HEADS * HEAD_DIM
    return pl.pallas_call(
        _proj_kernel,
        grid=(t // tm, n // tn),
        in_specs=[pl.BlockSpec((tm, D_MODEL), lambda i, j: (i, 0)),
                  pl.BlockSpec((1, D_MODEL), lambda i, j: (0, 0)),
                  pl.BlockSpec((D_MODEL, tn), lambda i, j: (0, j))],
        out_specs=[pl.BlockSpec((tm, tn), lambda i, j: (i, j)),
                   pl.BlockSpec((tm, wf), lambda i, j: (i, 0))],
        out_shape=[jax.ShapeDtypeStruct((t, n), BF16), jax.ShapeDtypeStruct((t, wf), F32)],
        scratch_shapes=[pltpu.VMEM((tm, D_MODEL), BF16)],
        compiler_params=_cparams(("parallel", "arbitrary")),
        name="in_proj",
    )(x, g, w)


def _relay_w_in(w_in):
    d = w_in.shape[0]
    sizes = (512,) * 8 + (512, 128, 128, 3 * D_MODEL)
    offs = np.concatenate([[0], np.cumsum(sizes)])
    hq, hf, hi, hg, rq, rk, rv, rg, sq, sk, sv, gates = [
        w_in[:, int(offs[i]):int(offs[i + 1])] for i in range(12)]

    def rot(w, nh, dh):
        w = w.reshape(d, nh, dh)
        return jnp.concatenate([-w[..., dh // 2:], w[..., :dh // 2]], axis=-1).reshape(d, nh * dh)

    def pad_heads(w, nh):
        w = w.reshape(d, nh, SWA_DIM)
        return jnp.pad(w, ((0, 0), (0, 0), (0, LANE - SWA_DIM))).reshape(d, nh * LANE)

    cols = [gates, hq, hf, hi, hg,
            rq, rot(rq, N_HEADS, HEAD_DIM), rk, rot(rk, N_HEADS, HEAD_DIM), rv, rg,
            pad_heads(sq, SWA_Q_HEADS), pad_heads(rot(sq, SWA_Q_HEADS, SWA_DIM), SWA_Q_HEADS),
            pad_heads(sk, SWA_KV_HEADS), pad_heads(rot(sk, SWA_KV_HEADS, SWA_DIM), SWA_KV_HEADS),
            pad_heads(sv, SWA_KV_HEADS),
            jnp.zeros((d, PROJ_COLS - (CB_SV + SWA_KV_HEADS) * LANE), w_in.dtype)]
    return jnp.concatenate(cols, axis=1).astype(BF16)


def _hgrn_kernel(hq_ref, hf_ref, hi_ref, lb_ref, o_ref, st_ref, *, n_chunks):
    @pl.when(pl.program_id(0) == 0)
    def _():
        st_ref[...] = jnp.zeros_like(st_ref)

    c = HG_CHUNK
    row = lax.broadcasted_iota(jnp.int32, (c, c), 0)
    col = lax.broadcasted_iota(jnp.int32, (c, c), 1)
    diag = row == col
    halves = [c >> (i + 1) for i in range(c.bit_length() - 1)]
    pair_mask = {b: (row // (2 * b) == col // (2 * b)) & (row % (2 * b) >= b) & (col % (2 * b) < b)
                 for b in halves}
    rw = lax.broadcasted_iota(jnp.int32, (c, HEAD_DIM), 0)
    r3 = lax.broadcasted_iota(jnp.int32, (c // 8, 8, HEAD_DIM), 1)
    scale = HEAD_DIM ** -0.5
    log2e = math.log2(math.e)

    def boundary(cum2, b):
        if b >= 8:
            ends = [cum2[s + b - 1:s + b, :] for s in range(0, c, 2 * b)]
            out = ends[-1]
            for i in range(len(ends) - 2, -1, -1):
                out = jnp.where(rw < (i + 1) * 2 * b, ends[i], out)
            return out
        c3 = cum2.reshape(c // 8, 8, HEAD_DIM)
        if b == 4:
            return jnp.broadcast_to(c3[:, 3:4, :], c3.shape).reshape(c, HEAD_DIM)
        if b == 2:
            lo = jnp.broadcast_to(c3[:, 1:2, :], c3.shape)
            hi = jnp.broadcast_to(c3[:, 5:6, :], c3.shape)
            return jnp.where(r3 < 4, lo, hi).reshape(c, HEAD_DIM)
        prev = pltpu.roll(cum2, 1, 0)
        return jnp.where(rw % 2 == 1, prev, cum2)

    def chunk_body(ci, carry):
        r0 = pl.multiple_of(ci * c, c)
        for h in range(N_HEADS):
            lanes = slice(h * HEAD_DIM, (h + 1) * HEAD_DIM)
            xq = hq_ref[pl.ds(r0, c), lanes].astype(F32)
            z = hf_ref[pl.ds(r0, c), lanes]
            v = hi_ref[pl.ds(r0, c), lanes]
            lb = lb_ref[:, lanes]
            omlb = 1.0 - lb
            q = xq * _sigmoid(xq) * scale
            f = lb + omlb * _sigmoid(z)
            logf = jnp.log(jnp.maximum(f, TINY))
            kk = omlb * _sigmoid(-z)
            cum2 = logf * log2e
            step = 1
            while step < c:
                cum2 = cum2 + jnp.where(rw >= step, pltpu.roll(cum2, step, 0), 0.0)
                step *= 2
            scores = jnp.where(diag, jnp.sum(q * kk, axis=-1, keepdims=True), 0.0)
            for b in halves:
                ref = boundary(cum2, b)
                qs = (q * jnp.exp2(jnp.minimum(cum2 - ref, 0.0))).astype(BF16)
                ks = (kk * jnp.exp2(jnp.minimum(ref - cum2, 0.0))).astype(BF16)
                scores = jnp.where(pair_mask[b], _dot_nt(qs, ks), scores)
            st = st_ref[h]
            qe = (q * jnp.exp2(cum2)).astype(BF16)
            o = _dot_nt(qe, st.astype(BF16)) + _dot(scores.astype(BF16), v.astype(BF16))
            o_ref[pl.ds(r0, c), lanes] = o
            last = cum2[c - 1:c, :]
            kl = (kk * jnp.exp2(last - cum2)).astype(BF16)
            st_ref[h] = jnp.exp2(last) * st + _dot_tn(v.astype(BF16), kl)
        return carry

    lax.fori_loop(0, n_chunks, chunk_body, 0, unroll=4)


def _hgrn(proj, hf, lower):
    t = proj.shape[0]
    tb = min(t, 512)
    w = N_HEADS * HEAD_DIM

    def cols(cb):
        return pl.BlockSpec((tb, w), lambda i: (i, cb // N_HEADS))

    return pl.pallas_call(
        functools.partial(_hgrn_kernel, n_chunks=tb // HG_CHUNK),
        grid=(t // tb,),
        in_specs=[cols(CB_HQ), pl.BlockSpec((tb, w), lambda i: (i, 0)), cols(CB_HI),
                  pl.BlockSpec((1, w), lambda i: (0, 0))],
        out_specs=pl.BlockSpec((tb, w), lambda i: (i, 0)),
        out_shape=jax.ShapeDtypeStruct((t, w), F32),
        scratch_shapes=[pltpu.VMEM((N_HEADS, HEAD_DIM, HEAD_DIM), F32)],
        compiler_params=_cparams(("arbitrary",)),
        name="hgrn2",
    )(proj, hf, proj, lower)


def _ret_kernel(q_ref, qr_ref, k_ref, kr_ref, v_ref, cos_ref, sin_ref,
                dec_ref, qd_ref, kd_ref, cd_ref, o_ref, s_ref):
    @pl.when(pl.program_id(0) == 0)
    def _():
        s_ref[...] = jnp.zeros_like(s_ref)

    cos = cos_ref[...]
    sin = sin_ref[...]
    scale = HEAD_DIM ** -0.5
    for h in range(N_HEADS):
        lanes = slice(h * HEAD_DIM, (h + 1) * HEAD_DIM)
        q = q_ref[:, lanes].astype(F32) * cos + qr_ref[:, lanes].astype(F32) * sin
        k = (k_ref[:, lanes].astype(F32) * cos + kr_ref[:, lanes].astype(F32) * sin) * scale
        v = v_ref[:, lanes]
        qb = q.astype(BF16)
        s = s_ref[h]
        scores = _dot_nt(qb, k.astype(BF16)) * dec_ref[h]
        o = _dot(qb, s.astype(BF16)) * qd_ref[h] + _dot(scores.astype(BF16), v)
        o_ref[:, lanes] = o
        s_ref[h] = cd_ref[h] * s + _dot_tn((k * kd_ref[h]).astype(BF16), v)


def _ret(proj, cos_r, sin_r):
    t = proj.shape[0]
    c = min(t, RET_CHUNK)
    w = N_HEADS * HEAD_DIM
    log_gamma = jnp.log(1.0 - 2.0 ** (-5.0 - jnp.arange(N_HEADS, dtype=F32)))
    idx = jnp.arange(c, dtype=F32)
    rel = idx[:, None] - idx[None, :]
    decay = jnp.exp(jnp.where(rel >= 0, log_gamma[:, None, None] * rel, NEG_BIG))
    ones = jnp.ones((1, 1, HEAD_DIM), F32)
    q_decay = jnp.exp(log_gamma[:, None] * (idx + 1.0))[:, :, None] * ones
    k_decay = jnp.exp(log_gamma[:, None] * (c - 1.0 - idx))[:, :, None] * ones
    c_decay = jnp.exp(log_gamma * c)[:, None, None] * ones

    def cols(cb):
        return pl.BlockSpec((c, w), lambda i: (i, cb // N_HEADS))

    def const(shape):
        return pl.BlockSpec(shape, lambda i: (0, 0, 0))

    tab = pl.BlockSpec((c, LANE), lambda i: (i, 0))
    return pl.pallas_call(
        _ret_kernel,
        grid=(t // c,),
        in_specs=[cols(CB_RQ), cols(CB_RQR), cols(CB_RK), cols(CB_RKR), cols(CB_RV), tab, tab,
                  const((N_HEADS, c, c)), const((N_HEADS, c, HEAD_DIM)),
                  const((N_HEADS, c, HEAD_DIM)), const((N_HEADS, 1, HEAD_DIM))],
        out_specs=pl.BlockSpec((c, w), lambda i: (i, 0)),
        out_shape=jax.ShapeDtypeStruct((t, w), F32),
        scratch_shapes=[pltpu.VMEM((N_HEADS, HEAD_DIM, HEAD_DIM), F32)],
        compiler_params=_cparams(("arbitrary",)),
        name="retention",
    )(proj, proj, proj, proj, proj, cos_r, sin_r, decay, q_decay, k_decay, c_decay)


def _swa_kernel(sink_ref, q_ref, qr_ref, kc_ref, krc_ref, vc_ref, kp_ref, krp_ref, vp_ref,
                cc_ref, sc_ref, cp_ref, sp_ref, o_ref):
    b = SWA_BLOCK
    blk = pl.program_id(0)
    cos_c, sin_c = cc_ref[...], sc_ref[...]
    cos_p, sin_p = cp_ref[...], sp_ref[...]
    qi = lax.broadcasted_iota(jnp.int32, (b, 2 * b), 0) + b
    ki = lax.broadcasted_iota(jnp.int32, (b, 2 * b), 1)
    rel = qi - ki
    keep = (rel >= 0) & (rel < b) & ((blk > 0) | (ki >= b))
    group = SWA_Q_HEADS // SWA_KV_HEADS
    for g in range(SWA_KV_HEADS):
        kl = slice(g * LANE, (g + 1) * LANE)
        k_cur = kc_ref[:, kl].astype(F32) * cos_c + krc_ref[:, kl].astype(F32) * sin_c
        k_prev = kp_ref[:, kl].astype(F32) * cos_p + krp_ref[:, kl].astype(F32) * sin_p
        kw = jnp.concatenate([k_prev, k_cur], axis=0).astype(BF16)
        vw = jnp.concatenate([vp_ref[:, kl], vc_ref[:, kl]], axis=0)
        for j in range(group):
            h = g * group + j
            ql = slice(h * LANE, (h + 1) * LANE)
            q = (q_ref[:, ql].astype(F32) * cos_c + qr_ref[:, ql].astype(F32) * sin_c).astype(BF16)
            s = _dot_nt(q, kw) * (SWA_DIM ** -0.5)
            s = jnp.where(keep, s, NEG_BIG)
            sink = sink_ref[h]
            m = jnp.maximum(jnp.max(s, axis=-1, keepdims=True), sink)
            p = jnp.exp(s - m)
            denom = jnp.sum(p, axis=-1, keepdims=True) + jnp.exp(sink - m)
            o_ref[:, ql] = _dot((p / denom).astype(BF16), vw)


def _swa(proj, cos_s, sin_s, sinks):
    t = proj.shape[0]
    b = SWA_BLOCK
    qw = SWA_Q_HEADS * LANE
    kw = SWA_KV_HEADS * LANE

    def cur(cb, width):
        return pl.BlockSpec((b, width), lambda i: (i, cb * LANE // width))

    def prev(cb, width):
        return pl.BlockSpec((b, width), lambda i: (jnp.maximum(i - 1, 0), cb * LANE // width))

    tab_c = pl.BlockSpec((b, LANE), lambda i: (i, 0))
    tab_p = pl.BlockSpec((b, LANE), lambda i: (jnp.maximum(i - 1, 0), 0))
    return pl.pallas_call(
        _swa_kernel,
        grid=(t // b,),
        in_specs=[pl.BlockSpec(memory_space=pltpu.SMEM),
                  cur(CB_SQ, qw), cur(CB_SQR, qw),
                  cur(CB_SK, kw), cur(CB_SKR, kw), cur(CB_SV, kw),
                  prev(CB_SK, kw), prev(CB_SKR, kw), prev(CB_SV, kw),
                  tab_c, tab_c, tab_p, tab_p],
        out_specs=pl.BlockSpec((b, qw), lambda i: (i, 0)),
        out_shape=jax.ShapeDtypeStruct((t, qw), F32),
        compiler_params=_cparams(("parallel",)),
        name="swa",
    )(sinks, proj, proj, proj, proj, proj, proj, proj, proj, cos_s, sin_s, cos_s, sin_s)


def _merge_kernel(x_ref, oa_ref, hg_ref, ob_ref, rg_ref, oc_ref, ga_ref, gb_ref, gc_ref,
                  hn_ref, wa_ref, wb_ref, wc_ref, wo_ref, o_ref):
    hg = hg_ref[...].astype(F32)
    a = _rms(oa_ref[...]) * hn_ref[...] * (hg * _sigmoid(hg))
    ya = _dot(a.astype(BF16), wa_ref[...])
    rg = rg_ref[...].astype(F32)
    ob = ob_ref[...]
    parts = []
    for h in range(N_HEADS):
        lanes = slice(h * HEAD_DIM, (h + 1) * HEAD_DIM)
        parts.append(_rms(ob[:, lanes]))
    bn = jnp.concatenate(parts, axis=-1) * (rg * _sigmoid(rg))
    yb = _dot(bn.astype(BF16), wb_ref[...])
    yc = _dot(oc_ref[...].astype(BF16), wc_ref[...])
    mixed = (_sigmoid(ga_ref[...].astype(F32)) * ya + _sigmoid(gb_ref[...].astype(F32)) * yb
             + _sigmoid(gc_ref[...].astype(F32)) * yc)
    o_ref[...] = x_ref[...] + _dot(mixed.astype(BF16), wo_ref[...])


def _merge(x, proj, oa, ob, oc, hn, wa, wb, wc, wo):
    t = x.shape[0]
    tt = min(t, 512)
    w = N_HEADS * HEAD_DIM
    qw = SWA_Q_HEADS * LANE

    def rows(width, cb=0):
        return pl.BlockSpec((tt, width), lambda i: (i, cb * LANE // width))

    def full(shape):
        return pl.BlockSpec(shape, lambda i: (0, 0))

    return pl.pallas_call(
        _merge_kernel,
        grid=(t // tt,),
        in_specs=[rows(D_MODEL), rows(w), rows(w, CB_HG), rows(w), rows(w, CB_RG), rows(qw),
                  rows(D_MODEL, CB_GATE), rows(D_MODEL, CB_GATE + 8), rows(D_MODEL, CB_GATE + 16),
                  full((1, w)), full((w, D_MODEL)), full((w, D_MODEL)), full((qw, D_MODEL)),
                  full((D_MODEL, D_MODEL))],
        out_specs=rows(D_MODEL),
        out_shape=jax.ShapeDtypeStruct((t, D_MODEL), F32),
        compiler_params=_cparams(("parallel",)),
        name="merge",
    )(x, oa, proj, ob, proj, oc, proj, proj, proj, hn, wa, wb, wc, wo)


def _memkv_kernel(m_ref, g_ref, wk_ref, wv_ref, k_ref, v_ref):
    h = (_rms(m_ref[...]) * g_ref[...]).astype(BF16)
    k_ref[...] = _dot(h, wk_ref[...]).astype(BF16)
    v_ref[...] = _dot(h, wv_ref[...]).astype(BF16)


def _memkv(mem, g, wk, wv):
    nm = mem.shape[0]
    out = jax.ShapeDtypeStruct((nm, D_MODEL), BF16)
    return pl.pallas_call(
        _memkv_kernel,
        out_shape=[out, out],
        compiler_params=pltpu.CompilerParams(vmem_limit_bytes=VMEM_LIMIT),
        name="mem_kv",
    )(mem, g, wk, wv)


def _xattn_kernel(x_ref, g_ref, wq_ref, k_ref, v_ref, wo_ref, o_ref):
    x = x_ref[...]
    h = (_rms(x) * g_ref[...]).astype(BF16)
    q = _dot(h, wq_ref[...])
    outs = []
    for hh in range(X_HEADS):
        lanes = slice(hh * X_DIM, (hh + 1) * X_DIM)
        s = _dot_nt(q[:, lanes].astype(BF16), k_ref[:, lanes]) * (X_DIM ** -0.5)
        m = jnp.max(s, axis=-1, keepdims=True)
        p = jnp.exp(s - m)
        p = p / jnp.sum(p, axis=-1, keepdims=True)
        outs.append(_dot(p.astype(BF16), v_ref[:, lanes]))
    o = jnp.concatenate(outs, axis=-1)
    o_ref[...] = x + _dot(o.astype(BF16), wo_ref[...])


def _xattn(x, g, wq, k, v, wo):
    t = x.shape[0]
    tt = min(t, 512)
    nm = k.shape[0]

    def full(shape):
        return pl.BlockSpec(shape, lambda i: (0, 0))

    rows = pl.BlockSpec((tt, D_MODEL), lambda i: (i, 0))
    return pl.pallas_call(
        _xattn_kernel,
        grid=(t // tt,),
        in_specs=[rows, full((1, D_MODEL)), full((D_MODEL, D_MODEL)),
                  full((nm, D_MODEL)), full((nm, D_MODEL)), full((D_MODEL, D_MODEL))],
        out_specs=rows,
        out_shape=jax.ShapeDtypeStruct((t, D_MODEL), F32),
        compiler_params=_cparams(("parallel",)),
        name="xattn",
    )(x, g, wq, k, v, wo)


_CAND_ROWS = [PEER_TOPK // (k1 + 1) for k1 in range(8)]


def _top_values(s, n, with_rank=False):
    vals = []
    rank = jnp.full(s.shape, float(n), F32)
    for i in range(n):
        m = jnp.max(s, axis=0, keepdims=True)
        vals.append(m)
        hit = s >= m
        if with_rank:
            rank = jnp.where(hit, float(i), rank)
        s = jnp.where(hit, -jnp.inf, s)
    return (vals, rank) if with_rank else vals


def _bf16_pair_bits(x):
    u = lax.bitcast_convert_type(x.astype(BF16).astype(F32), jnp.uint32)
    return u | (u >> 16)


def _stack_rows(rows, n):
    tt = rows[0].shape[1]
    r = lax.broadcasted_iota(jnp.int32, (n, tt), 0)
    out = jnp.zeros((n, tt), F32)
    for k in range(n):
        out = jnp.where(r == k, rows[k], out)
    return out


def _router_kernel(x_ref, g_ref, wq_ref, kh_ref, kl_ref,
                   hb_ref, a_ref, b_ref, r_ref, l_ref, qt_scr):
    h = _rms(x_ref[...]) * g_ref[...]
    hb_ref[...] = h.T.astype(BF16)
    qt_scr[...] = _dot_nt(wq_ref[...], h.astype(BF16))
    tt = x_ref.shape[0]
    k = PEER_TOPK
    r8 = lax.broadcasted_iota(jnp.int32, (8, tt), 0)

    def head_body(hh, carry):
        scores = []
        for p in range(2):
            r0 = pl.multiple_of(hh * (2 * PEER_HALF) + p * PEER_HALF, PEER_HALF)
            q_hi, q_lo = _split2(qt_scr[pl.ds(r0, PEER_HALF), :])
            kh = kh_ref[p * PEER_HEADS + hh]
            scores.append(_dot(kh, q_hi) + _dot(kl_ref[p * PEER_HEADS + hh], q_hi) + _dot(kh, q_lo))
        s1, s2 = scores
        v1, rank1 = _top_values(s1, k, with_rank=True)
        v2, rank2 = _top_values(s2, k, with_rank=True)
        sv1 = _stack_rows(v1, k)
        sv2 = _stack_rows(v2, k)
        groups = [v1[0] + sv2]
        for k1 in range(1, 8):
            groups.append(jnp.where(r8 < _CAND_ROWS[k1], v1[k1] + sv2[0:8], -jnp.inf))
        groups.append(sv1[8:16] + v2[0])
        cand = jnp.concatenate(groups, axis=0)
        tau = _top_values(cand, k)[k - 1]
        m0 = v1[0] + v2[0]
        z = jnp.sum(jnp.where(cand >= tau, jnp.exp(cand - m0), 0.0), axis=0, keepdims=True)
        a_ref[hh] = _bf16_pair_bits(jnp.exp(s1 - v1[0]))
        b_ref[hh] = (jnp.exp(s2 - v2[0]) / z).astype(BF16)
        count1 = jnp.zeros_like(s1)
        for k1 in range(k):
            n_k1 = jnp.sum((v1[k1] + sv2 >= tau).astype(F32), axis=0, keepdims=True)
            count1 = jnp.where(rank1 == float(k1), n_k1, count1)
        r_ref[hh] = rank2.astype(BF16)
        l_ref[hh] = _bf16_pair_bits(count1)
        return carry

    lax.fori_loop(0, PEER_HEADS, head_body, 0)


def _router(x, g, wq_t, k_hi, k_lo):
    t = x.shape[0]
    tt = min(t, 512)
    nq = wq_t.shape[0]
    sel = jax.ShapeDtypeStruct((PEER_HEADS, PEER_KEYS, t), jnp.uint32)
    sel_b = jax.ShapeDtypeStruct((PEER_HEADS, PEER_KEYS, t), BF16)
    sel_spec = pl.BlockSpec((PEER_HEADS, PEER_KEYS, tt), lambda i: (0, 0, i))
    return pl.pallas_call(
        _router_kernel,
        grid=(t // tt,),
        in_specs=[pl.BlockSpec((tt, D_MODEL), lambda i: (i, 0)),
                  pl.BlockSpec((1, D_MODEL), lambda i: (0, 0)),
                  pl.BlockSpec((nq, D_MODEL), lambda i: (0, 0)),
                  pl.BlockSpec((2 * PEER_HEADS, PEER_KEYS, PEER_HALF), lambda i: (0, 0, 0)),
                  pl.BlockSpec((2 * PEER_HEADS, PEER_KEYS, PEER_HALF), lambda i: (0, 0, 0))],
        out_specs=[pl.BlockSpec((D_MODEL, tt), lambda i: (0, i)), sel_spec, sel_spec, sel_spec, sel_spec],
        out_shape=[jax.ShapeDtypeStruct((D_MODEL, t), BF16), sel, sel_b, sel_b, sel],
        scratch_shapes=[pltpu.VMEM((nq, tt), F32)],
        compiler_params=_cparams(("parallel",)),
        name="peer_router",
    )(x, g, wq_t, k_hi, k_lo)


def _peer_kernel(x_ref, hb_ref, a_ref, l_ref, b_ref, r_ref, u_ref, vt_ref, gf_ref, o_ref,
                 acc_ref, ga_ref, *, final_norm):
    e = pl.program_id(1)

    @pl.when(e == 0)
    def _():
        acc_ref[...] = jnp.zeros_like(acc_ref)

    u_blk = pltpu.bitcast(u_ref[...], BF16)
    vt_blk = pltpu.bitcast(vt_ref[...], BF16)
    n_i1 = u_blk.shape[0] // PEER_KEYS
    sub = 16
    tt = hb_ref.shape[1]
    shape3 = (PEER_KEYS // sub, sub, tt)
    zero3 = jnp.zeros(shape3, BF16)
    piece = PEER_PIECE_KEYS * PEER_KEYS
    for p in range(n_i1 // PEER_PIECE_KEYS):
        for j in range(p * PEER_PIECE_KEYS, (p + 1) * PEER_PIECE_KEYS):
            gate = zero3
            for hh in range(PEER_HEADS):
                arow = pltpu.bitcast(jnp.broadcast_to(a_ref[hh, j:j + 1, :], (8, tt)), BF16)
                lrow = pltpu.bitcast(jnp.broadcast_to(l_ref[hh, j:j + 1, :], (8, tt)), BF16)
                b3 = b_ref[hh].reshape(shape3)
                r3 = r_ref[hh].reshape(shape3)
                gate = gate + arow[None] * jnp.where(r3 < lrow[None], b3, zero3)
            ga_ref[j * PEER_KEYS:(j + 1) * PEER_KEYS, :] = gate.reshape(PEER_KEYS, tt)
        rows = slice(p * piece, (p + 1) * piece)
        pre = _dot(u_blk[rows, :], hb_ref[...])
        act = pre + pre * lax.erf(pre)
        ga_ref[rows, :] = ga_ref[rows, :] * act.astype(BF16)
    acc_ref[...] += _dot(vt_blk, ga_ref[...])

    @pl.when(e == pl.num_programs(1) - 1)
    def _():
        y = x_ref[...] + acc_ref[...].T
        if final_norm:
            y = _rms(y) * gf_ref[...]
        o_ref[...] = y


def _pack_kernel(w_ref, o_ref, *, scale, transpose):
    w = w_ref[...]
    if transpose:
        w = w.T
    o_ref[...] = pltpu.bitcast((w * scale).astype(BF16), jnp.uint32)


def _pack_table(w, scale, transpose):
    rows, cols = w.shape
    if transpose:
        blk = 512
        in_spec = pl.BlockSpec((blk, cols), lambda i: (i, 0))
        out_spec = pl.BlockSpec((cols // 2, blk), lambda i: (0, i))
        out_shape = jax.ShapeDtypeStruct((cols // 2, rows), jnp.uint32)
    else:
        blk = 1024
        in_spec = pl.BlockSpec((blk, cols), lambda i: (i, 0))
        out_spec = pl.BlockSpec((blk // 2, cols), lambda i: (i, 0))
        out_shape = jax.ShapeDtypeStruct((rows // 2, cols), jnp.uint32)
    return pl.pallas_call(
        functools.partial(_pack_kernel, scale=scale, transpose=transpose),
        grid=(rows // blk,),
        in_specs=[in_spec],
        out_specs=out_spec,
        out_shape=out_shape,
        compiler_params=_cparams(("parallel",)),
        name="pack_table_t" if transpose else "pack_table",
    )(w)


def _peer(x, hb, a, l, b, r, u, vt, gf, final_norm):
    t = x.shape[0]
    tt = min(t, 512)
    eb = 16 * PEER_KEYS
    rows = pl.BlockSpec((tt, D_MODEL), lambda i, e: (i, 0))
    i1_rows = pl.BlockSpec((PEER_HEADS, eb // PEER_KEYS, tt), lambda i, e: (0, e, i))
    all_rows = pl.BlockSpec((PEER_HEADS, PEER_KEYS, tt), lambda i, e: (0, 0, i))
    return pl.pallas_call(
        functools.partial(_peer_kernel, final_norm=final_norm),
        grid=(t // tt, N_EXPERTS // eb),
        in_specs=[rows, pl.BlockSpec((D_MODEL, tt), lambda i, e: (0, i)),
                  i1_rows, i1_rows, all_rows, all_rows,
                  pl.BlockSpec((eb // 2, D_MODEL), lambda i, e: (e, 0)),
                  pl.BlockSpec((D_MODEL // 2, eb), lambda i, e: (0, e)),
                  pl.BlockSpec((1, D_MODEL), lambda i, e: (0, 0))],
        out_specs=rows,
        out_shape=jax.ShapeDtypeStruct((t, D_MODEL), F32),
        scratch_shapes=[pltpu.VMEM((D_MODEL, tt), F32), pltpu.VMEM((eb, tt), BF16)],
        compiler_params=_cparams(("parallel", "arbitrary")),
        name="peer_dense",
    )(x, hb, a, l, b, r, u, vt, gf)


def kernel(x, mem, positions, ln_mix, w_in, lb_param, hgrn_norm, swa_sinks, w_br_hgrn, w_br_ret,
           w_br_swa, w_out, ln_xq, ln_xkv, w_xq, w_xk, w_xv, w_xo, ln_ffn, peer_wq, peer_keys,
           peer_u, peer_v, ln_final):
    b, t, d = x.shape
    assert b == 1 and d == D_MODEL
    depth = w_in.shape[0]
    xs = x.reshape(t, d)
    ms = mem.reshape(mem.shape[1], d)
    cos_r, sin_r, cos_s, sin_s = _rope_tables(positions)
    lb_sm = jax.nn.softmax(lb_param.astype(F32), axis=0)
    lower = jnp.cumsum(lb_sm, axis=0) - lb_sm[0]

    for l in range(depth):
        proj, hf = _proj(xs, ln_mix[l][None, :], _relay_w_in(w_in[l]))
        oa = _hgrn(proj, hf, lower[l][None, :])
        ob = _ret(proj, cos_r, sin_r)
        oc = _swa(proj, cos_s, sin_s, swa_sinks[l])
        wc = jnp.pad(w_br_swa[l].reshape(SWA_Q_HEADS, SWA_DIM, d),
                     ((0, 0), (0, LANE - SWA_DIM), (0, 0))).reshape(SWA_Q_HEADS * LANE, d)
        xs = _merge(xs, proj, oa, ob, oc, hgrn_norm[l][None, :], w_br_hgrn[l].astype(BF16),
                    w_br_ret[l].astype(BF16), wc.astype(BF16), w_out[l].astype(BF16))

        km, vm = _memkv(ms, ln_xkv[l][None, :], w_xk[l].astype(BF16), w_xv[l].astype(BF16))
        xs = _xattn(xs, ln_xq[l][None, :], w_xq[l].astype(BF16), km, vm, w_xo[l].astype(BF16))

        k_hi, k_lo = _split2(peer_keys[l].reshape(2 * PEER_HEADS, PEER_KEYS, PEER_HALF))
        hb, a, bz, r2, cnt = _router(xs, ln_ffn[l][None, :], peer_wq[l].T.astype(BF16), k_hi, k_lo)
        c = 2.0 ** -0.5
        xs = _peer(xs, hb, a, cnt, bz, r2, _pack_table(peer_u[l], c, False), _pack_table(peer_v[l], c, True),
                   ln_final[None, :], final_norm=(l == depth - 1))

    return xs.reshape(b, t, d)
```

```python
import functools
import math

import numpy as np
import jax
import jax.numpy as jnp
from jax import lax
from jax.experimental import pallas as pl
from jax.experimental.pallas import tpu as pltpu

F32 = jnp.float32
BF16 = jnp.bfloat16

D_MODEL = 1024
HEAD_DIM = 128
N_HEADS = 4
SWA_Q_HEADS = 8
SWA_KV_HEADS = 2
SWA_DIM = 64
SWA_BLOCK = 128
ROPE_THETA = 10000.0
N_MEM = 256
X_HEADS = 4
X_DIM = D_MODEL // X_HEADS
PEER_HEADS = 8
PEER_KEYS = 128
PEER_TOPK = 16
PEER_HALF = 128
N_EXPERTS = PEER_KEYS * PEER_KEYS
EPS = 1e-6
NEG_BIG = -1e30
TINY = 1e-30

LANE = 128
HG_CHUNK = 64
RET_CHUNK = 256
PEER_PIECE_KEYS = 4
VMEM_LIMIT = 56 * 1024 * 1024

CB_GATE = 0
CB_HQ, CB_HF, CB_HI, CB_HG = 24, 28, 32, 36
CB_RQ, CB_RQR, CB_RK, CB_RKR, CB_RV, CB_RG = 40, 44, 48, 52, 56, 60
CB_SQ, CB_SQR, CB_SK, CB_SKR, CB_SVA, CB_SVB = 64, 68, 72, 74, 76, 78
N_CB = 80
PROJ_COLS = N_CB * LANE


def _cparams(sem):
    return pltpu.CompilerParams(dimension_semantics=sem, vmem_limit_bytes=VMEM_LIMIT)


def _dot(a, b):
    return jnp.dot(a, b, preferred_element_type=F32)


def _dot_nt(a, b):
    return lax.dot_general(a, b, (((1,), (1,)), ((), ())), preferred_element_type=F32)


def _dot_tn(a, b):
    return lax.dot_general(a, b, (((0,), (0,)), ((), ())), preferred_element_type=F32)


def _split2(x):
    hi = x.astype(BF16)
    lo = (x - hi.astype(F32)).astype(BF16)
    return hi, lo


def _split3(x):
    hi = x.astype(BF16)
    r = x - hi.astype(F32)
    mid = r.astype(BF16)
    lo = (r - mid.astype(F32)).astype(BF16)
    return hi, mid, lo


def _sigmoid(x):
    return 1.0 / (1.0 + jnp.exp(-x))


def _rms(x):
    return x * lax.rsqrt(jnp.mean(x * x, axis=-1, keepdims=True) + EPS)


def _rope_kernel(pos_ref, fr_ref, fs_ref, cr_ref, sr_ref, cs_ref, ss_ref):
    pos = pos_ref[...].astype(F32)
    ang_r = pos * fr_ref[...]
    cr_ref[...] = jnp.cos(ang_r)
    sr_ref[...] = jnp.sin(ang_r)
    ang_s = pos * fs_ref[...]
    cs_ref[...] = jnp.cos(ang_s)
    ss_ref[...] = jnp.sin(ang_s)


def _rope_tables(positions):
    t = positions.shape[1]
    tt = min(t, 1024)
    ret_f = ROPE_THETA ** (-jnp.linspace(0.0, 1.0, HEAD_DIM // 2, dtype=F32))
    swa_f = ROPE_THETA ** (-jnp.arange(0, SWA_DIM, 2, dtype=F32) / SWA_DIM)
    fr = jnp.concatenate([ret_f, ret_f])[None, :]
    fs = jnp.concatenate([swa_f] * (LANE // (SWA_DIM // 2)))[None, :]
    pos = positions.reshape(t, 1)
    tab = jax.ShapeDtypeStruct((t, LANE), F32)
    row = pl.BlockSpec((tt, LANE), lambda i: (i, 0))
    vec = pl.BlockSpec((1, LANE), lambda i: (0, 0))
    return pl.pallas_call(
        _rope_kernel,
        grid=(t // tt,),
        in_specs=[pl.BlockSpec((tt, 1), lambda i: (i, 0)), vec, vec],
        out_specs=[row, row, row, row],
        out_shape=[tab, tab, tab, tab],
        compiler_params=_cparams(("parallel",)),
        name="rope_tables",
    )(pos, fr, fs)


PROJ_TN = 1024


def _proj_kernel(x_ref, g_ref, w_ref, o_ref, f_ref, h_scr):
    @pl.when(pl.program_id(1) == 0)
    def _():
        h_scr[...] = (_rms(x_ref[...]) * g_ref[...]).astype(BF16)

    y = _dot(h_scr[...], w_ref[...])
    o_ref[...] = y.astype(BF16)

    @pl.when(pl.program_id(1) == CB_HF * LANE // PROJ_TN)
    def _():
        lo = CB_HF * LANE % PROJ_TN
        f_ref[...] = y[:, lo:lo + N_HEADS * HEAD_DIM]


def _proj(x, g, w):
    t = x.shape[0]
    n = w.shape[1]
    tm = min(t, 1024)
    tn = PROJ_TN
    wf = N_HEADS * HEAD_DIM
    return pl.pallas_call(
        _proj_kernel,
        grid=(t // tm, n // tn),
        in_specs=[pl.BlockSpec((tm, D_MODEL), lambda i, j: (i, 0)),
                  pl.BlockSpec((1, D_MODEL), lambda i, j: (0, 0)),
                  pl.BlockSpec((D_MODEL, tn), lambda i, j: (0, j))],
        out_specs=[pl.BlockSpec((tm, tn), lambda i, j: (i, j)),
                   pl.BlockSpec((tm, wf), lambda i, j: (i, 0))],
        out_shape=[jax.ShapeDtypeStruct((t, n), BF16), jax.ShapeDtypeStruct((t, wf), F32)],
        scratch_shapes=[pltpu.VMEM((tm, D_MODEL), BF16)],
        compiler_params=_cparams(("parallel", "arbitrary")),
        name="in_proj",
    )(x, g, w)


def _relay_w_in(w_in):
    d = w_in.shape[0]
    sizes = (512,) * 8 + (512, 128, 128, 3 * D_MODEL)
    offs = np.concatenate([[0], np.cumsum(sizes)])
    hq, hf, hi, hg, rq, rk, rv, rg, sq, sk, sv, gates = [
        w_in[:, int(offs[i]):int(offs[i + 1])] for i in range(12)]

    def rot(w, nh, dh):
        w = w.reshape(d, nh, dh)
        return jnp.concatenate([-w[..., dh // 2:], w[..., :dh // 2]], axis=-1).reshape(d, nh * dh)

    def widen(w, lo, hi):
        w = w.reshape(d, SWA_KV_HEADS, SWA_DIM)
        zero = jnp.zeros_like(w)
        parts = [w if lo else zero, w if hi else zero]
        return jnp.concatenate(parts, axis=-1).reshape(d, SWA_KV_HEADS * LANE)

    cols = [gates, hq, hf, hi, hg,
            rq, rot(rq, N_HEADS, HEAD_DIM), rk, rot(rk, N_HEADS, HEAD_DIM), rv, rg,
            sq, rot(sq, SWA_Q_HEADS, SWA_DIM),
            widen(sk, True, True), widen(rot(sk, SWA_KV_HEADS, SWA_DIM), True, True),
            widen(sv, True, False), widen(sv, False, True)]
    out = jnp.concatenate(cols, axis=1).astype(BF16)
    assert out.shape[1] == PROJ_COLS
    return out


def _hgrn_kernel(hq_ref, hf_ref, hi_ref, lb_ref, o_ref, st_ref, *, n_chunks):
    @pl.when(pl.program_id(0) == 0)
    def _():
        st_ref[...] = jnp.zeros_like(st_ref)

    c = HG_CHUNK
    row = lax.broadcasted_iota(jnp.int32, (c, c), 0)
    col = lax.broadcasted_iota(jnp.int32, (c, c), 1)
    diag = row == col
    halves = [c >> (i + 1) for i in range(c.bit_length() - 1)]
    pair_mask = {b: (row // (2 * b) == col // (2 * b)) & (row % (2 * b) >= b) & (col % (2 * b) < b)
                 for b in halves}
    rw = lax.broadcasted_iota(jnp.int32, (c, HEAD_DIM), 0)
    r3 = lax.broadcasted_iota(jnp.int32, (c // 8, 8, HEAD_DIM), 1)
    scale = HEAD_DIM ** -0.5
    log2e = math.log2(math.e)

    def boundary(cum2, b):
        if b >= 8:
            ends = [cum2[s + b - 1:s + b, :] for s in range(0, c, 2 * b)]
            out = ends[-1]
            for i in range(len(ends) - 2, -1, -1):
                out = jnp.where(rw < (i + 1) * 2 * b, ends[i], out)
            return out
        c3 = cum2.reshape(c // 8, 8, HEAD_DIM)
        if b == 4:
            return jnp.broadcast_to(c3[:, 3:4, :], c3.shape).reshape(c, HEAD_DIM)
        if b == 2:
            lo = jnp.broadcast_to(c3[:, 1:2, :], c3.shape)
            hi = jnp.broadcast_to(c3[:, 5:6, :], c3.shape)
            return jnp.where(r3 < 4, lo, hi).reshape(c, HEAD_DIM)
        prev = pltpu.roll(cum2, 1, 0)
        return jnp.where(rw % 2 == 1, prev, cum2)

    def chunk_body(ci, carry):
        r0 = pl.multiple_of(ci * c, c)
        for h in range(N_HEADS):
            lanes = slice(h * HEAD_DIM, (h + 1) * HEAD_DIM)
            xq = hq_ref[pl.ds(r0, c), lanes].astype(F32)
            z = hf_ref[pl.ds(r0, c), lanes]
            v = hi_ref[pl.ds(r0, c), lanes]
            lb = lb_ref[:, lanes]
            omlb = 1.0 - lb
            q = xq * _sigmoid(xq) * scale
            f = lb + omlb * _sigmoid(z)
            logf = jnp.log(jnp.maximum(f, TINY))
            kk = omlb * _sigmoid(-z)
            cum2 = logf * log2e
            step = 1
            while step < c:
                cum2 = cum2 + jnp.where(rw >= step, pltpu.roll(cum2, step, 0), 0.0)
                step *= 2
            scores = jnp.where(diag, jnp.sum(q * kk, axis=-1, keepdims=True), 0.0)
            for b in halves:
                ref = boundary(cum2, b)
                qs = (q * jnp.exp2(jnp.minimum(cum2 - ref, 0.0))).astype(BF16)
                ks = (kk * jnp.exp2(jnp.minimum(ref - cum2, 0.0))).astype(BF16)
                scores = jnp.where(pair_mask[b], _dot_nt(qs, ks), scores)
            st = st_ref[h]
            qe = (q * jnp.exp2(cum2)).astype(BF16)
            o = _dot_nt(qe, st.astype(BF16)) + _dot(scores.astype(BF16), v.astype(BF16))
            o_ref[pl.ds(r0, c), lanes] = o.astype(BF16)
            last = cum2[c - 1:c, :]
            kl = (kk * jnp.exp2(last - cum2)).astype(BF16)
            st_ref[h] = jnp.exp2(last) * st + _dot_tn(v.astype(BF16), kl)
        return carry

    lax.fori_loop(0, n_chunks, chunk_body, 0, unroll=4)


def _hgrn(proj, hf, lower):
    t = proj.shape[0]
    tb = min(t, 512)
    w = N_HEADS * HEAD_DIM

    def cols(cb):
        return pl.BlockSpec((tb, w), lambda i: (i, cb // N_HEADS))

    return pl.pallas_call(
        functools.partial(_hgrn_kernel, n_chunks=tb // HG_CHUNK),
        grid=(t // tb,),
        in_specs=[cols(CB_HQ), pl.BlockSpec((tb, w), lambda i: (i, 0)), cols(CB_HI),
                  pl.BlockSpec((1, w), lambda i: (0, 0))],
        out_specs=pl.BlockSpec((tb, w), lambda i: (i, 0)),
        out_shape=jax.ShapeDtypeStruct((t, w), BF16),
        scratch_shapes=[pltpu.VMEM((N_HEADS, HEAD_DIM, HEAD_DIM), F32)],
        compiler_params=_cparams(("arbitrary",)),
        name="hgrn2",
    )(proj, hf, proj, lower)


def _ret_kernel(q_ref, qr_ref, k_ref, kr_ref, v_ref, cos_ref, sin_ref,
                dec_ref, qd_ref, kd_ref, cd_ref, o_ref, s_ref):
    @pl.when(pl.program_id(0) == 0)
    def _():
        s_ref[...] = jnp.zeros_like(s_ref)

    cos = cos_ref[...]
    sin = sin_ref[...]
    scale = HEAD_DIM ** -0.5
    for h in range(N_HEADS):
        lanes = slice(h * HEAD_DIM, (h + 1) * HEAD_DIM)
        q = q_ref[:, lanes].astype(F32) * cos + qr_ref[:, lanes].astype(F32) * sin
        k = (k_ref[:, lanes].astype(F32) * cos + kr_ref[:, lanes].astype(F32) * sin) * scale
        v = v_ref[:, lanes]
        qb = q.astype(BF16)
        s = s_ref[h]
        scores = _dot_nt(qb, k.astype(BF16)) * dec_ref[h]
        o = _dot(qb, s.astype(BF16)) * qd_ref[h] + _dot(scores.astype(BF16), v)
        o_ref[:, lanes] = o.astype(BF16)
        s_ref[h] = cd_ref[h] * s + _dot_tn((k * kd_ref[h]).astype(BF16), v)


def _ret(proj, cos_r, sin_r):
    t = proj.shape[0]
    c = min(t, RET_CHUNK)
    w = N_HEADS * HEAD_DIM
    log_gamma = jnp.log(1.0 - 2.0 ** (-5.0 - jnp.arange(N_HEADS, dtype=F32)))
    idx = jnp.arange(c, dtype=F32)
    rel = idx[:, None] - idx[None, :]
    decay = jnp.exp(jnp.where(rel >= 0, log_gamma[:, None, None] * rel, NEG_BIG))
    ones = jnp.ones((1, 1, HEAD_DIM), F32)
    q_decay = jnp.exp(log_gamma[:, None] * (idx + 1.0))[:, :, None] * ones
    k_decay = jnp.exp(log_gamma[:, None] * (c - 1.0 - idx))[:, :, None] * ones
    c_decay = jnp.exp(log_gamma * c)[:, None, None] * ones

    def cols(cb):
        return pl.BlockSpec((c, w), lambda i: (i, cb // N_HEADS))

    def const(shape):
        return pl.BlockSpec(shape, lambda i: (0, 0, 0))

    tab = pl.BlockSpec((c, LANE), lambda i: (i, 0))
    return pl.pallas_call(
        _ret_kernel,
        grid=(t // c,),
        in_specs=[cols(CB_RQ), cols(CB_RQR), cols(CB_RK), cols(CB_RKR), cols(CB_RV), tab, tab,
                  const((N_HEADS, c, c)), const((N_HEADS, c, HEAD_DIM)),
                  const((N_HEADS, c, HEAD_DIM)), const((N_HEADS, 1, HEAD_DIM))],
        out_specs=pl.BlockSpec((c, w), lambda i: (i, 0)),
        out_shape=jax.ShapeDtypeStruct((t, w), BF16),
        scratch_shapes=[pltpu.VMEM((N_HEADS, HEAD_DIM, HEAD_DIM), F32)],
        compiler_params=_cparams(("arbitrary",)),
        name="retention",
    )(proj, proj, proj, proj, proj, cos_r, sin_r, decay, q_decay, k_decay, c_decay)


def _swa_kernel(sink_ref, q_ref, qr_ref, kc_ref, krc_ref, vac_ref, vbc_ref,
                kp_ref, krp_ref, vap_ref, vbp_ref, cc_ref, sc_ref, cp_ref, sp_ref, o_ref):
    b = SWA_BLOCK
    blk = pl.program_id(0)
    cos_c, sin_c = cc_ref[...], sc_ref[...]
    cos_p, sin_p = cp_ref[...], sp_ref[...]
    qi = lax.broadcasted_iota(jnp.int32, (b, 2 * b), 0) + b
    ki = lax.broadcasted_iota(jnp.int32, (b, 2 * b), 1)
    rel = qi - ki
    keep = (rel >= 0) & (rel < b) & ((blk > 0) | (ki >= b))
    low = lax.broadcasted_iota(jnp.int32, (b, LANE), 1) < SWA_DIM
    pairs_per_group = SWA_Q_HEADS // SWA_KV_HEADS // 2
    for g in range(SWA_KV_HEADS):
        kl = slice(g * LANE, (g + 1) * LANE)
        k_cur = kc_ref[:, kl].astype(F32) * cos_c + krc_ref[:, kl].astype(F32) * sin_c
        k_prev = kp_ref[:, kl].astype(F32) * cos_p + krp_ref[:, kl].astype(F32) * sin_p
        kw = jnp.concatenate([k_prev, k_cur], axis=0).astype(BF16)
        v_lo = jnp.concatenate([vap_ref[:, kl], vac_ref[:, kl]], axis=0)
        v_hi = jnp.concatenate([vbp_ref[:, kl], vbc_ref[:, kl]], axis=0)
        for j in range(pairs_per_group):
            pair = g * pairs_per_group + j
            ql = slice(pair * LANE, (pair + 1) * LANE)
            q = q_ref[:, ql].astype(F32) * cos_c + qr_ref[:, ql].astype(F32) * sin_c
            out = None
            for half, vw in ((0, v_lo), (1, v_hi)):
                qh = jnp.where(low if half == 0 else ~low, q, 0.0).astype(BF16)
                s = _dot_nt(qh, kw) * (SWA_DIM ** -0.5)
                s = jnp.where(keep, s, NEG_BIG)
                sink = sink_ref[2 * pair + half]
                m = jnp.maximum(jnp.max(s, axis=-1, keepdims=True), sink)
                p = jnp.exp(s - m)
                denom = jnp.sum(p, axis=-1, keepdims=True) + jnp.exp(sink - m)
                o = _dot((p / denom).astype(BF16), vw)
                out = o if out is None else out + o
            o_ref[:, ql] = out.astype(BF16)


def _swa(proj, cos_s, sin_s, sinks):
    t = proj.shape[0]
    b = SWA_BLOCK
    qw = SWA_Q_HEADS * SWA_DIM
    kw = SWA_KV_HEADS * LANE

    def cur(cb, width):
        return pl.BlockSpec((b, width), lambda i: (i, cb * LANE // width))

    def prev(cb, width):
        return pl.BlockSpec((b, width), lambda i: (jnp.maximum(i - 1, 0), cb * LANE // width))

    tab_c = pl.BlockSpec((b, LANE), lambda i: (i, 0))
    tab_p = pl.BlockSpec((b, LANE), lambda i: (jnp.maximum(i - 1, 0), 0))
    return pl.pallas_call(
        _swa_kernel,
        grid=(t // b,),
        in_specs=[pl.BlockSpec(memory_space=pltpu.SMEM),
                  cur(CB_SQ, qw), cur(CB_SQR, qw),
                  cur(CB_SK, kw), cur(CB_SKR, kw), cur(CB_SVA, kw), cur(CB_SVB, kw),
                  prev(CB_SK, kw), prev(CB_SKR, kw), prev(CB_SVA, kw), prev(CB_SVB, kw),
                  tab_c, tab_c, tab_p, tab_p],
        out_specs=pl.BlockSpec((b, qw), lambda i: (i, 0)),
        out_shape=jax.ShapeDtypeStruct((t, qw), BF16),
        compiler_params=_cparams(("parallel",)),
        name="swa",
    )(sinks, proj, proj, proj, proj, proj, proj, proj, proj, proj, proj, cos_s, sin_s, cos_s, sin_s)


def _merge_kernel(x_ref, oa_ref, hg_ref, ob_ref, rg_ref, oc_ref, ga_ref, gb_ref, gc_ref,
                  hn_ref, wa_ref, wb_ref, wc_ref, wo_ref, o_ref):
    hg = hg_ref[...].astype(F32)
    a = _rms(oa_ref[...].astype(F32)) * hn_ref[...] * (hg * _sigmoid(hg))
    ya = _dot(a.astype(BF16), wa_ref[...])
    rg = rg_ref[...].astype(F32)
    ob = ob_ref[...].astype(F32)
    parts = []
    for h in range(N_HEADS):
        lanes = slice(h * HEAD_DIM, (h + 1) * HEAD_DIM)
        parts.append(_rms(ob[:, lanes]))
    bn = jnp.concatenate(parts, axis=-1) * (rg * _sigmoid(rg))
    yb = _dot(bn.astype(BF16), wb_ref[...])
    yc = _dot(oc_ref[...], wc_ref[...])
    mixed = (_sigmoid(ga_ref[...].astype(F32)) * ya + _sigmoid(gb_ref[...].astype(F32)) * yb
             + _sigmoid(gc_ref[...].astype(F32)) * yc)
    o_ref[...] = x_ref[...] + _dot(mixed.astype(BF16), wo_ref[...])


def _merge(x, proj, oa, ob, oc, hn, wa, wb, wc, wo):
    t = x.shape[0]
    tt = min(t, 512)
    w = N_HEADS * HEAD_DIM
    qw = SWA_Q_HEADS * SWA_DIM

    def rows(width, cb=0):
        return pl.BlockSpec((tt, width), lambda i: (i, cb * LANE // width))

    def full(shape):
        return pl.BlockSpec(shape, lambda i: (0, 0))

    return pl.pallas_call(
        _merge_kernel,
        grid=(t // tt,),
        in_specs=[rows(D_MODEL), rows(w), rows(w, CB_HG), rows(w), rows(w, CB_RG), rows(qw),
                  rows(D_MODEL, CB_GATE), rows(D_MODEL, CB_GATE + 8), rows(D_MODEL, CB_GATE + 16),
                  full((1, w)), full((w, D_MODEL)), full((w, D_MODEL)), full((qw, D_MODEL)),
                  full((D_MODEL, D_MODEL))],
        out_specs=rows(D_MODEL),
        out_shape=jax.ShapeDtypeStruct((t, D_MODEL), F32),
        compiler_params=_cparams(("parallel",)),
        name="merge",
    )(x, oa, proj, ob, proj, oc, proj, proj, proj, hn, wa, wb, wc, wo)


def _memkv_kernel(m_ref, g_ref, wk_ref, wv_ref, k_ref, v_ref):
    h = (_rms(m_ref[...]) * g_ref[...]).astype(BF16)
    k_ref[...] = _dot(h, wk_ref[...]).astype(BF16)
    v_ref[...] = _dot(h, wv_ref[...]).astype(BF16)


def _memkv(mem, g, wk, wv):
    nm = mem.shape[0]
    out = jax.ShapeDtypeStruct((nm, D_MODEL), BF16)
    return pl.pallas_call(
        _memkv_kernel,
        out_shape=[out, out],
        compiler_params=pltpu.CompilerParams(vmem_limit_bytes=VMEM_LIMIT),
        name="mem_kv",
    )(mem, g, wk, wv)


def _xattn_kernel(x_ref, g_ref, wq_ref, k_ref, v_ref, wo_ref, o_ref):
    x = x_ref[...]
    h = (_rms(x) * g_ref[...]).astype(BF16)
    q = _dot(h, wq_ref[...])
    outs = []
    for hh in range(X_HEADS):
        lanes = slice(hh * X_DIM, (hh + 1) * X_DIM)
        s = _dot_nt(q[:, lanes].astype(BF16), k_ref[:, lanes]) * (X_DIM ** -0.5)
        m = jnp.max(s, axis=-1, keepdims=True)
        p = jnp.exp(s - m)
        p = p / jnp.sum(p, axis=-1, keepdims=True)
        outs.append(_dot(p.astype(BF16), v_ref[:, lanes]))
    o = jnp.concatenate(outs, axis=-1)
    o_ref[...] = x + _dot(o.astype(BF16), wo_ref[...])


def _xattn(x, g, wq, k, v, wo):
    t = x.shape[0]
    tt = min(t, 512)
    nm = k.shape[0]

    def full(shape):
        return pl.BlockSpec(shape, lambda i: (0, 0))

    rows = pl.BlockSpec((tt, D_MODEL), lambda i: (i, 0))
    return pl.pallas_call(
        _xattn_kernel,
        grid=(t // tt,),
        in_specs=[rows, full((1, D_MODEL)), full((D_MODEL, D_MODEL)),
                  full((nm, D_MODEL)), full((nm, D_MODEL)), full((D_MODEL, D_MODEL))],
        out_specs=rows,
        out_shape=jax.ShapeDtypeStruct((t, D_MODEL), F32),
        compiler_params=_cparams(("parallel",)),
        name="xattn",
    )(x, g, wq, k, v, wo)


_CAND_ROWS = [PEER_TOPK // (k1 + 1) for k1 in range(8)]


def _top_values(s, n, with_rank=False):
    vals = []
    rank = jnp.full(s.shape, float(n), F32)
    for i in range(n):
        m = jnp.max(s, axis=0, keepdims=True)
        vals.append(m)
        hit = s >= m
        if with_rank:
            rank = jnp.where(hit, float(i), rank)
        s = jnp.where(hit, -jnp.inf, s)
    return (vals, rank) if with_rank else vals


def _bf16_pair_bits(x):
    u = lax.bitcast_convert_type(x.astype(BF16).astype(F32), jnp.uint32)
    return u | (u >> 16)


def _stack_rows(rows, n):
    tt = rows[0].shape[1]
    r = lax.broadcasted_iota(jnp.int32, (n, tt), 0)
    out = jnp.zeros((n, tt), F32)
    for k in range(n):
        out = jnp.where(r == k, rows[k], out)
    return out


def _router_kernel(x_ref, g_ref, wq_ref, kh_ref, kl_ref,
                   hb_ref, a_ref, b_ref, r_ref, l_ref, qt_scr):
    h = _rms(x_ref[...]) * g_ref[...]
    hb_ref[...] = h.T.astype(BF16)
    qt_scr[...] = _dot_nt(wq_ref[...], h.astype(BF16))
    tt = x_ref.shape[0]
    k = PEER_TOPK
    r8 = lax.broadcasted_iota(jnp.int32, (8, tt), 0)

    def head_body(hh, carry):
        scores = []
        for p in range(2):
            r0 = pl.multiple_of(hh * (2 * PEER_HALF) + p * PEER_HALF, PEER_HALF)
            q_hi, q_lo = _split2(qt_scr[pl.ds(r0, PEER_HALF), :])
            kh = kh_ref[p * PEER_HEADS + hh]
            scores.append(_dot(kh, q_hi) + _dot(kl_ref[p * PEER_HEADS + hh], q_hi) + _dot(kh, q_lo))
        s1, s2 = scores
        v1, rank1 = _top_values(s1, k, with_rank=True)
        v2, rank2 = _top_values(s2, k, with_rank=True)
        sv1 = _stack_rows(v1, k)
        sv2 = _stack_rows(v2, k)
        groups = [v1[0] + sv2]
        for k1 in range(1, 8):
            groups.append(jnp.where(r8 < _CAND_ROWS[k1], v1[k1] + sv2[0:8], -jnp.inf))
        groups.append(sv1[8:16] + v2[0])
        cand = jnp.concatenate(groups, axis=0)
        tau = _top_values(cand, k)[k - 1]
        m0 = v1[0] + v2[0]
        z = jnp.sum(jnp.where(cand >= tau, jnp.exp(cand - m0), 0.0), axis=0, keepdims=True)
        a_ref[hh] = _bf16_pair_bits(jnp.exp(s1 - v1[0]))
        b_ref[hh] = (jnp.exp(s2 - v2[0]) / z).astype(BF16)
        count1 = jnp.zeros_like(s1)
        for k1 in range(k):
            n_k1 = jnp.sum((v1[k1] + sv2 >= tau).astype(F32), axis=0, keepdims=True)
            count1 = jnp.where(rank1 == float(k1), n_k1, count1)
        r_ref[hh] = rank2.astype(BF16)
        l_ref[hh] = _bf16_pair_bits(count1)
        return carry

    lax.fori_loop(0, PEER_HEADS, head_body, 0)


def _router(x, g, wq_t, k_hi, k_lo):
    t = x.shape[0]
    tt = min(t, 512)
    nq = wq_t.shape[0]
    sel = jax.ShapeDtypeStruct((PEER_HEADS, PEER_KEYS, t), jnp.uint32)
    sel_b = jax.ShapeDtypeStruct((PEER_HEADS, PEER_KEYS, t), BF16)
    sel_spec = pl.BlockSpec((PEER_HEADS, PEER_KEYS, tt), lambda i: (0, 0, i))
    return pl.pallas_call(
        _router_kernel,
        grid=(t // tt,),
        in_specs=[pl.BlockSpec((tt, D_MODEL), lambda i: (i, 0)),
                  pl.BlockSpec((1, D_MODEL), lambda i: (0, 0)),
                  pl.BlockSpec((nq, D_MODEL), lambda i: (0, 0)),
                  pl.BlockSpec((2 * PEER_HEADS, PEER_KEYS, PEER_HALF), lambda i: (0, 0, 0)),
                  pl.BlockSpec((2 * PEER_HEADS, PEER_KEYS, PEER_HALF), lambda i: (0, 0, 0))],
        out_specs=[pl.BlockSpec((D_MODEL, tt), lambda i: (0, i)), sel_spec, sel_spec, sel_spec, sel_spec],
        out_shape=[jax.ShapeDtypeStruct((D_MODEL, t), BF16), sel, sel_b, sel_b, sel],
        scratch_shapes=[pltpu.VMEM((nq, tt), F32)],
        compiler_params=_cparams(("parallel",)),
        name="peer_router",
    )(x, g, wq_t, k_hi, k_lo)


def _peer_kernel(x_ref, hb_ref, a_ref, l_ref, b_ref, r_ref, u_ref, vt_ref, gf_ref, o_ref,
                 acc_ref, ga_ref, *, final_norm):
    e = pl.program_id(1)

    @pl.when(e == 0)
    def _():
        acc_ref[...] = jnp.zeros_like(acc_ref)

    u_blk = pltpu.bitcast(u_ref[...], BF16)
    vt_blk = pltpu.bitcast(vt_ref[...], BF16)
    n_i1 = u_blk.shape[0] // PEER_KEYS
    sub = 16
    tt = hb_ref.shape[1]
    shape3 = (PEER_KEYS // sub, sub, tt)
    zero3 = jnp.zeros(shape3, BF16)
    piece = PEER_PIECE_KEYS * PEER_KEYS
    for p in range(n_i1 // PEER_PIECE_KEYS):
        for j in range(p * PEER_PIECE_KEYS, (p + 1) * PEER_PIECE_KEYS):
            gate = zero3
            for hh in range(PEER_HEADS):
                arow = pltpu.bitcast(jnp.broadcast_to(a_ref[hh, j:j + 1, :], (8, tt)), BF16)
                lrow = pltpu.bitcast(jnp.broadcast_to(l_ref[hh, j:j + 1, :], (8, tt)), BF16)
                b3 = b_ref[hh].reshape(shape3)
                r3 = r_ref[hh].reshape(shape3)
                gate = gate + arow[None] * jnp.where(r3 < lrow[None], b3, zero3)
            ga_ref[j * PEER_KEYS:(j + 1) * PEER_KEYS, :] = gate.reshape(PEER_KEYS, tt)
        rows = slice(p * piece, (p + 1) * piece)
        pre = _dot(u_blk[rows, :], hb_ref[...])
        act = pre + pre * lax.erf(pre)
        ga_ref[rows, :] = ga_ref[rows, :] * act.astype(BF16)
    acc_ref[...] += _dot(vt_blk, ga_ref[...])

    @pl.when(e == pl.num_programs(1) - 1)
    def _():
        y = x_ref[...] + acc_ref[...].T
        if final_norm:
            y = _rms(y) * gf_ref[...]
        o_ref[...] = y


def _pack_kernel(w_ref, o_ref, *, scale, transpose):
    w = w_ref[...]
    if transpose:
        w = w.T
    o_ref[...] = pltpu.bitcast((w * scale).astype(BF16), jnp.uint32)


def _pack_table(w, scale, transpose):
    rows, cols = w.shape
    if transpose:
        blk = 512
        in_spec = pl.BlockSpec((blk, cols), lambda i: (i, 0))
        out_spec = pl.BlockSpec((cols // 2, blk), lambda i: (0, i))
        out_shape = jax.ShapeDtypeStruct((cols // 2, rows), jnp.uint32)
    else:
        blk = 1024
        in_spec = pl.BlockSpec((blk, cols), lambda i: (i, 0))
        out_spec = pl.BlockSpec((blk // 2, cols), lambda i: (i, 0))
        out_shape = jax.ShapeDtypeStruct((rows // 2, cols), jnp.uint32)
    return pl.pallas_call(
        functools.partial(_pack_kernel, scale=scale, transpose=transpose),
        grid=(rows // blk,),
        in_specs=[in_spec],
        out_specs=out_spec,
        out_shape=out_shape,
        compiler_params=_cparams(("parallel",)),
        name="pack_table_t" if transpose else "pack_table",
    )(w)


def _peer(x, hb, a, l, b, r, u, vt, gf, final_norm):
    t = x.shape[0]
    tt = min(t, 512)
    eb = 16 * PEER_KEYS
    rows = pl.BlockSpec((tt, D_MODEL), lambda i, e: (i, 0))
    i1_rows = pl.BlockSpec((PEER_HEADS, eb // PEER_KEYS, tt), lambda i, e: (0, e, i))
    all_rows = pl.BlockSpec((PEER_HEADS, PEER_KEYS, tt), lambda i, e: (0, 0, i))
    return pl.pallas_call(
        functools.partial(_peer_kernel, final_norm=final_norm),
        grid=(t // tt, N_EXPERTS // eb),
        in_specs=[rows, pl.BlockSpec((D_MODEL, tt), lambda i, e: (0, i)),
                  i1_rows, i1_rows, all_rows, all_rows,
                  pl.BlockSpec((eb // 2, D_MODEL), lambda i, e: (e, 0)),
                  pl.BlockSpec((D_MODEL // 2, eb), lambda i, e: (0, e)),
                  pl.BlockSpec((1, D_MODEL), lambda i, e: (0, 0))],
        out_specs=rows,
        out_shape=jax.ShapeDtypeStruct((t, D_MODEL), F32),
        scratch_shapes=[pltpu.VMEM((D_MODEL, tt), F32), pltpu.VMEM((eb, tt), BF16)],
        compiler_params=_cparams(("parallel", "arbitrary")),
        name="peer_dense",
    )(x, hb, a, l, b, r, u, vt, gf)


def kernel(x, mem, positions, ln_mix, w_in, lb_param, hgrn_norm, swa_sinks, w_br_hgrn, w_br_ret,
           w_br_swa, w_out, ln_xq, ln_xkv, w_xq, w_xk, w_xv, w_xo, ln_ffn, peer_wq, peer_keys,
           peer_u, peer_v, ln_final):
    b, t, d = x.shape
    assert b == 1 and d == D_MODEL
    depth = w_in.shape[0]
    xs = x.reshape(t, d)
    ms = mem.reshape(mem.shape[1], d)
    cos_r, sin_r, cos_s, sin_s = _rope_tables(positions)
    lb_sm = jax.nn.softmax(lb_param.astype(F32), axis=0)
    lower = jnp.cumsum(lb_sm, axis=0) - lb_sm[0]

    for l in range(depth):
        proj, hf = _proj(xs, ln_mix[l][None, :], _relay_w_in(w_in[l]))
        oa = _hgrn(proj, hf, lower[l][None, :])
        ob = _ret(proj, cos_r, sin_r)
        oc = _swa(proj, cos_s, sin_s, swa_sinks[l])
        xs = _merge(xs, proj, oa, ob, oc, hgrn_norm[l][None, :], w_br_hgrn[l].astype(BF16),
                    w_br_ret[l].astype(BF16), w_br_swa[l].astype(BF16), w_out[l].astype(BF16))

        km, vm = _memkv(ms, ln_xkv[l][None, :], w_xk[l].astype(BF16), w_xv[l].astype(BF16))
        xs = _xattn(xs, ln_xq[l][None, :], w_xq[l].astype(BF16), km, vm, w_xo[l].astype(BF16))

        k_hi, k_lo = _split2(peer_keys[l].reshape(2 * PEER_HEADS, PEER_KEYS, PEER_HALF))
        hb, a, bz, r2, cnt = _router(xs, ln_ffn[l][None, :], peer_wq[l].T.astype(BF16), k_hi, k_lo)
        c = 2.0 ** -0.5
        xs = _peer(xs, hb, a, cnt, bz, r2, _pack_table(peer_u[l], c, False), _pack_table(peer_v[l], c, True),
                   ln_final[None, :], final_norm=(l == depth - 1))

    return xs.reshape(b, t, d)
```

```python
import functools
import math

import numpy as np
import jax
import jax.numpy as jnp
from jax import lax
from jax.experimental import pallas as pl
from jax.experimental.pallas import tpu as pltpu

F32 = jnp.float32
BF16 = jnp.bfloat16

D_MODEL = 1024
HEAD_DIM = 128
N_HEADS = 4
SWA_Q_HEADS = 8
SWA_KV_HEADS = 2
SWA_DIM = 64
SWA_BLOCK = 128
ROPE_THETA = 10000.0
N_MEM = 256
X_HEADS = 4
X_DIM = D_MODEL // X_HEADS
PEER_HEADS = 8
PEER_KEYS = 128
PEER_TOPK = 16
PEER_HALF = 128
N_EXPERTS = PEER_KEYS * PEER_KEYS
EPS = 1e-6
NEG_BIG = -1e30
TINY = 1e-30

LANE = 128
HG_CHUNK = 64
RET_CHUNK = 256
PEER_PIECE_KEYS = 4
VMEM_LIMIT = 56 * 1024 * 1024

CB_GATE = 0
CB_HQ, CB_HF, CB_HI, CB_HG = 24, 28, 32, 36
CB_RQ, CB_RQR, CB_RK, CB_RKR, CB_RV, CB_RG = 40, 44, 48, 52, 56, 60
CB_SQ, CB_SQR, CB_SK, CB_SKR, CB_SVA, CB_SVB = 64, 68, 72, 74, 76, 78
N_CB = 80
PROJ_COLS = N_CB * LANE


def _cparams(sem):
    return pltpu.CompilerParams(dimension_semantics=sem, vmem_limit_bytes=VMEM_LIMIT)


def _dot(a, b):
    return jnp.dot(a, b, preferred_element_type=F32)


def _dot_nt(a, b):
    return lax.dot_general(a, b, (((1,), (1,)), ((), ())), preferred_element_type=F32)


def _dot_tn(a, b):
    return lax.dot_general(a, b, (((0,), (0,)), ((), ())), preferred_element_type=F32)


def _split2(x):
    hi = x.astype(BF16)
    lo = (x - hi.astype(F32)).astype(BF16)
    return hi, lo


def _split3(x):
    hi = x.astype(BF16)
    r = x - hi.astype(F32)
    mid = r.astype(BF16)
    lo = (r - mid.astype(F32)).astype(BF16)
    return hi, mid, lo


def _sigmoid(x):
    return 1.0 / (1.0 + jnp.exp(-x))


def _rms(x):
    return x * lax.rsqrt(jnp.mean(x * x, axis=-1, keepdims=True) + EPS)


def _rope_kernel(pos_ref, fr_ref, fs_ref, cr_ref, sr_ref, cs_ref, ss_ref):
    pos = pos_ref[...].astype(F32)
    ang_r = pos * fr_ref[...]
    cr_ref[...] = jnp.cos(ang_r)
    sr_ref[...] = jnp.sin(ang_r)
    ang_s = pos * fs_ref[...]
    cs_ref[...] = jnp.cos(ang_s)
    ss_ref[...] = jnp.sin(ang_s)


def _rope_tables(positions):
    t = positions.shape[1]
    tt = min(t, 1024)
    ret_f = ROPE_THETA ** (-jnp.linspace(0.0, 1.0, HEAD_DIM // 2, dtype=F32))
    swa_f = ROPE_THETA ** (-jnp.arange(0, SWA_DIM, 2, dtype=F32) / SWA_DIM)
    fr = jnp.concatenate([ret_f, ret_f])[None, :]
    fs = jnp.concatenate([swa_f] * (LANE // (SWA_DIM // 2)))[None, :]
    pos = positions.reshape(t, 1)
    tab = jax.ShapeDtypeStruct((t, LANE), F32)
    row = pl.BlockSpec((tt, LANE), lambda i: (i, 0))
    vec = pl.BlockSpec((1, LANE), lambda i: (0, 0))
    return pl.pallas_call(
        _rope_kernel,
        grid=(t // tt,),
        in_specs=[pl.BlockSpec((tt, 1), lambda i: (i, 0)), vec, vec],
        out_specs=[row, row, row, row],
        out_shape=[tab, tab, tab, tab],
        compiler_params=_cparams(("parallel",)),
        name="rope_tables",
    )(pos, fr, fs)


PROJ_TN = 1024


def _proj_kernel(x_ref, g_ref, w_ref, o_ref, f_ref, h_scr):
    @pl.when(pl.program_id(1) == 0)
    def _():
        h_scr[...] = (_rms(x_ref[...]) * g_ref[...]).astype(BF16)

    y = _dot(h_scr[...], w_ref[...])
    o_ref[...] = y.astype(BF16)

    @pl.when(pl.program_id(1) == CB_HF * LANE // PROJ_TN)
    def _():
        lo = CB_HF * LANE % PROJ_TN
        f_ref[...] = y[:, lo:lo + N_HEADS * HEAD_DIM]


def _proj(x, g, w):
    t = x.shape[0]
    n = w.shape[1]
    tm = min(t, 2048)
    tn = PROJ_TN
    wf = N_HEADS * HEAD_DIM
    return pl.pallas_call(
        _proj_kernel,
        grid=(t // tm, n // tn),
        in_specs=[pl.BlockSpec((tm, D_MODEL), lambda i, j: (i, 0)),
                  pl.BlockSpec((1, D_MODEL), lambda i, j: (0, 0)),
                  pl.BlockSpec((D_MODEL, tn), lambda i, j: (0, j))],
        out_specs=[pl.BlockSpec((tm, tn), lambda i, j: (i, j)),
                   pl.BlockSpec((tm, wf), lambda i, j: (i, 0))],
        out_shape=[jax.ShapeDtypeStruct((t, n), BF16), jax.ShapeDtypeStruct((t, wf), F32)],
        scratch_shapes=[pltpu.VMEM((tm, D_MODEL), BF16)],
        compiler_params=_cparams(("parallel", "arbitrary")),
        name="in_proj",
    )(x, g, w)


def _relay_w_in(w_in):
    d = w_in.shape[0]
    sizes = (512,) * 8 + (512, 128, 128, 3 * D_MODEL)
    offs = np.concatenate([[0], np.cumsum(sizes)])
    hq, hf, hi, hg, rq, rk, rv, rg, sq, sk, sv, gates = [
        w_in[:, int(offs[i]):int(offs[i + 1])] for i in range(12)]

    def rot(w, nh, dh):
        w = w.reshape(d, nh, dh)
        return jnp.concatenate([-w[..., dh // 2:], w[..., :dh // 2]], axis=-1).reshape(d, nh * dh)

    def widen(w, lo, hi):
        w = w.reshape(d, SWA_KV_HEADS, SWA_DIM)
        zero = jnp.zeros_like(w)
        parts = [w if lo else zero, w if hi else zero]
        return jnp.concatenate(parts, axis=-1).reshape(d, SWA_KV_HEADS * LANE)

    cols = [gates, hq, hf, hi, hg,
            rq, rot(rq, N_HEADS, HEAD_DIM), rk, rot(rk, N_HEADS, HEAD_DIM), rv, rg,
            sq, rot(sq, SWA_Q_HEADS, SWA_DIM),
            widen(sk, True, True), widen(rot(sk, SWA_KV_HEADS, SWA_DIM), True, True),
            widen(sv, True, False), widen(sv, False, True)]
    out = jnp.concatenate(cols, axis=1).astype(BF16)
    assert out.shape[1] == PROJ_COLS
    return out


def _hgrn_kernel(hq_ref, hf_ref, hi_ref, lb_ref, o_ref, st_ref, *, n_chunks):
    @pl.when(pl.program_id(0) == 0)
    def _():
        st_ref[...] = jnp.zeros_like(st_ref)

    c = HG_CHUNK
    row = lax.broadcasted_iota(jnp.int32, (c, c), 0)
    col = lax.broadcasted_iota(jnp.int32, (c, c), 1)
    diag = row == col
    halves = [c >> (i + 1) for i in range(c.bit_length() - 1)]
    pair_mask = {b: (row // (2 * b) == col // (2 * b)) & (row % (2 * b) >= b) & (col % (2 * b) < b)
                 for b in halves}
    rw = lax.broadcasted_iota(jnp.int32, (c, HEAD_DIM), 0)
    r3 = lax.broadcasted_iota(jnp.int32, (c // 8, 8, HEAD_DIM), 1)
    scale = HEAD_DIM ** -0.5
    log2e = math.log2(math.e)

    def boundary(cum2, b):
        if b >= 8:
            ends = [cum2[s + b - 1:s + b, :] for s in range(0, c, 2 * b)]
            out = ends[-1]
            for i in range(len(ends) - 2, -1, -1):
                out = jnp.where(rw < (i + 1) * 2 * b, ends[i], out)
            return out
        c3 = cum2.reshape(c // 8, 8, HEAD_DIM)
        if b == 4:
            return jnp.broadcast_to(c3[:, 3:4, :], c3.shape).reshape(c, HEAD_DIM)
        if b == 2:
            lo = jnp.broadcast_to(c3[:, 1:2, :], c3.shape)
            hi = jnp.broadcast_to(c3[:, 5:6, :], c3.shape)
            return jnp.where(r3 < 4, lo, hi).reshape(c, HEAD_DIM)
        prev = pltpu.roll(cum2, 1, 0)
        return jnp.where(rw % 2 == 1, prev, cum2)

    def chunk_body(ci, carry):
        r0 = pl.multiple_of(ci * c, c)
        for h in range(N_HEADS):
            lanes = slice(h * HEAD_DIM, (h + 1) * HEAD_DIM)
            xq = hq_ref[pl.ds(r0, c), lanes].astype(F32)
            z = hf_ref[pl.ds(r0, c), lanes]
            v = hi_ref[pl.ds(r0, c), lanes]
            lb = lb_ref[:, lanes]
            omlb = 1.0 - lb
            q = xq * _sigmoid(xq) * scale
            f = lb + omlb * _sigmoid(z)
            logf = jnp.log(jnp.maximum(f, TINY))
            kk = omlb * _sigmoid(-z)
            cum2 = logf * log2e
            step = 1
            while step < c:
                cum2 = cum2 + jnp.where(rw >= step, pltpu.roll(cum2, step, 0), 0.0)
                step *= 2
            scores = jnp.where(diag, jnp.sum(q * kk, axis=-1, keepdims=True), 0.0)
            for b in halves:
                ref = boundary(cum2, b)
                qs = (q * jnp.exp2(jnp.minimum(cum2 - ref, 0.0))).astype(BF16)
                ks = (kk * jnp.exp2(jnp.minimum(ref - cum2, 0.0))).astype(BF16)
                scores = jnp.where(pair_mask[b], _dot_nt(qs, ks), scores)
            st = st_ref[h]
            qe = (q * jnp.exp2(cum2)).astype(BF16)
            o = _dot_nt(qe, st.astype(BF16)) + _dot(scores.astype(BF16), v.astype(BF16))
            o_ref[pl.ds(r0, c), lanes] = o.astype(BF16)
            last = cum2[c - 1:c, :]
            kl = (kk * jnp.exp2(last - cum2)).astype(BF16)
            st_ref[h] = jnp.exp2(last) * st + _dot_tn(v.astype(BF16), kl)
        return carry

    lax.fori_loop(0, n_chunks, chunk_body, 0, unroll=4)


def _hgrn(proj, hf, lower):
    t = proj.shape[0]
    tb = min(t, 512)
    w = N_HEADS * HEAD_DIM

    def cols(cb):
        return pl.BlockSpec((tb, w), lambda i: (i, cb // N_HEADS))

    return pl.pallas_call(
        functools.partial(_hgrn_kernel, n_chunks=tb // HG_CHUNK),
        grid=(t // tb,),
        in_specs=[cols(CB_HQ), pl.BlockSpec((tb, w), lambda i: (i, 0)), cols(CB_HI),
                  pl.BlockSpec((1, w), lambda i: (0, 0))],
        out_specs=pl.BlockSpec((tb, w), lambda i: (i, 0)),
        out_shape=jax.ShapeDtypeStruct((t, w), BF16),
        scratch_shapes=[pltpu.VMEM((N_HEADS, HEAD_DIM, HEAD_DIM), F32)],
        compiler_params=_cparams(("arbitrary",)),
        name="hgrn2",
    )(proj, hf, proj, lower)


def _ret_kernel(q_ref, qr_ref, k_ref, kr_ref, v_ref, cos_ref, sin_ref,
                dec_ref, qd_ref, kd_ref, cd_ref, o_ref, s_ref):
    @pl.when(pl.program_id(0) == 0)
    def _():
        s_ref[...] = jnp.zeros_like(s_ref)

    cos = cos_ref[...]
    sin = sin_ref[...]
    scale = HEAD_DIM ** -0.5
    for h in range(N_HEADS):
        lanes = slice(h * HEAD_DIM, (h + 1) * HEAD_DIM)
        q = q_ref[:, lanes].astype(F32) * cos + qr_ref[:, lanes].astype(F32) * sin
        k = (k_ref[:, lanes].astype(F32) * cos + kr_ref[:, lanes].astype(F32) * sin) * scale
        v = v_ref[:, lanes]
        qb = q.astype(BF16)
        s = s_ref[h]
        scores = _dot_nt(qb, k.astype(BF16)) * dec_ref[h]
        o = _dot(qb, s.astype(BF16)) * qd_ref[h] + _dot(scores.astype(BF16), v)
        o_ref[:, lanes] = o.astype(BF16)
        s_ref[h] = cd_ref[h] * s + _dot_tn((k * kd_ref[h]).astype(BF16), v)


def _ret(proj, cos_r, sin_r):
    t = proj.shape[0]
    c = min(t, RET_CHUNK)
    w = N_HEADS * HEAD_DIM
    log_gamma = jnp.log(1.0 - 2.0 ** (-5.0 - jnp.arange(N_HEADS, dtype=F32)))
    idx = jnp.arange(c, dtype=F32)
    rel = idx[:, None] - idx[None, :]
    decay = jnp.exp(jnp.where(rel >= 0, log_gamma[:, None, None] * rel, NEG_BIG))
    ones = jnp.ones((1, 1, HEAD_DIM), F32)
    q_decay = jnp.exp(log_gamma[:, None] * (idx + 1.0))[:, :, None] * ones
    k_decay = jnp.exp(log_gamma[:, None] * (c - 1.0 - idx))[:, :, None] * ones
    c_decay = jnp.exp(log_gamma * c)[:, None, None] * ones

    def cols(cb):
        return pl.BlockSpec((c, w), lambda i: (i, cb // N_HEADS))

    def const(shape):
        return pl.BlockSpec(shape, lambda i: (0, 0, 0))

    tab = pl.BlockSpec((c, LANE), lambda i: (i, 0))
    return pl.pallas_call(
        _ret_kernel,
        grid=(t // c,),
        in_specs=[cols(CB_RQ), cols(CB_RQR), cols(CB_RK), cols(CB_RKR), cols(CB_RV), tab, tab,
                  const((N_HEADS, c, c)), const((N_HEADS, c, HEAD_DIM)),
                  const((N_HEADS, c, HEAD_DIM)), const((N_HEADS, 1, HEAD_DIM))],
        out_specs=pl.BlockSpec((c, w), lambda i: (i, 0)),
        out_shape=jax.ShapeDtypeStruct((t, w), BF16),
        scratch_shapes=[pltpu.VMEM((N_HEADS, HEAD_DIM, HEAD_DIM), F32)],
        compiler_params=_cparams(("arbitrary",)),
        name="retention",
    )(proj, proj, proj, proj, proj, cos_r, sin_r, decay, q_decay, k_decay, c_decay)


def _swa_kernel(sink_ref, q_ref, qr_ref, kc_ref, krc_ref, vac_ref, vbc_ref,
                kp_ref, krp_ref, vap_ref, vbp_ref, cc_ref, sc_ref, cp_ref, sp_ref, o_ref):
    b = SWA_BLOCK
    nb = q_ref.shape[0] // b
    first = pl.program_id(0) * nb
    cos_c, sin_c = cc_ref[...], sc_ref[...]
    cos_p, sin_p = cp_ref[...], sp_ref[...]
    qi = lax.broadcasted_iota(jnp.int32, (b, 2 * b), 0) + b
    ki = lax.broadcasted_iota(jnp.int32, (b, 2 * b), 1)
    rel = qi - ki
    in_window = (rel >= 0) & (rel < b)
    low = lax.broadcasted_iota(jnp.int32, (b, LANE), 1) < SWA_DIM
    pairs_per_group = SWA_Q_HEADS // SWA_KV_HEADS // 2
    for g in range(SWA_KV_HEADS):
        kl = slice(g * LANE, (g + 1) * LANE)
        k_cur = kc_ref[:, kl].astype(F32) * cos_c + krc_ref[:, kl].astype(F32) * sin_c
        k_prev = kp_ref[:, kl].astype(F32) * cos_p + krp_ref[:, kl].astype(F32) * sin_p
        k_all = jnp.concatenate([k_prev, k_cur], axis=0).astype(BF16)
        v_lo = jnp.concatenate([vap_ref[:, kl], vac_ref[:, kl]], axis=0)
        v_hi = jnp.concatenate([vbp_ref[:, kl], vbc_ref[:, kl]], axis=0)
        for sb in range(nb):
            win = slice(sb * b, (sb + 2) * b)
            rows = slice(sb * b, (sb + 1) * b)
            keep = in_window & ((first + sb > 0) | (ki >= b))
            kw = k_all[win]
            for j in range(pairs_per_group):
                pair = g * pairs_per_group + j
                ql = slice(pair * LANE, (pair + 1) * LANE)
                q = (q_ref[rows, ql].astype(F32) * cos_c[rows] + qr_ref[rows, ql].astype(F32) * sin_c[rows])
                out = None
                for half, vw in ((0, v_lo[win]), (1, v_hi[win])):
                    qh = jnp.where(low if half == 0 else ~low, q, 0.0).astype(BF16)
                    s = _dot_nt(qh, kw) * (SWA_DIM ** -0.5)
                    s = jnp.where(keep, s, NEG_BIG)
                    sink = sink_ref[2 * pair + half]
                    m = jnp.maximum(jnp.max(s, axis=-1, keepdims=True), sink)
                    p = jnp.exp(s - m)
                    denom = jnp.sum(p, axis=-1, keepdims=True) + jnp.exp(sink - m)
                    o = _dot(p.astype(BF16), vw) / denom
                    out = o if out is None else out + o
                o_ref[rows, ql] = out.astype(BF16)


def _swa(proj, cos_s, sin_s, sinks):
    t = proj.shape[0]
    b = SWA_BLOCK
    nb = 4 if t % (4 * b) == 0 else 1
    qw = SWA_Q_HEADS * SWA_DIM
    kw = SWA_KV_HEADS * LANE

    def cur(cb, width):
        return pl.BlockSpec((nb * b, width), lambda i: (i, cb * LANE // width))

    def prev(cb, width):
        return pl.BlockSpec((b, width), lambda i: (jnp.maximum(nb * i - 1, 0), cb * LANE // width))

    tab_c = pl.BlockSpec((nb * b, LANE), lambda i: (i, 0))
    tab_p = pl.BlockSpec((b, LANE), lambda i: (jnp.maximum(nb * i - 1, 0), 0))
    return pl.pallas_call(
        _swa_kernel,
        grid=(t // (nb * b),),
        in_specs=[pl.BlockSpec(memory_space=pltpu.SMEM),
                  cur(CB_SQ, qw), cur(CB_SQR, qw),
                  cur(CB_SK, kw), cur(CB_SKR, kw), cur(CB_SVA, kw), cur(CB_SVB, kw),
                  prev(CB_SK, kw), prev(CB_SKR, kw), prev(CB_SVA, kw), prev(CB_SVB, kw),
                  tab_c, tab_c, tab_p, tab_p],
        out_specs=pl.BlockSpec((nb * b, qw), lambda i: (i, 0)),
        out_shape=jax.ShapeDtypeStruct((t, qw), BF16),
        compiler_params=_cparams(("parallel",)),
        name="swa",
    )(sinks, proj, proj, proj, proj, proj, proj, proj, proj, proj, proj, cos_s, sin_s, cos_s, sin_s)


def _merge_kernel(x_ref, oa_ref, hg_ref, ob_ref, rg_ref, oc_ref, ga_ref, gb_ref, gc_ref,
                  hn_ref, wa_ref, wb_ref, wc_ref, wo_ref, o_ref):
    hg = hg_ref[...].astype(F32)
    a = _rms(oa_ref[...].astype(F32)) * hn_ref[...] * (hg * _sigmoid(hg))
    ya = _dot(a.astype(BF16), wa_ref[...])
    rg = rg_ref[...].astype(F32)
    ob = ob_ref[...].astype(F32)
    parts = []
    for h in range(N_HEADS):
        lanes = slice(h * HEAD_DIM, (h + 1) * HEAD_DIM)
        parts.append(_rms(ob[:, lanes]))
    bn = jnp.concatenate(parts, axis=-1) * (rg * _sigmoid(rg))
    yb = _dot(bn.astype(BF16), wb_ref[...])
    yc = _dot(oc_ref[...], wc_ref[...])
    mixed = (_sigmoid(ga_ref[...].astype(F32)) * ya + _sigmoid(gb_ref[...].astype(F32)) * yb
             + _sigmoid(gc_ref[...].astype(F32)) * yc)
    o_ref[...] = x_ref[...] + _dot(mixed.astype(BF16), wo_ref[...])


def _merge(x, proj, oa, ob, oc, hn, wa, wb, wc, wo):
    t = x.shape[0]
    tt = min(t, 512)
    w = N_HEADS * HEAD_DIM
    qw = SWA_Q_HEADS * SWA_DIM

    def rows(width, cb=0):
        return pl.BlockSpec((tt, width), lambda i: (i, cb * LANE // width))

    def full(shape):
        return pl.BlockSpec(shape, lambda i: (0, 0))

    return pl.pallas_call(
        _merge_kernel,
        grid=(t // tt,),
        in_specs=[rows(D_MODEL), rows(w), rows(w, CB_HG), rows(w), rows(w, CB_RG), rows(qw),
                  rows(D_MODEL, CB_GATE), rows(D_MODEL, CB_GATE + 8), rows(D_MODEL, CB_GATE + 16),
                  full((1, w)), full((w, D_MODEL)), full((w, D_MODEL)), full((qw, D_MODEL)),
                  full((D_MODEL, D_MODEL))],
        out_specs=rows(D_MODEL),
        out_shape=jax.ShapeDtypeStruct((t, D_MODEL), F32),
        compiler_params=_cparams(("parallel",)),
        name="merge",
    )(x, oa, proj, ob, proj, oc, proj, proj, proj, hn, wa, wb, wc, wo)


def _memkv_kernel(m_ref, g_ref, wk_ref, wv_ref, k_ref, v_ref):
    h = (_rms(m_ref[...]) * g_ref[...]).astype(BF16)
    k_ref[...] = _dot(h, wk_ref[...]).astype(BF16)
    v_ref[...] = _dot(h, wv_ref[...]).astype(BF16)


def _memkv(mem, g, wk, wv):
    nm = mem.shape[0]
    out = jax.ShapeDtypeStruct((nm, D_MODEL), BF16)
    return pl.pallas_call(
        _memkv_kernel,
        out_shape=[out, out],
        compiler_params=pltpu.CompilerParams(vmem_limit_bytes=VMEM_LIMIT),
        name="mem_kv",
    )(mem, g, wk, wv)


def _xattn_kernel(x_ref, g_ref, wq_ref, k_ref, v_ref, wo_ref, o_ref):
    x = x_ref[...]
    h = (_rms(x) * g_ref[...]).astype(BF16)
    q = _dot(h, wq_ref[...])
    outs = []
    for hh in range(X_HEADS):
        lanes = slice(hh * X_DIM, (hh + 1) * X_DIM)
        s = _dot_nt(q[:, lanes].astype(BF16), k_ref[:, lanes]) * (X_DIM ** -0.5)
        m = jnp.max(s, axis=-1, keepdims=True)
        p = jnp.exp(s - m)
        p = p / jnp.sum(p, axis=-1, keepdims=True)
        outs.append(_dot(p.astype(BF16), v_ref[:, lanes]))
    o = jnp.concatenate(outs, axis=-1)
    o_ref[...] = x + _dot(o.astype(BF16), wo_ref[...])


def _xattn(x, g, wq, k, v, wo):
    t = x.shape[0]
    tt = min(t, 512)
    nm = k.shape[0]

    def full(shape):
        return pl.BlockSpec(shape, lambda i: (0, 0))

    rows = pl.BlockSpec((tt, D_MODEL), lambda i: (i, 0))
    return pl.pallas_call(
        _xattn_kernel,
        grid=(t // tt,),
        in_specs=[rows, full((1, D_MODEL)), full((D_MODEL, D_MODEL)),
                  full((nm, D_MODEL)), full((nm, D_MODEL)), full((D_MODEL, D_MODEL))],
        out_specs=rows,
        out_shape=jax.ShapeDtypeStruct((t, D_MODEL), F32),
        compiler_params=_cparams(("parallel",)),
        name="xattn",
    )(x, g, wq, k, v, wo)


_CAND_ROWS = [PEER_TOPK // (k1 + 1) for k1 in range(8)]


def _top_values(s, n, with_rank=False):
    vals = []
    rank = jnp.full(s.shape, float(n), F32)
    for i in range(n):
        m = jnp.max(s, axis=0, keepdims=True)
        vals.append(m)
        hit = s >= m
        if with_rank:
            rank = jnp.where(hit, float(i), rank)
        s = jnp.where(hit, -jnp.inf, s)
    return (vals, rank) if with_rank else vals


def _bf16_pair_bits(x):
    u = lax.bitcast_convert_type(x.astype(BF16).astype(F32), jnp.uint32)
    return u | (u >> 16)


def _stack_rows(rows, n):
    tt = rows[0].shape[1]
    r = lax.broadcasted_iota(jnp.int32, (n, tt), 0)
    out = jnp.zeros((n, tt), F32)
    for k in range(n):
        out = jnp.where(r == k, rows[k], out)
    return out


def _router_kernel(x_ref, g_ref, wq_ref, kh_ref, kl_ref,
                   hb_ref, a_ref, b_ref, r_ref, l_ref, qt_scr):
    h = _rms(x_ref[...]) * g_ref[...]
    hb_ref[...] = h.T.astype(BF16)
    qt_scr[...] = _dot_nt(wq_ref[...], h.astype(BF16))
    tt = x_ref.shape[0]
    k = PEER_TOPK
    r8 = lax.broadcasted_iota(jnp.int32, (8, tt), 0)

    def head_body(hh, carry):
        scores = []
        for p in range(2):
            r0 = pl.multiple_of(hh * (2 * PEER_HALF) + p * PEER_HALF, PEER_HALF)
            q_hi, q_lo = _split2(qt_scr[pl.ds(r0, PEER_HALF), :])
            kh = kh_ref[p * PEER_HEADS + hh]
            scores.append(_dot(kh, q_hi) + _dot(kl_ref[p * PEER_HEADS + hh], q_hi) + _dot(kh, q_lo))
        s1, s2 = scores
        v1, rank1 = _top_values(s1, k, with_rank=True)
        v2, rank2 = _top_values(s2, k, with_rank=True)
        sv1 = _stack_rows(v1, k)
        sv2 = _stack_rows(v2, k)
        groups = [v1[0] + sv2]
        for k1 in range(1, 8):
            groups.append(jnp.where(r8 < _CAND_ROWS[k1], v1[k1] + sv2[0:8], -jnp.inf))
        groups.append(sv1[8:16] + v2[0])
        cand = jnp.concatenate(groups, axis=0)
        tau = _top_values(cand, k)[k - 1]
        m0 = v1[0] + v2[0]
        z = jnp.sum(jnp.where(cand >= tau, jnp.exp(cand - m0), 0.0), axis=0, keepdims=True)
        a_ref[hh] = _bf16_pair_bits(jnp.exp(s1 - v1[0]))
        b_ref[hh] = (jnp.exp(s2 - v2[0]) / z).astype(BF16)
        count1 = jnp.zeros_like(s1)
        for k1 in range(k):
            n_k1 = jnp.sum((v1[k1] + sv2 >= tau).astype(F32), axis=0, keepdims=True)
            count1 = jnp.where(rank1 == float(k1), n_k1, count1)
        r_ref[hh] = rank2.astype(BF16)
        l_ref[hh] = _bf16_pair_bits(count1)
        return carry

    lax.fori_loop(0, PEER_HEADS, head_body, 0)


def _router(x, g, wq_t, k_hi, k_lo):
    t = x.shape[0]
    tt = min(t, 512)
    nq = wq_t.shape[0]
    sel = jax.ShapeDtypeStruct((PEER_HEADS, PEER_KEYS, t), jnp.uint32)
    sel_b = jax.ShapeDtypeStruct((PEER_HEADS, PEER_KEYS, t), BF16)
    sel_spec = pl.BlockSpec((PEER_HEADS, PEER_KEYS, tt), lambda i: (0, 0, i))
    return pl.pallas_call(
        _router_kernel,
        grid=(t // tt,),
        in_specs=[pl.BlockSpec((tt, D_MODEL), lambda i: (i, 0)),
                  pl.BlockSpec((1, D_MODEL), lambda i: (0, 0)),
                  pl.BlockSpec((nq, D_MODEL), lambda i: (0, 0)),
                  pl.BlockSpec((2 * PEER_HEADS, PEER_KEYS, PEER_HALF), lambda i: (0, 0, 0)),
                  pl.BlockSpec((2 * PEER_HEADS, PEER_KEYS, PEER_HALF), lambda i: (0, 0, 0))],
        out_specs=[pl.BlockSpec((D_MODEL, tt), lambda i: (0, i)), sel_spec, sel_spec, sel_spec, sel_spec],
        out_shape=[jax.ShapeDtypeStruct((D_MODEL, t), BF16), sel, sel_b, sel_b, sel],
        scratch_shapes=[pltpu.VMEM((nq, tt), F32)],
        compiler_params=_cparams(("parallel",)),
        name="peer_router",
    )(x, g, wq_t, k_hi, k_lo)


def _peer_kernel(x_ref, hb_ref, a_ref, l_ref, b_ref, r_ref, u_ref, vt_ref, gf_ref, o_ref,
                 acc_ref, ga_ref, *, final_norm):
    e = pl.program_id(1)

    @pl.when(e == 0)
    def _():
        acc_ref[...] = jnp.zeros_like(acc_ref)

    u_blk = pltpu.bitcast(u_ref[...], BF16)
    vt_blk = pltpu.bitcast(vt_ref[...], BF16)
    n_i1 = u_blk.shape[0] // PEER_KEYS
    sub = 16
    tt = hb_ref.shape[1]
    shape3 = (PEER_KEYS // sub, sub, tt)
    zero3 = jnp.zeros(shape3, BF16)
    piece = PEER_PIECE_KEYS * PEER_KEYS
    for p in range(n_i1 // PEER_PIECE_KEYS):
        for j in range(p * PEER_PIECE_KEYS, (p + 1) * PEER_PIECE_KEYS):
            gate = zero3
            for hh in range(PEER_HEADS):
                arow = pltpu.bitcast(jnp.broadcast_to(a_ref[hh, j:j + 1, :], (8, tt)), BF16)
                lrow = pltpu.bitcast(jnp.broadcast_to(l_ref[hh, j:j + 1, :], (8, tt)), BF16)
                b3 = b_ref[hh].reshape(shape3)
                r3 = r_ref[hh].reshape(shape3)
                gate = gate + arow[None] * jnp.where(r3 < lrow[None], b3, zero3)
            ga_ref[j * PEER_KEYS:(j + 1) * PEER_KEYS, :] = gate.reshape(PEER_KEYS, tt)
        rows = slice(p * piece, (p + 1) * piece)
        pre = _dot(u_blk[rows, :], hb_ref[...])
        act = pre + pre * lax.erf(pre)
        ga_ref[rows, :] = ga_ref[rows, :] * act.astype(BF16)
    acc_ref[...] += _dot(vt_blk, ga_ref[...])

    @pl.when(e == pl.num_programs(1) - 1)
    def _():
        y = x_ref[...] + acc_ref[...].T
        if final_norm:
            y = _rms(y) * gf_ref[...]
        o_ref[...] = y


def _pack_kernel(w_ref, o_ref, *, scale, transpose):
    w = w_ref[...]
    if transpose:
        w = w.T
    o_ref[...] = pltpu.bitcast((w * scale).astype(BF16), jnp.uint32)


def _pack_table(w, scale, transpose):
    rows, cols = w.shape
    if transpose:
        blk = 512
        in_spec = pl.BlockSpec((blk, cols), lambda i: (i, 0))
        out_spec = pl.BlockSpec((cols // 2, blk), lambda i: (0, i))
        out_shape = jax.ShapeDtypeStruct((cols // 2, rows), jnp.uint32)
    else:
        blk = 1024
        in_spec = pl.BlockSpec((blk, cols), lambda i: (i, 0))
        out_spec = pl.BlockSpec((blk // 2, cols), lambda i: (i, 0))
        out_shape = jax.ShapeDtypeStruct((rows // 2, cols), jnp.uint32)
    return pl.pallas_call(
        functools.partial(_pack_kernel, scale=scale, transpose=transpose),
        grid=(rows // blk,),
        in_specs=[in_spec],
        out_specs=out_spec,
        out_shape=out_shape,
        compiler_params=_cparams(("parallel",)),
        name="pack_table_t" if transpose else "pack_table",
    )(w)


def _peer(x, hb, a, l, b, r, u, vt, gf, final_norm):
    t = x.shape[0]
    tt = min(t, 512)
    eb = 16 * PEER_KEYS
    rows = pl.BlockSpec((tt, D_MODEL), lambda i, e: (i, 0))
    i1_rows = pl.BlockSpec((PEER_HEADS, eb // PEER_KEYS, tt), lambda i, e: (0, e, i))
    all_rows = pl.BlockSpec((PEER_HEADS, PEER_KEYS, tt), lambda i, e: (0, 0, i))
    return pl.pallas_call(
        functools.partial(_peer_kernel, final_norm=final_norm),
        grid=(t // tt, N_EXPERTS // eb),
        in_specs=[rows, pl.BlockSpec((D_MODEL, tt), lambda i, e: (0, i)),
                  i1_rows, i1_rows, all_rows, all_rows,
                  pl.BlockSpec((eb // 2, D_MODEL), lambda i, e: (e, 0)),
                  pl.BlockSpec((D_MODEL // 2, eb), lambda i, e: (0, e)),
                  pl.BlockSpec((1, D_MODEL), lambda i, e: (0, 0))],
        out_specs=rows,
        out_shape=jax.ShapeDtypeStruct((t, D_MODEL), F32),
        scratch_shapes=[pltpu.VMEM((D_MODEL, tt), F32), pltpu.VMEM((eb, tt), BF16)],
        compiler_params=_cparams(("parallel", "arbitrary")),
        name="peer_dense",
    )(x, hb, a, l, b, r, u, vt, gf)


def kernel(x, mem, positions, ln_mix, w_in, lb_param, hgrn_norm, swa_sinks, w_br_hgrn, w_br_ret,
           w_br_swa, w_out, ln_xq, ln_xkv, w_xq, w_xk, w_xv, w_xo, ln_ffn, peer_wq, peer_keys,
           peer_u, peer_v, ln_final):
    b, t, d = x.shape
    assert b == 1 and d == D_MODEL
    depth = w_in.shape[0]
    xs = x.reshape(t, d)
    ms = mem.reshape(mem.shape[1], d)
    cos_r, sin_r, cos_s, sin_s = _rope_tables(positions)
    lb_sm = jax.nn.softmax(lb_param.astype(F32), axis=0)
    lower = jnp.cumsum(lb_sm, axis=0) - lb_sm[0]

    for l in range(depth):
        proj, hf = _proj(xs, ln_mix[l][None, :], _relay_w_in(w_in[l]))
        oa = _hgrn(proj, hf, lower[l][None, :])
        ob = _ret(proj, cos_r, sin_r)
        oc = _swa(proj, cos_s, sin_s, swa_sinks[l])
        xs = _merge(xs, proj, oa, ob, oc, hgrn_norm[l][None, :], w_br_hgrn[l].astype(BF16),
                    w_br_ret[l].astype(BF16), w_br_swa[l].astype(BF16), w_out[l].astype(BF16))

        km, vm = _memkv(ms, ln_xkv[l][None, :], w_xk[l].astype(BF16), w_xv[l].astype(BF16))
        xs = _xattn(xs, ln_xq[l][None, :], w_xq[l].astype(BF16), km, vm, w_xo[l].astype(BF16))

        k_hi, k_lo = _split2(peer_keys[l].reshape(2 * PEER_HEADS, PEER_KEYS, PEER_HALF))
        hb, a, bz, r2, cnt = _router(xs, ln_ffn[l][None, :], peer_wq[l].T.astype(BF16), k_hi, k_lo)
        c = 2.0 ** -0.5
        xs = _peer(xs, hb, a, cnt, bz, r2, _pack_table(peer_u[l], c, False), _pack_table(peer_v[l], c, True),
                   ln_final[None, :], final_norm=(l == depth - 1))

    return xs.reshape(b, t, d)
```

```python
import functools
import math

import numpy as np
import jax
import jax.numpy as jnp
from jax import lax
from jax.experimental import pallas as pl
from jax.experimental.pallas import tpu as pltpu

F32 = jnp.float32
BF16 = jnp.bfloat16

D_MODEL = 1024
HEAD_DIM = 128
N_HEADS = 4
SWA_Q_HEADS = 8
SWA_KV_HEADS = 2
SWA_DIM = 64
SWA_BLOCK = 128
ROPE_THETA = 10000.0
N_MEM = 256
X_HEADS = 4
X_DIM = D_MODEL // X_HEADS
PEER_HEADS = 8
PEER_KEYS = 128
PEER_TOPK = 16
PEER_HALF = 128
N_EXPERTS = PEER_KEYS * PEER_KEYS
EPS = 1e-6
NEG_BIG = -1e30
TINY = 1e-30

LANE = 128
HG_CHUNK = 64
RET_CHUNK = 256
PEER_PIECE_KEYS = 4
VMEM_LIMIT = 56 * 1024 * 1024

CB_GATE = 0
CB_HQ, CB_HF, CB_HI, CB_HG = 24, 28, 32, 36
CB_RQ, CB_RQR, CB_RK, CB_RKR, CB_RV, CB_RG = 40, 44, 48, 52, 56, 60
CB_SQ, CB_SQR, CB_SK, CB_SKR, CB_SVA, CB_SVB = 64, 68, 72, 74, 76, 78
N_CB = 80
PROJ_COLS = N_CB * LANE


def _cparams(sem):
    return pltpu.CompilerParams(dimension_semantics=sem, vmem_limit_bytes=VMEM_LIMIT)


def _dot(a, b):
    return jnp.dot(a, b, preferred_element_type=F32)


def _dot_nt(a, b):
    return lax.dot_general(a, b, (((1,), (1,)), ((), ())), preferred_element_type=F32)


def _dot_tn(a, b):
    return lax.dot_general(a, b, (((0,), (0,)), ((), ())), preferred_element_type=F32)


def _split2(x):
    hi = x.astype(BF16)
    lo = (x - hi.astype(F32)).astype(BF16)
    return hi, lo


def _split3(x):
    hi = x.astype(BF16)
    r = x - hi.astype(F32)
    mid = r.astype(BF16)
    lo = (r - mid.astype(F32)).astype(BF16)
    return hi, mid, lo


def _sigmoid(x):
    return 1.0 / (1.0 + jnp.exp(-x))


def _rms(x):
    return x * lax.rsqrt(jnp.mean(x * x, axis=-1, keepdims=True) + EPS)


def _rope_kernel(pos_ref, fr_ref, fs_ref, cr_ref, sr_ref, cs_ref, ss_ref):
    pos = pos_ref[...].astype(F32)
    ang_r = pos * fr_ref[...]
    cr_ref[...] = jnp.cos(ang_r)
    sr_ref[...] = jnp.sin(ang_r)
    ang_s = pos * fs_ref[...]
    cs_ref[...] = jnp.cos(ang_s)
    ss_ref[...] = jnp.sin(ang_s)


def _rope_tables(positions):
    t = positions.shape[1]
    tt = min(t, 1024)
    ret_f = ROPE_THETA ** (-jnp.linspace(0.0, 1.0, HEAD_DIM // 2, dtype=F32))
    swa_f = ROPE_THETA ** (-jnp.arange(0, SWA_DIM, 2, dtype=F32) / SWA_DIM)
    fr = jnp.concatenate([ret_f, ret_f])[None, :]
    fs = jnp.concatenate([swa_f] * (LANE // (SWA_DIM // 2)))[None, :]
    pos = positions.reshape(t, 1)
    tab = jax.ShapeDtypeStruct((t, LANE), F32)
    row = pl.BlockSpec((tt, LANE), lambda i: (i, 0))
    vec = pl.BlockSpec((1, LANE), lambda i: (0, 0))
    return pl.pallas_call(
        _rope_kernel,
        grid=(t // tt,),
        in_specs=[pl.BlockSpec((tt, 1), lambda i: (i, 0)), vec, vec],
        out_specs=[row, row, row, row],
        out_shape=[tab, tab, tab, tab],
        compiler_params=_cparams(("parallel",)),
        name="rope_tables",
    )(pos, fr, fs)


PROJ_TN = 1024


def _proj_kernel(x_ref, g_ref, w_ref, o_ref, f_ref, h_scr):
    @pl.when(pl.program_id(1) == 0)
    def _():
        h_scr[...] = (_rms(x_ref[...]) * g_ref[...]).astype(BF16)

    y = _dot(h_scr[...], w_ref[...])
    o_ref[...] = y.astype(BF16)

    @pl.when(pl.program_id(1) == CB_HF * LANE // PROJ_TN)
    def _():
        lo = CB_HF * LANE % PROJ_TN
        f_ref[...] = y[:, lo:lo + N_HEADS * HEAD_DIM]


def _proj(x, g, w):
    t = x.shape[0]
    n = w.shape[1]
    tm = min(t, 2048)
    tn = PROJ_TN
    wf = N_HEADS * HEAD_DIM
    return pl.pallas_call(
        _proj_kernel,
        grid=(t // tm, n // tn),
        in_specs=[pl.BlockSpec((tm, D_MODEL), lambda i, j: (i, 0)),
                  pl.BlockSpec((1, D_MODEL), lambda i, j: (0, 0)),
                  pl.BlockSpec((D_MODEL, tn), lambda i, j: (0, j))],
        out_specs=[pl.BlockSpec((tm, tn), lambda i, j: (i, j)),
                   pl.BlockSpec((tm, wf), lambda i, j: (i, 0))],
        out_shape=[jax.ShapeDtypeStruct((t, n), BF16), jax.ShapeDtypeStruct((t, wf), F32)],
        scratch_shapes=[pltpu.VMEM((tm, D_MODEL), BF16)],
        compiler_params=_cparams(("parallel", "arbitrary")),
        name="in_proj",
    )(x, g, w)


def _relay_w_in(w_in):
    d = w_in.shape[0]
    sizes = (512,) * 8 + (512, 128, 128, 3 * D_MODEL)
    offs = np.concatenate([[0], np.cumsum(sizes)])
    hq, hf, hi, hg, rq, rk, rv, rg, sq, sk, sv, gates = [
        w_in[:, int(offs[i]):int(offs[i + 1])] for i in range(12)]

    def rot(w, nh, dh):
        w = w.reshape(d, nh, dh)
        return jnp.concatenate([-w[..., dh // 2:], w[..., :dh // 2]], axis=-1).reshape(d, nh * dh)

    def widen(w, lo, hi):
        w = w.reshape(d, SWA_KV_HEADS, SWA_DIM)
        zero = jnp.zeros_like(w)
        parts = [w if lo else zero, w if hi else zero]
        return jnp.concatenate(parts, axis=-1).reshape(d, SWA_KV_HEADS * LANE)

    cols = [gates, hq, hf, hi, hg,
            rq, rot(rq, N_HEADS, HEAD_DIM), rk, rot(rk, N_HEADS, HEAD_DIM), rv, rg,
            sq, rot(sq, SWA_Q_HEADS, SWA_DIM),
            widen(sk, True, True), widen(rot(sk, SWA_KV_HEADS, SWA_DIM), True, True),
            widen(sv, True, False), widen(sv, False, True)]
    out = jnp.concatenate(cols, axis=1).astype(BF16)
    assert out.shape[1] == PROJ_COLS
    return out


def _hgrn_kernel(hq_ref, hf_ref, hi_ref, lb_ref, o_ref, st_ref, *, n_chunks):
    @pl.when(pl.program_id(0) == 0)
    def _():
        st_ref[...] = jnp.zeros_like(st_ref)

    c = HG_CHUNK
    row = lax.broadcasted_iota(jnp.int32, (c, c), 0)
    col = lax.broadcasted_iota(jnp.int32, (c, c), 1)
    diag = row == col
    halves = [c >> (i + 1) for i in range(c.bit_length() - 1)]
    pair_mask = {b: (row // (2 * b) == col // (2 * b)) & (row % (2 * b) >= b) & (col % (2 * b) < b)
                 for b in halves}
    rw = lax.broadcasted_iota(jnp.int32, (c, HEAD_DIM), 0)
    r3 = lax.broadcasted_iota(jnp.int32, (c // 8, 8, HEAD_DIM), 1)
    scale = HEAD_DIM ** -0.5
    log2e = math.log2(math.e)

    def boundary(cum2, b):
        if b >= 8:
            ends = [cum2[s + b - 1:s + b, :] for s in range(0, c, 2 * b)]
            out = ends[-1]
            for i in range(len(ends) - 2, -1, -1):
                out = jnp.where(rw < (i + 1) * 2 * b, ends[i], out)
            return out
        c3 = cum2.reshape(c // 8, 8, HEAD_DIM)
        if b == 4:
            return jnp.broadcast_to(c3[:, 3:4, :], c3.shape).reshape(c, HEAD_DIM)
        if b == 2:
            lo = jnp.broadcast_to(c3[:, 1:2, :], c3.shape)
            hi = jnp.broadcast_to(c3[:, 5:6, :], c3.shape)
            return jnp.where(r3 < 4, lo, hi).reshape(c, HEAD_DIM)
        prev = pltpu.roll(cum2, 1, 0)
        return jnp.where(rw % 2 == 1, prev, cum2)

    def chunk_body(ci, carry):
        r0 = pl.multiple_of(ci * c, c)
        for h in range(N_HEADS):
            lanes = slice(h * HEAD_DIM, (h + 1) * HEAD_DIM)
            xq = hq_ref[pl.ds(r0, c), lanes].astype(F32)
            z = hf_ref[pl.ds(r0, c), lanes]
            v = hi_ref[pl.ds(r0, c), lanes]
            lb = lb_ref[:, lanes]
            omlb = 1.0 - lb
            q = xq * _sigmoid(xq) * scale
            f = lb + omlb * _sigmoid(z)
            logf = jnp.log(jnp.maximum(f, TINY))
            kk = omlb * _sigmoid(-z)
            cum2 = logf * log2e
            step = 1
            while step < c:
                cum2 = cum2 + jnp.where(rw >= step, pltpu.roll(cum2, step, 0), 0.0)
                step *= 2
            scores = jnp.where(diag, jnp.sum(q * kk, axis=-1, keepdims=True), 0.0)
            for b in halves:
                ref = boundary(cum2, b)
                qs = (q * jnp.exp2(jnp.minimum(cum2 - ref, 0.0))).astype(BF16)
                ks = (kk * jnp.exp2(jnp.minimum(ref - cum2, 0.0))).astype(BF16)
                scores = jnp.where(pair_mask[b], _dot_nt(qs, ks), scores)
            st = st_ref[h]
            qe = (q * jnp.exp2(cum2)).astype(BF16)
            o = _dot_nt(qe, st.astype(BF16)) + _dot(scores.astype(BF16), v.astype(BF16))
            o_ref[pl.ds(r0, c), lanes] = o.astype(BF16)
            last = cum2[c - 1:c, :]
            kl = (kk * jnp.exp2(last - cum2)).astype(BF16)
            st_ref[h] = jnp.exp2(last) * st + _dot_tn(v.astype(BF16), kl)
        return carry

    lax.fori_loop(0, n_chunks, chunk_body, 0, unroll=4)


def _hgrn(proj, hf, lower):
    t = proj.shape[0]
    tb = min(t, 512)
    w = N_HEADS * HEAD_DIM

    def cols(cb):
        return pl.BlockSpec((tb, w), lambda i: (i, cb // N_HEADS))

    return pl.pallas_call(
        functools.partial(_hgrn_kernel, n_chunks=tb // HG_CHUNK),
        grid=(t // tb,),
        in_specs=[cols(CB_HQ), pl.BlockSpec((tb, w), lambda i: (i, 0)), cols(CB_HI),
                  pl.BlockSpec((1, w), lambda i: (0, 0))],
        out_specs=pl.BlockSpec((tb, w), lambda i: (i, 0)),
        out_shape=jax.ShapeDtypeStruct((t, w), BF16),
        scratch_shapes=[pltpu.VMEM((N_HEADS, HEAD_DIM, HEAD_DIM), F32)],
        compiler_params=_cparams(("arbitrary",)),
        name="hgrn2",
    )(proj, hf, proj, lower)


def _ret_kernel(q_ref, qr_ref, k_ref, kr_ref, v_ref, cos_ref, sin_ref,
                dec_ref, qd_ref, kd_ref, cd_ref, o_ref, s_ref):
    @pl.when(pl.program_id(0) == 0)
    def _():
        s_ref[...] = jnp.zeros_like(s_ref)

    c = dec_ref.shape[1]
    scale = HEAD_DIM ** -0.5
    for ci in range(q_ref.shape[0] // c):
        rows = slice(ci * c, (ci + 1) * c)
        cos = cos_ref[rows, :]
        sin = sin_ref[rows, :]
        for h in range(N_HEADS):
            lanes = slice(h * HEAD_DIM, (h + 1) * HEAD_DIM)
            q = q_ref[rows, lanes].astype(F32) * cos + qr_ref[rows, lanes].astype(F32) * sin
            k = (k_ref[rows, lanes].astype(F32) * cos + kr_ref[rows, lanes].astype(F32) * sin) * scale
            v = v_ref[rows, lanes]
            qb = q.astype(BF16)
            s = s_ref[h]
            scores = _dot_nt(qb, k.astype(BF16)) * dec_ref[h]
            o = _dot(qb, s.astype(BF16)) * qd_ref[h] + _dot(scores.astype(BF16), v)
            o_ref[rows, lanes] = o.astype(BF16)
            s_ref[h] = cd_ref[h] * s + _dot_tn((k * kd_ref[h]).astype(BF16), v)


def _ret(proj, cos_r, sin_r):
    t = proj.shape[0]
    c = min(t, RET_CHUNK)
    tb = 4 * c if t % (4 * c) == 0 else c
    w = N_HEADS * HEAD_DIM
    log_gamma = jnp.log(1.0 - 2.0 ** (-5.0 - jnp.arange(N_HEADS, dtype=F32)))
    idx = jnp.arange(c, dtype=F32)
    rel = idx[:, None] - idx[None, :]
    decay = jnp.exp(jnp.where(rel >= 0, log_gamma[:, None, None] * rel, NEG_BIG))
    ones = jnp.ones((1, 1, HEAD_DIM), F32)
    q_decay = jnp.exp(log_gamma[:, None] * (idx + 1.0))[:, :, None] * ones
    k_decay = jnp.exp(log_gamma[:, None] * (c - 1.0 - idx))[:, :, None] * ones
    c_decay = jnp.exp(log_gamma * c)[:, None, None] * ones

    def cols(cb):
        return pl.BlockSpec((tb, w), lambda i: (i, cb // N_HEADS))

    def const(shape):
        return pl.BlockSpec(shape, lambda i: (0, 0, 0))

    tab = pl.BlockSpec((tb, LANE), lambda i: (i, 0))
    return pl.pallas_call(
        _ret_kernel,
        grid=(t // tb,),
        in_specs=[cols(CB_RQ), cols(CB_RQR), cols(CB_RK), cols(CB_RKR), cols(CB_RV), tab, tab,
                  const((N_HEADS, c, c)), const((N_HEADS, c, HEAD_DIM)),
                  const((N_HEADS, c, HEAD_DIM)), const((N_HEADS, 1, HEAD_DIM))],
        out_specs=pl.BlockSpec((tb, w), lambda i: (i, 0)),
        out_shape=jax.ShapeDtypeStruct((t, w), BF16),
        scratch_shapes=[pltpu.VMEM((N_HEADS, HEAD_DIM, HEAD_DIM), F32)],
        compiler_params=_cparams(("arbitrary",)),
        name="retention",
    )(proj, proj, proj, proj, proj, cos_r, sin_r, decay, q_decay, k_decay, c_decay)


def _swa_kernel(sink_ref, q_ref, qr_ref, kc_ref, krc_ref, vac_ref, vbc_ref,
                kp_ref, krp_ref, vap_ref, vbp_ref, cc_ref, sc_ref, cp_ref, sp_ref, o_ref):
    b = SWA_BLOCK
    nb = q_ref.shape[0] // b
    first = pl.program_id(0) * nb
    cos_c, sin_c = cc_ref[...], sc_ref[...]
    cos_p, sin_p = cp_ref[...], sp_ref[...]
    qi = lax.broadcasted_iota(jnp.int32, (b, 2 * b), 0) + b
    ki = lax.broadcasted_iota(jnp.int32, (b, 2 * b), 1)
    rel = qi - ki
    in_window = (rel >= 0) & (rel < b)
    low = lax.broadcasted_iota(jnp.int32, (b, LANE), 1) < SWA_DIM
    pairs_per_group = SWA_Q_HEADS // SWA_KV_HEADS // 2
    for g in range(SWA_KV_HEADS):
        kl = slice(g * LANE, (g + 1) * LANE)
        k_cur = kc_ref[:, kl].astype(F32) * cos_c + krc_ref[:, kl].astype(F32) * sin_c
        k_prev = kp_ref[:, kl].astype(F32) * cos_p + krp_ref[:, kl].astype(F32) * sin_p
        k_all = jnp.concatenate([k_prev, k_cur], axis=0).astype(BF16)
        v_lo = jnp.concatenate([vap_ref[:, kl], vac_ref[:, kl]], axis=0)
        v_hi = jnp.concatenate([vbp_ref[:, kl], vbc_ref[:, kl]], axis=0)
        for sb in range(nb):
            win = slice(sb * b, (sb + 2) * b)
            rows = slice(sb * b, (sb + 1) * b)
            keep = in_window & ((first + sb > 0) | (ki >= b))
            kw = k_all[win]
            for j in range(pairs_per_group):
                pair = g * pairs_per_group + j
                ql = slice(pair * LANE, (pair + 1) * LANE)
                q = (q_ref[rows, ql].astype(F32) * cos_c[rows] + qr_ref[rows, ql].astype(F32) * sin_c[rows])
                out = None
                for half, vw in ((0, v_lo[win]), (1, v_hi[win])):
                    qh = jnp.where(low if half == 0 else ~low, q, 0.0).astype(BF16)
                    s = _dot_nt(qh, kw) * (SWA_DIM ** -0.5)
                    s = jnp.where(keep, s, NEG_BIG)
                    sink = sink_ref[2 * pair + half]
                    m = jnp.maximum(jnp.max(s, axis=-1, keepdims=True), sink)
                    p = jnp.exp(s - m)
                    denom = jnp.sum(p, axis=-1, keepdims=True) + jnp.exp(sink - m)
                    o = _dot(p.astype(BF16), vw) / denom
                    out = o if out is None else out + o
                o_ref[rows, ql] = out.astype(BF16)


def _swa(proj, cos_s, sin_s, sinks):
    t = proj.shape[0]
    b = SWA_BLOCK
    nb = 4 if t % (4 * b) == 0 else 1
    qw = SWA_Q_HEADS * SWA_DIM
    kw = SWA_KV_HEADS * LANE

    def cur(cb, width):
        return pl.BlockSpec((nb * b, width), lambda i: (i, cb * LANE // width))

    def prev(cb, width):
        return pl.BlockSpec((b, width), lambda i: (jnp.maximum(nb * i - 1, 0), cb * LANE // width))

    tab_c = pl.BlockSpec((nb * b, LANE), lambda i: (i, 0))
    tab_p = pl.BlockSpec((b, LANE), lambda i: (jnp.maximum(nb * i - 1, 0), 0))
    return pl.pallas_call(
        _swa_kernel,
        grid=(t // (nb * b),),
        in_specs=[pl.BlockSpec(memory_space=pltpu.SMEM),
                  cur(CB_SQ, qw), cur(CB_SQR, qw),
                  cur(CB_SK, kw), cur(CB_SKR, kw), cur(CB_SVA, kw), cur(CB_SVB, kw),
                  prev(CB_SK, kw), prev(CB_SKR, kw), prev(CB_SVA, kw), prev(CB_SVB, kw),
                  tab_c, tab_c, tab_p, tab_p],
        out_specs=pl.BlockSpec((nb * b, qw), lambda i: (i, 0)),
        out_shape=jax.ShapeDtypeStruct((t, qw), BF16),
        compiler_params=_cparams(("parallel",)),
        name="swa",
    )(sinks, proj, proj, proj, proj, proj, proj, proj, proj, proj, proj, cos_s, sin_s, cos_s, sin_s)


def _merge_kernel(x_ref, oa_ref, hg_ref, ob_ref, rg_ref, oc_ref, ga_ref, gb_ref, gc_ref,
                  hn_ref, wa_ref, wb_ref, wc_ref, wo_ref, o_ref):
    hg = hg_ref[...].astype(F32)
    a = _rms(oa_ref[...].astype(F32)) * hn_ref[...] * (hg * _sigmoid(hg))
    ya = _dot(a.astype(BF16), wa_ref[...])
    rg = rg_ref[...].astype(F32)
    ob = ob_ref[...].astype(F32)
    parts = []
    for h in range(N_HEADS):
        lanes = slice(h * HEAD_DIM, (h + 1) * HEAD_DIM)
        parts.append(_rms(ob[:, lanes]))
    bn = jnp.concatenate(parts, axis=-1) * (rg * _sigmoid(rg))
    yb = _dot(bn.astype(BF16), wb_ref[...])
    yc = _dot(oc_ref[...], wc_ref[...])
    mixed = (_sigmoid(ga_ref[...].astype(F32)) * ya + _sigmoid(gb_ref[...].astype(F32)) * yb
             + _sigmoid(gc_ref[...].astype(F32)) * yc)
    o_ref[...] = x_ref[...] + _dot(mixed.astype(BF16), wo_ref[...])


def _merge(x, proj, oa, ob, oc, hn, wa, wb, wc, wo):
    t = x.shape[0]
    tt = min(t, 512)
    w = N_HEADS * HEAD_DIM
    qw = SWA_Q_HEADS * SWA_DIM

    def rows(width, cb=0):
        return pl.BlockSpec((tt, width), lambda i: (i, cb * LANE // width))

    def full(shape):
        return pl.BlockSpec(shape, lambda i: (0, 0))

    return pl.pallas_call(
        _merge_kernel,
        grid=(t // tt,),
        in_specs=[rows(D_MODEL), rows(w), rows(w, CB_HG), rows(w), rows(w, CB_RG), rows(qw),
                  rows(D_MODEL, CB_GATE), rows(D_MODEL, CB_GATE + 8), rows(D_MODEL, CB_GATE + 16),
                  full((1, w)), full((w, D_MODEL)), full((w, D_MODEL)), full((qw, D_MODEL)),
                  full((D_MODEL, D_MODEL))],
        out_specs=rows(D_MODEL),
        out_shape=jax.ShapeDtypeStruct((t, D_MODEL), F32),
        compiler_params=_cparams(("parallel",)),
        name="merge",
    )(x, oa, proj, ob, proj, oc, proj, proj, proj, hn, wa, wb, wc, wo)


def _memkv_kernel(m_ref, g_ref, wk_ref, wv_ref, k_ref, v_ref):
    h = (_rms(m_ref[...]) * g_ref[...]).astype(BF16)
    k_ref[...] = _dot(h, wk_ref[...]).astype(BF16)
    v_ref[...] = _dot(h, wv_ref[...]).astype(BF16)


def _memkv(mem, g, wk, wv):
    nm = mem.shape[0]
    out = jax.ShapeDtypeStruct((nm, D_MODEL), BF16)
    return pl.pallas_call(
        _memkv_kernel,
        out_shape=[out, out],
        compiler_params=pltpu.CompilerParams(vmem_limit_bytes=VMEM_LIMIT),
        name="mem_kv",
    )(mem, g, wk, wv)


def _xattn_kernel(x_ref, g_ref, wq_ref, k_ref, v_ref, wo_ref, o_ref):
    x = x_ref[...]
    h = (_rms(x) * g_ref[...]).astype(BF16)
    q = _dot(h, wq_ref[...])
    outs = []
    for hh in range(X_HEADS):
        lanes = slice(hh * X_DIM, (hh + 1) * X_DIM)
        s = _dot_nt(q[:, lanes].astype(BF16), k_ref[:, lanes]) * (X_DIM ** -0.5)
        m = jnp.max(s, axis=-1, keepdims=True)
        p = jnp.exp(s - m)
        p = p / jnp.sum(p, axis=-1, keepdims=True)
        outs.append(_dot(p.astype(BF16), v_ref[:, lanes]))
    o = jnp.concatenate(outs, axis=-1)
    o_ref[...] = x + _dot(o.astype(BF16), wo_ref[...])


def _xattn(x, g, wq, k, v, wo):
    t = x.shape[0]
    tt = min(t, 512)
    nm = k.shape[0]

    def full(shape):
        return pl.BlockSpec(shape, lambda i: (0, 0))

    rows = pl.BlockSpec((tt, D_MODEL), lambda i: (i, 0))
    return pl.pallas_call(
        _xattn_kernel,
        grid=(t // tt,),
        in_specs=[rows, full((1, D_MODEL)), full((D_MODEL, D_MODEL)),
                  full((nm, D_MODEL)), full((nm, D_MODEL)), full((D_MODEL, D_MODEL))],
        out_specs=rows,
        out_shape=jax.ShapeDtypeStruct((t, D_MODEL), F32),
        compiler_params=_cparams(("parallel",)),
        name="xattn",
    )(x, g, wq, k, v, wo)


_CAND_ROWS = [PEER_TOPK // (k1 + 1) for k1 in range(8)]


def _top_values(s, n, with_rank=False):
    vals = []
    rank = jnp.full(s.shape, float(n), F32)
    for i in range(n):
        m = jnp.max(s, axis=0, keepdims=True)
        vals.append(m)
        hit = s >= m
        if with_rank:
            rank = jnp.where(hit, float(i), rank)
        s = jnp.where(hit, -jnp.inf, s)
    return (vals, rank) if with_rank else vals


def _bf16_pair_bits(x):
    u = lax.bitcast_convert_type(x.astype(BF16).astype(F32), jnp.uint32)
    return u | (u >> 16)


def _stack_rows(rows, n):
    tt = rows[0].shape[1]
    r = lax.broadcasted_iota(jnp.int32, (n, tt), 0)
    out = jnp.zeros((n, tt), F32)
    for k in range(n):
        out = jnp.where(r == k, rows[k], out)
    return out


def _router_kernel(x_ref, g_ref, wq_ref, kh_ref, kl_ref,
                   hb_ref, a_ref, b_ref, r_ref, l_ref, qt_scr):
    h = _rms(x_ref[...]) * g_ref[...]
    hb_ref[...] = h.T.astype(BF16)
    qt_scr[...] = _dot_nt(wq_ref[...], h.astype(BF16))
    tt = x_ref.shape[0]
    k = PEER_TOPK
    r8 = lax.broadcasted_iota(jnp.int32, (8, tt), 0)

    def head_body(hh, carry):
        scores = []
        for p in range(2):
            r0 = pl.multiple_of(hh * (2 * PEER_HALF) + p * PEER_HALF, PEER_HALF)
            q_hi, q_lo = _split2(qt_scr[pl.ds(r0, PEER_HALF), :])
            kh = kh_ref[p * PEER_HEADS + hh]
            scores.append(_dot(kh, q_hi) + _dot(kl_ref[p * PEER_HEADS + hh], q_hi) + _dot(kh, q_lo))
        s1, s2 = scores
        v1, rank1 = _top_values(s1, k, with_rank=True)
        v2, rank2 = _top_values(s2, k, with_rank=True)
        sv1 = _stack_rows(v1, k)
        sv2 = _stack_rows(v2, k)
        groups = [v1[0] + sv2]
        for k1 in range(1, 8):
            groups.append(jnp.where(r8 < _CAND_ROWS[k1], v1[k1] + sv2[0:8], -jnp.inf))
        groups.append(sv1[8:16] + v2[0])
        cand = jnp.concatenate(groups, axis=0)
        tau = _top_values(cand, k)[k - 1]
        m0 = v1[0] + v2[0]
        z = jnp.sum(jnp.where(cand >= tau, jnp.exp(cand - m0), 0.0), axis=0, keepdims=True)
        a_ref[hh] = _bf16_pair_bits(jnp.exp(s1 - v1[0]))
        b_ref[hh] = (jnp.exp(s2 - v2[0]) / z).astype(BF16)
        count1 = jnp.zeros_like(s1)
        for k1 in range(k):
            n_k1 = jnp.sum((v1[k1] + sv2 >= tau).astype(F32), axis=0, keepdims=True)
            count1 = jnp.where(rank1 == float(k1), n_k1, count1)
        r_ref[hh] = rank2.astype(BF16)
        l_ref[hh] = _bf16_pair_bits(count1)
        return carry

    lax.fori_loop(0, PEER_HEADS, head_body, 0, unroll=2)


def _router(x, g, wq_t, k_hi, k_lo):
    t = x.shape[0]
    tt = min(t, 512)
    nq = wq_t.shape[0]
    sel = jax.ShapeDtypeStruct((PEER_HEADS, PEER_KEYS, t), jnp.uint32)
    sel_b = jax.ShapeDtypeStruct((PEER_HEADS, PEER_KEYS, t), BF16)
    sel_spec = pl.BlockSpec((PEER_HEADS, PEER_KEYS, tt), lambda i: (0, 0, i))
    return pl.pallas_call(
        _router_kernel,
        grid=(t // tt,),
        in_specs=[pl.BlockSpec((tt, D_MODEL), lambda i: (i, 0)),
                  pl.BlockSpec((1, D_MODEL), lambda i: (0, 0)),
                  pl.BlockSpec((nq, D_MODEL), lambda i: (0, 0)),
                  pl.BlockSpec((2 * PEER_HEADS, PEER_KEYS, PEER_HALF), lambda i: (0, 0, 0)),
                  pl.BlockSpec((2 * PEER_HEADS, PEER_KEYS, PEER_HALF), lambda i: (0, 0, 0))],
        out_specs=[pl.BlockSpec((D_MODEL, tt), lambda i: (0, i)), sel_spec, sel_spec, sel_spec, sel_spec],
        out_shape=[jax.ShapeDtypeStruct((D_MODEL, t), BF16), sel, sel_b, sel_b, sel],
        scratch_shapes=[pltpu.VMEM((nq, tt), F32)],
        compiler_params=_cparams(("parallel",)),
        name="peer_router",
    )(x, g, wq_t, k_hi, k_lo)


def _peer_kernel(x_ref, hb_ref, a_ref, l_ref, b_ref, r_ref, u_ref, vt_ref, gf_ref, o_ref,
                 acc_ref, ga_ref, *, final_norm):
    e = pl.program_id(1)

    @pl.when(e == 0)
    def _():
        acc_ref[...] = jnp.zeros_like(acc_ref)

    u_blk = pltpu.bitcast(u_ref[...], BF16)
    vt_blk = pltpu.bitcast(vt_ref[...], BF16)
    n_i1 = u_blk.shape[0] // PEER_KEYS
    sub = 16
    tt = hb_ref.shape[1]
    shape3 = (PEER_KEYS // sub, sub, tt)
    zero3 = jnp.zeros(shape3, BF16)
    piece = PEER_PIECE_KEYS * PEER_KEYS
    for p in range(n_i1 // PEER_PIECE_KEYS):
        for j in range(p * PEER_PIECE_KEYS, (p + 1) * PEER_PIECE_KEYS):
            gate = zero3
            for hh in range(PEER_HEADS):
                arow = pltpu.bitcast(jnp.broadcast_to(a_ref[hh, j:j + 1, :], (8, tt)), BF16)
                lrow = pltpu.bitcast(jnp.broadcast_to(l_ref[hh, j:j + 1, :], (8, tt)), BF16)
                b3 = b_ref[hh].reshape(shape3)
                r3 = r_ref[hh].reshape(shape3)
                gate = gate + arow[None] * jnp.where(r3 < lrow[None], b3, zero3)
            ga_ref[j * PEER_KEYS:(j + 1) * PEER_KEYS, :] = gate.reshape(PEER_KEYS, tt)
        rows = slice(p * piece, (p + 1) * piece)
        pre = _dot(u_blk[rows, :], hb_ref[...])
        act = pre + pre * lax.erf(pre)
        ga_ref[rows, :] = ga_ref[rows, :] * act.astype(BF16)
    acc_ref[...] += _dot(vt_blk, ga_ref[...])

    @pl.when(e == pl.num_programs(1) - 1)
    def _():
        y = x_ref[...] + acc_ref[...].T
        if final_norm:
            y = _rms(y) * gf_ref[...]
        o_ref[...] = y


def _pack_kernel(w_ref, o_ref, *, scale, transpose):
    w = w_ref[...]
    if transpose:
        w = w.T
    o_ref[...] = pltpu.bitcast((w * scale).astype(BF16), jnp.uint32)


def _pack_table(w, scale, transpose):
    rows, cols = w.shape
    if transpose:
        blk = 512
        in_spec = pl.BlockSpec((blk, cols), lambda i: (i, 0))
        out_spec = pl.BlockSpec((cols // 2, blk), lambda i: (0, i))
        out_shape = jax.ShapeDtypeStruct((cols // 2, rows), jnp.uint32)
    else:
        blk = 1024
        in_spec = pl.BlockSpec((blk, cols), lambda i: (i, 0))
        out_spec = pl.BlockSpec((blk // 2, cols), lambda i: (i, 0))
        out_shape = jax.ShapeDtypeStruct((rows // 2, cols), jnp.uint32)
    return pl.pallas_call(
        functools.partial(_pack_kernel, scale=scale, transpose=transpose),
        grid=(rows // blk,),
        in_specs=[in_spec],
        out_specs=out_spec,
        out_shape=out_shape,
        compiler_params=_cparams(("parallel",)),
        name="pack_table_t" if transpose else "pack_table",
    )(w)


def _peer(x, hb, a, l, b, r, u, vt, gf, final_norm):
    t = x.shape[0]
    tt = min(t, 512)
    eb = 16 * PEER_KEYS
    rows = pl.BlockSpec((tt, D_MODEL), lambda i, e: (i, 0))
    i1_rows = pl.BlockSpec((PEER_HEADS, eb // PEER_KEYS, tt), lambda i, e: (0, e, i))
    all_rows = pl.BlockSpec((PEER_HEADS, PEER_KEYS, tt), lambda i, e: (0, 0, i))
    return pl.pallas_call(
        functools.partial(_peer_kernel, final_norm=final_norm),
        grid=(t // tt, N_EXPERTS // eb),
        in_specs=[rows, pl.BlockSpec((D_MODEL, tt), lambda i, e: (0, i)),
                  i1_rows, i1_rows, all_rows, all_rows,
                  pl.BlockSpec((eb // 2, D_MODEL), lambda i, e: (e, 0)),
                  pl.BlockSpec((D_MODEL // 2, eb), lambda i, e: (0, e)),
                  pl.BlockSpec((1, D_MODEL), lambda i, e: (0, 0))],
        out_specs=rows,
        out_shape=jax.ShapeDtypeStruct((t, D_MODEL), F32),
        scratch_shapes=[pltpu.VMEM((D_MODEL, tt), F32), pltpu.VMEM((eb, tt), BF16)],
        compiler_params=_cparams(("parallel", "arbitrary")),
        name="peer_dense",
    )(x, hb, a, l, b, r, u, vt, gf)


def kernel(x, mem, positions, ln_mix, w_in, lb_param, hgrn_norm, swa_sinks, w_br_hgrn, w_br_ret,
           w_br_swa, w_out, ln_xq, ln_xkv, w_xq, w_xk, w_xv, w_xo, ln_ffn, peer_wq, peer_keys,
           peer_u, peer_v, ln_final):
    b, t, d = x.shape
    assert b == 1 and d == D_MODEL
    depth = w_in.shape[0]
    xs = x.reshape(t, d)
    ms = mem.reshape(mem.shape[1], d)
    cos_r, sin_r, cos_s, sin_s = _rope_tables(positions)
    lb_sm = jax.nn.softmax(lb_param.astype(F32), axis=0)
    lower = jnp.cumsum(lb_sm, axis=0) - lb_sm[0]

    for l in range(depth):
        proj, hf = _proj(xs, ln_mix[l][None, :], _relay_w_in(w_in[l]))
        oa = _hgrn(proj, hf, lower[l][None, :])
        ob = _ret(proj, cos_r, sin_r)
        oc = _swa(proj, cos_s, sin_s, swa_sinks[l])
        xs = _merge(xs, proj, oa, ob, oc, hgrn_norm[l][None, :], w_br_hgrn[l].astype(BF16),
                    w_br_ret[l].astype(BF16), w_br_swa[l].astype(BF16), w_out[l].astype(BF16))

        km, vm = _memkv(ms, ln_xkv[l][None, :], w_xk[l].astype(BF16), w_xv[l].astype(BF16))
        xs = _xattn(xs, ln_xq[l][None, :], w_xq[l].astype(BF16), km, vm, w_xo[l].astype(BF16))

        k_hi, k_lo = _split2(peer_keys[l].reshape(2 * PEER_HEADS, PEER_KEYS, PEER_HALF))
        hb, a, bz, r2, cnt = _router(xs, ln_ffn[l][None, :], peer_wq[l].T.astype(BF16), k_hi, k_lo)
        c = 2.0 ** -0.5
        xs = _peer(xs, hb, a, cnt, bz, r2, _pack_table(peer_u[l], c, False), _pack_table(peer_v[l], c, True),
                   ln_final[None, :], final_norm=(l == depth - 1))

    return xs.reshape(b, t, d)
```

```python
import functools
import math

import numpy as np
import jax
import jax.numpy as jnp
from jax import lax
from jax.experimental import pallas as pl
from jax.experimental.pallas import tpu as pltpu

F32 = jnp.float32
BF16 = jnp.bfloat16

D_MODEL = 1024
HEAD_DIM = 128
N_HEADS = 4
SWA_Q_HEADS = 8
SWA_KV_HEADS = 2
SWA_DIM = 64
SWA_BLOCK = 128
ROPE_THETA = 10000.0
N_MEM = 256
X_HEADS = 4
X_DIM = D_MODEL // X_HEADS
PEER_HEADS = 8
PEER_KEYS = 128
PEER_TOPK = 16
PEER_HALF = 128
N_EXPERTS = PEER_KEYS * PEER_KEYS
EPS = 1e-6
NEG_BIG = -1e30
TINY = 1e-30

LANE = 128
HG_CHUNK = 64
RET_CHUNK = 256
PEER_PIECE_KEYS = 4
VMEM_LIMIT = 56 * 1024 * 1024

CB_GATE = 0
CB_HQ, CB_HF, CB_HI, CB_HG = 24, 28, 32, 36
CB_RQ, CB_RK, CB_RV, CB_RG = 40, 44, 48, 52
CB_SQ, CB_SQR, CB_SK, CB_SKR, CB_SVA, CB_SVB = 56, 60, 64, 66, 68, 70
N_CB = 72
PROJ_COLS = N_CB * LANE


def _cparams(sem):
    return pltpu.CompilerParams(dimension_semantics=sem, vmem_limit_bytes=VMEM_LIMIT)


def _dot(a, b):
    return jnp.dot(a, b, preferred_element_type=F32)


def _dot_nt(a, b):
    return lax.dot_general(a, b, (((1,), (1,)), ((), ())), preferred_element_type=F32)


def _dot_tn(a, b):
    return lax.dot_general(a, b, (((0,), (0,)), ((), ())), preferred_element_type=F32)


def _split2(x):
    hi = x.astype(BF16)
    lo = (x - hi.astype(F32)).astype(BF16)
    return hi, lo


def _split3(x):
    hi = x.astype(BF16)
    r = x - hi.astype(F32)
    mid = r.astype(BF16)
    lo = (r - mid.astype(F32)).astype(BF16)
    return hi, mid, lo


def _sigmoid(x):
    return 1.0 / (1.0 + jnp.exp(-x))


def _rms(x):
    return x * lax.rsqrt(jnp.mean(x * x, axis=-1, keepdims=True) + EPS)


def _rope_kernel(pos_ref, fr_ref, fs_ref, cr_ref, sr_ref, cs_ref, ss_ref):
    pos = pos_ref[...].astype(F32)
    lane = lax.broadcasted_iota(jnp.int32, (1, LANE), 1)
    ang_r = pos * fr_ref[...]
    cr_ref[...] = jnp.cos(ang_r)
    sr_ref[...] = jnp.sin(ang_r) * jnp.where(lane < HEAD_DIM // 2, -1.0, 1.0)
    ang_s = pos * fs_ref[...]
    cs_ref[...] = jnp.cos(ang_s)
    ss_ref[...] = jnp.sin(ang_s)


def _rope_tables(positions):
    t = positions.shape[1]
    tt = min(t, 1024)
    ret_f = ROPE_THETA ** (-jnp.linspace(0.0, 1.0, HEAD_DIM // 2, dtype=F32))
    swa_f = ROPE_THETA ** (-jnp.arange(0, SWA_DIM, 2, dtype=F32) / SWA_DIM)
    fr = jnp.concatenate([ret_f, ret_f])[None, :]
    fs = jnp.concatenate([swa_f] * (LANE // (SWA_DIM // 2)))[None, :]
    pos = positions.reshape(t, 1)
    tab = jax.ShapeDtypeStruct((t, LANE), F32)
    row = pl.BlockSpec((tt, LANE), lambda i: (i, 0))
    vec = pl.BlockSpec((1, LANE), lambda i: (0, 0))
    return pl.pallas_call(
        _rope_kernel,
        grid=(t // tt,),
        in_specs=[pl.BlockSpec((tt, 1), lambda i: (i, 0)), vec, vec],
        out_specs=[row, row, row, row],
        out_shape=[tab, tab, tab, tab],
        compiler_params=_cparams(("parallel",)),
        name="rope_tables",
    )(pos, fr, fs)


PROJ_TN = 1024


def _proj_kernel(x_ref, g_ref, w_ref, o_ref, f_ref, h_scr):
    @pl.when(pl.program_id(1) == 0)
    def _():
        h_scr[...] = (_rms(x_ref[...]) * g_ref[...]).astype(BF16)

    y = _dot(h_scr[...], w_ref[...])
    o_ref[...] = y.astype(BF16)

    @pl.when(pl.program_id(1) == CB_HF * LANE // PROJ_TN)
    def _():
        lo = CB_HF * LANE % PROJ_TN
        f_ref[...] = y[:, lo:lo + N_HEADS * HEAD_DIM]


def _proj(x, g, w):
    t = x.shape[0]
    n = w.shape[1]
    tm = min(t, 2048)
    tn = PROJ_TN
    wf = N_HEADS * HEAD_DIM
    return pl.pallas_call(
        _proj_kernel,
        grid=(t // tm, n // tn),
        in_specs=[pl.BlockSpec((tm, D_MODEL), lambda i, j: (i, 0)),
                  pl.BlockSpec((1, D_MODEL), lambda i, j: (0, 0)),
                  pl.BlockSpec((D_MODEL, tn), lambda i, j: (0, j))],
        out_specs=[pl.BlockSpec((tm, tn), lambda i, j: (i, j)),
                   pl.BlockSpec((tm, wf), lambda i, j: (i, 0))],
        out_shape=[jax.ShapeDtypeStruct((t, n), BF16), jax.ShapeDtypeStruct((t, wf), F32)],
        scratch_shapes=[pltpu.VMEM((tm, D_MODEL), BF16)],
        compiler_params=_cparams(("parallel", "arbitrary")),
        name="in_proj",
    )(x, g, w)


def _relay_w_in(w_in):
    d = w_in.shape[0]
    sizes = (512,) * 8 + (512, 128, 128, 3 * D_MODEL)
    offs = np.concatenate([[0], np.cumsum(sizes)])
    hq, hf, hi, hg, rq, rk, rv, rg, sq, sk, sv, gates = [
        w_in[:, int(offs[i]):int(offs[i + 1])] for i in range(12)]

    def rot(w, nh, dh):
        w = w.reshape(d, nh, dh)
        return jnp.concatenate([-w[..., dh // 2:], w[..., :dh // 2]], axis=-1).reshape(d, nh * dh)

    def widen(w, lo, hi):
        w = w.reshape(d, SWA_KV_HEADS, SWA_DIM)
        zero = jnp.zeros_like(w)
        parts = [w if lo else zero, w if hi else zero]
        return jnp.concatenate(parts, axis=-1).reshape(d, SWA_KV_HEADS * LANE)

    cols = [gates, hq, hf, hi, hg, rq, rk, rv, rg,
            sq, rot(sq, SWA_Q_HEADS, SWA_DIM),
            widen(sk, True, True), widen(rot(sk, SWA_KV_HEADS, SWA_DIM), True, True),
            widen(sv, True, False), widen(sv, False, True)]
    out = jnp.concatenate(cols, axis=1).astype(BF16)
    assert out.shape[1] == PROJ_COLS
    return out


def _hgrn_kernel(hq_ref, hf_ref, hi_ref, lb_ref, o_ref, st_ref, *, n_chunks):
    @pl.when(pl.program_id(0) == 0)
    def _():
        st_ref[...] = jnp.zeros_like(st_ref)

    c = HG_CHUNK
    row = lax.broadcasted_iota(jnp.int32, (c, c), 0)
    col = lax.broadcasted_iota(jnp.int32, (c, c), 1)
    diag = row == col
    halves = [c >> (i + 1) for i in range(c.bit_length() - 1)]
    pair_mask = {b: (row // (2 * b) == col // (2 * b)) & (row % (2 * b) >= b) & (col % (2 * b) < b)
                 for b in halves}
    rw = lax.broadcasted_iota(jnp.int32, (c, HEAD_DIM), 0)
    r3 = lax.broadcasted_iota(jnp.int32, (c // 8, 8, HEAD_DIM), 1)
    scale = HEAD_DIM ** -0.5
    log2e = math.log2(math.e)

    def boundary(cum2, b):
        if b >= 8:
            ends = [cum2[s + b - 1:s + b, :] for s in range(0, c, 2 * b)]
            out = ends[-1]
            for i in range(len(ends) - 2, -1, -1):
                out = jnp.where(rw < (i + 1) * 2 * b, ends[i], out)
            return out
        c3 = cum2.reshape(c // 8, 8, HEAD_DIM)
        if b == 4:
            return jnp.broadcast_to(c3[:, 3:4, :], c3.shape).reshape(c, HEAD_DIM)
        if b == 2:
            lo = jnp.broadcast_to(c3[:, 1:2, :], c3.shape)
            hi = jnp.broadcast_to(c3[:, 5:6, :], c3.shape)
            return jnp.where(r3 < 4, lo, hi).reshape(c, HEAD_DIM)
        prev = pltpu.roll(cum2, 1, 0)
        return jnp.where(rw % 2 == 1, prev, cum2)

    def chunk_body(ci, carry):
        r0 = pl.multiple_of(ci * c, c)
        for h in range(N_HEADS):
            lanes = slice(h * HEAD_DIM, (h + 1) * HEAD_DIM)
            xq = hq_ref[pl.ds(r0, c), lanes].astype(F32)
            z = hf_ref[pl.ds(r0, c), lanes]
            v = hi_ref[pl.ds(r0, c), lanes]
            lb = lb_ref[:, lanes]
            omlb = 1.0 - lb
            q = xq * _sigmoid(xq) * scale
            f = lb + omlb * _sigmoid(z)
            logf = jnp.log(jnp.maximum(f, TINY))
            kk = omlb * _sigmoid(-z)
            cum2 = logf * log2e
            step = 1
            while step < c:
                cum2 = cum2 + jnp.where(rw >= step, pltpu.roll(cum2, step, 0), 0.0)
                step *= 2
            scores = jnp.where(diag, jnp.sum(q * kk, axis=-1, keepdims=True), 0.0)
            for b in halves:
                ref = boundary(cum2, b)
                qs = (q * jnp.exp2(jnp.minimum(cum2 - ref, 0.0))).astype(BF16)
                ks = (kk * jnp.exp2(jnp.minimum(ref - cum2, 0.0))).astype(BF16)
                scores = jnp.where(pair_mask[b], _dot_nt(qs, ks), scores)
            st = st_ref[h]
            qe = (q * jnp.exp2(cum2)).astype(BF16)
            o = _dot_nt(qe, st.astype(BF16)) + _dot(scores.astype(BF16), v.astype(BF16))
            o_ref[pl.ds(r0, c), lanes] = o.astype(BF16)
            last = cum2[c - 1:c, :]
            kl = (kk * jnp.exp2(last - cum2)).astype(BF16)
            st_ref[h] = jnp.exp2(last) * st + _dot_tn(v.astype(BF16), kl)
        return carry

    lax.fori_loop(0, n_chunks, chunk_body, 0, unroll=4)


def _hgrn(proj, hf, lower):
    t = proj.shape[0]
    tb = min(t, 512)
    w = N_HEADS * HEAD_DIM

    def cols(cb):
        return pl.BlockSpec((tb, w), lambda i: (i, cb // N_HEADS))

    return pl.pallas_call(
        functools.partial(_hgrn_kernel, n_chunks=tb // HG_CHUNK),
        grid=(t // tb,),
        in_specs=[cols(CB_HQ), pl.BlockSpec((tb, w), lambda i: (i, 0)), cols(CB_HI),
                  pl.BlockSpec((1, w), lambda i: (0, 0))],
        out_specs=pl.BlockSpec((tb, w), lambda i: (i, 0)),
        out_shape=jax.ShapeDtypeStruct((t, w), BF16),
        scratch_shapes=[pltpu.VMEM((N_HEADS, HEAD_DIM, HEAD_DIM), F32)],
        compiler_params=_cparams(("arbitrary",)),
        name="hgrn2",
    )(proj, hf, proj, lower)


def _ret_kernel(q_ref, k_ref, v_ref, cos_ref, sin_ref,
                dec_ref, qd_ref, kd_ref, cd_ref, o_ref, s_ref):
    @pl.when(pl.program_id(0) == 0)
    def _():
        s_ref[...] = jnp.zeros_like(s_ref)

    c = dec_ref.shape[1]
    scale = HEAD_DIM ** -0.5
    for ci in range(q_ref.shape[0] // c):
        rows = slice(ci * c, (ci + 1) * c)
        cos = cos_ref[rows, :]
        sin = sin_ref[rows, :]
        for h in range(N_HEADS):
            lanes = slice(h * HEAD_DIM, (h + 1) * HEAD_DIM)
            q = q_ref[rows, lanes].astype(F32)
            k = k_ref[rows, lanes].astype(F32)
            q = q * cos + pltpu.roll(q, HEAD_DIM // 2, 1) * sin
            k = (k * cos + pltpu.roll(k, HEAD_DIM // 2, 1) * sin) * scale
            v = v_ref[rows, lanes]
            qb = q.astype(BF16)
            s = s_ref[h]
            scores = _dot_nt(qb, k.astype(BF16)) * dec_ref[h]
            o = _dot(qb, s.astype(BF16)) * qd_ref[h] + _dot(scores.astype(BF16), v)
            o_ref[rows, lanes] = o.astype(BF16)
            s_ref[h] = cd_ref[h] * s + _dot_tn((k * kd_ref[h]).astype(BF16), v)


def _ret(proj, cos_r, sin_r):
    t = proj.shape[0]
    c = min(t, RET_CHUNK)
    tb = 4 * c if t % (4 * c) == 0 else c
    w = N_HEADS * HEAD_DIM
    log_gamma = jnp.log(1.0 - 2.0 ** (-5.0 - jnp.arange(N_HEADS, dtype=F32)))
    idx = jnp.arange(c, dtype=F32)
    rel = idx[:, None] - idx[None, :]
    decay = jnp.exp(jnp.where(rel >= 0, log_gamma[:, None, None] * rel, NEG_BIG))
    ones = jnp.ones((1, 1, HEAD_DIM), F32)
    q_decay = jnp.exp(log_gamma[:, None] * (idx + 1.0))[:, :, None] * ones
    k_decay = jnp.exp(log_gamma[:, None] * (c - 1.0 - idx))[:, :, None] * ones
    c_decay = jnp.exp(log_gamma * c)[:, None, None] * ones

    def cols(cb):
        return pl.BlockSpec((tb, w), lambda i: (i, cb // N_HEADS))

    def const(shape):
        return pl.BlockSpec(shape, lambda i: (0, 0, 0))

    tab = pl.BlockSpec((tb, LANE), lambda i: (i, 0))
    return pl.pallas_call(
        _ret_kernel,
        grid=(t // tb,),
        in_specs=[cols(CB_RQ), cols(CB_RK), cols(CB_RV), tab, tab,
                  const((N_HEADS, c, c)), const((N_HEADS, c, HEAD_DIM)),
                  const((N_HEADS, c, HEAD_DIM)), const((N_HEADS, 1, HEAD_DIM))],
        out_specs=pl.BlockSpec((tb, w), lambda i: (i, 0)),
        out_shape=jax.ShapeDtypeStruct((t, w), BF16),
        scratch_shapes=[pltpu.VMEM((N_HEADS, HEAD_DIM, HEAD_DIM), F32)],
        compiler_params=_cparams(("arbitrary",)),
        name="retention",
    )(proj, proj, proj, cos_r, sin_r, decay, q_decay, k_decay, c_decay)


def _swa_kernel(sink_ref, q_ref, qr_ref, kc_ref, krc_ref, vac_ref, vbc_ref,
                kp_ref, krp_ref, vap_ref, vbp_ref, cc_ref, sc_ref, cp_ref, sp_ref, o_ref):
    b = SWA_BLOCK
    nb = q_ref.shape[0] // b
    first = pl.program_id(0) * nb
    cos_c, sin_c = cc_ref[...], sc_ref[...]
    cos_p, sin_p = cp_ref[...], sp_ref[...]
    qi = lax.broadcasted_iota(jnp.int32, (b, 2 * b), 0) + b
    ki = lax.broadcasted_iota(jnp.int32, (b, 2 * b), 1)
    rel = qi - ki
    in_window = (rel >= 0) & (rel < b)
    low = lax.broadcasted_iota(jnp.int32, (b, LANE), 1) < SWA_DIM
    pairs_per_group = SWA_Q_HEADS // SWA_KV_HEADS // 2
    for g in range(SWA_KV_HEADS):
        kl = slice(g * LANE, (g + 1) * LANE)
        k_cur = kc_ref[:, kl].astype(F32) * cos_c + krc_ref[:, kl].astype(F32) * sin_c
        k_prev = kp_ref[:, kl].astype(F32) * cos_p + krp_ref[:, kl].astype(F32) * sin_p
        k_all = jnp.concatenate([k_prev, k_cur], axis=0).astype(BF16)
        v_lo = jnp.concatenate([vap_ref[:, kl], vac_ref[:, kl]], axis=0)
        v_hi = jnp.concatenate([vbp_ref[:, kl], vbc_ref[:, kl]], axis=0)
        for sb in range(nb):
            win = slice(sb * b, (sb + 2) * b)
            rows = slice(sb * b, (sb + 1) * b)
            keep = in_window & ((first + sb > 0) | (ki >= b))
            kw = k_all[win]
            for j in range(pairs_per_group):
                pair = g * pairs_per_group + j
                ql = slice(pair * LANE, (pair + 1) * LANE)
                q = (q_ref[rows, ql].astype(F32) * cos_c[rows] + qr_ref[rows, ql].astype(F32) * sin_c[rows])
                out = None
                for half, vw in ((0, v_lo[win]), (1, v_hi[win])):
                    qh = jnp.where(low if half == 0 else ~low, q, 0.0).astype(BF16)
                    s = _dot_nt(qh, kw) * (SWA_DIM ** -0.5)
                    s = jnp.where(keep, s, NEG_BIG)
                    sink = sink_ref[2 * pair + half]
                    m = jnp.maximum(jnp.max(s, axis=-1, keepdims=True), sink)
                    p = jnp.exp(s - m)
                    denom = jnp.sum(p, axis=-1, keepdims=True) + jnp.exp(sink - m)
                    o = _dot(p.astype(BF16), vw) / denom
                    out = o if out is None else out + o
                o_ref[rows, ql] = out.astype(BF16)


def _swa(proj, cos_s, sin_s, sinks):
    t = proj.shape[0]
    b = SWA_BLOCK
    nb = 4 if t % (4 * b) == 0 else 1
    qw = SWA_Q_HEADS * SWA_DIM
    kw = SWA_KV_HEADS * LANE

    def cur(cb, width):
        return pl.BlockSpec((nb * b, width), lambda i: (i, cb * LANE // width))

    def prev(cb, width):
        return pl.BlockSpec((b, width), lambda i: (jnp.maximum(nb * i - 1, 0), cb * LANE // width))

    tab_c = pl.BlockSpec((nb * b, LANE), lambda i: (i, 0))
    tab_p = pl.BlockSpec((b, LANE), lambda i: (jnp.maximum(nb * i - 1, 0), 0))
    return pl.pallas_call(
        _swa_kernel,
        grid=(t // (nb * b),),
        in_specs=[pl.BlockSpec(memory_space=pltpu.SMEM),
                  cur(CB_SQ, qw), cur(CB_SQR, qw),
                  cur(CB_SK, kw), cur(CB_SKR, kw), cur(CB_SVA, kw), cur(CB_SVB, kw),
                  prev(CB_SK, kw), prev(CB_SKR, kw), prev(CB_SVA, kw), prev(CB_SVB, kw),
                  tab_c, tab_c, tab_p, tab_p],
        out_specs=pl.BlockSpec((nb * b, qw), lambda i: (i, 0)),
        out_shape=jax.ShapeDtypeStruct((t, qw), BF16),
        compiler_params=_cparams(("parallel",)),
        name="swa",
    )(sinks, proj, proj, proj, proj, proj, proj, proj, proj, proj, proj, cos_s, sin_s, cos_s, sin_s)


def _merge_kernel(x_ref, oa_ref, hg_ref, ob_ref, rg_ref, oc_ref, ga_ref, gb_ref, gc_ref,
                  hn_ref, wa_ref, wb_ref, wc_ref, wo_ref, o_ref):
    hg = hg_ref[...].astype(F32)
    a = _rms(oa_ref[...].astype(F32)) * hn_ref[...] * (hg * _sigmoid(hg))
    ya = _dot(a.astype(BF16), wa_ref[...])
    rg = rg_ref[...].astype(F32)
    ob = ob_ref[...].astype(F32)
    parts = []
    for h in range(N_HEADS):
        lanes = slice(h * HEAD_DIM, (h + 1) * HEAD_DIM)
        parts.append(_rms(ob[:, lanes]))
    bn = jnp.concatenate(parts, axis=-1) * (rg * _sigmoid(rg))
    yb = _dot(bn.astype(BF16), wb_ref[...])
    yc = _dot(oc_ref[...], wc_ref[...])
    mixed = (_sigmoid(ga_ref[...].astype(F32)) * ya + _sigmoid(gb_ref[...].astype(F32)) * yb
             + _sigmoid(gc_ref[...].astype(F32)) * yc)
    o_ref[...] = x_ref[...] + _dot(mixed.astype(BF16), wo_ref[...])


def _merge(x, proj, oa, ob, oc, hn, wa, wb, wc, wo):
    t = x.shape[0]
    tt = min(t, 512)
    w = N_HEADS * HEAD_DIM
    qw = SWA_Q_HEADS * SWA_DIM

    def rows(width, cb=0):
        return pl.BlockSpec((tt, width), lambda i: (i, cb * LANE // width))

    def full(shape):
        return pl.BlockSpec(shape, lambda i: (0, 0))

    return pl.pallas_call(
        _merge_kernel,
        grid=(t // tt,),
        in_specs=[rows(D_MODEL), rows(w), rows(w, CB_HG), rows(w), rows(w, CB_RG), rows(qw),
                  rows(D_MODEL, CB_GATE), rows(D_MODEL, CB_GATE + 8), rows(D_MODEL, CB_GATE + 16),
                  full((1, w)), full((w, D_MODEL)), full((w, D_MODEL)), full((qw, D_MODEL)),
                  full((D_MODEL, D_MODEL))],
        out_specs=rows(D_MODEL),
        out_shape=jax.ShapeDtypeStruct((t, D_MODEL), F32),
        compiler_params=_cparams(("parallel",)),
        name="merge",
    )(x, oa, proj, ob, proj, oc, proj, proj, proj, hn, wa, wb, wc, wo)


def _memkv_kernel(m_ref, g_ref, wk_ref, wv_ref, k_ref, v_ref):
    h = (_rms(m_ref[...]) * g_ref[...]).astype(BF16)
    k_ref[...] = _dot(h, wk_ref[...]).astype(BF16)
    v_ref[...] = _dot(h, wv_ref[...]).astype(BF16)


def _memkv(mem, g, wk, wv):
    nm = mem.shape[0]
    out = jax.ShapeDtypeStruct((nm, D_MODEL), BF16)
    return pl.pallas_call(
        _memkv_kernel,
        out_shape=[out, out],
        compiler_params=pltpu.CompilerParams(vmem_limit_bytes=VMEM_LIMIT),
        name="mem_kv",
    )(mem, g, wk, wv)


def _xattn_kernel(x_ref, g_ref, wq_ref, k_ref, v_ref, wo_ref, o_ref):
    x = x_ref[...]
    h = (_rms(x) * g_ref[...]).astype(BF16)
    q = _dot(h, wq_ref[...])
    outs = []
    for hh in range(X_HEADS):
        lanes = slice(hh * X_DIM, (hh + 1) * X_DIM)
        s = _dot_nt(q[:, lanes].astype(BF16), k_ref[:, lanes]) * (X_DIM ** -0.5)
        m = jnp.max(s, axis=-1, keepdims=True)
        p = jnp.exp(s - m)
        p = p / jnp.sum(p, axis=-1, keepdims=True)
        outs.append(_dot(p.astype(BF16), v_ref[:, lanes]))
    o = jnp.concatenate(outs, axis=-1)
    o_ref[...] = x + _dot(o.astype(BF16), wo_ref[...])


def _xattn(x, g, wq, k, v, wo):
    t = x.shape[0]
    tt = min(t, 512)
    nm = k.shape[0]

    def full(shape):
        return pl.BlockSpec(shape, lambda i: (0, 0))

    rows = pl.BlockSpec((tt, D_MODEL), lambda i: (i, 0))
    return pl.pallas_call(
        _xattn_kernel,
        grid=(t // tt,),
        in_specs=[rows, full((1, D_MODEL)), full((D_MODEL, D_MODEL)),
                  full((nm, D_MODEL)), full((nm, D_MODEL)), full((D_MODEL, D_MODEL))],
        out_specs=rows,
        out_shape=jax.ShapeDtypeStruct((t, D_MODEL), F32),
        compiler_params=_cparams(("parallel",)),
        name="xattn",
    )(x, g, wq, k, v, wo)


_CAND_ROWS = [PEER_TOPK // (k1 + 1) for k1 in range(8)]


def _top_values(s, n, with_rank=False):
    vals = []
    rank = jnp.full(s.shape, float(n), F32)
    for i in range(n):
        m = jnp.max(s, axis=0, keepdims=True)
        vals.append(m)
        hit = s >= m
        if with_rank:
            rank = jnp.where(hit, float(i), rank)
        s = jnp.where(hit, -jnp.inf, s)
    return (vals, rank) if with_rank else vals


def _bf16_pair_bits(x):
    u = lax.bitcast_convert_type(x.astype(BF16).astype(F32), jnp.uint32)
    return u | (u >> 16)


def _stack_rows(rows, n):
    tt = rows[0].shape[1]
    r = lax.broadcasted_iota(jnp.int32, (n, tt), 0)
    out = jnp.zeros((n, tt), F32)
    for k in range(n):
        out = jnp.where(r == k, rows[k], out)
    return out


def _router_kernel(x_ref, g_ref, wq_ref, kh_ref, kl_ref,
                   hb_ref, a_ref, b_ref, r_ref, l_ref, qt_scr):
    h = _rms(x_ref[...]) * g_ref[...]
    hb_ref[...] = h.T.astype(BF16)
    qt_scr[...] = _dot_nt(wq_ref[...], h.astype(BF16))
    tt = x_ref.shape[0]
    k = PEER_TOPK
    r8 = lax.broadcasted_iota(jnp.int32, (8, tt), 0)

    def head_body(hh, carry):
        scores = []
        for p in range(2):
            r0 = pl.multiple_of(hh * (2 * PEER_HALF) + p * PEER_HALF, PEER_HALF)
            q_hi, q_lo = _split2(qt_scr[pl.ds(r0, PEER_HALF), :])
            kh = kh_ref[p * PEER_HEADS + hh]
            scores.append(_dot(kh, q_hi) + _dot(kl_ref[p * PEER_HEADS + hh], q_hi) + _dot(kh, q_lo))
        s1, s2 = scores
        v1, rank1 = _top_values(s1, k, with_rank=True)
        v2, rank2 = _top_values(s2, k, with_rank=True)
        sv1 = _stack_rows(v1, k)
        sv2 = _stack_rows(v2, k)
        groups = [v1[0] + sv2]
        for k1 in range(1, 8):
            groups.append(jnp.where(r8 < _CAND_ROWS[k1], v1[k1] + sv2[0:8], -jnp.inf))
        groups.append(sv1[8:16] + v2[0])
        cand = jnp.concatenate(groups, axis=0)
        tau = _top_values(cand, k)[k - 1]
        m0 = v1[0] + v2[0]
        z = jnp.sum(jnp.where(cand >= tau, jnp.exp(cand - m0), 0.0), axis=0, keepdims=True)
        a_ref[hh] = _bf16_pair_bits(jnp.exp(s1 - v1[0]))
        b_ref[hh] = (jnp.exp(s2 - v2[0]) / z).astype(BF16)
        count1 = jnp.zeros_like(s1)
        for k1 in range(k):
            n_k1 = jnp.sum((v1[k1] + sv2 >= tau).astype(F32), axis=0, keepdims=True)
            count1 = jnp.where(rank1 == float(k1), n_k1, count1)
        r_ref[hh] = rank2.astype(BF16)
        l_ref[hh] = _bf16_pair_bits(count1)
        return carry

    lax.fori_loop(0, PEER_HEADS, head_body, 0, unroll=2)


def _router(x, g, wq_t, k_hi, k_lo):
    t = x.shape[0]
    tt = min(t, 512)
    nq = wq_t.shape[0]
    sel = jax.ShapeDtypeStruct((PEER_HEADS, PEER_KEYS, t), jnp.uint32)
    sel_b = jax.ShapeDtypeStruct((PEER_HEADS, PEER_KEYS, t), BF16)
    sel_spec = pl.BlockSpec((PEER_HEADS, PEER_KEYS, tt), lambda i: (0, 0, i))
    return pl.pallas_call(
        _router_kernel,
        grid=(t // tt,),
        in_specs=[pl.BlockSpec((tt, D_MODEL), lambda i: (i, 0)),
                  pl.BlockSpec((1, D_MODEL), lambda i: (0, 0)),
                  pl.BlockSpec((nq, D_MODEL), lambda i: (0, 0)),
                  pl.BlockSpec((2 * PEER_HEADS, PEER_KEYS, PEER_HALF), lambda i: (0, 0, 0)),
                  pl.BlockSpec((2 * PEER_HEADS, PEER_KEYS, PEER_HALF), lambda i: (0, 0, 0))],
        out_specs=[pl.BlockSpec((D_MODEL, tt), lambda i: (0, i)), sel_spec, sel_spec, sel_spec, sel_spec],
        out_shape=[jax.ShapeDtypeStruct((D_MODEL, t), BF16), sel, sel_b, sel_b, sel],
        scratch_shapes=[pltpu.VMEM((nq, tt), F32)],
        compiler_params=_cparams(("parallel",)),
        name="peer_router",
    )(x, g, wq_t, k_hi, k_lo)


def _peer_kernel(x_ref, hb_ref, a_ref, l_ref, b_ref, r_ref, u_ref, vt_ref, gf_ref, o_ref,
                 acc_ref, ga_ref, *, final_norm):
    e = pl.program_id(1)

    @pl.when(e == 0)
    def _():
        acc_ref[...] = jnp.zeros_like(acc_ref)

    u_blk = pltpu.bitcast(u_ref[...], BF16)
    vt_blk = pltpu.bitcast(vt_ref[...], BF16)
    n_i1 = u_blk.shape[0] // PEER_KEYS
    sub = 16
    tt = hb_ref.shape[1]
    shape3 = (PEER_KEYS // sub, sub, tt)
    zero3 = jnp.zeros(shape3, BF16)
    piece = PEER_PIECE_KEYS * PEER_KEYS
    for p in range(n_i1 // PEER_PIECE_KEYS):
        for j in range(p * PEER_PIECE_KEYS, (p + 1) * PEER_PIECE_KEYS):
            gate = zero3
            for hh in range(PEER_HEADS):
                arow = pltpu.bitcast(jnp.broadcast_to(a_ref[hh, j:j + 1, :], (8, tt)), BF16)
                lrow = pltpu.bitcast(jnp.broadcast_to(l_ref[hh, j:j + 1, :], (8, tt)), BF16)
                b3 = b_ref[hh].reshape(shape3)
                r3 = r_ref[hh].reshape(shape3)
                gate = gate + arow[None] * jnp.where(r3 < lrow[None], b3, zero3)
            ga_ref[j * PEER_KEYS:(j + 1) * PEER_KEYS, :] = gate.reshape(PEER_KEYS, tt)
        rows = slice(p * piece, (p + 1) * piece)
        pre = _dot(u_blk[rows, :], hb_ref[...])
        act = pre + pre * lax.erf(pre)
        ga_ref[rows, :] = ga_ref[rows, :] * act.astype(BF16)
    acc_ref[...] += _dot(vt_blk, ga_ref[...])

    @pl.when(e == pl.num_programs(1) - 1)
    def _():
        y = x_ref[...] + acc_ref[...].T
        if final_norm:
            y = _rms(y) * gf_ref[...]
        o_ref[...] = y


def _pack_kernel(w_ref, o_ref, *, scale, transpose):
    w = w_ref[...]
    if transpose:
        w = w.T
    o_ref[...] = pltpu.bitcast((w * scale).astype(BF16), jnp.uint32)


def _pack_table(w, scale, transpose):
    rows, cols = w.shape
    if transpose:
        blk = 512
        in_spec = pl.BlockSpec((blk, cols), lambda i: (i, 0))
        out_spec = pl.BlockSpec((cols // 2, blk), lambda i: (0, i))
        out_shape = jax.ShapeDtypeStruct((cols // 2, rows), jnp.uint32)
    else:
        blk = 1024
        in_spec = pl.BlockSpec((blk, cols), lambda i: (i, 0))
        out_spec = pl.BlockSpec((blk // 2, cols), lambda i: (i, 0))
        out_shape = jax.ShapeDtypeStruct((rows // 2, cols), jnp.uint32)
    return pl.pallas_call(
        functools.partial(_pack_kernel, scale=scale, transpose=transpose),
        grid=(rows // blk,),
        in_specs=[in_spec],
        out_specs=out_spec,
        out_shape=out_shape,
        compiler_params=_cparams(("parallel",)),
        name="pack_table_t" if transpose else "pack_table",
    )(w)


def _peer(x, hb, a, l, b, r, u, vt, gf, final_norm):
    t = x.shape[0]
    tt = min(t, 512)
    eb = 16 * PEER_KEYS
    rows = pl.BlockSpec((tt, D_MODEL), lambda i, e: (i, 0))
    i1_rows = pl.BlockSpec((PEER_HEADS, eb // PEER_KEYS, tt), lambda i, e: (0, e, i))
    all_rows = pl.BlockSpec((PEER_HEADS, PEER_KEYS, tt), lambda i, e: (0, 0, i))
    return pl.pallas_call(
        functools.partial(_peer_kernel, final_norm=final_norm),
        grid=(t // tt, N_EXPERTS // eb),
        in_specs=[rows, pl.BlockSpec((D_MODEL, tt), lambda i, e: (0, i)),
                  i1_rows, i1_rows, all_rows, all_rows,
                  pl.BlockSpec((eb // 2, D_MODEL), lambda i, e: (e, 0)),
                  pl.BlockSpec((D_MODEL // 2, eb), lambda i, e: (0, e)),
                  pl.BlockSpec((1, D_MODEL), lambda i, e: (0, 0))],
        out_specs=rows,
        out_shape=jax.ShapeDtypeStruct((t, D_MODEL), F32),
        scratch_shapes=[pltpu.VMEM((D_MODEL, tt), F32), pltpu.VMEM((eb, tt), BF16)],
        compiler_params=_cparams(("parallel", "arbitrary")),
        name="peer_dense",
    )(x, hb, a, l, b, r, u, vt, gf)


def kernel(x, mem, positions, ln_mix, w_in, lb_param, hgrn_norm, swa_sinks, w_br_hgrn, w_br_ret,
           w_br_swa, w_out, ln_xq, ln_xkv, w_xq, w_xk, w_xv, w_xo, ln_ffn, peer_wq, peer_keys,
           peer_u, peer_v, ln_final):
    b, t, d = x.shape
    assert b == 1 and d == D_MODEL
    depth = w_in.shape[0]
    xs = x.reshape(t, d)
    ms = mem.reshape(mem.shape[1], d)
    cos_r, sin_r, cos_s, sin_s = _rope_tables(positions)
    lb_sm = jax.nn.softmax(lb_param.astype(F32), axis=0)
    lower = jnp.cumsum(lb_sm, axis=0) - lb_sm[0]

    for l in range(depth):
        proj, hf = _proj(xs, ln_mix[l][None, :], _relay_w_in(w_in[l]))
        oa = _hgrn(proj, hf, lower[l][None, :])
        ob = _ret(proj, cos_r, sin_r)
        oc = _swa(proj, cos_s, sin_s, swa_sinks[l])
        xs = _merge(xs, proj, oa, ob, oc, hgrn_norm[l][None, :], w_br_hgrn[l].astype(BF16),
                    w_br_ret[l].astype(BF16), w_br_swa[l].astype(BF16), w_out[l].astype(BF16))

        km, vm = _memkv(ms, ln_xkv[l][None, :], w_xk[l].astype(BF16), w_xv[l].astype(BF16))
        xs = _xattn(xs, ln_xq[l][None, :], w_xq[l].astype(BF16), km, vm, w_xo[l].astype(BF16))

        k_hi, k_lo = _split2(peer_keys[l].reshape(2 * PEER_HEADS, PEER_KEYS, PEER_HALF))
        hb, a, bz, r2, cnt = _router(xs, ln_ffn[l][None, :], peer_wq[l].T.astype(BF16), k_hi, k_lo)
        c = 2.0 ** -0.5
        xs = _peer(xs, hb, a, cnt, bz, r2, _pack_table(peer_u[l], c, False), _pack_table(peer_v[l], c, True),
                   ln_final[None, :], final_norm=(l == depth - 1))

    return xs.reshape(b, t, d)
```

```python
import functools
import math

import numpy as np
import jax
import jax.numpy as jnp
from jax import lax
from jax.experimental import pallas as pl
from jax.experimental.pallas import tpu as pltpu

F32 = jnp.float32
BF16 = jnp.bfloat16

D_MODEL = 1024
HEAD_DIM = 128
N_HEADS = 4
SWA_Q_HEADS = 8
SWA_KV_HEADS = 2
SWA_DIM = 64
SWA_BLOCK = 128
ROPE_THETA = 10000.0
N_MEM = 256
X_HEADS = 4
X_DIM = D_MODEL // X_HEADS
PEER_HEADS = 8
PEER_KEYS = 128
PEER_TOPK = 16
PEER_HALF = 128
N_EXPERTS = PEER_KEYS * PEER_KEYS
EPS = 1e-6
NEG_BIG = -1e30
TINY = 1e-30

LANE = 128
HG_CHUNK = 64
RET_CHUNK = 256
PEER_PIECE_KEYS = 4
VMEM_LIMIT = 56 * 1024 * 1024

CB_GATE = 0
CB_HQ, CB_HF, CB_HI, CB_HG = 24, 28, 32, 36
CB_RQ, CB_RK, CB_RV, CB_RG = 40, 44, 48, 52
CB_SQ, CB_SQR, CB_SK, CB_SKR, CB_SVA, CB_SVB = 56, 60, 64, 66, 68, 70
N_CB = 72
PROJ_COLS = N_CB * LANE


def _cparams(sem):
    return pltpu.CompilerParams(dimension_semantics=sem, vmem_limit_bytes=VMEM_LIMIT)


def _dot(a, b):
    return jnp.dot(a, b, preferred_element_type=F32)


def _dot_nt(a, b):
    return lax.dot_general(a, b, (((1,), (1,)), ((), ())), preferred_element_type=F32)


def _dot_tn(a, b):
    return lax.dot_general(a, b, (((0,), (0,)), ((), ())), preferred_element_type=F32)


def _split2(x):
    hi = x.astype(BF16)
    lo = (x - hi.astype(F32)).astype(BF16)
    return hi, lo


def _split3(x):
    hi = x.astype(BF16)
    r = x - hi.astype(F32)
    mid = r.astype(BF16)
    lo = (r - mid.astype(F32)).astype(BF16)
    return hi, mid, lo


def _sigmoid(x):
    return 1.0 / (1.0 + jnp.exp(-x))


def _rms(x):
    return x * lax.rsqrt(jnp.mean(x * x, axis=-1, keepdims=True) + EPS)


def _rope_kernel(pos_ref, fr_ref, fs_ref, cr_ref, sr_ref, cs_ref, ss_ref):
    pos = pos_ref[...].astype(F32)
    lane = lax.broadcasted_iota(jnp.int32, (1, LANE), 1)
    ang_r = pos * fr_ref[...]
    cr_ref[...] = jnp.cos(ang_r)
    sr_ref[...] = jnp.sin(ang_r) * jnp.where(lane < HEAD_DIM // 2, -1.0, 1.0)
    ang_s = pos * fs_ref[...]
    cs_ref[...] = jnp.cos(ang_s)
    ss_ref[...] = jnp.sin(ang_s)


def _rope_tables(positions):
    t = positions.shape[1]
    tt = min(t, 1024)
    ret_f = ROPE_THETA ** (-jnp.linspace(0.0, 1.0, HEAD_DIM // 2, dtype=F32))
    swa_f = ROPE_THETA ** (-jnp.arange(0, SWA_DIM, 2, dtype=F32) / SWA_DIM)
    fr = jnp.concatenate([ret_f, ret_f])[None, :]
    fs = jnp.concatenate([swa_f] * (LANE // (SWA_DIM // 2)))[None, :]
    pos = positions.reshape(t, 1)
    tab = jax.ShapeDtypeStruct((t, LANE), F32)
    row = pl.BlockSpec((tt, LANE), lambda i: (i, 0))
    vec = pl.BlockSpec((1, LANE), lambda i: (0, 0))
    return pl.pallas_call(
        _rope_kernel,
        grid=(t // tt,),
        in_specs=[pl.BlockSpec((tt, 1), lambda i: (i, 0)), vec, vec],
        out_specs=[row, row, row, row],
        out_shape=[tab, tab, tab, tab],
        compiler_params=_cparams(("parallel",)),
        name="rope_tables",
    )(pos, fr, fs)


PROJ_TN = 1024


def _proj_kernel(x_ref, g_ref, w_ref, o_ref, f_ref, h_scr):
    @pl.when(pl.program_id(1) == 0)
    def _():
        h_scr[...] = (_rms(x_ref[...]) * g_ref[...]).astype(BF16)

    y = _dot(h_scr[...], w_ref[...])
    o_ref[...] = y.astype(BF16)

    @pl.when(pl.program_id(1) == CB_HF * LANE // PROJ_TN)
    def _():
        lo = CB_HF * LANE % PROJ_TN
        f_ref[...] = y[:, lo:lo + N_HEADS * HEAD_DIM]


def _proj(x, g, w):
    t = x.shape[0]
    n = w.shape[1]
    tm = min(t, 2048)
    tn = PROJ_TN
    wf = N_HEADS * HEAD_DIM
    return pl.pallas_call(
        _proj_kernel,
        grid=(t // tm, n // tn),
        in_specs=[pl.BlockSpec((tm, D_MODEL), lambda i, j: (i, 0)),
                  pl.BlockSpec((1, D_MODEL), lambda i, j: (0, 0)),
                  pl.BlockSpec((D_MODEL, tn), lambda i, j: (0, j))],
        out_specs=[pl.BlockSpec((tm, tn), lambda i, j: (i, j)),
                   pl.BlockSpec((tm, wf), lambda i, j: (i, 0))],
        out_shape=[jax.ShapeDtypeStruct((t, n), BF16), jax.ShapeDtypeStruct((t, wf), F32)],
        scratch_shapes=[pltpu.VMEM((tm, D_MODEL), BF16)],
        compiler_params=_cparams(("parallel", "arbitrary")),
        name="in_proj",
    )(x, g, w)


def _relay_w_in(w_in):
    d = w_in.shape[0]
    sizes = (512,) * 8 + (512, 128, 128, 3 * D_MODEL)
    offs = np.concatenate([[0], np.cumsum(sizes)])
    hq, hf, hi, hg, rq, rk, rv, rg, sq, sk, sv, gates = [
        w_in[:, int(offs[i]):int(offs[i + 1])] for i in range(12)]

    def rot(w, nh, dh):
        w = w.reshape(d, nh, dh)
        return jnp.concatenate([-w[..., dh // 2:], w[..., :dh // 2]], axis=-1).reshape(d, nh * dh)

    def widen(w, lo, hi):
        w = w.reshape(d, SWA_KV_HEADS, SWA_DIM)
        zero = jnp.zeros_like(w)
        parts = [w if lo else zero, w if hi else zero]
        return jnp.concatenate(parts, axis=-1).reshape(d, SWA_KV_HEADS * LANE)

    cols = [gates, hq, hf, hi, hg, rq, rk, rv, rg,
            sq, rot(sq, SWA_Q_HEADS, SWA_DIM),
            widen(sk, True, True), widen(rot(sk, SWA_KV_HEADS, SWA_DIM), True, True),
            widen(sv, True, False), widen(sv, False, True)]
    out = jnp.concatenate(cols, axis=1).astype(BF16)
    assert out.shape[1] == PROJ_COLS
    return out


def _hgrn_kernel(hq_ref, hf_ref, hi_ref, lb_ref, o_ref, st_ref, *, n_chunks):
    @pl.when(pl.program_id(0) == 0)
    def _():
        st_ref[...] = jnp.zeros_like(st_ref)

    c = HG_CHUNK
    row = lax.broadcasted_iota(jnp.int32, (c, c), 0)
    col = lax.broadcasted_iota(jnp.int32, (c, c), 1)
    diag = row == col
    halves = [c >> (i + 1) for i in range(c.bit_length() - 1)]
    pair_mask = {b: (row // (2 * b) == col // (2 * b)) & (row % (2 * b) >= b) & (col % (2 * b) < b)
                 for b in halves}
    rw = lax.broadcasted_iota(jnp.int32, (c, HEAD_DIM), 0)
    r3 = lax.broadcasted_iota(jnp.int32, (c // 8, 8, HEAD_DIM), 1)
    scale = HEAD_DIM ** -0.5
    log2e = math.log2(math.e)

    def boundary(cum2, b):
        if b >= 8:
            ends = [cum2[s + b - 1:s + b, :] for s in range(0, c, 2 * b)]
            out = ends[-1]
            for i in range(len(ends) - 2, -1, -1):
                out = jnp.where(rw < (i + 1) * 2 * b, ends[i], out)
            return out
        c3 = cum2.reshape(c // 8, 8, HEAD_DIM)
        if b == 4:
            return jnp.broadcast_to(c3[:, 3:4, :], c3.shape).reshape(c, HEAD_DIM)
        if b == 2:
            lo = jnp.broadcast_to(c3[:, 1:2, :], c3.shape)
            hi = jnp.broadcast_to(c3[:, 5:6, :], c3.shape)
            return jnp.where(r3 < 4, lo, hi).reshape(c, HEAD_DIM)
        prev = pltpu.roll(cum2, 1, 0)
        return jnp.where(rw % 2 == 1, prev, cum2)

    def chunk_body(ci, carry):
        r0 = pl.multiple_of(ci * c, c)
        for h in range(N_HEADS):
            lanes = slice(h * HEAD_DIM, (h + 1) * HEAD_DIM)
            xq = hq_ref[pl.ds(r0, c), lanes].astype(F32)
            z = hf_ref[pl.ds(r0, c), lanes]
            v = hi_ref[pl.ds(r0, c), lanes]
            lb = lb_ref[:, lanes]
            omlb = 1.0 - lb
            q = xq * _sigmoid(xq) * scale
            f = lb + omlb * _sigmoid(z)
            logf = jnp.log(jnp.maximum(f, TINY))
            kk = omlb * _sigmoid(-z)
            cum2 = logf * log2e
            step = 1
            while step < c:
                cum2 = cum2 + jnp.where(rw >= step, pltpu.roll(cum2, step, 0), 0.0)
                step *= 2
            scores = jnp.where(diag, jnp.sum(q * kk, axis=-1, keepdims=True), 0.0)
            for b in halves:
                ref = boundary(cum2, b)
                qs = (q * jnp.exp2(jnp.minimum(cum2 - ref, 0.0))).astype(BF16)
                ks = (kk * jnp.exp2(jnp.minimum(ref - cum2, 0.0))).astype(BF16)
                scores = jnp.where(pair_mask[b], _dot_nt(qs, ks), scores)
            st = st_ref[h]
            qe = (q * jnp.exp2(cum2)).astype(BF16)
            o = _dot_nt(qe, st.astype(BF16)) + _dot(scores.astype(BF16), v.astype(BF16))
            o_ref[pl.ds(r0, c), lanes] = o.astype(BF16)
            last = cum2[c - 1:c, :]
            kl = (kk * jnp.exp2(last - cum2)).astype(BF16)
            st_ref[h] = jnp.exp2(last) * st + _dot_tn(v.astype(BF16), kl)
        return carry

    lax.fori_loop(0, n_chunks, chunk_body, 0, unroll=4)


def _hgrn(proj, hf, lower):
    t = proj.shape[0]
    tb = min(t, 512)
    w = N_HEADS * HEAD_DIM

    def cols(cb):
        return pl.BlockSpec((tb, w), lambda i: (i, cb // N_HEADS))

    return pl.pallas_call(
        functools.partial(_hgrn_kernel, n_chunks=tb // HG_CHUNK),
        grid=(t // tb,),
        in_specs=[cols(CB_HQ), pl.BlockSpec((tb, w), lambda i: (i, 0)), cols(CB_HI),
                  pl.BlockSpec((1, w), lambda i: (0, 0))],
        out_specs=pl.BlockSpec((tb, w), lambda i: (i, 0)),
        out_shape=jax.ShapeDtypeStruct((t, w), BF16),
        scratch_shapes=[pltpu.VMEM((N_HEADS, HEAD_DIM, HEAD_DIM), F32)],
        compiler_params=_cparams(("arbitrary",)),
        name="hgrn2",
    )(proj, hf, proj, lower)


def _ret_kernel(q_ref, k_ref, v_ref, cos_ref, sin_ref,
                dec_ref, qd_ref, kd_ref, cd_ref, o_ref, s_ref):
    @pl.when(pl.program_id(0) == 0)
    def _():
        s_ref[...] = jnp.zeros_like(s_ref)

    c = dec_ref.shape[1]
    scale = HEAD_DIM ** -0.5
    for ci in range(q_ref.shape[0] // c):
        rows = slice(ci * c, (ci + 1) * c)
        cos = cos_ref[rows, :]
        sin = sin_ref[rows, :]
        for h in range(N_HEADS):
            lanes = slice(h * HEAD_DIM, (h + 1) * HEAD_DIM)
            q = q_ref[rows, lanes].astype(F32)
            k = k_ref[rows, lanes].astype(F32)
            q = q * cos + pltpu.roll(q, HEAD_DIM // 2, 1) * sin
            k = (k * cos + pltpu.roll(k, HEAD_DIM // 2, 1) * sin) * scale
            v = v_ref[rows, lanes]
            qb = q.astype(BF16)
            s = s_ref[h]
            scores = _dot_nt(qb, k.astype(BF16)) * dec_ref[h]
            o = _dot(qb, s.astype(BF16)) * qd_ref[h] + _dot(scores.astype(BF16), v)
            o_ref[rows, lanes] = o.astype(BF16)
            s_ref[h] = cd_ref[h] * s + _dot_tn((k * kd_ref[h]).astype(BF16), v)


def _ret(proj, cos_r, sin_r):
    t = proj.shape[0]
    c = min(t, RET_CHUNK)
    tb = 4 * c if t % (4 * c) == 0 else c
    w = N_HEADS * HEAD_DIM
    log_gamma = jnp.log(1.0 - 2.0 ** (-5.0 - jnp.arange(N_HEADS, dtype=F32)))
    idx = jnp.arange(c, dtype=F32)
    rel = idx[:, None] - idx[None, :]
    decay = jnp.exp(jnp.where(rel >= 0, log_gamma[:, None, None] * rel, NEG_BIG))
    ones = jnp.ones((1, 1, HEAD_DIM), F32)
    q_decay = jnp.exp(log_gamma[:, None] * (idx + 1.0))[:, :, None] * ones
    k_decay = jnp.exp(log_gamma[:, None] * (c - 1.0 - idx))[:, :, None] * ones
    c_decay = jnp.exp(log_gamma * c)[:, None, None] * ones

    def cols(cb):
        return pl.BlockSpec((tb, w), lambda i: (i, cb // N_HEADS))

    def const(shape):
        return pl.BlockSpec(shape, lambda i: (0, 0, 0))

    tab = pl.BlockSpec((tb, LANE), lambda i: (i, 0))
    return pl.pallas_call(
        _ret_kernel,
        grid=(t // tb,),
        in_specs=[cols(CB_RQ), cols(CB_RK), cols(CB_RV), tab, tab,
                  const((N_HEADS, c, c)), const((N_HEADS, c, HEAD_DIM)),
                  const((N_HEADS, c, HEAD_DIM)), const((N_HEADS, 1, HEAD_DIM))],
        out_specs=pl.BlockSpec((tb, w), lambda i: (i, 0)),
        out_shape=jax.ShapeDtypeStruct((t, w), BF16),
        scratch_shapes=[pltpu.VMEM((N_HEADS, HEAD_DIM, HEAD_DIM), F32)],
        compiler_params=_cparams(("arbitrary",)),
        name="retention",
    )(proj, proj, proj, cos_r, sin_r, decay, q_decay, k_decay, c_decay)


def _swa_kernel(sink_ref, q_ref, qr_ref, kc_ref, krc_ref, vac_ref, vbc_ref,
                kp_ref, krp_ref, vap_ref, vbp_ref, cc_ref, sc_ref, cp_ref, sp_ref, o_ref):
    b = SWA_BLOCK
    nb = q_ref.shape[0] // b
    first = pl.program_id(0) * nb
    cos_c, sin_c = cc_ref[...], sc_ref[...]
    cos_p, sin_p = cp_ref[...], sp_ref[...]
    qi = lax.broadcasted_iota(jnp.int32, (b, 2 * b), 0) + b
    ki = lax.broadcasted_iota(jnp.int32, (b, 2 * b), 1)
    rel = qi - ki
    in_window = (rel >= 0) & (rel < b)
    low = lax.broadcasted_iota(jnp.int32, (b, LANE), 1) < SWA_DIM
    pairs_per_group = SWA_Q_HEADS // SWA_KV_HEADS // 2
    for g in range(SWA_KV_HEADS):
        kl = slice(g * LANE, (g + 1) * LANE)
        k_cur = kc_ref[:, kl].astype(F32) * cos_c + krc_ref[:, kl].astype(F32) * sin_c
        k_prev = kp_ref[:, kl].astype(F32) * cos_p + krp_ref[:, kl].astype(F32) * sin_p
        k_all = jnp.concatenate([k_prev, k_cur], axis=0).astype(BF16)
        v_lo = jnp.concatenate([vap_ref[:, kl], vac_ref[:, kl]], axis=0)
        v_hi = jnp.concatenate([vbp_ref[:, kl], vbc_ref[:, kl]], axis=0)
        for sb in range(nb):
            win = slice(sb * b, (sb + 2) * b)
            rows = slice(sb * b, (sb + 1) * b)
            keep = in_window & ((first + sb > 0) | (ki >= b))
            kw = k_all[win]
            for j in range(pairs_per_group):
                pair = g * pairs_per_group + j
                ql = slice(pair * LANE, (pair + 1) * LANE)
                q = (q_ref[rows, ql].astype(F32) * cos_c[rows] + qr_ref[rows, ql].astype(F32) * sin_c[rows])
                out = None
                for half, vw in ((0, v_lo[win]), (1, v_hi[win])):
                    qh = jnp.where(low if half == 0 else ~low, q, 0.0).astype(BF16)
                    s = _dot_nt(qh, kw) * (SWA_DIM ** -0.5)
                    s = jnp.where(keep, s, NEG_BIG)
                    sink = sink_ref[2 * pair + half]
                    m = jnp.maximum(jnp.max(s, axis=-1, keepdims=True), sink)
                    p = jnp.exp(s - m)
                    denom = jnp.sum(p, axis=-1, keepdims=True) + jnp.exp(sink - m)
                    o = _dot(p.astype(BF16), vw) / denom
                    out = o if out is None else out + o
                o_ref[rows, ql] = out.astype(BF16)


def _swa(proj, cos_s, sin_s, sinks):
    t = proj.shape[0]
    b = SWA_BLOCK
    nb = 4 if t % (4 * b) == 0 else 1
    qw = SWA_Q_HEADS * SWA_DIM
    kw = SWA_KV_HEADS * LANE

    def cur(cb, width):
        return pl.BlockSpec((nb * b, width), lambda i: (i, cb * LANE // width))

    def prev(cb, width):
        return pl.BlockSpec((b, width), lambda i: (jnp.maximum(nb * i - 1, 0), cb * LANE // width))

    tab_c = pl.BlockSpec((nb * b, LANE), lambda i: (i, 0))
    tab_p = pl.BlockSpec((b, LANE), lambda i: (jnp.maximum(nb * i - 1, 0), 0))
    return pl.pallas_call(
        _swa_kernel,
        grid=(t // (nb * b),),
        in_specs=[pl.BlockSpec(memory_space=pltpu.SMEM),
                  cur(CB_SQ, qw), cur(CB_SQR, qw),
                  cur(CB_SK, kw), cur(CB_SKR, kw), cur(CB_SVA, kw), cur(CB_SVB, kw),
                  prev(CB_SK, kw), prev(CB_SKR, kw), prev(CB_SVA, kw), prev(CB_SVB, kw),
                  tab_c, tab_c, tab_p, tab_p],
        out_specs=pl.BlockSpec((nb * b, qw), lambda i: (i, 0)),
        out_shape=jax.ShapeDtypeStruct((t, qw), BF16),
        compiler_params=_cparams(("parallel",)),
        name="swa",
    )(sinks, proj, proj, proj, proj, proj, proj, proj, proj, proj, proj, cos_s, sin_s, cos_s, sin_s)


def _merge_kernel(x_ref, oa_ref, hg_ref, ob_ref, rg_ref, oc_ref, ga_ref, gb_ref, gc_ref,
                  hn_ref, wa_ref, wb_ref, wc_ref, wo_ref, o_ref):
    hg = hg_ref[...].astype(F32)
    a = _rms(oa_ref[...].astype(F32)) * hn_ref[...] * (hg * _sigmoid(hg))
    ya = _dot(a.astype(BF16), wa_ref[...])
    rg = rg_ref[...].astype(F32)
    ob = ob_ref[...].astype(F32)
    parts = []
    for h in range(N_HEADS):
        lanes = slice(h * HEAD_DIM, (h + 1) * HEAD_DIM)
        parts.append(_rms(ob[:, lanes]))
    bn = jnp.concatenate(parts, axis=-1) * (rg * _sigmoid(rg))
    yb = _dot(bn.astype(BF16), wb_ref[...])
    yc = _dot(oc_ref[...], wc_ref[...])
    mixed = (_sigmoid(ga_ref[...].astype(F32)) * ya + _sigmoid(gb_ref[...].astype(F32)) * yb
             + _sigmoid(gc_ref[...].astype(F32)) * yc)
    o_ref[...] = x_ref[...] + _dot(mixed.astype(BF16), wo_ref[...])


def _merge(x, proj, oa, ob, oc, hn, wa, wb, wc, wo):
    t = x.shape[0]
    tt = min(t, 512)
    w = N_HEADS * HEAD_DIM
    qw = SWA_Q_HEADS * SWA_DIM

    def rows(width, cb=0):
        return pl.BlockSpec((tt, width), lambda i: (i, cb * LANE // width))

    def full(shape):
        return pl.BlockSpec(shape, lambda i: (0, 0))

    return pl.pallas_call(
        _merge_kernel,
        grid=(t // tt,),
        in_specs=[rows(D_MODEL), rows(w), rows(w, CB_HG), rows(w), rows(w, CB_RG), rows(qw),
                  rows(D_MODEL, CB_GATE), rows(D_MODEL, CB_GATE + 8), rows(D_MODEL, CB_GATE + 16),
                  full((1, w)), full((w, D_MODEL)), full((w, D_MODEL)), full((qw, D_MODEL)),
                  full((D_MODEL, D_MODEL))],
        out_specs=rows(D_MODEL),
        out_shape=jax.ShapeDtypeStruct((t, D_MODEL), F32),
        compiler_params=_cparams(("parallel",)),
        name="merge",
    )(x, oa, proj, ob, proj, oc, proj, proj, proj, hn, wa, wb, wc, wo)


def _memkv_kernel(m_ref, g_ref, wk_ref, wv_ref, k_ref, v_ref):
    h = (_rms(m_ref[...]) * g_ref[...]).astype(BF16)
    k_ref[...] = _dot(h, wk_ref[...]).astype(BF16)
    v_ref[...] = _dot(h, wv_ref[...]).astype(BF16)


def _memkv(mem, g, wk, wv):
    nm = mem.shape[0]
    out = jax.ShapeDtypeStruct((nm, D_MODEL), BF16)
    return pl.pallas_call(
        _memkv_kernel,
        out_shape=[out, out],
        compiler_params=pltpu.CompilerParams(vmem_limit_bytes=VMEM_LIMIT),
        name="mem_kv",
    )(mem, g, wk, wv)


def _xattn_kernel(x_ref, g_ref, wq_ref, k_ref, v_ref, wo_ref, o_ref):
    x = x_ref[...]
    h = (_rms(x) * g_ref[...]).astype(BF16)
    q = _dot(h, wq_ref[...])
    outs = []
    for hh in range(X_HEADS):
        lanes = slice(hh * X_DIM, (hh + 1) * X_DIM)
        s = _dot_nt(q[:, lanes].astype(BF16), k_ref[:, lanes]) * (X_DIM ** -0.5)
        m = jnp.max(s, axis=-1, keepdims=True)
        p = jnp.exp(s - m)
        p = p / jnp.sum(p, axis=-1, keepdims=True)
        outs.append(_dot(p.astype(BF16), v_ref[:, lanes]))
    o = jnp.concatenate(outs, axis=-1)
    o_ref[...] = x + _dot(o.astype(BF16), wo_ref[...])


def _xattn(x, g, wq, k, v, wo):
    t = x.shape[0]
    tt = min(t, 512)
    nm = k.shape[0]

    def full(shape):
        return pl.BlockSpec(shape, lambda i: (0, 0))

    rows = pl.BlockSpec((tt, D_MODEL), lambda i: (i, 0))
    return pl.pallas_call(
        _xattn_kernel,
        grid=(t // tt,),
        in_specs=[rows, full((1, D_MODEL)), full((D_MODEL, D_MODEL)),
                  full((nm, D_MODEL)), full((nm, D_MODEL)), full((D_MODEL, D_MODEL))],
        out_specs=rows,
        out_shape=jax.ShapeDtypeStruct((t, D_MODEL), F32),
        compiler_params=_cparams(("parallel",)),
        name="xattn",
    )(x, g, wq, k, v, wo)


_CAND_ROWS = [PEER_TOPK // (k1 + 1) for k1 in range(8)]
_RANK_BASE = 1e30
_RANK_STEP = 1e25


def _top_values(s, n, with_rank=False):
    vals = []
    for i in range(n):
        m = jnp.max(s, axis=0, keepdims=True)
        vals.append(m)
        s = jnp.where(s >= m, -(_RANK_BASE + i * _RANK_STEP), s)
    if not with_rank:
        return vals
    rank = jnp.where(s < -0.5 * _RANK_BASE, jnp.floor((-s - _RANK_BASE) * (1.0 / _RANK_STEP) + 0.5), float(n))
    return vals, rank


def _bf16_pair_bits(x):
    u = lax.bitcast_convert_type(x.astype(BF16).astype(F32), jnp.uint32)
    return u | (u >> 16)


def _stack_rows(rows, n):
    tt = rows[0].shape[1]
    r = lax.broadcasted_iota(jnp.int32, (n, tt), 0)
    out = jnp.zeros((n, tt), F32)
    for k in range(n):
        out = jnp.where(r == k, rows[k], out)
    return out


def _router_kernel(x_ref, g_ref, wq_ref, kh_ref, kl_ref,
                   hb_ref, a_ref, b_ref, r_ref, l_ref, qt_scr):
    h = _rms(x_ref[...]) * g_ref[...]
    hb_ref[...] = h.T.astype(BF16)
    qt_scr[...] = _dot_nt(wq_ref[...], h.astype(BF16))
    tt = x_ref.shape[0]
    k = PEER_TOPK
    r8 = lax.broadcasted_iota(jnp.int32, (8, tt), 0)

    def head_body(hh, carry):
        scores = []
        for p in range(2):
            r0 = pl.multiple_of(hh * (2 * PEER_HALF) + p * PEER_HALF, PEER_HALF)
            q_hi, q_lo = _split2(qt_scr[pl.ds(r0, PEER_HALF), :])
            kh = kh_ref[p * PEER_HEADS + hh]
            scores.append(_dot(kh, q_hi) + _dot(kl_ref[p * PEER_HEADS + hh], q_hi) + _dot(kh, q_lo))
        s1, s2 = scores
        v1, rank1 = _top_values(s1, k, with_rank=True)
        v2, rank2 = _top_values(s2, k, with_rank=True)
        sv1 = _stack_rows(v1, k)
        sv2 = _stack_rows(v2, k)
        groups = [v1[0] + sv2]
        for k1 in range(1, 8):
            groups.append(jnp.where(r8 < _CAND_ROWS[k1], v1[k1] + sv2[0:8], -jnp.inf))
        groups.append(sv1[8:16] + v2[0])
        cand = jnp.concatenate(groups, axis=0)
        tau = _top_values(cand, k)[k - 1]
        m0 = v1[0] + v2[0]
        z = jnp.sum(jnp.where(cand >= tau, jnp.exp(cand - m0), 0.0), axis=0, keepdims=True)
        a_ref[hh] = _bf16_pair_bits(jnp.exp(s1 - v1[0]))
        b_ref[hh] = (jnp.exp(s2 - v2[0]) / z).astype(BF16)
        count1 = jnp.zeros_like(s1)
        for k1 in range(k):
            n_k1 = jnp.sum((v1[k1] + sv2 >= tau).astype(F32), axis=0, keepdims=True)
            count1 = jnp.where(rank1 == float(k1), n_k1, count1)
        r_ref[hh] = rank2.astype(BF16)
        l_ref[hh] = _bf16_pair_bits(count1)
        return carry

    lax.fori_loop(0, PEER_HEADS, head_body, 0, unroll=2)


def _router(x, g, wq_t, k_hi, k_lo):
    t = x.shape[0]
    tt = min(t, 512)
    nq = wq_t.shape[0]
    sel = jax.ShapeDtypeStruct((PEER_HEADS, PEER_KEYS, t), jnp.uint32)
    sel_b = jax.ShapeDtypeStruct((PEER_HEADS, PEER_KEYS, t), BF16)
    sel_spec = pl.BlockSpec((PEER_HEADS, PEER_KEYS, tt), lambda i: (0, 0, i))
    return pl.pallas_call(
        _router_kernel,
        grid=(t // tt,),
        in_specs=[pl.BlockSpec((tt, D_MODEL), lambda i: (i, 0)),
                  pl.BlockSpec((1, D_MODEL), lambda i: (0, 0)),
                  pl.BlockSpec((nq, D_MODEL), lambda i: (0, 0)),
                  pl.BlockSpec((2 * PEER_HEADS, PEER_KEYS, PEER_HALF), lambda i: (0, 0, 0)),
                  pl.BlockSpec((2 * PEER_HEADS, PEER_KEYS, PEER_HALF), lambda i: (0, 0, 0))],
        out_specs=[pl.BlockSpec((D_MODEL, tt), lambda i: (0, i)), sel_spec, sel_spec, sel_spec, sel_spec],
        out_shape=[jax.ShapeDtypeStruct((D_MODEL, t), BF16), sel, sel_b, sel_b, sel],
        scratch_shapes=[pltpu.VMEM((nq, tt), F32)],
        compiler_params=_cparams(("parallel",)),
        name="peer_router",
    )(x, g, wq_t, k_hi, k_lo)


def _peer_kernel(x_ref, hb_ref, a_ref, l_ref, b_ref, r_ref, u_ref, vt_ref, gf_ref, o_ref,
                 acc_ref, ga_ref, *, final_norm):
    e = pl.program_id(1)

    @pl.when(e == 0)
    def _():
        acc_ref[...] = jnp.zeros_like(acc_ref)

    u_blk = pltpu.bitcast(u_ref[...], BF16)
    vt_blk = pltpu.bitcast(vt_ref[...], BF16)
    n_i1 = u_blk.shape[0] // PEER_KEYS
    sub = 16
    tt = hb_ref.shape[1]
    shape3 = (PEER_KEYS // sub, sub, tt)
    zero3 = jnp.zeros(shape3, BF16)
    piece = PEER_PIECE_KEYS * PEER_KEYS
    for p in range(n_i1 // PEER_PIECE_KEYS):
        for j in range(p * PEER_PIECE_KEYS, (p + 1) * PEER_PIECE_KEYS):
            gate = zero3
            for hh in range(PEER_HEADS):
                arow = pltpu.bitcast(jnp.broadcast_to(a_ref[hh, j:j + 1, :], (8, tt)), BF16)
                lrow = pltpu.bitcast(jnp.broadcast_to(l_ref[hh, j:j + 1, :], (8, tt)), BF16)
                b3 = b_ref[hh].reshape(shape3)
                r3 = r_ref[hh].reshape(shape3)
                gate = gate + arow[None] * jnp.where(r3 < lrow[None], b3, zero3)
            ga_ref[j * PEER_KEYS:(j + 1) * PEER_KEYS, :] = gate.reshape(PEER_KEYS, tt)
        rows = slice(p * piece, (p + 1) * piece)
        pre = _dot(u_blk[rows, :], hb_ref[...])
        act = pre + pre * lax.erf(pre)
        ga_ref[rows, :] = ga_ref[rows, :] * act.astype(BF16)
    acc_ref[...] += _dot(vt_blk, ga_ref[...])

    @pl.when(e == pl.num_programs(1) - 1)
    def _():
        y = x_ref[...] + acc_ref[...].T
        if final_norm:
            y = _rms(y) * gf_ref[...]
        o_ref[...] = y


def _pack_kernel(w_ref, o_ref, *, scale, transpose):
    w = w_ref[...]
    if transpose:
        w = w.T
    o_ref[...] = pltpu.bitcast((w * scale).astype(BF16), jnp.uint32)


def _pack_table(w, scale, transpose):
    rows, cols = w.shape
    if transpose:
        blk = 512
        in_spec = pl.BlockSpec((blk, cols), lambda i: (i, 0))
        out_spec = pl.BlockSpec((cols // 2, blk), lambda i: (0, i))
        out_shape = jax.ShapeDtypeStruct((cols // 2, rows), jnp.uint32)
    else:
        blk = 1024
        in_spec = pl.BlockSpec((blk, cols), lambda i: (i, 0))
        out_spec = pl.BlockSpec((blk // 2, cols), lambda i: (i, 0))
        out_shape = jax.ShapeDtypeStruct((rows // 2, cols), jnp.uint32)
    return pl.pallas_call(
        functools.partial(_pack_kernel, scale=scale, transpose=transpose),
        grid=(rows // blk,),
        in_specs=[in_spec],
        out_specs=out_spec,
        out_shape=out_shape,
        compiler_params=_cparams(("parallel",)),
        name="pack_table_t" if transpose else "pack_table",
    )(w)


def _peer(x, hb, a, l, b, r, u, vt, gf, final_norm):
    t = x.shape[0]
    tt = min(t, 512)
    eb = 16 * PEER_KEYS
    rows = pl.BlockSpec((tt, D_MODEL), lambda i, e: (i, 0))
    i1_rows = pl.BlockSpec((PEER_HEADS, eb // PEER_KEYS, tt), lambda i, e: (0, e, i))
    all_rows = pl.BlockSpec((PEER_HEADS, PEER_KEYS, tt), lambda i, e: (0, 0, i))
    return pl.pallas_call(
        functools.partial(_peer_kernel, final_norm=final_norm),
        grid=(t // tt, N_EXPERTS // eb),
        in_specs=[rows, pl.BlockSpec((D_MODEL, tt), lambda i, e: (0, i)),
                  i1_rows, i1_rows, all_rows, all_rows,
                  pl.BlockSpec((eb // 2, D_MODEL), lambda i, e: (e, 0)),
                  pl.BlockSpec((D_MODEL // 2, eb), lambda i, e: (0, e)),
                  pl.BlockSpec((1, D_MODEL), lambda i, e: (0, 0))],
        out_specs=rows,
        out_shape=jax.ShapeDtypeStruct((t, D_MODEL), F32),
        scratch_shapes=[pltpu.VMEM((D_MODEL, tt), F32), pltpu.VMEM((eb, tt), BF16)],
        compiler_params=_cparams(("parallel", "arbitrary")),
        name="peer_dense",
    )(x, hb, a, l, b, r, u, vt, gf)


def kernel(x, mem, positions, ln_mix, w_in, lb_param, hgrn_norm, swa_sinks, w_br_hgrn, w_br_ret,
           w_br_swa, w_out, ln_xq, ln_xkv, w_xq, w_xk, w_xv, w_xo, ln_ffn, peer_wq, peer_keys,
           peer_u, peer_v, ln_final):
    b, t, d = x.shape
    assert b == 1 and d == D_MODEL
    depth = w_in.shape[0]
    xs = x.reshape(t, d)
    ms = mem.reshape(mem.shape[1], d)
    cos_r, sin_r, cos_s, sin_s = _rope_tables(positions)
    lb_sm = jax.nn.softmax(lb_param.astype(F32), axis=0)
    lower = jnp.cumsum(lb_sm, axis=0) - lb_sm[0]

    for l in range(depth):
        proj, hf = _proj(xs, ln_mix[l][None, :], _relay_w_in(w_in[l]))
        oa = _hgrn(proj, hf, lower[l][None, :])
        ob = _ret(proj, cos_r, sin_r)
        oc = _swa(proj, cos_s, sin_s, swa_sinks[l])
        xs = _merge(xs, proj, oa, ob, oc, hgrn_norm[l][None, :], w_br_hgrn[l].astype(BF16),
                    w_br_ret[l].astype(BF16), w_br_swa[l].astype(BF16), w_out[l].astype(BF16))

        km, vm = _memkv(ms, ln_xkv[l][None, :], w_xk[l].astype(BF16), w_xv[l].astype(BF16))
        xs = _xattn(xs, ln_xq[l][None, :], w_xq[l].astype(BF16), km, vm, w_xo[l].astype(BF16))

        k_hi, k_lo = _split2(peer_keys[l].reshape(2 * PEER_HEADS, PEER_KEYS, PEER_HALF))
        hb, a, bz, r2, cnt = _router(xs, ln_ffn[l][None, :], peer_wq[l].T.astype(BF16), k_hi, k_lo)
        c = 2.0 ** -0.5
        xs = _peer(xs, hb, a, cnt, bz, r2, _pack_table(peer_u[l], c, False), _pack_table(peer_v[l], c, True),
                   ln_final[None, :], final_norm=(l == depth - 1))

    return xs.reshape(b, t, d)
```

```python
import functools
import math

import numpy as np
import jax
import jax.numpy as jnp
from jax import lax
from jax.experimental import pallas as pl
from jax.experimental.pallas import tpu as pltpu

F32 = jnp.float32
BF16 = jnp.bfloat16

D_MODEL = 1024
HEAD_DIM = 128
N_HEADS = 4
SWA_Q_HEADS = 8
SWA_KV_HEADS = 2
SWA_DIM = 64
SWA_BLOCK = 128
ROPE_THETA = 10000.0
N_MEM = 256
X_HEADS = 4
X_DIM = D_MODEL // X_HEADS
PEER_HEADS = 8
PEER_KEYS = 128
PEER_TOPK = 16
PEER_HALF = 128
N_EXPERTS = PEER_KEYS * PEER_KEYS
EPS = 1e-6
NEG_BIG = -1e30
TINY = 1e-30

LANE = 128
HG_CHUNK = 64
RET_CHUNK = 256
PEER_PIECE_KEYS = 4
VMEM_LIMIT = 56 * 1024 * 1024

CB_GATE = 0
CB_HQ, CB_HF, CB_HI, CB_HG = 24, 28, 32, 36
CB_RQ, CB_RK, CB_RV, CB_RG = 40, 44, 48, 52
CB_SQ, CB_SQR, CB_SK, CB_SKR, CB_SVA, CB_SVB = 56, 60, 64, 66, 68, 70
N_CB = 72
PROJ_COLS = N_CB * LANE


def _cparams(sem):
    return pltpu.CompilerParams(dimension_semantics=sem, vmem_limit_bytes=VMEM_LIMIT)


def _dot(a, b):
    return jnp.dot(a, b, preferred_element_type=F32)


def _dot_nt(a, b):
    return lax.dot_general(a, b, (((1,), (1,)), ((), ())), preferred_element_type=F32)


def _dot_tn(a, b):
    return lax.dot_general(a, b, (((0,), (0,)), ((), ())), preferred_element_type=F32)


def _split2(x):
    hi = x.astype(BF16)
    lo = (x - hi.astype(F32)).astype(BF16)
    return hi, lo


def _split3(x):
    hi = x.astype(BF16)
    r = x - hi.astype(F32)
    mid = r.astype(BF16)
    lo = (r - mid.astype(F32)).astype(BF16)
    return hi, mid, lo


def _sigmoid(x):
    return 1.0 / (1.0 + jnp.exp(-x))


def _rms(x):
    return x * lax.rsqrt(jnp.mean(x * x, axis=-1, keepdims=True) + EPS)


def _rope_kernel(pos_ref, fr_ref, fs_ref, cr_ref, sr_ref, cs_ref, ss_ref):
    pos = pos_ref[...].astype(F32)
    lane = lax.broadcasted_iota(jnp.int32, (1, LANE), 1)
    ang_r = pos * fr_ref[...]
    cr_ref[...] = jnp.cos(ang_r)
    sr_ref[...] = jnp.sin(ang_r) * jnp.where(lane < HEAD_DIM // 2, -1.0, 1.0)
    ang_s = pos * fs_ref[...]
    cs_ref[...] = jnp.cos(ang_s)
    ss_ref[...] = jnp.sin(ang_s)


def _rope_tables(positions):
    t = positions.shape[1]
    tt = min(t, 1024)
    ret_f = ROPE_THETA ** (-jnp.linspace(0.0, 1.0, HEAD_DIM // 2, dtype=F32))
    swa_f = ROPE_THETA ** (-jnp.arange(0, SWA_DIM, 2, dtype=F32) / SWA_DIM)
    fr = jnp.concatenate([ret_f, ret_f])[None, :]
    fs = jnp.concatenate([swa_f] * (LANE // (SWA_DIM // 2)))[None, :]
    pos = positions.reshape(t, 1)
    tab = jax.ShapeDtypeStruct((t, LANE), F32)
    row = pl.BlockSpec((tt, LANE), lambda i: (i, 0))
    vec = pl.BlockSpec((1, LANE), lambda i: (0, 0))
    return pl.pallas_call(
        _rope_kernel,
        grid=(t // tt,),
        in_specs=[pl.BlockSpec((tt, 1), lambda i: (i, 0)), vec, vec],
        out_specs=[row, row, row, row],
        out_shape=[tab, tab, tab, tab],
        compiler_params=_cparams(("parallel",)),
        name="rope_tables",
    )(pos, fr, fs)


PROJ_TN = 1024


def _proj_kernel(x_ref, g_ref, w_ref, o_ref, f_ref, h_scr):
    @pl.when(pl.program_id(1) == 0)
    def _():
        h_scr[...] = (_rms(x_ref[...]) * g_ref[...]).astype(BF16)

    y = _dot(h_scr[...], w_ref[...])
    o_ref[...] = y.astype(BF16)

    @pl.when(pl.program_id(1) == CB_HF * LANE // PROJ_TN)
    def _():
        lo = CB_HF * LANE % PROJ_TN
        f_ref[...] = y[:, lo:lo + N_HEADS * HEAD_DIM]


def _proj(x, g, w):
    t = x.shape[0]
    n = w.shape[1]
    tm = min(t, 2048)
    tn = PROJ_TN
    wf = N_HEADS * HEAD_DIM
    return pl.pallas_call(
        _proj_kernel,
        grid=(t // tm, n // tn),
        in_specs=[pl.BlockSpec((tm, D_MODEL), lambda i, j: (i, 0)),
                  pl.BlockSpec((1, D_MODEL), lambda i, j: (0, 0)),
                  pl.BlockSpec((D_MODEL, tn), lambda i, j: (0, j))],
        out_specs=[pl.BlockSpec((tm, tn), lambda i, j: (i, j)),
                   pl.BlockSpec((tm, wf), lambda i, j: (i, 0))],
        out_shape=[jax.ShapeDtypeStruct((t, n), BF16), jax.ShapeDtypeStruct((t, wf), F32)],
        scratch_shapes=[pltpu.VMEM((tm, D_MODEL), BF16)],
        compiler_params=_cparams(("parallel", "arbitrary")),
        name="in_proj",
    )(x, g, w)


def _relay_w_in(w_in):
    d = w_in.shape[0]
    sizes = (512,) * 8 + (512, 128, 128, 3 * D_MODEL)
    offs = np.concatenate([[0], np.cumsum(sizes)])
    hq, hf, hi, hg, rq, rk, rv, rg, sq, sk, sv, gates = [
        w_in[:, int(offs[i]):int(offs[i + 1])] for i in range(12)]

    def rot(w, nh, dh):
        w = w.reshape(d, nh, dh)
        return jnp.concatenate([-w[..., dh // 2:], w[..., :dh // 2]], axis=-1).reshape(d, nh * dh)

    def widen(w, lo, hi):
        w = w.reshape(d, SWA_KV_HEADS, SWA_DIM)
        zero = jnp.zeros_like(w)
        parts = [w if lo else zero, w if hi else zero]
        return jnp.concatenate(parts, axis=-1).reshape(d, SWA_KV_HEADS * LANE)

    cols = [gates, hq, hf, hi, hg, rq, rk, rv, rg,
            sq, rot(sq, SWA_Q_HEADS, SWA_DIM),
            widen(sk, True, True), widen(rot(sk, SWA_KV_HEADS, SWA_DIM), True, True),
            widen(sv, True, False), widen(sv, False, True)]
    out = jnp.concatenate(cols, axis=1).astype(BF16)
    assert out.shape[1] == PROJ_COLS
    return out


def _hgrn_kernel(hq_ref, hf_ref, hi_ref, lb_ref, o_ref, st_ref, *, n_chunks):
    @pl.when(pl.program_id(0) == 0)
    def _():
        st_ref[...] = jnp.zeros_like(st_ref)

    c = HG_CHUNK
    row = lax.broadcasted_iota(jnp.int32, (c, c), 0)
    col = lax.broadcasted_iota(jnp.int32, (c, c), 1)
    diag = row == col
    halves = [c >> (i + 1) for i in range(c.bit_length() - 1)]
    pair_mask = {b: (row // (2 * b) == col // (2 * b)) & (row % (2 * b) >= b) & (col % (2 * b) < b)
                 for b in halves}
    rw = lax.broadcasted_iota(jnp.int32, (c, HEAD_DIM), 0)
    r3 = lax.broadcasted_iota(jnp.int32, (c // 8, 8, HEAD_DIM), 1)
    scale = HEAD_DIM ** -0.5
    log2e = math.log2(math.e)

    def boundary(cum2, b):
        if b >= 8:
            ends = [cum2[s + b - 1:s + b, :] for s in range(0, c, 2 * b)]
            out = ends[-1]
            for i in range(len(ends) - 2, -1, -1):
                out = jnp.where(rw < (i + 1) * 2 * b, ends[i], out)
            return out
        c3 = cum2.reshape(c // 8, 8, HEAD_DIM)
        if b == 4:
            return jnp.broadcast_to(c3[:, 3:4, :], c3.shape).reshape(c, HEAD_DIM)
        if b == 2:
            lo = jnp.broadcast_to(c3[:, 1:2, :], c3.shape)
            hi = jnp.broadcast_to(c3[:, 5:6, :], c3.shape)
            return jnp.where(r3 < 4, lo, hi).reshape(c, HEAD_DIM)
        prev = pltpu.roll(cum2, 1, 0)
        return jnp.where(rw % 2 == 1, prev, cum2)

    def chunk_body(ci, carry):
        r0 = pl.multiple_of(ci * c, c)
        for h in range(N_HEADS):
            lanes = slice(h * HEAD_DIM, (h + 1) * HEAD_DIM)
            xq = hq_ref[pl.ds(r0, c), lanes].astype(F32)
            z = hf_ref[pl.ds(r0, c), lanes]
            v = hi_ref[pl.ds(r0, c), lanes]
            lb = lb_ref[:, lanes]
            omlb = 1.0 - lb
            q = xq * _sigmoid(xq) * scale
            f = lb + omlb * _sigmoid(z)
            logf = jnp.log(jnp.maximum(f, TINY))
            kk = omlb * _sigmoid(-z)
            cum2 = logf * log2e
            step = 1
            while step < c:
                cum2 = cum2 + jnp.where(rw >= step, pltpu.roll(cum2, step, 0), 0.0)
                step *= 2
            scores = jnp.where(diag, jnp.sum(q * kk, axis=-1, keepdims=True), 0.0)
            for b in halves:
                ref = boundary(cum2, b)
                qs = (q * jnp.exp2(jnp.minimum(cum2 - ref, 0.0))).astype(BF16)
                ks = (kk * jnp.exp2(jnp.minimum(ref - cum2, 0.0))).astype(BF16)
                scores = jnp.where(pair_mask[b], _dot_nt(qs, ks), scores)
            st = st_ref[h]
            qe = (q * jnp.exp2(cum2)).astype(BF16)
            o = _dot_nt(qe, st.astype(BF16)) + _dot(scores.astype(BF16), v.astype(BF16))
            o_ref[pl.ds(r0, c), lanes] = o.astype(BF16)
            last = cum2[c - 1:c, :]
            kl = (kk * jnp.exp2(last - cum2)).astype(BF16)
            st_ref[h] = jnp.exp2(last) * st + _dot_tn(v.astype(BF16), kl)
        return carry

    lax.fori_loop(0, n_chunks, chunk_body, 0, unroll=4)


def _hgrn(proj, hf, lower):
    t = proj.shape[0]
    tb = min(t, 512)
    w = N_HEADS * HEAD_DIM

    def cols(cb):
        return pl.BlockSpec((tb, w), lambda i: (i, cb // N_HEADS))

    return pl.pallas_call(
        functools.partial(_hgrn_kernel, n_chunks=tb // HG_CHUNK),
        grid=(t // tb,),
        in_specs=[cols(CB_HQ), pl.BlockSpec((tb, w), lambda i: (i, 0)), cols(CB_HI),
                  pl.BlockSpec((1, w), lambda i: (0, 0))],
        out_specs=pl.BlockSpec((tb, w), lambda i: (i, 0)),
        out_shape=jax.ShapeDtypeStruct((t, w), BF16),
        scratch_shapes=[pltpu.VMEM((N_HEADS, HEAD_DIM, HEAD_DIM), F32)],
        compiler_params=_cparams(("arbitrary",)),
        name="hgrn2",
    )(proj, hf, proj, lower)


def _ret_kernel(q_ref, k_ref, v_ref, cos_ref, sin_ref,
                dec_ref, qd_ref, kd_ref, cd_ref, o_ref, s_ref):
    @pl.when(pl.program_id(0) == 0)
    def _():
        s_ref[...] = jnp.zeros_like(s_ref)

    c = dec_ref.shape[1]
    scale = HEAD_DIM ** -0.5
    for ci in range(q_ref.shape[0] // c):
        rows = slice(ci * c, (ci + 1) * c)
        cos = cos_ref[rows, :]
        sin = sin_ref[rows, :]
        for h in range(N_HEADS):
            lanes = slice(h * HEAD_DIM, (h + 1) * HEAD_DIM)
            q = q_ref[rows, lanes].astype(F32)
            k = k_ref[rows, lanes].astype(F32)
            q = q * cos + pltpu.roll(q, HEAD_DIM // 2, 1) * sin
            k = (k * cos + pltpu.roll(k, HEAD_DIM // 2, 1) * sin) * scale
            v = v_ref[rows, lanes]
            qb = q.astype(BF16)
            s = s_ref[h]
            scores = _dot_nt(qb, k.astype(BF16)) * dec_ref[h]
            o = _dot(qb, s.astype(BF16)) * qd_ref[h] + _dot(scores.astype(BF16), v)
            o_ref[rows, lanes] = o.astype(BF16)
            s_ref[h] = cd_ref[h] * s + _dot_tn((k * kd_ref[h]).astype(BF16), v)


def _ret(proj, cos_r, sin_r):
    t = proj.shape[0]
    c = min(t, RET_CHUNK)
    tb = 4 * c if t % (4 * c) == 0 else c
    w = N_HEADS * HEAD_DIM
    log_gamma = jnp.log(1.0 - 2.0 ** (-5.0 - jnp.arange(N_HEADS, dtype=F32)))
    idx = jnp.arange(c, dtype=F32)
    rel = idx[:, None] - idx[None, :]
    decay = jnp.exp(jnp.where(rel >= 0, log_gamma[:, None, None] * rel, NEG_BIG))
    ones = jnp.ones((1, 1, HEAD_DIM), F32)
    q_decay = jnp.exp(log_gamma[:, None] * (idx + 1.0))[:, :, None] * ones
    k_decay = jnp.exp(log_gamma[:, None] * (c - 1.0 - idx))[:, :, None] * ones
    c_decay = jnp.exp(log_gamma * c)[:, None, None] * ones

    def cols(cb):
        return pl.BlockSpec((tb, w), lambda i: (i, cb // N_HEADS))

    def const(shape):
        return pl.BlockSpec(shape, lambda i: (0, 0, 0))

    tab = pl.BlockSpec((tb, LANE), lambda i: (i, 0))
    return pl.pallas_call(
        _ret_kernel,
        grid=(t // tb,),
        in_specs=[cols(CB_RQ), cols(CB_RK), cols(CB_RV), tab, tab,
                  const((N_HEADS, c, c)), const((N_HEADS, c, HEAD_DIM)),
                  const((N_HEADS, c, HEAD_DIM)), const((N_HEADS, 1, HEAD_DIM))],
        out_specs=pl.BlockSpec((tb, w), lambda i: (i, 0)),
        out_shape=jax.ShapeDtypeStruct((t, w), BF16),
        scratch_shapes=[pltpu.VMEM((N_HEADS, HEAD_DIM, HEAD_DIM), F32)],
        compiler_params=_cparams(("arbitrary",)),
        name="retention",
    )(proj, proj, proj, cos_r, sin_r, decay, q_decay, k_decay, c_decay)


def _swa_kernel(sink_ref, q_ref, qr_ref, kc_ref, krc_ref, vac_ref, vbc_ref,
                kp_ref, krp_ref, vap_ref, vbp_ref, cc_ref, sc_ref, cp_ref, sp_ref, o_ref):
    b = SWA_BLOCK
    nb = q_ref.shape[0] // b
    first = pl.program_id(0) * nb
    cos_c, sin_c = cc_ref[...], sc_ref[...]
    cos_p, sin_p = cp_ref[...], sp_ref[...]
    qi = lax.broadcasted_iota(jnp.int32, (b, 2 * b), 0) + b
    ki = lax.broadcasted_iota(jnp.int32, (b, 2 * b), 1)
    rel = qi - ki
    in_window = (rel >= 0) & (rel < b)
    low = lax.broadcasted_iota(jnp.int32, (b, LANE), 1) < SWA_DIM
    pairs_per_group = SWA_Q_HEADS // SWA_KV_HEADS // 2
    for g in range(SWA_KV_HEADS):
        kl = slice(g * LANE, (g + 1) * LANE)
        k_cur = kc_ref[:, kl].astype(F32) * cos_c + krc_ref[:, kl].astype(F32) * sin_c
        k_prev = kp_ref[:, kl].astype(F32) * cos_p + krp_ref[:, kl].astype(F32) * sin_p
        k_all = jnp.concatenate([k_prev, k_cur], axis=0).astype(BF16)
        v_lo = jnp.concatenate([vap_ref[:, kl], vac_ref[:, kl]], axis=0)
        v_hi = jnp.concatenate([vbp_ref[:, kl], vbc_ref[:, kl]], axis=0)
        for sb in range(nb):
            win = slice(sb * b, (sb + 2) * b)
            rows = slice(sb * b, (sb + 1) * b)
            keep = in_window & ((first + sb > 0) | (ki >= b))
            kw = k_all[win]
            for j in range(pairs_per_group):
                pair = g * pairs_per_group + j
                ql = slice(pair * LANE, (pair + 1) * LANE)
                q = (q_ref[rows, ql].astype(F32) * cos_c[rows] + qr_ref[rows, ql].astype(F32) * sin_c[rows])
                out = None
                for half, vw in ((0, v_lo[win]), (1, v_hi[win])):
                    qh = jnp.where(low if half == 0 else ~low, q, 0.0).astype(BF16)
                    s = _dot_nt(qh, kw) * (SWA_DIM ** -0.5)
                    s = jnp.where(keep, s, NEG_BIG)
                    sink = sink_ref[2 * pair + half]
                    m = jnp.maximum(jnp.max(s, axis=-1, keepdims=True), sink)
                    p = jnp.exp(s - m)
                    denom = jnp.sum(p, axis=-1, keepdims=True) + jnp.exp(sink - m)
                    o = _dot(p.astype(BF16), vw) / denom
                    out = o if out is None else out + o
                o_ref[rows, ql] = out.astype(BF16)


def _swa(proj, cos_s, sin_s, sinks):
    t = proj.shape[0]
    b = SWA_BLOCK
    nb = 4 if t % (4 * b) == 0 else 1
    qw = SWA_Q_HEADS * SWA_DIM
    kw = SWA_KV_HEADS * LANE

    def cur(cb, width):
        return pl.BlockSpec((nb * b, width), lambda i: (i, cb * LANE // width))

    def prev(cb, width):
        return pl.BlockSpec((b, width), lambda i: (jnp.maximum(nb * i - 1, 0), cb * LANE // width))

    tab_c = pl.BlockSpec((nb * b, LANE), lambda i: (i, 0))
    tab_p = pl.BlockSpec((b, LANE), lambda i: (jnp.maximum(nb * i - 1, 0), 0))
    return pl.pallas_call(
        _swa_kernel,
        grid=(t // (nb * b),),
        in_specs=[pl.BlockSpec(memory_space=pltpu.SMEM),
                  cur(CB_SQ, qw), cur(CB_SQR, qw),
                  cur(CB_SK, kw), cur(CB_SKR, kw), cur(CB_SVA, kw), cur(CB_SVB, kw),
                  prev(CB_SK, kw), prev(CB_SKR, kw), prev(CB_SVA, kw), prev(CB_SVB, kw),
                  tab_c, tab_c, tab_p, tab_p],
        out_specs=pl.BlockSpec((nb * b, qw), lambda i: (i, 0)),
        out_shape=jax.ShapeDtypeStruct((t, qw), BF16),
        compiler_params=_cparams(("parallel",)),
        name="swa",
    )(sinks, proj, proj, proj, proj, proj, proj, proj, proj, proj, proj, cos_s, sin_s, cos_s, sin_s)


def _merge_kernel(x_ref, oa_ref, hg_ref, ob_ref, rg_ref, oc_ref, ga_ref, gb_ref, gc_ref,
                  hn_ref, wa_ref, wb_ref, wc_ref, wo_ref, o_ref):
    hg = hg_ref[...].astype(F32)
    a = _rms(oa_ref[...].astype(F32)) * hn_ref[...] * (hg * _sigmoid(hg))
    ya = _dot(a.astype(BF16), wa_ref[...])
    rg = rg_ref[...].astype(F32)
    ob = ob_ref[...].astype(F32)
    parts = []
    for h in range(N_HEADS):
        lanes = slice(h * HEAD_DIM, (h + 1) * HEAD_DIM)
        parts.append(_rms(ob[:, lanes]))
    bn = jnp.concatenate(parts, axis=-1) * (rg * _sigmoid(rg))
    yb = _dot(bn.astype(BF16), wb_ref[...])
    yc = _dot(oc_ref[...], wc_ref[...])
    mixed = (_sigmoid(ga_ref[...].astype(F32)) * ya + _sigmoid(gb_ref[...].astype(F32)) * yb
             + _sigmoid(gc_ref[...].astype(F32)) * yc)
    o_ref[...] = x_ref[...] + _dot(mixed.astype(BF16), wo_ref[...])


def _merge(x, proj, oa, ob, oc, hn, wa, wb, wc, wo):
    t = x.shape[0]
    tt = min(t, 512)
    w = N_HEADS * HEAD_DIM
    qw = SWA_Q_HEADS * SWA_DIM

    def rows(width, cb=0):
        return pl.BlockSpec((tt, width), lambda i: (i, cb * LANE // width))

    def full(shape):
        return pl.BlockSpec(shape, lambda i: (0, 0))

    return pl.pallas_call(
        _merge_kernel,
        grid=(t // tt,),
        in_specs=[rows(D_MODEL), rows(w), rows(w, CB_HG), rows(w), rows(w, CB_RG), rows(qw),
                  rows(D_MODEL, CB_GATE), rows(D_MODEL, CB_GATE + 8), rows(D_MODEL, CB_GATE + 16),
                  full((1, w)), full((w, D_MODEL)), full((w, D_MODEL)), full((qw, D_MODEL)),
                  full((D_MODEL, D_MODEL))],
        out_specs=rows(D_MODEL),
        out_shape=jax.ShapeDtypeStruct((t, D_MODEL), F32),
        compiler_params=_cparams(("parallel",)),
        name="merge",
    )(x, oa, proj, ob, proj, oc, proj, proj, proj, hn, wa, wb, wc, wo)


def _memkv_kernel(m_ref, g_ref, wk_ref, wv_ref, k_ref, v_ref):
    h = (_rms(m_ref[...]) * g_ref[...]).astype(BF16)
    k_ref[...] = _dot(h, wk_ref[...]).astype(BF16)
    v_ref[...] = _dot(h, wv_ref[...]).astype(BF16)


def _memkv(mem, g, wk, wv):
    nm = mem.shape[0]
    out = jax.ShapeDtypeStruct((nm, D_MODEL), BF16)
    return pl.pallas_call(
        _memkv_kernel,
        out_shape=[out, out],
        compiler_params=pltpu.CompilerParams(vmem_limit_bytes=VMEM_LIMIT),
        name="mem_kv",
    )(mem, g, wk, wv)


def _xattn_kernel(x_ref, g_ref, wq_ref, k_ref, v_ref, wo_ref, o_ref):
    x = x_ref[...]
    h = (_rms(x) * g_ref[...]).astype(BF16)
    q = _dot(h, wq_ref[...])
    outs = []
    for hh in range(X_HEADS):
        lanes = slice(hh * X_DIM, (hh + 1) * X_DIM)
        s = _dot_nt(q[:, lanes].astype(BF16), k_ref[:, lanes]) * (X_DIM ** -0.5)
        m = jnp.max(s, axis=-1, keepdims=True)
        p = jnp.exp(s - m)
        p = p / jnp.sum(p, axis=-1, keepdims=True)
        outs.append(_dot(p.astype(BF16), v_ref[:, lanes]))
    o = jnp.concatenate(outs, axis=-1)
    o_ref[...] = x + _dot(o.astype(BF16), wo_ref[...])


def _xattn(x, g, wq, k, v, wo):
    t = x.shape[0]
    tt = min(t, 512)
    nm = k.shape[0]

    def full(shape):
        return pl.BlockSpec(shape, lambda i: (0, 0))

    rows = pl.BlockSpec((tt, D_MODEL), lambda i: (i, 0))
    return pl.pallas_call(
        _xattn_kernel,
        grid=(t // tt,),
        in_specs=[rows, full((1, D_MODEL)), full((D_MODEL, D_MODEL)),
                  full((nm, D_MODEL)), full((nm, D_MODEL)), full((D_MODEL, D_MODEL))],
        out_specs=rows,
        out_shape=jax.ShapeDtypeStruct((t, D_MODEL), F32),
        compiler_params=_cparams(("parallel",)),
        name="xattn",
    )(x, g, wq, k, v, wo)


_CAND_ROWS = [PEER_TOPK // (k1 + 1) for k1 in range(8)]
_RANK_BASE = 3.0e38
_RANK_STEP = 3.0e33


def _top_values(s, n, with_rank=False):
    vals = []
    for i in range(n):
        m = jnp.max(s, axis=0, keepdims=True)
        vals.append(m)
        s = jnp.where(s >= m, -(_RANK_BASE + i * _RANK_STEP), s)
    if not with_rank:
        return vals
    rank = jnp.where(s <= -_RANK_BASE, jnp.floor((-s - _RANK_BASE) * (1.0 / _RANK_STEP) + 0.5), float(n))
    return vals, rank


def _bf16_pair_bits(x):
    u = lax.bitcast_convert_type(x.astype(BF16).astype(F32), jnp.uint32)
    return u | (u >> 16)


def _stack_rows(rows, n):
    tt = rows[0].shape[1]
    r = lax.broadcasted_iota(jnp.int32, (n, tt), 0)
    out = jnp.zeros((n, tt), F32)
    for k in range(n):
        out = jnp.where(r == k, rows[k], out)
    return out


def _router_kernel(x_ref, g_ref, wq_ref, kh_ref, kl_ref,
                   hb_ref, a_ref, b_ref, r_ref, l_ref, qt_scr):
    h = _rms(x_ref[...]) * g_ref[...]
    hb_ref[...] = h.T.astype(BF16)
    qt_scr[...] = _dot_nt(wq_ref[...], h.astype(BF16))
    tt = x_ref.shape[0]
    k = PEER_TOPK
    r8 = lax.broadcasted_iota(jnp.int32, (8, tt), 0)

    def head_body(hh, carry):
        scores = []
        for p in range(2):
            r0 = pl.multiple_of(hh * (2 * PEER_HALF) + p * PEER_HALF, PEER_HALF)
            q_hi, q_lo = _split2(qt_scr[pl.ds(r0, PEER_HALF), :])
            kh = kh_ref[p * PEER_HEADS + hh]
            scores.append(_dot(kh, q_hi) + _dot(kl_ref[p * PEER_HEADS + hh], q_hi) + _dot(kh, q_lo))
        s1, s2 = scores
        v1, rank1 = _top_values(s1, k, with_rank=True)
        v2, rank2 = _top_values(s2, k, with_rank=True)
        sv1 = _stack_rows(v1, k)
        sv2 = _stack_rows(v2, k)
        groups = [v1[0] + sv2]
        for k1 in range(1, 8):
            groups.append(jnp.where(r8 < _CAND_ROWS[k1], v1[k1] + sv2[0:8], -jnp.inf))
        groups.append(sv1[8:16] + v2[0])
        cand = jnp.concatenate(groups, axis=0)
        tau = _top_values(cand, k)[k - 1]
        m0 = v1[0] + v2[0]
        z = jnp.sum(jnp.where(cand >= tau, jnp.exp(cand - m0), 0.0), axis=0, keepdims=True)
        a_ref[hh] = _bf16_pair_bits(jnp.exp(s1 - v1[0]))
        b_ref[hh] = (jnp.exp(s2 - v2[0]) / z).astype(BF16)
        count1 = jnp.zeros_like(s1)
        for k1 in range(k):
            n_k1 = jnp.sum((v1[k1] + sv2 >= tau).astype(F32), axis=0, keepdims=True)
            count1 = jnp.where(rank1 == float(k1), n_k1, count1)
        r_ref[hh] = rank2.astype(BF16)
        l_ref[hh] = _bf16_pair_bits(count1)
        return carry

    lax.fori_loop(0, PEER_HEADS, head_body, 0, unroll=2)


def _router(x, g, wq_t, k_hi, k_lo):
    t = x.shape[0]
    tt = min(t, 512)
    nq = wq_t.shape[0]
    sel = jax.ShapeDtypeStruct((PEER_HEADS, PEER_KEYS, t), jnp.uint32)
    sel_b = jax.ShapeDtypeStruct((PEER_HEADS, PEER_KEYS, t), BF16)
    sel_spec = pl.BlockSpec((PEER_HEADS, PEER_KEYS, tt), lambda i: (0, 0, i))
    return pl.pallas_call(
        _router_kernel,
        grid=(t // tt,),
        in_specs=[pl.BlockSpec((tt, D_MODEL), lambda i: (i, 0)),
                  pl.BlockSpec((1, D_MODEL), lambda i: (0, 0)),
                  pl.BlockSpec((nq, D_MODEL), lambda i: (0, 0)),
                  pl.BlockSpec((2 * PEER_HEADS, PEER_KEYS, PEER_HALF), lambda i: (0, 0, 0)),
                  pl.BlockSpec((2 * PEER_HEADS, PEER_KEYS, PEER_HALF), lambda i: (0, 0, 0))],
        out_specs=[pl.BlockSpec((D_MODEL, tt), lambda i: (0, i)), sel_spec, sel_spec, sel_spec, sel_spec],
        out_shape=[jax.ShapeDtypeStruct((D_MODEL, t), BF16), sel, sel_b, sel_b, sel],
        scratch_shapes=[pltpu.VMEM((nq, tt), F32)],
        compiler_params=_cparams(("parallel",)),
        name="peer_router",
    )(x, g, wq_t, k_hi, k_lo)


def _peer_kernel(x_ref, hb_ref, a_ref, l_ref, b_ref, r_ref, u_ref, vt_ref, gf_ref, o_ref,
                 acc_ref, ga_ref, *, final_norm):
    e = pl.program_id(1)

    @pl.when(e == 0)
    def _():
        acc_ref[...] = jnp.zeros_like(acc_ref)

    u_blk = pltpu.bitcast(u_ref[...], BF16)
    vt_blk = pltpu.bitcast(vt_ref[...], BF16)
    n_i1 = u_blk.shape[0] // PEER_KEYS
    sub = 16
    tt = hb_ref.shape[1]
    shape3 = (PEER_KEYS // sub, sub, tt)
    zero3 = jnp.zeros(shape3, BF16)
    piece = PEER_PIECE_KEYS * PEER_KEYS
    for p in range(n_i1 // PEER_PIECE_KEYS):
        for j in range(p * PEER_PIECE_KEYS, (p + 1) * PEER_PIECE_KEYS):
            gate = zero3
            for hh in range(PEER_HEADS):
                arow = pltpu.bitcast(jnp.broadcast_to(a_ref[hh, j:j + 1, :], (8, tt)), BF16)
                lrow = pltpu.bitcast(jnp.broadcast_to(l_ref[hh, j:j + 1, :], (8, tt)), BF16)
                b3 = b_ref[hh].reshape(shape3)
                r3 = r_ref[hh].reshape(shape3)
                gate = gate + arow[None] * jnp.where(r3 < lrow[None], b3, zero3)
            ga_ref[j * PEER_KEYS:(j + 1) * PEER_KEYS, :] = gate.reshape(PEER_KEYS, tt)
        rows = slice(p * piece, (p + 1) * piece)
        pre = _dot(u_blk[rows, :], hb_ref[...])
        act = pre + pre * lax.erf(pre)
        ga_ref[rows, :] = ga_ref[rows, :] * act.astype(BF16)
    acc_ref[...] += _dot(vt_blk, ga_ref[...])

    @pl.when(e == pl.num_programs(1) - 1)
    def _():
        y = x_ref[...] + acc_ref[...].T
        if final_norm:
            y = _rms(y) * gf_ref[...]
        o_ref[...] = y


def _pack_kernel(w_ref, o_ref, *, scale, transpose):
    w = w_ref[...]
    if transpose:
        w = w.T
    o_ref[...] = pltpu.bitcast((w * scale).astype(BF16), jnp.uint32)


def _pack_table(w, scale, transpose):
    rows, cols = w.shape
    if transpose:
        blk = 512
        in_spec = pl.BlockSpec((blk, cols), lambda i: (i, 0))
        out_spec = pl.BlockSpec((cols // 2, blk), lambda i: (0, i))
        out_shape = jax.ShapeDtypeStruct((cols // 2, rows), jnp.uint32)
    else:
        blk = 1024
        in_spec = pl.BlockSpec((blk, cols), lambda i: (i, 0))
        out_spec = pl.BlockSpec((blk // 2, cols), lambda i: (i, 0))
        out_shape = jax.ShapeDtypeStruct((rows // 2, cols), jnp.uint32)
    return pl.pallas_call(
        functools.partial(_pack_kernel, scale=scale, transpose=transpose),
        grid=(rows // blk,),
        in_specs=[in_spec],
        out_specs=out_spec,
        out_shape=out_shape,
        compiler_params=_cparams(("parallel",)),
        name="pack_table_t" if transpose else "pack_table",
    )(w)


def _peer(x, hb, a, l, b, r, u, vt, gf, final_norm):
    t = x.shape[0]
    tt = min(t, 512)
    eb = 16 * PEER_KEYS
    rows = pl.BlockSpec((tt, D_MODEL), lambda i, e: (i, 0))
    i1_rows = pl.BlockSpec((PEER_HEADS, eb // PEER_KEYS, tt), lambda i, e: (0, e, i))
    all_rows = pl.BlockSpec((PEER_HEADS, PEER_KEYS, tt), lambda i, e: (0, 0, i))
    return pl.pallas_call(
        functools.partial(_peer_kernel, final_norm=final_norm),
        grid=(t // tt, N_EXPERTS // eb),
        in_specs=[rows, pl.BlockSpec((D_MODEL, tt), lambda i, e: (0, i)),
                  i1_rows, i1_rows, all_rows, all_rows,
                  pl.BlockSpec((eb // 2, D_MODEL), lambda i, e: (e, 0)),
                  pl.BlockSpec((D_MODEL // 2, eb), lambda i, e: (0, e)),
                  pl.BlockSpec((1, D_MODEL), lambda i, e: (0, 0))],
        out_specs=rows,
        out_shape=jax.ShapeDtypeStruct((t, D_MODEL), F32),
        scratch_shapes=[pltpu.VMEM((D_MODEL, tt), F32), pltpu.VMEM((eb, tt), BF16)],
        compiler_params=_cparams(("parallel", "arbitrary")),
        name="peer_dense",
    )(x, hb, a, l, b, r, u, vt, gf)


def kernel(x, mem, positions, ln_mix, w_in, lb_param, hgrn_norm, swa_sinks, w_br_hgrn, w_br_ret,
           w_br_swa, w_out, ln_xq, ln_xkv, w_xq, w_xk, w_xv, w_xo, ln_ffn, peer_wq, peer_keys,
           peer_u, peer_v, ln_final):
    b, t, d = x.shape
    assert b == 1 and d == D_MODEL
    depth = w_in.shape[0]
    xs = x.reshape(t, d)
    ms = mem.reshape(mem.shape[1], d)
    cos_r, sin_r, cos_s, sin_s = _rope_tables(positions)
    lb_sm = jax.nn.softmax(lb_param.astype(F32), axis=0)
    lower = jnp.cumsum(lb_sm, axis=0) - lb_sm[0]

    for l in range(depth):
        proj, hf = _proj(xs, ln_mix[l][None, :], _relay_w_in(w_in[l]))
        oa = _hgrn(proj, hf, lower[l][None, :])
        ob = _ret(proj, cos_r, sin_r)
        oc = _swa(proj, cos_s, sin_s, swa_sinks[l])
        xs = _merge(xs, proj, oa, ob, oc, hgrn_norm[l][None, :], w_br_hgrn[l].astype(BF16),
                    w_br_ret[l].astype(BF16), w_br_swa[l].astype(BF16), w_out[l].astype(BF16))

        km, vm = _memkv(ms, ln_xkv[l][None, :], w_xk[l].astype(BF16), w_xv[l].astype(BF16))
        xs = _xattn(xs, ln_xq[l][None, :], w_xq[l].astype(BF16), km, vm, w_xo[l].astype(BF16))

        k_hi, k_lo = _split2(peer_keys[l].reshape(2 * PEER_HEADS, PEER_KEYS, PEER_HALF))
        hb, a, bz, r2, cnt = _router(xs, ln_ffn[l][None, :], peer_wq[l].T.astype(BF16), k_hi, k_lo)
        c = 2.0 ** -0.5
        xs = _peer(xs, hb, a, cnt, bz, r2, _pack_table(peer_u[l], c, False), _pack_table(peer_v[l], c, True),
                   ln_final[None, :], final_norm=(l == depth - 1))

    return xs.reshape(b, t, d)
```

```python
import functools
import math

import numpy as np
import jax
import jax.numpy as jnp
from jax import lax
from jax.experimental import pallas as pl
from jax.experimental.pallas import tpu as pltpu

F32 = jnp.float32
BF16 = jnp.bfloat16

D_MODEL = 1024
HEAD_DIM = 128
N_HEADS = 4
SWA_Q_HEADS = 8
SWA_KV_HEADS = 2
SWA_DIM = 64
SWA_BLOCK = 128
ROPE_THETA = 10000.0
N_MEM = 256
X_HEADS = 4
X_DIM = D_MODEL // X_HEADS
PEER_HEADS = 8
PEER_KEYS = 128
PEER_TOPK = 16
PEER_HALF = 128
N_EXPERTS = PEER_KEYS * PEER_KEYS
EPS = 1e-6
NEG_BIG = -1e30
TINY = 1e-30

LANE = 128
HG_CHUNK = 64
RET_CHUNK = 256
PEER_PIECE_KEYS = 4
VMEM_LIMIT = 56 * 1024 * 1024

CB_GATE = 0
CB_HQ, CB_HF, CB_HI, CB_HG = 24, 28, 32, 36
CB_RQ, CB_RK, CB_RV, CB_RG = 40, 44, 48, 52
CB_SQ, CB_SQR, CB_SK, CB_SKR, CB_SVA, CB_SVB = 56, 60, 64, 66, 68, 70
N_CB = 72
PROJ_COLS = N_CB * LANE


def _cparams(sem):
    return pltpu.CompilerParams(dimension_semantics=sem, vmem_limit_bytes=VMEM_LIMIT)


def _dot(a, b):
    return jnp.dot(a, b, preferred_element_type=F32)


def _dot_nt(a, b):
    return lax.dot_general(a, b, (((1,), (1,)), ((), ())), preferred_element_type=F32)


def _dot_tn(a, b):
    return lax.dot_general(a, b, (((0,), (0,)), ((), ())), preferred_element_type=F32)


def _split2(x):
    hi = x.astype(BF16)
    lo = (x - hi.astype(F32)).astype(BF16)
    return hi, lo


def _split3(x):
    hi = x.astype(BF16)
    r = x - hi.astype(F32)
    mid = r.astype(BF16)
    lo = (r - mid.astype(F32)).astype(BF16)
    return hi, mid, lo


def _sigmoid(x):
    return 1.0 / (1.0 + jnp.exp(-x))


def _rms(x):
    return x * lax.rsqrt(jnp.mean(x * x, axis=-1, keepdims=True) + EPS)


def _rope_kernel(pos_ref, fr_ref, fs_ref, cr_ref, sr_ref, cs_ref, ss_ref):
    pos = pos_ref[...].astype(F32)
    lane = lax.broadcasted_iota(jnp.int32, (1, LANE), 1)
    ang_r = pos * fr_ref[...]
    cr_ref[...] = jnp.cos(ang_r)
    sr_ref[...] = jnp.sin(ang_r) * jnp.where(lane < HEAD_DIM // 2, -1.0, 1.0)
    ang_s = pos * fs_ref[...]
    cs_ref[...] = jnp.cos(ang_s)
    ss_ref[...] = jnp.sin(ang_s)


def _rope_tables(positions):
    t = positions.shape[1]
    tt = min(t, 1024)
    ret_f = ROPE_THETA ** (-jnp.linspace(0.0, 1.0, HEAD_DIM // 2, dtype=F32))
    swa_f = ROPE_THETA ** (-jnp.arange(0, SWA_DIM, 2, dtype=F32) / SWA_DIM)
    fr = jnp.concatenate([ret_f, ret_f])[None, :]
    fs = jnp.concatenate([swa_f] * (LANE // (SWA_DIM // 2)))[None, :]
    pos = positions.reshape(t, 1)
    tab = jax.ShapeDtypeStruct((t, LANE), F32)
    row = pl.BlockSpec((tt, LANE), lambda i: (i, 0))
    vec = pl.BlockSpec((1, LANE), lambda i: (0, 0))
    return pl.pallas_call(
        _rope_kernel,
        grid=(t // tt,),
        in_specs=[pl.BlockSpec((tt, 1), lambda i: (i, 0)), vec, vec],
        out_specs=[row, row, row, row],
        out_shape=[tab, tab, tab, tab],
        compiler_params=_cparams(("parallel",)),
        name="rope_tables",
    )(pos, fr, fs)


PROJ_TN = 1024


def _proj_kernel(x_ref, g_ref, w_ref, o_ref, f_ref, h_scr):
    @pl.when(pl.program_id(1) == 0)
    def _():
        h_scr[...] = (_rms(x_ref[...]) * g_ref[...]).astype(BF16)

    y = _dot(h_scr[...], w_ref[...])
    o_ref[...] = y.astype(BF16)

    @pl.when(pl.program_id(1) == CB_HF * LANE // PROJ_TN)
    def _():
        lo = CB_HF * LANE % PROJ_TN
        f_ref[...] = y[:, lo:lo + N_HEADS * HEAD_DIM]


def _proj(x, g, w):
    t = x.shape[0]
    n = w.shape[1]
    tm = min(t, 2048)
    tn = PROJ_TN
    wf = N_HEADS * HEAD_DIM
    return pl.pallas_call(
        _proj_kernel,
        grid=(t // tm, n // tn),
        in_specs=[pl.BlockSpec((tm, D_MODEL), lambda i, j: (i, 0)),
                  pl.BlockSpec((1, D_MODEL), lambda i, j: (0, 0)),
                  pl.BlockSpec((D_MODEL, tn), lambda i, j: (0, j))],
        out_specs=[pl.BlockSpec((tm, tn), lambda i, j: (i, j)),
                   pl.BlockSpec((tm, wf), lambda i, j: (i, 0))],
        out_shape=[jax.ShapeDtypeStruct((t, n), BF16), jax.ShapeDtypeStruct((t, wf), F32)],
        scratch_shapes=[pltpu.VMEM((tm, D_MODEL), BF16)],
        compiler_params=_cparams(("parallel", "arbitrary")),
        name="in_proj",
    )(x, g, w)


def _relay_plan():
    names = (["hq"] * 4 + ["hf"] * 4 + ["hi"] * 4 + ["hg"] * 4 + ["rq"] * 4 + ["rk"] * 4 + ["rv"] * 4
             + ["rg"] * 4 + ["sq"] * 4 + ["sk", "sv"] + ["gate"] * 24)
    first = {}
    for i, n in enumerate(names):
        first.setdefault(n, i)
    plan = [(first["gate"] + i, 0, 0) for i in range(24)]
    for n in ("hq", "hf", "hi", "hg", "rq", "rk", "rv", "rg", "sq"):
        plan += [(first[n] + i, 0, 0) for i in range(4)]
    plan += [(first["sq"] + i, 1, 0) for i in range(4)]
    plan += [(first["sk"], 0, 1), (first["sk"], 0, 2)]
    plan += [(first["sk"], 1, 1), (first["sk"], 1, 2)]
    plan += [(first["sv"], 0, 3), (first["sv"], 0, 4)]
    plan += [(first["sv"], 0, 5), (first["sv"], 0, 6)]
    assert len(plan) == N_CB
    return plan


def _relay_kernel(src_ref, rot_ref, put_ref, w_ref, o_ref):
    j = pl.program_id(0)
    x = w_ref[...]
    lane = lax.broadcasted_iota(jnp.int32, x.shape, 1)
    swapped = pltpu.roll(x, SWA_DIM, 1)
    first_half = lane % SWA_DIM < SWA_DIM // 2
    rotated = jnp.where(first_half, -pltpu.roll(x, LANE - SWA_DIM // 2, 1), pltpu.roll(x, SWA_DIM // 2, 1))
    rotated_swapped = pltpu.roll(rotated, SWA_DIM, 1)
    rot = rot_ref[j] == 1
    y = jnp.where(rot, rotated, x)
    ys = jnp.where(rot, rotated_swapped, swapped)
    low = lane < SWA_DIM
    put = put_ref[j]
    out = y
    out = jnp.where(put == 1, jnp.where(low, y, ys), out)
    out = jnp.where(put == 2, jnp.where(low, ys, y), out)
    out = jnp.where(put == 3, jnp.where(low, y, 0.0), out)
    out = jnp.where(put == 4, jnp.where(low, ys, 0.0), out)
    out = jnp.where(put == 5, jnp.where(low, 0.0, ys), out)
    out = jnp.where(put == 6, jnp.where(low, 0.0, y), out)
    o_ref[...] = out.astype(BF16)


def _relay_w_in(w_in):
    d = w_in.shape[0]
    plan = np.asarray(_relay_plan(), np.int32)
    grid_spec = pltpu.PrefetchScalarGridSpec(
        num_scalar_prefetch=3,
        grid=(N_CB,),
        in_specs=[pl.BlockSpec((d, LANE), lambda j, src, rot, put: (0, src[j]))],
        out_specs=pl.BlockSpec((d, LANE), lambda j, src, rot, put: (0, j)),
    )
    return pl.pallas_call(
        _relay_kernel,
        grid_spec=grid_spec,
        out_shape=jax.ShapeDtypeStruct((d, PROJ_COLS), BF16),
        compiler_params=_cparams(("arbitrary",)),
        name="relay_w_in",
    )(jnp.asarray(plan[:, 0]), jnp.asarray(plan[:, 1]), jnp.asarray(plan[:, 2]), w_in)


def _hgrn_kernel(hq_ref, hf_ref, hi_ref, lb_ref, o_ref, st_ref, *, n_chunks):
    @pl.when(pl.program_id(0) == 0)
    def _():
        st_ref[...] = jnp.zeros_like(st_ref)

    c = HG_CHUNK
    row = lax.broadcasted_iota(jnp.int32, (c, c), 0)
    col = lax.broadcasted_iota(jnp.int32, (c, c), 1)
    diag = row == col
    halves = [c >> (i + 1) for i in range(c.bit_length() - 1)]
    pair_mask = {b: (row // (2 * b) == col // (2 * b)) & (row % (2 * b) >= b) & (col % (2 * b) < b)
                 for b in halves}
    rw = lax.broadcasted_iota(jnp.int32, (c, HEAD_DIM), 0)
    r3 = lax.broadcasted_iota(jnp.int32, (c // 8, 8, HEAD_DIM), 1)
    scale = HEAD_DIM ** -0.5
    log2e = math.log2(math.e)

    def boundary(cum2, b):
        if b >= 8:
            ends = [cum2[s + b - 1:s + b, :] for s in range(0, c, 2 * b)]
            out = ends[-1]
            for i in range(len(ends) - 2, -1, -1):
                out = jnp.where(rw < (i + 1) * 2 * b, ends[i], out)
            return out
        c3 = cum2.reshape(c // 8, 8, HEAD_DIM)
        if b == 4:
            return jnp.broadcast_to(c3[:, 3:4, :], c3.shape).reshape(c, HEAD_DIM)
        if b == 2:
            lo = jnp.broadcast_to(c3[:, 1:2, :], c3.shape)
            hi = jnp.broadcast_to(c3[:, 5:6, :], c3.shape)
            return jnp.where(r3 < 4, lo, hi).reshape(c, HEAD_DIM)
        prev = pltpu.roll(cum2, 1, 0)
        return jnp.where(rw % 2 == 1, prev, cum2)

    def chunk_body(ci, carry):
        r0 = pl.multiple_of(ci * c, c)
        for h in range(N_HEADS):
            lanes = slice(h * HEAD_DIM, (h + 1) * HEAD_DIM)
            xq = hq_ref[pl.ds(r0, c), lanes].astype(F32)
            z = hf_ref[pl.ds(r0, c), lanes]
            v = hi_ref[pl.ds(r0, c), lanes]
            lb = lb_ref[:, lanes]
            omlb = 1.0 - lb
            q = xq * _sigmoid(xq) * scale
            f = lb + omlb * _sigmoid(z)
            logf = jnp.log(jnp.maximum(f, TINY))
            kk = omlb * _sigmoid(-z)
            cum2 = logf * log2e
            step = 1
            while step < c:
                cum2 = cum2 + jnp.where(rw >= step, pltpu.roll(cum2, step, 0), 0.0)
                step *= 2
            scores = jnp.where(diag, jnp.sum(q * kk, axis=-1, keepdims=True), 0.0)
            for b in halves:
                ref = boundary(cum2, b)
                qs = (q * jnp.exp2(jnp.minimum(cum2 - ref, 0.0))).astype(BF16)
                ks = (kk * jnp.exp2(jnp.minimum(ref - cum2, 0.0))).astype(BF16)
                scores = jnp.where(pair_mask[b], _dot_nt(qs, ks), scores)
            st = st_ref[h]
            qe = (q * jnp.exp2(cum2)).astype(BF16)
            o = _dot_nt(qe, st.astype(BF16)) + _dot(scores.astype(BF16), v.astype(BF16))
            o_ref[pl.ds(r0, c), lanes] = o.astype(BF16)
            last = cum2[c - 1:c, :]
            kl = (kk * jnp.exp2(last - cum2)).astype(BF16)
            st_ref[h] = jnp.exp2(last) * st + _dot_tn(v.astype(BF16), kl)
        return carry

    lax.fori_loop(0, n_chunks, chunk_body, 0, unroll=4)


def _hgrn(proj, hf, lower):
    t = proj.shape[0]
    tb = min(t, 512)
    w = N_HEADS * HEAD_DIM

    def cols(cb):
        return pl.BlockSpec((tb, w), lambda i: (i, cb // N_HEADS))

    return pl.pallas_call(
        functools.partial(_hgrn_kernel, n_chunks=tb // HG_CHUNK),
        grid=(t // tb,),
        in_specs=[cols(CB_HQ), pl.BlockSpec((tb, w), lambda i: (i, 0)), cols(CB_HI),
                  pl.BlockSpec((1, w), lambda i: (0, 0))],
        out_specs=pl.BlockSpec((tb, w), lambda i: (i, 0)),
        out_shape=jax.ShapeDtypeStruct((t, w), BF16),
        scratch_shapes=[pltpu.VMEM((N_HEADS, HEAD_DIM, HEAD_DIM), F32)],
        compiler_params=_cparams(("arbitrary",)),
        name="hgrn2",
    )(proj, hf, proj, lower)


def _ret_kernel(q_ref, k_ref, v_ref, cos_ref, sin_ref,
                dec_ref, qd_ref, kd_ref, cd_ref, o_ref, s_ref):
    @pl.when(pl.program_id(0) == 0)
    def _():
        s_ref[...] = jnp.zeros_like(s_ref)

    c = dec_ref.shape[1]
    scale = HEAD_DIM ** -0.5
    for ci in range(q_ref.shape[0] // c):
        rows = slice(ci * c, (ci + 1) * c)
        cos = cos_ref[rows, :]
        sin = sin_ref[rows, :]
        for h in range(N_HEADS):
            lanes = slice(h * HEAD_DIM, (h + 1) * HEAD_DIM)
            q = q_ref[rows, lanes].astype(F32)
            k = k_ref[rows, lanes].astype(F32)
            q = q * cos + pltpu.roll(q, HEAD_DIM // 2, 1) * sin
            k = (k * cos + pltpu.roll(k, HEAD_DIM // 2, 1) * sin) * scale
            v = v_ref[rows, lanes]
            qb = q.astype(BF16)
            s = s_ref[h]
            scores = _dot_nt(qb, k.astype(BF16)) * dec_ref[h]
            o = _dot(qb, s.astype(BF16)) * qd_ref[h] + _dot(scores.astype(BF16), v)
            o_ref[rows, lanes] = o.astype(BF16)
            s_ref[h] = cd_ref[h] * s + _dot_tn((k * kd_ref[h]).astype(BF16), v)


def _ret(proj, cos_r, sin_r):
    t = proj.shape[0]
    c = min(t, RET_CHUNK)
    tb = 4 * c if t % (4 * c) == 0 else c
    w = N_HEADS * HEAD_DIM
    log_gamma = jnp.log(1.0 - 2.0 ** (-5.0 - jnp.arange(N_HEADS, dtype=F32)))
    idx = jnp.arange(c, dtype=F32)
    rel = idx[:, None] - idx[None, :]
    decay = jnp.exp(jnp.where(rel >= 0, log_gamma[:, None, None] * rel, NEG_BIG))
    ones = jnp.ones((1, 1, HEAD_DIM), F32)
    q_decay = jnp.exp(log_gamma[:, None] * (idx + 1.0))[:, :, None] * ones
    k_decay = jnp.exp(log_gamma[:, None] * (c - 1.0 - idx))[:, :, None] * ones
    c_decay = jnp.exp(log_gamma * c)[:, None, None] * ones

    def cols(cb):
        return pl.BlockSpec((tb, w), lambda i: (i, cb // N_HEADS))

    def const(shape):
        return pl.BlockSpec(shape, lambda i: (0, 0, 0))

    tab = pl.BlockSpec((tb, LANE), lambda i: (i, 0))
    return pl.pallas_call(
        _ret_kernel,
        grid=(t // tb,),
        in_specs=[cols(CB_RQ), cols(CB_RK), cols(CB_RV), tab, tab,
                  const((N_HEADS, c, c)), const((N_HEADS, c, HEAD_DIM)),
                  const((N_HEADS, c, HEAD_DIM)), const((N_HEADS, 1, HEAD_DIM))],
        out_specs=pl.BlockSpec((tb, w), lambda i: (i, 0)),
        out_shape=jax.ShapeDtypeStruct((t, w), BF16),
        scratch_shapes=[pltpu.VMEM((N_HEADS, HEAD_DIM, HEAD_DIM), F32)],
        compiler_params=_cparams(("arbitrary",)),
        name="retention",
    )(proj, proj, proj, cos_r, sin_r, decay, q_decay, k_decay, c_decay)


def _swa_kernel(sink_ref, q_ref, qr_ref, kc_ref, krc_ref, vac_ref, vbc_ref,
                kp_ref, krp_ref, vap_ref, vbp_ref, cc_ref, sc_ref, cp_ref, sp_ref, o_ref):
    b = SWA_BLOCK
    nb = q_ref.shape[0] // b
    first = pl.program_id(0) * nb
    cos_c, sin_c = cc_ref[...], sc_ref[...]
    cos_p, sin_p = cp_ref[...], sp_ref[...]
    qi = lax.broadcasted_iota(jnp.int32, (b, 2 * b), 0) + b
    ki = lax.broadcasted_iota(jnp.int32, (b, 2 * b), 1)
    rel = qi - ki
    in_window = (rel >= 0) & (rel < b)
    low = lax.broadcasted_iota(jnp.int32, (b, LANE), 1) < SWA_DIM
    pairs_per_group = SWA_Q_HEADS // SWA_KV_HEADS // 2
    for g in range(SWA_KV_HEADS):
        kl = slice(g * LANE, (g + 1) * LANE)
        k_cur = kc_ref[:, kl].astype(F32) * cos_c + krc_ref[:, kl].astype(F32) * sin_c
        k_prev = kp_ref[:, kl].astype(F32) * cos_p + krp_ref[:, kl].astype(F32) * sin_p
        k_all = jnp.concatenate([k_prev, k_cur], axis=0).astype(BF16)
        v_lo = jnp.concatenate([vap_ref[:, kl], vac_ref[:, kl]], axis=0)
        v_hi = jnp.concatenate([vbp_ref[:, kl], vbc_ref[:, kl]], axis=0)
        for sb in range(nb):
            win = slice(sb * b, (sb + 2) * b)
            rows = slice(sb * b, (sb + 1) * b)
            keep = in_window & ((first + sb > 0) | (ki >= b))
            kw = k_all[win]
            for j in range(pairs_per_group):
                pair = g * pairs_per_group + j
                ql = slice(pair * LANE, (pair + 1) * LANE)
                q = (q_ref[rows, ql].astype(F32) * cos_c[rows] + qr_ref[rows, ql].astype(F32) * sin_c[rows])
                out = None
                for half, vw in ((0, v_lo[win]), (1, v_hi[win])):
                    qh = jnp.where(low if half == 0 else ~low, q, 0.0).astype(BF16)
                    s = _dot_nt(qh, kw) * (SWA_DIM ** -0.5)
                    s = jnp.where(keep, s, NEG_BIG)
                    sink = sink_ref[2 * pair + half]
                    m = jnp.maximum(jnp.max(s, axis=-1, keepdims=True), sink)
                    p = jnp.exp(s - m)
                    denom = jnp.sum(p, axis=-1, keepdims=True) + jnp.exp(sink - m)
                    o = _dot(p.astype(BF16), vw) / denom
                    out = o if out is None else out + o
                o_ref[rows, ql] = out.astype(BF16)


def _swa(proj, cos_s, sin_s, sinks):
    t = proj.shape[0]
    b = SWA_BLOCK
    nb = 4 if t % (4 * b) == 0 else 1
    qw = SWA_Q_HEADS * SWA_DIM
    kw = SWA_KV_HEADS * LANE

    def cur(cb, width):
        return pl.BlockSpec((nb * b, width), lambda i: (i, cb * LANE // width))

    def prev(cb, width):
        return pl.BlockSpec((b, width), lambda i: (jnp.maximum(nb * i - 1, 0), cb * LANE // width))

    tab_c = pl.BlockSpec((nb * b, LANE), lambda i: (i, 0))
    tab_p = pl.BlockSpec((b, LANE), lambda i: (jnp.maximum(nb * i - 1, 0), 0))
    return pl.pallas_call(
        _swa_kernel,
        grid=(t // (nb * b),),
        in_specs=[pl.BlockSpec(memory_space=pltpu.SMEM),
                  cur(CB_SQ, qw), cur(CB_SQR, qw),
                  cur(CB_SK, kw), cur(CB_SKR, kw), cur(CB_SVA, kw), cur(CB_SVB, kw),
                  prev(CB_SK, kw), prev(CB_SKR, kw), prev(CB_SVA, kw), prev(CB_SVB, kw),
                  tab_c, tab_c, tab_p, tab_p],
        out_specs=pl.BlockSpec((nb * b, qw), lambda i: (i, 0)),
        out_shape=jax.ShapeDtypeStruct((t, qw), BF16),
        compiler_params=_cparams(("parallel",)),
        name="swa",
    )(sinks, proj, proj, proj, proj, proj, proj, proj, proj, proj, proj, cos_s, sin_s, cos_s, sin_s)


def _merge_kernel(x_ref, oa_ref, hg_ref, ob_ref, rg_ref, oc_ref, ga_ref, gb_ref, gc_ref,
                  hn_ref, wa_ref, wb_ref, wc_ref, wo_ref, o_ref):
    hg = hg_ref[...].astype(F32)
    a = _rms(oa_ref[...].astype(F32)) * hn_ref[...] * (hg * _sigmoid(hg))
    ya = _dot(a.astype(BF16), wa_ref[...])
    rg = rg_ref[...].astype(F32)
    ob = ob_ref[...].astype(F32)
    parts = []
    for h in range(N_HEADS):
        lanes = slice(h * HEAD_DIM, (h + 1) * HEAD_DIM)
        parts.append(_rms(ob[:, lanes]))
    bn = jnp.concatenate(parts, axis=-1) * (rg * _sigmoid(rg))
    yb = _dot(bn.astype(BF16), wb_ref[...])
    yc = _dot(oc_ref[...], wc_ref[...])
    mixed = (_sigmoid(ga_ref[...].astype(F32)) * ya + _sigmoid(gb_ref[...].astype(F32)) * yb
             + _sigmoid(gc_ref[...].astype(F32)) * yc)
    o_ref[...] = x_ref[...] + _dot(mixed.astype(BF16), wo_ref[...])


def _merge(x, proj, oa, ob, oc, hn, wa, wb, wc, wo):
    t = x.shape[0]
    tt = min(t, 512)
    w = N_HEADS * HEAD_DIM
    qw = SWA_Q_HEADS * SWA_DIM

    def rows(width, cb=0):
        return pl.BlockSpec((tt, width), lambda i: (i, cb * LANE // width))

    def full(shape):
        return pl.BlockSpec(shape, lambda i: (0, 0))

    return pl.pallas_call(
        _merge_kernel,
        grid=(t // tt,),
        in_specs=[rows(D_MODEL), rows(w), rows(w, CB_HG), rows(w), rows(w, CB_RG), rows(qw),
                  rows(D_MODEL, CB_GATE), rows(D_MODEL, CB_GATE + 8), rows(D_MODEL, CB_GATE + 16),
                  full((1, w)), full((w, D_MODEL)), full((w, D_MODEL)), full((qw, D_MODEL)),
                  full((D_MODEL, D_MODEL))],
        out_specs=rows(D_MODEL),
        out_shape=jax.ShapeDtypeStruct((t, D_MODEL), F32),
        compiler_params=_cparams(("parallel",)),
        name="merge",
    )(x, oa, proj, ob, proj, oc, proj, proj, proj, hn, wa, wb, wc, wo)


def _memkv_kernel(m_ref, g_ref, wk_ref, wv_ref, k_ref, v_ref):
    h = (_rms(m_ref[...]) * g_ref[...]).astype(BF16)
    k_ref[...] = _dot(h, wk_ref[...]).astype(BF16)
    v_ref[...] = _dot(h, wv_ref[...]).astype(BF16)


def _memkv(mem, g, wk, wv):
    nm = mem.shape[0]
    out = jax.ShapeDtypeStruct((nm, D_MODEL), BF16)
    return pl.pallas_call(
        _memkv_kernel,
        out_shape=[out, out],
        compiler_params=pltpu.CompilerParams(vmem_limit_bytes=VMEM_LIMIT),
        name="mem_kv",
    )(mem, g, wk, wv)


def _xattn_kernel(x_ref, g_ref, wq_ref, k_ref, v_ref, wo_ref, o_ref):
    x = x_ref[...]
    h = (_rms(x) * g_ref[...]).astype(BF16)
    q = _dot(h, wq_ref[...])
    outs = []
    for hh in range(X_HEADS):
        lanes = slice(hh * X_DIM, (hh + 1) * X_DIM)
        s = _dot_nt(q[:, lanes].astype(BF16), k_ref[:, lanes]) * (X_DIM ** -0.5)
        m = jnp.max(s, axis=-1, keepdims=True)
        p = jnp.exp(s - m)
        p = p / jnp.sum(p, axis=-1, keepdims=True)
        outs.append(_dot(p.astype(BF16), v_ref[:, lanes]))
    o = jnp.concatenate(outs, axis=-1)
    o_ref[...] = x + _dot(o.astype(BF16), wo_ref[...])


def _xattn(x, g, wq, k, v, wo):
    t = x.shape[0]
    tt = min(t, 512)
    nm = k.shape[0]

    def full(shape):
        return pl.BlockSpec(shape, lambda i: (0, 0))

    rows = pl.BlockSpec((tt, D_MODEL), lambda i: (i, 0))
    return pl.pallas_call(
        _xattn_kernel,
        grid=(t // tt,),
        in_specs=[rows, full((1, D_MODEL)), full((D_MODEL, D_MODEL)),
                  full((nm, D_MODEL)), full((nm, D_MODEL)), full((D_MODEL, D_MODEL))],
        out_specs=rows,
        out_shape=jax.ShapeDtypeStruct((t, D_MODEL), F32),
        compiler_params=_cparams(("parallel",)),
        name="xattn",
    )(x, g, wq, k, v, wo)


_CAND_ROWS = [PEER_TOPK // (k1 + 1) for k1 in range(8)]
_RANK_BASE = 3.0e38
_RANK_STEP = 3.0e33


def _top_values(s, n, with_rank=False):
    vals = []
    for i in range(n):
        m = jnp.max(s, axis=0, keepdims=True)
        vals.append(m)
        s = jnp.where(s >= m, -(_RANK_BASE + i * _RANK_STEP), s)
    if not with_rank:
        return vals
    rank = jnp.where(s <= -_RANK_BASE, jnp.floor((-s - _RANK_BASE) * (1.0 / _RANK_STEP) + 0.5), float(n))
    return vals, rank


def _bf16_pair_bits(x):
    u = lax.bitcast_convert_type(x.astype(BF16).astype(F32), jnp.uint32)
    return u | (u >> 16)


def _stack_rows(rows, n):
    tt = rows[0].shape[1]
    r = lax.broadcasted_iota(jnp.int32, (n, tt), 0)
    out = jnp.zeros((n, tt), F32)
    for k in range(n):
        out = jnp.where(r == k, rows[k], out)
    return out


def _router_kernel(x_ref, g_ref, wq_ref, kh_ref, kl_ref,
                   hb_ref, a_ref, b_ref, r_ref, l_ref, qt_scr):
    h = _rms(x_ref[...]) * g_ref[...]
    hb_ref[...] = h.T.astype(BF16)
    qt_scr[...] = _dot_nt(wq_ref[...], h.astype(BF16))
    tt = x_ref.shape[0]
    k = PEER_TOPK
    r8 = lax.broadcasted_iota(jnp.int32, (8, tt), 0)

    def head_body(hh, carry):
        scores = []
        for p in range(2):
            r0 = pl.multiple_of(hh * (2 * PEER_HALF) + p * PEER_HALF, PEER_HALF)
            q_hi, q_lo = _split2(qt_scr[pl.ds(r0, PEER_HALF), :])
            kh = kh_ref[p * PEER_HEADS + hh]
            scores.append(_dot(kh, q_hi) + _dot(kl_ref[p * PEER_HEADS + hh], q_hi) + _dot(kh, q_lo))
        s1, s2 = scores
        v1, rank1 = _top_values(s1, k, with_rank=True)
        v2, rank2 = _top_values(s2, k, with_rank=True)
        sv1 = _stack_rows(v1, k)
        sv2 = _stack_rows(v2, k)
        groups = [v1[0] + sv2]
        for k1 in range(1, 8):
            groups.append(jnp.where(r8 < _CAND_ROWS[k1], v1[k1] + sv2[0:8], -jnp.inf))
        groups.append(sv1[8:16] + v2[0])
        cand = jnp.concatenate(groups, axis=0)
        tau = _top_values(cand, k)[k - 1]
        m0 = v1[0] + v2[0]
        z = jnp.sum(jnp.where(cand >= tau, jnp.exp(cand - m0), 0.0), axis=0, keepdims=True)
        a_ref[hh] = _bf16_pair_bits(jnp.exp(s1 - v1[0]))
        b_ref[hh] = (jnp.exp(s2 - v2[0]) / z).astype(BF16)
        count1 = jnp.zeros_like(s1)
        for k1 in range(k):
            n_k1 = jnp.sum((v1[k1] + sv2 >= tau).astype(F32), axis=0, keepdims=True)
            count1 = jnp.where(rank1 == float(k1), n_k1, count1)
        r_ref[hh] = rank2.astype(BF16)
        l_ref[hh] = _bf16_pair_bits(count1)
        return carry

    lax.fori_loop(0, PEER_HEADS, head_body, 0, unroll=2)


def _router(x, g, wq_t, k_hi, k_lo):
    t = x.shape[0]
    tt = min(t, 512)
    nq = wq_t.shape[0]
    sel = jax.ShapeDtypeStruct((PEER_HEADS, PEER_KEYS, t), jnp.uint32)
    sel_b = jax.ShapeDtypeStruct((PEER_HEADS, PEER_KEYS, t), BF16)
    sel_spec = pl.BlockSpec((PEER_HEADS, PEER_KEYS, tt), lambda i: (0, 0, i))
    return pl.pallas_call(
        _router_kernel,
        grid=(t // tt,),
        in_specs=[pl.BlockSpec((tt, D_MODEL), lambda i: (i, 0)),
                  pl.BlockSpec((1, D_MODEL), lambda i: (0, 0)),
                  pl.BlockSpec((nq, D_MODEL), lambda i: (0, 0)),
                  pl.BlockSpec((2 * PEER_HEADS, PEER_KEYS, PEER_HALF), lambda i: (0, 0, 0)),
                  pl.BlockSpec((2 * PEER_HEADS, PEER_KEYS, PEER_HALF), lambda i: (0, 0, 0))],
        out_specs=[pl.BlockSpec((D_MODEL, tt), lambda i: (0, i)), sel_spec, sel_spec, sel_spec, sel_spec],
        out_shape=[jax.ShapeDtypeStruct((D_MODEL, t), BF16), sel, sel_b, sel_b, sel],
        scratch_shapes=[pltpu.VMEM((nq, tt), F32)],
        compiler_params=_cparams(("parallel",)),
        name="peer_router",
    )(x, g, wq_t, k_hi, k_lo)


def _peer_kernel(x_ref, hb_ref, a_ref, l_ref, b_ref, r_ref, u_ref, vt_ref, gf_ref, o_ref,
                 acc_ref, ga_ref, *, final_norm):
    e = pl.program_id(1)

    @pl.when(e == 0)
    def _():
        acc_ref[...] = jnp.zeros_like(acc_ref)

    u_blk = pltpu.bitcast(u_ref[...], BF16)
    vt_blk = pltpu.bitcast(vt_ref[...], BF16)
    n_i1 = u_blk.shape[0] // PEER_KEYS
    sub = 16
    tt = hb_ref.shape[1]
    shape3 = (PEER_KEYS // sub, sub, tt)
    zero3 = jnp.zeros(shape3, BF16)
    piece = PEER_PIECE_KEYS * PEER_KEYS
    for p in range(n_i1 // PEER_PIECE_KEYS):
        for j in range(p * PEER_PIECE_KEYS, (p + 1) * PEER_PIECE_KEYS):
            gate = zero3
            for hh in range(PEER_HEADS):
                arow = pltpu.bitcast(jnp.broadcast_to(a_ref[hh, j:j + 1, :], (8, tt)), BF16)
                lrow = pltpu.bitcast(jnp.broadcast_to(l_ref[hh, j:j + 1, :], (8, tt)), BF16)
                b3 = b_ref[hh].reshape(shape3)
                r3 = r_ref[hh].reshape(shape3)
                gate = gate + arow[None] * jnp.where(r3 < lrow[None], b3, zero3)
            ga_ref[j * PEER_KEYS:(j + 1) * PEER_KEYS, :] = gate.reshape(PEER_KEYS, tt)
        rows = slice(p * piece, (p + 1) * piece)
        pre = _dot(u_blk[rows, :], hb_ref[...])
        act = pre + pre * lax.erf(pre)
        ga_ref[rows, :] = ga_ref[rows, :] * act.astype(BF16)
    acc_ref[...] += _dot(vt_blk, ga_ref[...])

    @pl.when(e == pl.num_programs(1) - 1)
    def _():
        y = x_ref[...] + acc_ref[...].T
        if final_norm:
            y = _rms(y) * gf_ref[...]
        o_ref[...] = y


def _pack_kernel(w_ref, o_ref, *, scale, transpose):
    w = w_ref[...]
    if transpose:
        w = w.T
    o_ref[...] = pltpu.bitcast((w * scale).astype(BF16), jnp.uint32)


def _pack_table(w, scale, transpose):
    rows, cols = w.shape
    if transpose:
        blk = 512
        in_spec = pl.BlockSpec((blk, cols), lambda i: (i, 0))
        out_spec = pl.BlockSpec((cols // 2, blk), lambda i: (0, i))
        out_shape = jax.ShapeDtypeStruct((cols // 2, rows), jnp.uint32)
    else:
        blk = 1024
        in_spec = pl.BlockSpec((blk, cols), lambda i: (i, 0))
        out_spec = pl.BlockSpec((blk // 2, cols), lambda i: (i, 0))
        out_shape = jax.ShapeDtypeStruct((rows // 2, cols), jnp.uint32)
    return pl.pallas_call(
        functools.partial(_pack_kernel, scale=scale, transpose=transpose),
        grid=(rows // blk,),
        in_specs=[in_spec],
        out_specs=out_spec,
        out_shape=out_shape,
        compiler_params=_cparams(("parallel",)),
        name="pack_table_t" if transpose else "pack_table",
    )(w)


def _peer(x, hb, a, l, b, r, u, vt, gf, final_norm):
    t = x.shape[0]
    tt = min(t, 512)
    eb = 16 * PEER_KEYS
    rows = pl.BlockSpec((tt, D_MODEL), lambda i, e: (i, 0))
    i1_rows = pl.BlockSpec((PEER_HEADS, eb // PEER_KEYS, tt), lambda i, e: (0, e, i))
    all_rows = pl.BlockSpec((PEER_HEADS, PEER_KEYS, tt), lambda i, e: (0, 0, i))
    return pl.pallas_call(
        functools.partial(_peer_kernel, final_norm=final_norm),
        grid=(t // tt, N_EXPERTS // eb),
        in_specs=[rows, pl.BlockSpec((D_MODEL, tt), lambda i, e: (0, i)),
                  i1_rows, i1_rows, all_rows, all_rows,
                  pl.BlockSpec((eb // 2, D_MODEL), lambda i, e: (e, 0)),
                  pl.BlockSpec((D_MODEL // 2, eb), lambda i, e: (0, e)),
                  pl.BlockSpec((1, D_MODEL), lambda i, e: (0, 0))],
        out_specs=rows,
        out_shape=jax.ShapeDtypeStruct((t, D_MODEL), F32),
        scratch_shapes=[pltpu.VMEM((D_MODEL, tt), F32), pltpu.VMEM((eb, tt), BF16)],
        compiler_params=_cparams(("parallel", "arbitrary")),
        name="peer_dense",
    )(x, hb, a, l, b, r, u, vt, gf)


def kernel(x, mem, positions, ln_mix, w_in, lb_param, hgrn_norm, swa_sinks, w_br_hgrn, w_br_ret,
           w_br_swa, w_out, ln_xq, ln_xkv, w_xq, w_xk, w_xv, w_xo, ln_ffn, peer_wq, peer_keys,
           peer_u, peer_v, ln_final):
    b, t, d = x.shape
    assert b == 1 and d == D_MODEL
    depth = w_in.shape[0]
    xs = x.reshape(t, d)
    ms = mem.reshape(mem.shape[1], d)
    cos_r, sin_r, cos_s, sin_s = _rope_tables(positions)
    lb_sm = jax.nn.softmax(lb_param.astype(F32), axis=0)
    lower = jnp.cumsum(lb_sm, axis=0) - lb_sm[0]

    for l in range(depth):
        proj, hf = _proj(xs, ln_mix[l][None, :], _relay_w_in(w_in[l]))
        oa = _hgrn(proj, hf, lower[l][None, :])
        ob = _ret(proj, cos_r, sin_r)
        oc = _swa(proj, cos_s, sin_s, swa_sinks[l])
        xs = _merge(xs, proj, oa, ob, oc, hgrn_norm[l][None, :], w_br_hgrn[l].astype(BF16),
                    w_br_ret[l].astype(BF16), w_br_swa[l].astype(BF16), w_out[l].astype(BF16))

        km, vm = _memkv(ms, ln_xkv[l][None, :], w_xk[l].astype(BF16), w_xv[l].astype(BF16))
        xs = _xattn(xs, ln_xq[l][None, :], w_xq[l].astype(BF16), km, vm, w_xo[l].astype(BF16))

        k_hi, k_lo = _split2(peer_keys[l].reshape(2 * PEER_HEADS, PEER_KEYS, PEER_HALF))
        hb, a, bz, r2, cnt = _router(xs, ln_ffn[l][None, :], peer_wq[l].T.astype(BF16), k_hi, k_lo)
        c = 2.0 ** -0.5
        xs = _peer(xs, hb, a, cnt, bz, r2, _pack_table(peer_u[l], c, False), _pack_table(peer_v[l], c, True),
                   ln_final[None, :], final_norm=(l == depth - 1))

    return xs.reshape(b, t, d)
```

```python
import functools
import math

import numpy as np
import jax
import jax.numpy as jnp
from jax import lax
from jax.experimental import pallas as pl
from jax.experimental.pallas import tpu as pltpu

F32 = jnp.float32
BF16 = jnp.bfloat16

D_MODEL = 1024
HEAD_DIM = 128
N_HEADS = 4
SWA_Q_HEADS = 8
SWA_KV_HEADS = 2
SWA_DIM = 64
SWA_BLOCK = 128
ROPE_THETA = 10000.0
N_MEM = 256
X_HEADS = 4
X_DIM = D_MODEL // X_HEADS
PEER_HEADS = 8
PEER_KEYS = 128
PEER_TOPK = 16
PEER_HALF = 128
N_EXPERTS = PEER_KEYS * PEER_KEYS
EPS = 1e-6
NEG_BIG = -1e30
TINY = 1e-30

LANE = 128
HG_CHUNK = 64
RET_CHUNK = 256
PEER_PIECE_KEYS = 4
VMEM_LIMIT = 56 * 1024 * 1024

CB_GATE = 0
CB_HQ, CB_HF, CB_HI, CB_HG = 24, 28, 32, 36
CB_RQ, CB_RK, CB_RV, CB_RG = 40, 44, 48, 52
CB_SQ, CB_SQR, CB_SK, CB_SKR, CB_SVA, CB_SVB = 56, 60, 64, 66, 68, 70
N_CB = 72
PROJ_COLS = N_CB * LANE


def _cparams(sem):
    return pltpu.CompilerParams(dimension_semantics=sem, vmem_limit_bytes=VMEM_LIMIT)


def _dot(a, b):
    return jnp.dot(a, b, preferred_element_type=F32)


def _dot_nt(a, b):
    return lax.dot_general(a, b, (((1,), (1,)), ((), ())), preferred_element_type=F32)


def _dot_tn(a, b):
    return lax.dot_general(a, b, (((0,), (0,)), ((), ())), preferred_element_type=F32)


def _split2(x):
    hi = x.astype(BF16)
    lo = (x - hi.astype(F32)).astype(BF16)
    return hi, lo


def _split3(x):
    hi = x.astype(BF16)
    r = x - hi.astype(F32)
    mid = r.astype(BF16)
    lo = (r - mid.astype(F32)).astype(BF16)
    return hi, mid, lo


def _sigmoid(x):
    return 1.0 / (1.0 + jnp.exp(-x))


def _rms(x):
    return x * lax.rsqrt(jnp.mean(x * x, axis=-1, keepdims=True) + EPS)


def _rope_kernel(pos_ref, fr_ref, fs_ref, cr_ref, sr_ref, cs_ref, ss_ref):
    pos = pos_ref[...].astype(F32)
    lane = lax.broadcasted_iota(jnp.int32, (1, LANE), 1)
    ang_r = pos * fr_ref[...]
    cr_ref[...] = jnp.cos(ang_r)
    sr_ref[...] = jnp.sin(ang_r) * jnp.where(lane < HEAD_DIM // 2, -1.0, 1.0)
    ang_s = pos * fs_ref[...]
    cs_ref[...] = jnp.cos(ang_s)
    ss_ref[...] = jnp.sin(ang_s)


def _rope_tables(positions):
    t = positions.shape[1]
    tt = min(t, 1024)
    ret_f = ROPE_THETA ** (-jnp.linspace(0.0, 1.0, HEAD_DIM // 2, dtype=F32))
    swa_f = ROPE_THETA ** (-jnp.arange(0, SWA_DIM, 2, dtype=F32) / SWA_DIM)
    fr = jnp.concatenate([ret_f, ret_f])[None, :]
    fs = jnp.concatenate([swa_f] * (LANE // (SWA_DIM // 2)))[None, :]
    pos = positions.reshape(t, 1)
    tab = jax.ShapeDtypeStruct((t, LANE), F32)
    row = pl.BlockSpec((tt, LANE), lambda i: (i, 0))
    vec = pl.BlockSpec((1, LANE), lambda i: (0, 0))
    return pl.pallas_call(
        _rope_kernel,
        grid=(t // tt,),
        in_specs=[pl.BlockSpec((tt, 1), lambda i: (i, 0)), vec, vec],
        out_specs=[row, row, row, row],
        out_shape=[tab, tab, tab, tab],
        compiler_params=_cparams(("parallel",)),
        name="rope_tables",
    )(pos, fr, fs)


PROJ_TN = 1024


def _proj_kernel(x_ref, g_ref, w_ref, o_ref, f_ref, h_scr):
    @pl.when(pl.program_id(1) == 0)
    def _():
        h_scr[...] = (_rms(x_ref[...]) * g_ref[...]).astype(BF16)

    y = _dot(h_scr[...], w_ref[...])
    o_ref[...] = y.astype(BF16)

    @pl.when(pl.program_id(1) == CB_HF * LANE // PROJ_TN)
    def _():
        lo = CB_HF * LANE % PROJ_TN
        f_ref[...] = y[:, lo:lo + N_HEADS * HEAD_DIM]


def _proj(x, g, w):
    t = x.shape[0]
    n = w.shape[1]
    tm = min(t, 2048)
    tn = PROJ_TN
    wf = N_HEADS * HEAD_DIM
    return pl.pallas_call(
        _proj_kernel,
        grid=(t // tm, n // tn),
        in_specs=[pl.BlockSpec((tm, D_MODEL), lambda i, j: (i, 0)),
                  pl.BlockSpec((1, D_MODEL), lambda i, j: (0, 0)),
                  pl.BlockSpec((D_MODEL, tn), lambda i, j: (0, j))],
        out_specs=[pl.BlockSpec((tm, tn), lambda i, j: (i, j)),
                   pl.BlockSpec((tm, wf), lambda i, j: (i, 0))],
        out_shape=[jax.ShapeDtypeStruct((t, n), BF16), jax.ShapeDtypeStruct((t, wf), F32)],
        scratch_shapes=[pltpu.VMEM((tm, D_MODEL), BF16)],
        compiler_params=_cparams(("parallel", "arbitrary")),
        name="in_proj",
    )(x, g, w)


def _relay_plan():
    names = (["hq"] * 4 + ["hf"] * 4 + ["hi"] * 4 + ["hg"] * 4 + ["rq"] * 4 + ["rk"] * 4 + ["rv"] * 4
             + ["rg"] * 4 + ["sq"] * 4 + ["sk", "sv"] + ["gate"] * 24)
    first = {}
    for i, n in enumerate(names):
        first.setdefault(n, i)
    plan = [(first["gate"] + i, 0, 0) for i in range(24)]
    for n in ("hq", "hf", "hi", "hg", "rq", "rk", "rv", "rg", "sq"):
        plan += [(first[n] + i, 0, 0) for i in range(4)]
    plan += [(first["sq"] + i, 1, 0) for i in range(4)]
    plan += [(first["sk"], 0, 1), (first["sk"], 0, 2)]
    plan += [(first["sk"], 1, 1), (first["sk"], 1, 2)]
    plan += [(first["sv"], 0, 3), (first["sv"], 0, 4)]
    plan += [(first["sv"], 0, 5), (first["sv"], 0, 6)]
    assert len(plan) == N_CB
    return plan


def _relay_kernel(src_ref, rot_ref, put_ref, w_ref, o_ref):
    j = pl.program_id(0)
    x = w_ref[...]
    lane = lax.broadcasted_iota(jnp.int32, x.shape, 1)
    swapped = pltpu.roll(x, SWA_DIM, 1)
    first_half = lane % SWA_DIM < SWA_DIM // 2
    rotated = jnp.where(first_half, -pltpu.roll(x, LANE - SWA_DIM // 2, 1), pltpu.roll(x, SWA_DIM // 2, 1))
    rotated_swapped = pltpu.roll(rotated, SWA_DIM, 1)
    rot = rot_ref[j] == 1
    y = jnp.where(rot, rotated, x)
    ys = jnp.where(rot, rotated_swapped, swapped)
    low = lane < SWA_DIM
    put = put_ref[j]
    out = y
    out = jnp.where(put == 1, jnp.where(low, y, ys), out)
    out = jnp.where(put == 2, jnp.where(low, ys, y), out)
    out = jnp.where(put == 3, jnp.where(low, y, 0.0), out)
    out = jnp.where(put == 4, jnp.where(low, ys, 0.0), out)
    out = jnp.where(put == 5, jnp.where(low, 0.0, ys), out)
    out = jnp.where(put == 6, jnp.where(low, 0.0, y), out)
    o_ref[...] = out.astype(BF16)


def _relay_w_in(w_in, layer):
    d = w_in.shape[1]
    plan = np.asarray(_relay_plan(), np.int32)
    grid_spec = pltpu.PrefetchScalarGridSpec(
        num_scalar_prefetch=3,
        grid=(N_CB,),
        in_specs=[pl.BlockSpec((None, d, LANE), lambda j, src, rot, put: (layer, 0, src[j]))],
        out_specs=pl.BlockSpec((d, LANE), lambda j, src, rot, put: (0, j)),
    )
    return pl.pallas_call(
        _relay_kernel,
        grid_spec=grid_spec,
        out_shape=jax.ShapeDtypeStruct((d, PROJ_COLS), BF16),
        compiler_params=_cparams(("arbitrary",)),
        name="relay_w_in",
    )(jnp.asarray(plan[:, 0]), jnp.asarray(plan[:, 1]), jnp.asarray(plan[:, 2]), w_in)


def _hgrn_kernel(hq_ref, hf_ref, hi_ref, lb_ref, o_ref, st_ref, *, n_chunks):
    @pl.when(pl.program_id(0) == 0)
    def _():
        st_ref[...] = jnp.zeros_like(st_ref)

    c = HG_CHUNK
    row = lax.broadcasted_iota(jnp.int32, (c, c), 0)
    col = lax.broadcasted_iota(jnp.int32, (c, c), 1)
    diag = row == col
    halves = [c >> (i + 1) for i in range(c.bit_length() - 1)]
    pair_mask = {b: (row // (2 * b) == col // (2 * b)) & (row % (2 * b) >= b) & (col % (2 * b) < b)
                 for b in halves}
    rw = lax.broadcasted_iota(jnp.int32, (c, HEAD_DIM), 0)
    r3 = lax.broadcasted_iota(jnp.int32, (c // 8, 8, HEAD_DIM), 1)
    scale = HEAD_DIM ** -0.5
    log2e = math.log2(math.e)

    def boundary(cum2, b):
        if b >= 8:
            ends = [cum2[s + b - 1:s + b, :] for s in range(0, c, 2 * b)]
            out = ends[-1]
            for i in range(len(ends) - 2, -1, -1):
                out = jnp.where(rw < (i + 1) * 2 * b, ends[i], out)
            return out
        c3 = cum2.reshape(c // 8, 8, HEAD_DIM)
        if b == 4:
            return jnp.broadcast_to(c3[:, 3:4, :], c3.shape).reshape(c, HEAD_DIM)
        if b == 2:
            lo = jnp.broadcast_to(c3[:, 1:2, :], c3.shape)
            hi = jnp.broadcast_to(c3[:, 5:6, :], c3.shape)
            return jnp.where(r3 < 4, lo, hi).reshape(c, HEAD_DIM)
        prev = pltpu.roll(cum2, 1, 0)
        return jnp.where(rw % 2 == 1, prev, cum2)

    def chunk_body(ci, carry):
        r0 = pl.multiple_of(ci * c, c)
        for h in range(N_HEADS):
            lanes = slice(h * HEAD_DIM, (h + 1) * HEAD_DIM)
            xq = hq_ref[pl.ds(r0, c), lanes].astype(F32)
            z = hf_ref[pl.ds(r0, c), lanes]
            v = hi_ref[pl.ds(r0, c), lanes]
            lb = lb_ref[:, lanes]
            omlb = 1.0 - lb
            q = xq * _sigmoid(xq) * scale
            f = lb + omlb * _sigmoid(z)
            logf = jnp.log(jnp.maximum(f, TINY))
            kk = omlb * _sigmoid(-z)
            cum2 = logf * log2e
            step = 1
            while step < c:
                cum2 = cum2 + jnp.where(rw >= step, pltpu.roll(cum2, step, 0), 0.0)
                step *= 2
            scores = jnp.where(diag, jnp.sum(q * kk, axis=-1, keepdims=True), 0.0)
            for b in halves:
                ref = boundary(cum2, b)
                qs = (q * jnp.exp2(jnp.minimum(cum2 - ref, 0.0))).astype(BF16)
                ks = (kk * jnp.exp2(jnp.minimum(ref - cum2, 0.0))).astype(BF16)
                scores = jnp.where(pair_mask[b], _dot_nt(qs, ks), scores)
            st = st_ref[h]
            qe = (q * jnp.exp2(cum2)).astype(BF16)
            o = _dot_nt(qe, st.astype(BF16)) + _dot(scores.astype(BF16), v.astype(BF16))
            o_ref[pl.ds(r0, c), lanes] = o.astype(BF16)
            last = cum2[c - 1:c, :]
            kl = (kk * jnp.exp2(last - cum2)).astype(BF16)
            st_ref[h] = jnp.exp2(last) * st + _dot_tn(v.astype(BF16), kl)
        return carry

    lax.fori_loop(0, n_chunks, chunk_body, 0, unroll=4)


def _hgrn(proj, hf, lower):
    t = proj.shape[0]
    tb = min(t, 512)
    w = N_HEADS * HEAD_DIM

    def cols(cb):
        return pl.BlockSpec((tb, w), lambda i: (i, cb // N_HEADS))

    return pl.pallas_call(
        functools.partial(_hgrn_kernel, n_chunks=tb // HG_CHUNK),
        grid=(t // tb,),
        in_specs=[cols(CB_HQ), pl.BlockSpec((tb, w), lambda i: (i, 0)), cols(CB_HI),
                  pl.BlockSpec((1, w), lambda i: (0, 0))],
        out_specs=pl.BlockSpec((tb, w), lambda i: (i, 0)),
        out_shape=jax.ShapeDtypeStruct((t, w), BF16),
        scratch_shapes=[pltpu.VMEM((N_HEADS, HEAD_DIM, HEAD_DIM), F32)],
        compiler_params=_cparams(("arbitrary",)),
        name="hgrn2",
    )(proj, hf, proj, lower)


def _ret_kernel(q_ref, k_ref, v_ref, cos_ref, sin_ref,
                dec_ref, qd_ref, kd_ref, cd_ref, o_ref, s_ref):
    @pl.when(pl.program_id(0) == 0)
    def _():
        s_ref[...] = jnp.zeros_like(s_ref)

    c = dec_ref.shape[1]
    scale = HEAD_DIM ** -0.5
    for ci in range(q_ref.shape[0] // c):
        rows = slice(ci * c, (ci + 1) * c)
        cos = cos_ref[rows, :]
        sin = sin_ref[rows, :]
        for h in range(N_HEADS):
            lanes = slice(h * HEAD_DIM, (h + 1) * HEAD_DIM)
            q = q_ref[rows, lanes].astype(F32)
            k = k_ref[rows, lanes].astype(F32)
            q = q * cos + pltpu.roll(q, HEAD_DIM // 2, 1) * sin
            k = (k * cos + pltpu.roll(k, HEAD_DIM // 2, 1) * sin) * scale
            v = v_ref[rows, lanes]
            qb = q.astype(BF16)
            s = s_ref[h]
            scores = _dot_nt(qb, k.astype(BF16)) * dec_ref[h]
            o = _dot(qb, s.astype(BF16)) * qd_ref[h] + _dot(scores.astype(BF16), v)
            o_ref[rows, lanes] = o.astype(BF16)
            s_ref[h] = cd_ref[h] * s + _dot_tn((k * kd_ref[h]).astype(BF16), v)


def _ret(proj, cos_r, sin_r):
    t = proj.shape[0]
    c = min(t, RET_CHUNK)
    tb = 4 * c if t % (4 * c) == 0 else c
    w = N_HEADS * HEAD_DIM
    log_gamma = jnp.log(1.0 - 2.0 ** (-5.0 - jnp.arange(N_HEADS, dtype=F32)))
    idx = jnp.arange(c, dtype=F32)
    rel = idx[:, None] - idx[None, :]
    decay = jnp.exp(jnp.where(rel >= 0, log_gamma[:, None, None] * rel, NEG_BIG))
    ones = jnp.ones((1, 1, HEAD_DIM), F32)
    q_decay = jnp.exp(log_gamma[:, None] * (idx + 1.0))[:, :, None] * ones
    k_decay = jnp.exp(log_gamma[:, None] * (c - 1.0 - idx))[:, :, None] * ones
    c_decay = jnp.exp(log_gamma * c)[:, None, None] * ones

    def cols(cb):
        return pl.BlockSpec((tb, w), lambda i: (i, cb // N_HEADS))

    def const(shape):
        return pl.BlockSpec(shape, lambda i: (0, 0, 0))

    tab = pl.BlockSpec((tb, LANE), lambda i: (i, 0))
    return pl.pallas_call(
        _ret_kernel,
        grid=(t // tb,),
        in_specs=[cols(CB_RQ), cols(CB_RK), cols(CB_RV), tab, tab,
                  const((N_HEADS, c, c)), const((N_HEADS, c, HEAD_DIM)),
                  const((N_HEADS, c, HEAD_DIM)), const((N_HEADS, 1, HEAD_DIM))],
        out_specs=pl.BlockSpec((tb, w), lambda i: (i, 0)),
        out_shape=jax.ShapeDtypeStruct((t, w), BF16),
        scratch_shapes=[pltpu.VMEM((N_HEADS, HEAD_DIM, HEAD_DIM), F32)],
        compiler_params=_cparams(("arbitrary",)),
        name="retention",
    )(proj, proj, proj, cos_r, sin_r, decay, q_decay, k_decay, c_decay)


def _swa_kernel(sink_ref, q_ref, qr_ref, kc_ref, krc_ref, vac_ref, vbc_ref,
                kp_ref, krp_ref, vap_ref, vbp_ref, cc_ref, sc_ref, cp_ref, sp_ref, o_ref):
    b = SWA_BLOCK
    nb = q_ref.shape[0] // b
    first = pl.program_id(0) * nb
    cos_c, sin_c = cc_ref[...], sc_ref[...]
    cos_p, sin_p = cp_ref[...], sp_ref[...]
    qi = lax.broadcasted_iota(jnp.int32, (b, 2 * b), 0) + b
    ki = lax.broadcasted_iota(jnp.int32, (b, 2 * b), 1)
    rel = qi - ki
    in_window = (rel >= 0) & (rel < b)
    low = lax.broadcasted_iota(jnp.int32, (b, LANE), 1) < SWA_DIM
    pairs_per_group = SWA_Q_HEADS // SWA_KV_HEADS // 2
    for g in range(SWA_KV_HEADS):
        kl = slice(g * LANE, (g + 1) * LANE)
        k_cur = kc_ref[:, kl].astype(F32) * cos_c + krc_ref[:, kl].astype(F32) * sin_c
        k_prev = kp_ref[:, kl].astype(F32) * cos_p + krp_ref[:, kl].astype(F32) * sin_p
        k_all = jnp.concatenate([k_prev, k_cur], axis=0).astype(BF16)
        v_lo = jnp.concatenate([vap_ref[:, kl], vac_ref[:, kl]], axis=0)
        v_hi = jnp.concatenate([vbp_ref[:, kl], vbc_ref[:, kl]], axis=0)
        for sb in range(nb):
            win = slice(sb * b, (sb + 2) * b)
            rows = slice(sb * b, (sb + 1) * b)
            keep = in_window & ((first + sb > 0) | (ki >= b))
            kw = k_all[win]
            for j in range(pairs_per_group):
                pair = g * pairs_per_group + j
                ql = slice(pair * LANE, (pair + 1) * LANE)
                q = (q_ref[rows, ql].astype(F32) * cos_c[rows] + qr_ref[rows, ql].astype(F32) * sin_c[rows])
                out = None
                for half, vw in ((0, v_lo[win]), (1, v_hi[win])):
                    qh = jnp.where(low if half == 0 else ~low, q, 0.0).astype(BF16)
                    s = _dot_nt(qh, kw) * (SWA_DIM ** -0.5)
                    s = jnp.where(keep, s, NEG_BIG)
                    sink = sink_ref[2 * pair + half]
                    m = jnp.maximum(jnp.max(s, axis=-1, keepdims=True), sink)
                    p = jnp.exp(s - m)
                    denom = jnp.sum(p, axis=-1, keepdims=True) + jnp.exp(sink - m)
                    o = _dot(p.astype(BF16), vw) / denom
                    out = o if out is None else out + o
                o_ref[rows, ql] = out.astype(BF16)


def _swa(proj, cos_s, sin_s, sinks):
    t = proj.shape[0]
    b = SWA_BLOCK
    nb = 4 if t % (4 * b) == 0 else 1
    qw = SWA_Q_HEADS * SWA_DIM
    kw = SWA_KV_HEADS * LANE

    def cur(cb, width):
        return pl.BlockSpec((nb * b, width), lambda i: (i, cb * LANE // width))

    def prev(cb, width):
        return pl.BlockSpec((b, width), lambda i: (jnp.maximum(nb * i - 1, 0), cb * LANE // width))

    tab_c = pl.BlockSpec((nb * b, LANE), lambda i: (i, 0))
    tab_p = pl.BlockSpec((b, LANE), lambda i: (jnp.maximum(nb * i - 1, 0), 0))
    return pl.pallas_call(
        _swa_kernel,
        grid=(t // (nb * b),),
        in_specs=[pl.BlockSpec(memory_space=pltpu.SMEM),
                  cur(CB_SQ, qw), cur(CB_SQR, qw),
                  cur(CB_SK, kw), cur(CB_SKR, kw), cur(CB_SVA, kw), cur(CB_SVB, kw),
                  prev(CB_SK, kw), prev(CB_SKR, kw), prev(CB_SVA, kw), prev(CB_SVB, kw),
                  tab_c, tab_c, tab_p, tab_p],
        out_specs=pl.BlockSpec((nb * b, qw), lambda i: (i, 0)),
        out_shape=jax.ShapeDtypeStruct((t, qw), BF16),
        compiler_params=_cparams(("parallel",)),
        name="swa",
    )(sinks, proj, proj, proj, proj, proj, proj, proj, proj, proj, proj, cos_s, sin_s, cos_s, sin_s)


def _merge_kernel(x_ref, oa_ref, hg_ref, ob_ref, rg_ref, oc_ref, ga_ref, gb_ref, gc_ref,
                  hn_ref, wa_ref, wb_ref, wc_ref, wo_ref, o_ref):
    hg = hg_ref[...].astype(F32)
    a = _rms(oa_ref[...].astype(F32)) * hn_ref[...] * (hg * _sigmoid(hg))
    ya = _dot(a.astype(BF16), wa_ref[...])
    rg = rg_ref[...].astype(F32)
    ob = ob_ref[...].astype(F32)
    parts = []
    for h in range(N_HEADS):
        lanes = slice(h * HEAD_DIM, (h + 1) * HEAD_DIM)
        parts.append(_rms(ob[:, lanes]))
    bn = jnp.concatenate(parts, axis=-1) * (rg * _sigmoid(rg))
    yb = _dot(bn.astype(BF16), wb_ref[...])
    yc = _dot(oc_ref[...], wc_ref[...])
    mixed = (_sigmoid(ga_ref[...].astype(F32)) * ya + _sigmoid(gb_ref[...].astype(F32)) * yb
             + _sigmoid(gc_ref[...].astype(F32)) * yc)
    o_ref[...] = x_ref[...] + _dot(mixed.astype(BF16), wo_ref[...])


def _merge(x, proj, oa, ob, oc, hn, wa, wb, wc, wo):
    t = x.shape[0]
    tt = min(t, 512)
    w = N_HEADS * HEAD_DIM
    qw = SWA_Q_HEADS * SWA_DIM

    def rows(width, cb=0):
        return pl.BlockSpec((tt, width), lambda i: (i, cb * LANE // width))

    def full(shape):
        return pl.BlockSpec(shape, lambda i: (0, 0))

    return pl.pallas_call(
        _merge_kernel,
        grid=(t // tt,),
        in_specs=[rows(D_MODEL), rows(w), rows(w, CB_HG), rows(w), rows(w, CB_RG), rows(qw),
                  rows(D_MODEL, CB_GATE), rows(D_MODEL, CB_GATE + 8), rows(D_MODEL, CB_GATE + 16),
                  full((1, w)), full((w, D_MODEL)), full((w, D_MODEL)), full((qw, D_MODEL)),
                  full((D_MODEL, D_MODEL))],
        out_specs=rows(D_MODEL),
        out_shape=jax.ShapeDtypeStruct((t, D_MODEL), F32),
        compiler_params=_cparams(("parallel",)),
        name="merge",
    )(x, oa, proj, ob, proj, oc, proj, proj, proj, hn, wa, wb, wc, wo)


def _memkv_kernel(m_ref, g_ref, wk_ref, wv_ref, k_ref, v_ref):
    h = (_rms(m_ref[...]) * g_ref[...]).astype(BF16)
    k_ref[...] = _dot(h, wk_ref[...]).astype(BF16)
    v_ref[...] = _dot(h, wv_ref[...]).astype(BF16)


def _memkv(mem, g, wk, wv):
    nm = mem.shape[0]
    out = jax.ShapeDtypeStruct((nm, D_MODEL), BF16)
    return pl.pallas_call(
        _memkv_kernel,
        out_shape=[out, out],
        compiler_params=pltpu.CompilerParams(vmem_limit_bytes=VMEM_LIMIT),
        name="mem_kv",
    )(mem, g, wk, wv)


def _xattn_kernel(x_ref, g_ref, wq_ref, k_ref, v_ref, wo_ref, o_ref):
    x = x_ref[...]
    h = (_rms(x) * g_ref[...]).astype(BF16)
    q = _dot(h, wq_ref[...])
    outs = []
    for hh in range(X_HEADS):
        lanes = slice(hh * X_DIM, (hh + 1) * X_DIM)
        s = _dot_nt(q[:, lanes].astype(BF16), k_ref[:, lanes]) * (X_DIM ** -0.5)
        m = jnp.max(s, axis=-1, keepdims=True)
        p = jnp.exp(s - m)
        p = p / jnp.sum(p, axis=-1, keepdims=True)
        outs.append(_dot(p.astype(BF16), v_ref[:, lanes]))
    o = jnp.concatenate(outs, axis=-1)
    o_ref[...] = x + _dot(o.astype(BF16), wo_ref[...])


def _xattn(x, g, wq, k, v, wo):
    t = x.shape[0]
    tt = min(t, 512)
    nm = k.shape[0]

    def full(shape):
        return pl.BlockSpec(shape, lambda i: (0, 0))

    rows = pl.BlockSpec((tt, D_MODEL), lambda i: (i, 0))
    return pl.pallas_call(
        _xattn_kernel,
        grid=(t // tt,),
        in_specs=[rows, full((1, D_MODEL)), full((D_MODEL, D_MODEL)),
                  full((nm, D_MODEL)), full((nm, D_MODEL)), full((D_MODEL, D_MODEL))],
        out_specs=rows,
        out_shape=jax.ShapeDtypeStruct((t, D_MODEL), F32),
        compiler_params=_cparams(("parallel",)),
        name="xattn",
    )(x, g, wq, k, v, wo)


_CAND_ROWS = [PEER_TOPK // (k1 + 1) for k1 in range(8)]
_RANK_BASE = 3.0e38
_RANK_STEP = 3.0e33


def _top_values(s, n, with_rank=False):
    vals = []
    for i in range(n):
        m = jnp.max(s, axis=0, keepdims=True)
        vals.append(m)
        s = jnp.where(s >= m, -(_RANK_BASE + i * _RANK_STEP), s)
    if not with_rank:
        return vals
    rank = jnp.where(s <= -_RANK_BASE, jnp.floor((-s - _RANK_BASE) * (1.0 / _RANK_STEP) + 0.5), float(n))
    return vals, rank


def _bf16_pair_bits(x):
    u = lax.bitcast_convert_type(x.astype(BF16).astype(F32), jnp.uint32)
    return u | (u >> 16)


def _stack_rows(rows, n):
    tt = rows[0].shape[1]
    r = lax.broadcasted_iota(jnp.int32, (n, tt), 0)
    out = jnp.zeros((n, tt), F32)
    for k in range(n):
        out = jnp.where(r == k, rows[k], out)
    return out


def _router_kernel(x_ref, g_ref, wq_ref, kh_ref, kl_ref,
                   hb_ref, a_ref, b_ref, r_ref, l_ref, qt_scr):
    h = _rms(x_ref[...]) * g_ref[...]
    hb_ref[...] = h.T.astype(BF16)
    qt_scr[...] = _dot_nt(wq_ref[...], h.astype(BF16))
    tt = x_ref.shape[0]
    k = PEER_TOPK
    r8 = lax.broadcasted_iota(jnp.int32, (8, tt), 0)

    def head_body(hh, carry):
        scores = []
        for p in range(2):
            r0 = pl.multiple_of(hh * (2 * PEER_HALF) + p * PEER_HALF, PEER_HALF)
            q_hi, q_lo = _split2(qt_scr[pl.ds(r0, PEER_HALF), :])
            kh = kh_ref[p * PEER_HEADS + hh]
            scores.append(_dot(kh, q_hi) + _dot(kl_ref[p * PEER_HEADS + hh], q_hi) + _dot(kh, q_lo))
        s1, s2 = scores
        v1, rank1 = _top_values(s1, k, with_rank=True)
        v2, rank2 = _top_values(s2, k, with_rank=True)
        sv1 = _stack_rows(v1, k)
        sv2 = _stack_rows(v2, k)
        groups = [v1[0] + sv2]
        for k1 in range(1, 8):
            groups.append(jnp.where(r8 < _CAND_ROWS[k1], v1[k1] + sv2[0:8], -jnp.inf))
        groups.append(sv1[8:16] + v2[0])
        cand = jnp.concatenate(groups, axis=0)
        tau = _top_values(cand, k)[k - 1]
        m0 = v1[0] + v2[0]
        z = jnp.sum(jnp.where(cand >= tau, jnp.exp(cand - m0), 0.0), axis=0, keepdims=True)
        a_ref[hh] = _bf16_pair_bits(jnp.exp(s1 - v1[0]))
        b_ref[hh] = (jnp.exp(s2 - v2[0]) / z).astype(BF16)
        count1 = jnp.zeros_like(s1)
        for k1 in range(k):
            n_k1 = jnp.sum((v1[k1] + sv2 >= tau).astype(F32), axis=0, keepdims=True)
            count1 = jnp.where(rank1 == float(k1), n_k1, count1)
        r_ref[hh] = rank2.astype(BF16)
        l_ref[hh] = _bf16_pair_bits(count1)
        return carry

    lax.fori_loop(0, PEER_HEADS, head_body, 0, unroll=2)


def _router(x, g, wq_t, k_hi, k_lo):
    t = x.shape[0]
    tt = min(t, 512)
    nq = wq_t.shape[0]
    sel = jax.ShapeDtypeStruct((PEER_HEADS, PEER_KEYS, t), jnp.uint32)
    sel_b = jax.ShapeDtypeStruct((PEER_HEADS, PEER_KEYS, t), BF16)
    sel_spec = pl.BlockSpec((PEER_HEADS, PEER_KEYS, tt), lambda i: (0, 0, i))
    return pl.pallas_call(
        _router_kernel,
        grid=(t // tt,),
        in_specs=[pl.BlockSpec((tt, D_MODEL), lambda i: (i, 0)),
                  pl.BlockSpec((1, D_MODEL), lambda i: (0, 0)),
                  pl.BlockSpec((nq, D_MODEL), lambda i: (0, 0)),
                  pl.BlockSpec((2 * PEER_HEADS, PEER_KEYS, PEER_HALF), lambda i: (0, 0, 0)),
                  pl.BlockSpec((2 * PEER_HEADS, PEER_KEYS, PEER_HALF), lambda i: (0, 0, 0))],
        out_specs=[pl.BlockSpec((D_MODEL, tt), lambda i: (0, i)), sel_spec, sel_spec, sel_spec, sel_spec],
        out_shape=[jax.ShapeDtypeStruct((D_MODEL, t), BF16), sel, sel_b, sel_b, sel],
        scratch_shapes=[pltpu.VMEM((nq, tt), F32)],
        compiler_params=_cparams(("parallel",)),
        name="peer_router",
    )(x, g, wq_t, k_hi, k_lo)


def _peer_kernel(x_ref, hb_ref, a_ref, l_ref, b_ref, r_ref, u_ref, vt_ref, gf_ref, o_ref,
                 acc_ref, ga_ref, *, final_norm):
    e = pl.program_id(1)

    @pl.when(e == 0)
    def _():
        acc_ref[...] = jnp.zeros_like(acc_ref)

    u_blk = pltpu.bitcast(u_ref[...], BF16)
    vt_blk = pltpu.bitcast(vt_ref[...], BF16)
    n_i1 = u_blk.shape[0] // PEER_KEYS
    sub = 16
    tt = hb_ref.shape[1]
    shape3 = (PEER_KEYS // sub, sub, tt)
    zero3 = jnp.zeros(shape3, BF16)
    piece = PEER_PIECE_KEYS * PEER_KEYS
    for p in range(n_i1 // PEER_PIECE_KEYS):
        for j in range(p * PEER_PIECE_KEYS, (p + 1) * PEER_PIECE_KEYS):
            gate = zero3
            for hh in range(PEER_HEADS):
                arow = pltpu.bitcast(jnp.broadcast_to(a_ref[hh, j:j + 1, :], (8, tt)), BF16)
                lrow = pltpu.bitcast(jnp.broadcast_to(l_ref[hh, j:j + 1, :], (8, tt)), BF16)
                b3 = b_ref[hh].reshape(shape3)
                r3 = r_ref[hh].reshape(shape3)
                gate = gate + arow[None] * jnp.where(r3 < lrow[None], b3, zero3)
            ga_ref[j * PEER_KEYS:(j + 1) * PEER_KEYS, :] = gate.reshape(PEER_KEYS, tt)
        rows = slice(p * piece, (p + 1) * piece)
        pre = _dot(u_blk[rows, :], hb_ref[...])
        act = pre + pre * lax.erf(pre)
        ga_ref[rows, :] = ga_ref[rows, :] * act.astype(BF16)
    acc_ref[...] += _dot(vt_blk, ga_ref[...])

    @pl.when(e == pl.num_programs(1) - 1)
    def _():
        y = x_ref[...] + acc_ref[...].T
        if final_norm:
            y = _rms(y) * gf_ref[...]
        o_ref[...] = y


def _pack_kernel(w_ref, o_ref, *, scale, transpose):
    w = w_ref[...]
    if transpose:
        w = w.T
    o_ref[...] = pltpu.bitcast((w * scale).astype(BF16), jnp.uint32)


def _pack_table(w_all, layer, scale, transpose):
    _, rows, cols = w_all.shape
    if transpose:
        blk = 512
        in_spec = pl.BlockSpec((None, blk, cols), lambda i: (layer, i, 0))
        out_spec = pl.BlockSpec((cols // 2, blk), lambda i: (0, i))
        out_shape = jax.ShapeDtypeStruct((cols // 2, rows), jnp.uint32)
    else:
        blk = 1024
        in_spec = pl.BlockSpec((None, blk, cols), lambda i: (layer, i, 0))
        out_spec = pl.BlockSpec((blk // 2, cols), lambda i: (i, 0))
        out_shape = jax.ShapeDtypeStruct((rows // 2, cols), jnp.uint32)
    return pl.pallas_call(
        functools.partial(_pack_kernel, scale=scale, transpose=transpose),
        grid=(rows // blk,),
        in_specs=[in_spec],
        out_specs=out_spec,
        out_shape=out_shape,
        compiler_params=_cparams(("parallel",)),
        name="pack_table_t" if transpose else "pack_table",
    )(w_all)


def _peer(x, hb, a, l, b, r, u, vt, gf, final_norm):
    t = x.shape[0]
    tt = min(t, 512)
    eb = 16 * PEER_KEYS
    rows = pl.BlockSpec((tt, D_MODEL), lambda i, e: (i, 0))
    i1_rows = pl.BlockSpec((PEER_HEADS, eb // PEER_KEYS, tt), lambda i, e: (0, e, i))
    all_rows = pl.BlockSpec((PEER_HEADS, PEER_KEYS, tt), lambda i, e: (0, 0, i))
    return pl.pallas_call(
        functools.partial(_peer_kernel, final_norm=final_norm),
        grid=(t // tt, N_EXPERTS // eb),
        in_specs=[rows, pl.BlockSpec((D_MODEL, tt), lambda i, e: (0, i)),
                  i1_rows, i1_rows, all_rows, all_rows,
                  pl.BlockSpec((eb // 2, D_MODEL), lambda i, e: (e, 0)),
                  pl.BlockSpec((D_MODEL // 2, eb), lambda i, e: (0, e)),
                  pl.BlockSpec((1, D_MODEL), lambda i, e: (0, 0))],
        out_specs=rows,
        out_shape=jax.ShapeDtypeStruct((t, D_MODEL), F32),
        scratch_shapes=[pltpu.VMEM((D_MODEL, tt), F32), pltpu.VMEM((eb, tt), BF16)],
        compiler_params=_cparams(("parallel", "arbitrary")),
        name="peer_dense",
    )(x, hb, a, l, b, r, u, vt, gf)


def kernel(x, mem, positions, ln_mix, w_in, lb_param, hgrn_norm, swa_sinks, w_br_hgrn, w_br_ret,
           w_br_swa, w_out, ln_xq, ln_xkv, w_xq, w_xk, w_xv, w_xo, ln_ffn, peer_wq, peer_keys,
           peer_u, peer_v, ln_final):
    b, t, d = x.shape
    assert b == 1 and d == D_MODEL
    depth = w_in.shape[0]
    xs = x.reshape(t, d)
    ms = mem.reshape(mem.shape[1], d)
    cos_r, sin_r, cos_s, sin_s = _rope_tables(positions)
    lb_sm = jax.nn.softmax(lb_param.astype(F32), axis=0)
    lower = jnp.cumsum(lb_sm, axis=0) - lb_sm[0]

    for l in range(depth):
        proj, hf = _proj(xs, ln_mix[l][None, :], _relay_w_in(w_in, l))
        oa = _hgrn(proj, hf, lower[l][None, :])
        ob = _ret(proj, cos_r, sin_r)
        oc = _swa(proj, cos_s, sin_s, swa_sinks[l])
        xs = _merge(xs, proj, oa, ob, oc, hgrn_norm[l][None, :], w_br_hgrn[l].astype(BF16),
                    w_br_ret[l].astype(BF16), w_br_swa[l].astype(BF16), w_out[l].astype(BF16))

        km, vm = _memkv(ms, ln_xkv[l][None, :], w_xk[l].astype(BF16), w_xv[l].astype(BF16))
        xs = _xattn(xs, ln_xq[l][None, :], w_xq[l].astype(BF16), km, vm, w_xo[l].astype(BF16))

        k_hi, k_lo = _split2(peer_keys[l].reshape(2 * PEER_HEADS, PEER_KEYS, PEER_HALF))
        hb, a, bz, r2, cnt = _router(xs, ln_ffn[l][None, :], peer_wq[l].T.astype(BF16), k_hi, k_lo)
        c = 2.0 ** -0.5
        xs = _peer(xs, hb, a, cnt, bz, r2, _pack_table(peer_u, l, c, False), _pack_table(peer_v, l, c, True),
                   ln_final[None, :], final_norm=(l == depth - 1))

    return xs.reshape(b, t, d)
```

```python
import functools
import math

import numpy as np
import jax
import jax.numpy as jnp
from jax import lax
from jax.experimental import pallas as pl
from jax.experimental.pallas import tpu as pltpu

F32 = jnp.float32
BF16 = jnp.bfloat16

D_MODEL = 1024
HEAD_DIM = 128
N_HEADS = 4
SWA_Q_HEADS = 8
SWA_KV_HEADS = 2
SWA_DIM = 64
SWA_BLOCK = 128
ROPE_THETA = 10000.0
X_HEADS = 4
X_DIM = D_MODEL // X_HEADS
PEER_HEADS = 8
PEER_KEYS = 128
PEER_TOPK = 16
PEER_HALF = 128
N_EXPERTS = PEER_KEYS * PEER_KEYS
EPS = 1e-6
NEG_BIG = -1e30
TINY = 1e-30

LANE = 128
HG_CHUNK = 64
RET_CHUNK = 256
PEER_PIECE_KEYS = 4
VMEM_LIMIT = 56 * 1024 * 1024

CB_GATE = 0
CB_HQ, CB_HF, CB_HI, CB_HG = 24, 28, 32, 36
CB_RQ, CB_RK, CB_RV, CB_RG = 40, 44, 48, 52
CB_SQ, CB_SQR, CB_SK, CB_SKR, CB_SVA, CB_SVB = 56, 60, 64, 66, 68, 70
N_CB = 72
PROJ_COLS = N_CB * LANE


def _cparams(sem):
    return pltpu.CompilerParams(dimension_semantics=sem, vmem_limit_bytes=VMEM_LIMIT)


def _dot(a, b):
    return jnp.dot(a, b, preferred_element_type=F32)


def _dot_nt(a, b):
    return lax.dot_general(a, b, (((1,), (1,)), ((), ())), preferred_element_type=F32)


def _dot_tn(a, b):
    return lax.dot_general(a, b, (((0,), (0,)), ((), ())), preferred_element_type=F32)


def _split2(x):
    hi = x.astype(BF16)
    lo = (x - hi.astype(F32)).astype(BF16)
    return hi, lo


def _sigmoid(x):
    return 1.0 / (1.0 + jnp.exp(-x))


def _rms(x):
    return x * lax.rsqrt(jnp.mean(x * x, axis=-1, keepdims=True) + EPS)


def _rope_kernel(pos_ref, f_ref, cr_ref, sr_ref, cs_ref, ss_ref):
    ang = pos_ref[...].astype(F32) * f_ref[...]
    c, s = jnp.cos(ang), jnp.sin(ang)
    lane = lax.broadcasted_iota(jnp.int32, c.shape, 1)
    half, quarter = LANE // 2, LANE // 4

    def retention(x, first):
        return jnp.where(lane < half, first * x, pltpu.roll(x, half, 1))

    def window(x):
        y = pltpu.roll(x, half, 1)
        y = jnp.where(lane < quarter, y, pltpu.roll(y, quarter, 1))
        return jnp.where(lane < half, y, pltpu.roll(y, half, 1))

    cr_ref[...] = retention(c, 1.0)
    sr_ref[...] = retention(s, -1.0)
    cs_ref[...] = window(c)
    ss_ref[...] = window(s)


def _rope_tables(positions):
    t = positions.shape[1]
    tt = min(t, 1024)
    ret_f = ROPE_THETA ** (-jnp.linspace(0.0, 1.0, HEAD_DIM // 2, dtype=F32))
    swa_f = ROPE_THETA ** (-jnp.arange(0, SWA_DIM, 2, dtype=F32) / SWA_DIM)
    freqs = jnp.concatenate([ret_f, swa_f, jnp.zeros((LANE - HEAD_DIM // 2 - SWA_DIM // 2,), F32)])[None, :]
    pos = positions.reshape(t, 1)
    tab = jax.ShapeDtypeStruct((t, LANE), F32)
    row = pl.BlockSpec((tt, LANE), lambda i: (i, 0))
    return pl.pallas_call(
        _rope_kernel,
        grid=(t // tt,),
        in_specs=[pl.BlockSpec((tt, 1), lambda i: (i, 0)), pl.BlockSpec((1, LANE), lambda i: (0, 0))],
        out_specs=[row, row, row, row],
        out_shape=[tab, tab, tab, tab],
        compiler_params=_cparams(("parallel",)),
        name="rope_tables",
    )(pos, freqs)


PROJ_TN = 1024


def _proj_kernel(x_ref, g_ref, w_ref, o_ref, f_ref, h_scr):
    @pl.when(pl.program_id(1) == 0)
    def _():
        h_scr[...] = (_rms(x_ref[...]) * g_ref[...]).astype(BF16)

    y = _dot(h_scr[...], w_ref[...])
    o_ref[...] = y.astype(BF16)

    @pl.when(pl.program_id(1) == CB_HF * LANE // PROJ_TN)
    def _():
        lo = CB_HF * LANE % PROJ_TN
        f_ref[...] = y[:, lo:lo + N_HEADS * HEAD_DIM]


def _proj(x, g, w):
    t = x.shape[0]
    n = w.shape[1]
    tm = min(t, 2048)
    tn = PROJ_TN
    wf = N_HEADS * HEAD_DIM
    return pl.pallas_call(
        _proj_kernel,
        grid=(t // tm, n // tn),
        in_specs=[pl.BlockSpec((tm, D_MODEL), lambda i, j: (i, 0)),
                  pl.BlockSpec((1, D_MODEL), lambda i, j: (0, 0)),
                  pl.BlockSpec((D_MODEL, tn), lambda i, j: (0, j))],
        out_specs=[pl.BlockSpec((tm, tn), lambda i, j: (i, j)),
                   pl.BlockSpec((tm, wf), lambda i, j: (i, 0))],
        out_shape=[jax.ShapeDtypeStruct((t, n), BF16), jax.ShapeDtypeStruct((t, wf), F32)],
        scratch_shapes=[pltpu.VMEM((tm, D_MODEL), BF16)],
        compiler_params=_cparams(("parallel", "arbitrary")),
        name="in_proj",
    )(x, g, w)


def _relay_plan():
    names = (["hq"] * 4 + ["hf"] * 4 + ["hi"] * 4 + ["hg"] * 4 + ["rq"] * 4 + ["rk"] * 4 + ["rv"] * 4
             + ["rg"] * 4 + ["sq"] * 4 + ["sk", "sv"] + ["gate"] * 24)
    first = {}
    for i, n in enumerate(names):
        first.setdefault(n, i)
    plan = [(first["gate"] + i, 0, 0) for i in range(24)]
    for n in ("hq", "hf", "hi", "hg", "rq", "rk", "rv", "rg", "sq"):
        plan += [(first[n] + i, 0, 0) for i in range(4)]
    plan += [(first["sq"] + i, 1, 0) for i in range(4)]
    plan += [(first["sk"], 0, 1), (first["sk"], 0, 2)]
    plan += [(first["sk"], 1, 1), (first["sk"], 1, 2)]
    plan += [(first["sv"], 0, 3), (first["sv"], 0, 4)]
    plan += [(first["sv"], 0, 5), (first["sv"], 0, 6)]
    assert len(plan) == N_CB
    return plan


def _relay_kernel(src_ref, rot_ref, put_ref, w_ref, o_ref):
    j = pl.program_id(0)
    x = w_ref[...]
    lane = lax.broadcasted_iota(jnp.int32, x.shape, 1)
    swapped = pltpu.roll(x, SWA_DIM, 1)
    first_half = lane % SWA_DIM < SWA_DIM // 2
    rotated = jnp.where(first_half, -pltpu.roll(x, LANE - SWA_DIM // 2, 1), pltpu.roll(x, SWA_DIM // 2, 1))
    rotated_swapped = pltpu.roll(rotated, SWA_DIM, 1)
    rot = rot_ref[j] == 1
    y = jnp.where(rot, rotated, x)
    ys = jnp.where(rot, rotated_swapped, swapped)
    low = lane < SWA_DIM
    put = put_ref[j]
    out = y
    out = jnp.where(put == 1, jnp.where(low, y, ys), out)
    out = jnp.where(put == 2, jnp.where(low, ys, y), out)
    out = jnp.where(put == 3, jnp.where(low, y, 0.0), out)
    out = jnp.where(put == 4, jnp.where(low, ys, 0.0), out)
    out = jnp.where(put == 5, jnp.where(low, 0.0, ys), out)
    out = jnp.where(put == 6, jnp.where(low, 0.0, y), out)
    o_ref[...] = out.astype(BF16)


def _relay_w_in(w_in, layer):
    d = w_in.shape[1]
    plan = np.asarray(_relay_plan(), np.int32)
    grid_spec = pltpu.PrefetchScalarGridSpec(
        num_scalar_prefetch=3,
        grid=(N_CB,),
        in_specs=[pl.BlockSpec((None, d, LANE), lambda j, src, rot, put: (layer, 0, src[j]))],
        out_specs=pl.BlockSpec((d, LANE), lambda j, src, rot, put: (0, j)),
    )
    return pl.pallas_call(
        _relay_kernel,
        grid_spec=grid_spec,
        out_shape=jax.ShapeDtypeStruct((d, PROJ_COLS), BF16),
        compiler_params=_cparams(("arbitrary",)),
        name="relay_w_in",
    )(jnp.asarray(plan[:, 0]), jnp.asarray(plan[:, 1]), jnp.asarray(plan[:, 2]), w_in)


def _hgrn_kernel(hq_ref, hf_ref, hi_ref, lb_ref, o_ref, st_ref, *, n_chunks):
    @pl.when(pl.program_id(0) == 0)
    def _():
        st_ref[...] = jnp.zeros_like(st_ref)

    c = HG_CHUNK
    row = lax.broadcasted_iota(jnp.int32, (c, c), 0)
    col = lax.broadcasted_iota(jnp.int32, (c, c), 1)
    diag = row == col
    halves = [c >> (i + 1) for i in range(c.bit_length() - 1)]
    pair_mask = {b: (row // (2 * b) == col // (2 * b)) & (row % (2 * b) >= b) & (col % (2 * b) < b)
                 for b in halves}
    rw = lax.broadcasted_iota(jnp.int32, (c, HEAD_DIM), 0)
    r3 = lax.broadcasted_iota(jnp.int32, (c // 8, 8, HEAD_DIM), 1)
    scale = HEAD_DIM ** -0.5
    log2e = math.log2(math.e)

    def boundary(cum2, b):
        if b >= 8:
            ends = [cum2[s + b - 1:s + b, :] for s in range(0, c, 2 * b)]
            out = ends[-1]
            for i in range(len(ends) - 2, -1, -1):
                out = jnp.where(rw < (i + 1) * 2 * b, ends[i], out)
            return out
        c3 = cum2.reshape(c // 8, 8, HEAD_DIM)
        if b == 4:
            return jnp.broadcast_to(c3[:, 3:4, :], c3.shape).reshape(c, HEAD_DIM)
        if b == 2:
            lo = jnp.broadcast_to(c3[:, 1:2, :], c3.shape)
            hi = jnp.broadcast_to(c3[:, 5:6, :], c3.shape)
            return jnp.where(r3 < 4, lo, hi).reshape(c, HEAD_DIM)
        prev = pltpu.roll(cum2, 1, 0)
        return jnp.where(rw % 2 == 1, prev, cum2)

    def chunk_body(ci, carry):
        r0 = pl.multiple_of(ci * c, c)
        for h in range(N_HEADS):
            lanes = slice(h * HEAD_DIM, (h + 1) * HEAD_DIM)
            xq = hq_ref[pl.ds(r0, c), lanes].astype(F32)
            z = hf_ref[pl.ds(r0, c), lanes]
            v = hi_ref[pl.ds(r0, c), lanes]
            lb = lb_ref[:, lanes]
            omlb = 1.0 - lb
            q = xq * _sigmoid(xq) * scale
            f = lb + omlb * _sigmoid(z)
            logf = jnp.log(jnp.maximum(f, TINY))
            kk = omlb * _sigmoid(-z)
            cum2 = logf * log2e
            step = 1
            while step < c:
                cum2 = cum2 + jnp.where(rw >= step, pltpu.roll(cum2, step, 0), 0.0)
                step *= 2
            scores = jnp.where(diag, jnp.sum(q * kk, axis=-1, keepdims=True), 0.0)
            for b in halves:
                ref = boundary(cum2, b)
                qs = (q * jnp.exp2(jnp.minimum(cum2 - ref, 0.0))).astype(BF16)
                ks = (kk * jnp.exp2(jnp.minimum(ref - cum2, 0.0))).astype(BF16)
                scores = jnp.where(pair_mask[b], _dot_nt(qs, ks), scores)
            st = st_ref[h]
            qe = (q * jnp.exp2(cum2)).astype(BF16)
            o = _dot_nt(qe, st.astype(BF16)) + _dot(scores.astype(BF16), v.astype(BF16))
            o_ref[pl.ds(r0, c), lanes] = o.astype(BF16)
            last = cum2[c - 1:c, :]
            kl = (kk * jnp.exp2(last - cum2)).astype(BF16)
            st_ref[h] = jnp.exp2(last) * st + _dot_tn(v.astype(BF16), kl)
        return carry

    lax.fori_loop(0, n_chunks, chunk_body, 0, unroll=4)


def _hgrn(proj, hf, lower):
    t = proj.shape[0]
    tb = min(t, 512)
    w = N_HEADS * HEAD_DIM

    def cols(cb):
        return pl.BlockSpec((tb, w), lambda i: (i, cb // N_HEADS))

    return pl.pallas_call(
        functools.partial(_hgrn_kernel, n_chunks=tb // HG_CHUNK),
        grid=(t // tb,),
        in_specs=[cols(CB_HQ), pl.BlockSpec((tb, w), lambda i: (i, 0)), cols(CB_HI),
                  pl.BlockSpec((1, w), lambda i: (0, 0))],
        out_specs=pl.BlockSpec((tb, w), lambda i: (i, 0)),
        out_shape=jax.ShapeDtypeStruct((t, w), BF16),
        scratch_shapes=[pltpu.VMEM((N_HEADS, HEAD_DIM, HEAD_DIM), F32)],
        compiler_params=_cparams(("arbitrary",)),
        name="hgrn2",
    )(proj, hf, proj, lower)


def _ret_kernel(q_ref, k_ref, v_ref, cos_ref, sin_ref,
                dec_ref, qd_ref, kd_ref, cd_ref, o_ref, s_ref):
    @pl.when(pl.program_id(0) == 0)
    def _():
        s_ref[...] = jnp.zeros_like(s_ref)

    c = dec_ref.shape[1]
    scale = HEAD_DIM ** -0.5
    for ci in range(q_ref.shape[0] // c):
        rows = slice(ci * c, (ci + 1) * c)
        cos = cos_ref[rows, :]
        sin = sin_ref[rows, :]
        for h in range(N_HEADS):
            lanes = slice(h * HEAD_DIM, (h + 1) * HEAD_DIM)
            q = q_ref[rows, lanes].astype(F32)
            k = k_ref[rows, lanes].astype(F32)
            q = q * cos + pltpu.roll(q, HEAD_DIM // 2, 1) * sin
            k = (k * cos + pltpu.roll(k, HEAD_DIM // 2, 1) * sin) * scale
            v = v_ref[rows, lanes]
            qb = q.astype(BF16)
            s = s_ref[h]
            scores = _dot_nt(qb, k.astype(BF16)) * dec_ref[h]
            o = _dot(qb, s.astype(BF16)) * qd_ref[h] + _dot(scores.astype(BF16), v)
            o_ref[rows, lanes] = o.astype(BF16)
            s_ref[h] = cd_ref[h] * s + _dot_tn((k * kd_ref[h]).astype(BF16), v)


def _ret(proj, cos_r, sin_r):
    t = proj.shape[0]
    c = min(t, RET_CHUNK)
    tb = 4 * c if t % (4 * c) == 0 else c
    w = N_HEADS * HEAD_DIM
    log_gamma = jnp.log(1.0 - 2.0 ** (-5.0 - jnp.arange(N_HEADS, dtype=F32)))
    idx = jnp.arange(c, dtype=F32)
    rel = idx[:, None] - idx[None, :]
    decay = jnp.exp(jnp.where(rel >= 0, log_gamma[:, None, None] * rel, NEG_BIG))
    ones = jnp.ones((1, 1, HEAD_DIM), F32)
    q_decay = jnp.exp(log_gamma[:, None] * (idx + 1.0))[:, :, None] * ones
    k_decay = jnp.exp(log_gamma[:, None] * (c - 1.0 - idx))[:, :, None] * ones
    c_decay = jnp.exp(log_gamma * c)[:, None, None] * ones

    def cols(cb):
        return pl.BlockSpec((tb, w), lambda i: (i, cb // N_HEADS))

    def const(shape):
        return pl.BlockSpec(shape, lambda i: (0, 0, 0))

    tab = pl.BlockSpec((tb, LANE), lambda i: (i, 0))
    return pl.pallas_call(
        _ret_kernel,
        grid=(t // tb,),
        in_specs=[cols(CB_RQ), cols(CB_RK), cols(CB_RV), tab, tab,
                  const((N_HEADS, c, c)), const((N_HEADS, c, HEAD_DIM)),
                  const((N_HEADS, c, HEAD_DIM)), const((N_HEADS, 1, HEAD_DIM))],
        out_specs=pl.BlockSpec((tb, w), lambda i: (i, 0)),
        out_shape=jax.ShapeDtypeStruct((t, w), BF16),
        scratch_shapes=[pltpu.VMEM((N_HEADS, HEAD_DIM, HEAD_DIM), F32)],
        compiler_params=_cparams(("arbitrary",)),
        name="retention",
    )(proj, proj, proj, cos_r, sin_r, decay, q_decay, k_decay, c_decay)


def _swa_kernel(sink_ref, q_ref, qr_ref, kc_ref, krc_ref, vac_ref, vbc_ref,
                kp_ref, krp_ref, vap_ref, vbp_ref, cc_ref, sc_ref, cp_ref, sp_ref, o_ref):
    b = SWA_BLOCK
    nb = q_ref.shape[0] // b
    first = pl.program_id(0) * nb
    cos_c, sin_c = cc_ref[...], sc_ref[...]
    cos_p, sin_p = cp_ref[...], sp_ref[...]
    qi = lax.broadcasted_iota(jnp.int32, (b, 2 * b), 0) + b
    ki = lax.broadcasted_iota(jnp.int32, (b, 2 * b), 1)
    rel = qi - ki
    in_window = (rel >= 0) & (rel < b)
    low = lax.broadcasted_iota(jnp.int32, (b, LANE), 1) < SWA_DIM
    pairs_per_group = SWA_Q_HEADS // SWA_KV_HEADS // 2
    for g in range(SWA_KV_HEADS):
        kl = slice(g * LANE, (g + 1) * LANE)
        k_cur = kc_ref[:, kl].astype(F32) * cos_c + krc_ref[:, kl].astype(F32) * sin_c
        k_prev = kp_ref[:, kl].astype(F32) * cos_p + krp_ref[:, kl].astype(F32) * sin_p
        k_all = jnp.concatenate([k_prev, k_cur], axis=0).astype(BF16)
        v_lo = jnp.concatenate([vap_ref[:, kl], vac_ref[:, kl]], axis=0)
        v_hi = jnp.concatenate([vbp_ref[:, kl], vbc_ref[:, kl]], axis=0)
        for sb in range(nb):
            win = slice(sb * b, (sb + 2) * b)
            rows = slice(sb * b, (sb + 1) * b)
            keep = in_window & ((first + sb > 0) | (ki >= b))
            kw = k_all[win]
            for j in range(pairs_per_group):
                pair = g * pairs_per_group + j
                ql = slice(pair * LANE, (pair + 1) * LANE)
                q = (q_ref[rows, ql].astype(F32) * cos_c[rows] + qr_ref[rows, ql].astype(F32) * sin_c[rows])
                out = None
                for half, vw in ((0, v_lo[win]), (1, v_hi[win])):
                    qh = jnp.where(low if half == 0 else ~low, q, 0.0).astype(BF16)
                    s = _dot_nt(qh, kw) * (SWA_DIM ** -0.5)
                    s = jnp.where(keep, s, NEG_BIG)
                    sink = sink_ref[2 * pair + half]
                    m = jnp.maximum(jnp.max(s, axis=-1, keepdims=True), sink)
                    p = jnp.exp(s - m)
                    denom = jnp.sum(p, axis=-1, keepdims=True) + jnp.exp(sink - m)
                    o = _dot(p.astype(BF16), vw) / denom
                    out = o if out is None else out + o
                o_ref[rows, ql] = out.astype(BF16)


def _swa(proj, cos_s, sin_s, sinks):
    t = proj.shape[0]
    b = SWA_BLOCK
    nb = 4 if t % (4 * b) == 0 else 1
    qw = SWA_Q_HEADS * SWA_DIM
    kw = SWA_KV_HEADS * LANE

    def cur(cb, width):
        return pl.BlockSpec((nb * b, width), lambda i: (i, cb * LANE // width))

    def prev(cb, width):
        return pl.BlockSpec((b, width), lambda i: (jnp.maximum(nb * i - 1, 0), cb * LANE // width))

    tab_c = pl.BlockSpec((nb * b, LANE), lambda i: (i, 0))
    tab_p = pl.BlockSpec((b, LANE), lambda i: (jnp.maximum(nb * i - 1, 0), 0))
    return pl.pallas_call(
        _swa_kernel,
        grid=(t // (nb * b),),
        in_specs=[pl.BlockSpec(memory_space=pltpu.SMEM),
                  cur(CB_SQ, qw), cur(CB_SQR, qw),
                  cur(CB_SK, kw), cur(CB_SKR, kw), cur(CB_SVA, kw), cur(CB_SVB, kw),
                  prev(CB_SK, kw), prev(CB_SKR, kw), prev(CB_SVA, kw), prev(CB_SVB, kw),
                  tab_c, tab_c, tab_p, tab_p],
        out_specs=pl.BlockSpec((nb * b, qw), lambda i: (i, 0)),
        out_shape=jax.ShapeDtypeStruct((t, qw), BF16),
        compiler_params=_cparams(("parallel",)),
        name="swa",
    )(sinks, proj, proj, proj, proj, proj, proj, proj, proj, proj, proj, cos_s, sin_s, cos_s, sin_s)


def _merge_kernel(x_ref, oa_ref, hg_ref, ob_ref, rg_ref, oc_ref, ga_ref, gb_ref, gc_ref,
                  hn_ref, wa_ref, wb_ref, wc_ref, wo_ref, o_ref):
    hg = hg_ref[...].astype(F32)
    a = _rms(oa_ref[...].astype(F32)) * hn_ref[...] * (hg * _sigmoid(hg))
    ya = _dot(a.astype(BF16), wa_ref[...])
    rg = rg_ref[...].astype(F32)
    ob = ob_ref[...].astype(F32)
    parts = []
    for h in range(N_HEADS):
        lanes = slice(h * HEAD_DIM, (h + 1) * HEAD_DIM)
        parts.append(_rms(ob[:, lanes]))
    bn = jnp.concatenate(parts, axis=-1) * (rg * _sigmoid(rg))
    yb = _dot(bn.astype(BF16), wb_ref[...])
    yc = _dot(oc_ref[...], wc_ref[...])
    mixed = (_sigmoid(ga_ref[...].astype(F32)) * ya + _sigmoid(gb_ref[...].astype(F32)) * yb
             + _sigmoid(gc_ref[...].astype(F32)) * yc)
    o_ref[...] = x_ref[...] + _dot(mixed.astype(BF16), wo_ref[...])


def _merge(x, proj, oa, ob, oc, hn, wa, wb, wc, wo):
    t = x.shape[0]
    tt = min(t, 1024)
    w = N_HEADS * HEAD_DIM
    qw = SWA_Q_HEADS * SWA_DIM

    def rows(width, cb=0):
        return pl.BlockSpec((tt, width), lambda i: (i, cb * LANE // width))

    def full(shape):
        return pl.BlockSpec(shape, lambda i: (0, 0))

    return pl.pallas_call(
        _merge_kernel,
        grid=(t // tt,),
        in_specs=[rows(D_MODEL), rows(w), rows(w, CB_HG), rows(w), rows(w, CB_RG), rows(qw),
                  rows(D_MODEL, CB_GATE), rows(D_MODEL, CB_GATE + 8), rows(D_MODEL, CB_GATE + 16),
                  full((1, w)), full((w, D_MODEL)), full((w, D_MODEL)), full((qw, D_MODEL)),
                  full((D_MODEL, D_MODEL))],
        out_specs=rows(D_MODEL),
        out_shape=jax.ShapeDtypeStruct((t, D_MODEL), F32),
        compiler_params=_cparams(("parallel",)),
        name="merge",
    )(x, oa, proj, ob, proj, oc, proj, proj, proj, hn, wa, wb, wc, wo)


def _memkv_kernel(m_ref, g_ref, wk_ref, wv_ref, k_ref, v_ref):
    h = (_rms(m_ref[...]) * g_ref[...]).astype(BF16)
    k_ref[...] = _dot(h, wk_ref[...]).astype(BF16)
    v_ref[...] = _dot(h, wv_ref[...]).astype(BF16)


def _memkv(mem, g, wk, wv):
    nm = mem.shape[0]
    out = jax.ShapeDtypeStruct((nm, D_MODEL), BF16)
    return pl.pallas_call(
        _memkv_kernel,
        out_shape=[out, out],
        compiler_params=pltpu.CompilerParams(vmem_limit_bytes=VMEM_LIMIT),
        name="mem_kv",
    )(mem, g, wk, wv)


def _xattn_kernel(x_ref, g_ref, wq_ref, k_ref, v_ref, wo_ref, o_ref):
    x = x_ref[...]
    h = (_rms(x) * g_ref[...]).astype(BF16)
    q = _dot(h, wq_ref[...])
    outs = []
    for hh in range(X_HEADS):
        lanes = slice(hh * X_DIM, (hh + 1) * X_DIM)
        s = _dot_nt(q[:, lanes].astype(BF16), k_ref[:, lanes]) * (X_DIM ** -0.5)
        m = jnp.max(s, axis=-1, keepdims=True)
        p = jnp.exp(s - m)
        p = p / jnp.sum(p, axis=-1, keepdims=True)
        outs.append(_dot(p.astype(BF16), v_ref[:, lanes]))
    o = jnp.concatenate(outs, axis=-1)
    o_ref[...] = x + _dot(o.astype(BF16), wo_ref[...])


def _xattn(x, g, wq, k, v, wo):
    t = x.shape[0]
    tt = min(t, 1024)
    nm = k.shape[0]

    def full(shape):
        return pl.BlockSpec(shape, lambda i: (0, 0))

    rows = pl.BlockSpec((tt, D_MODEL), lambda i: (i, 0))
    return pl.pallas_call(
        _xattn_kernel,
        grid=(t // tt,),
        in_specs=[rows, full((1, D_MODEL)), full((D_MODEL, D_MODEL)),
                  full((nm, D_MODEL)), full((nm, D_MODEL)), full((D_MODEL, D_MODEL))],
        out_specs=rows,
        out_shape=jax.ShapeDtypeStruct((t, D_MODEL), F32),
        compiler_params=_cparams(("parallel",)),
        name="xattn",
    )(x, g, wq, k, v, wo)


_CAND_ROWS = [PEER_TOPK // (k1 + 1) for k1 in range(8)]
_RANK_BASE = 3.0e38
_RANK_STEP = 3.0e33


def _top_values(s, n, with_rank=False):
    vals = []
    for i in range(n):
        m = jnp.max(s, axis=0, keepdims=True)
        vals.append(m)
        s = jnp.where(s >= m, -(_RANK_BASE + i * _RANK_STEP), s)
    if not with_rank:
        return vals
    rank = jnp.where(s <= -_RANK_BASE, jnp.floor((-s - _RANK_BASE) * (1.0 / _RANK_STEP) + 0.5), float(n))
    return vals, rank


def _bf16_pair_bits(x):
    u = lax.bitcast_convert_type(x.astype(BF16).astype(F32), jnp.uint32)
    return u | (u >> 16)


def _stack_rows(rows, n):
    tt = rows[0].shape[1]
    r = lax.broadcasted_iota(jnp.int32, (n, tt), 0)
    out = jnp.zeros((n, tt), F32)
    for k in range(n):
        out = jnp.where(r == k, rows[k], out)
    return out


def _router_kernel(x_ref, g_ref, wq_ref, kh_ref, kl_ref,
                   hb_ref, a_ref, b_ref, r_ref, l_ref, qt_scr):
    h = _rms(x_ref[...]) * g_ref[...]
    hb_ref[...] = h.T.astype(BF16)
    qt_scr[...] = _dot_nt(wq_ref[...], h.astype(BF16))
    tt = x_ref.shape[0]
    k = PEER_TOPK
    r8 = lax.broadcasted_iota(jnp.int32, (8, tt), 0)

    def head_body(hh, carry):
        scores = []
        for p in range(2):
            r0 = pl.multiple_of(hh * (2 * PEER_HALF) + p * PEER_HALF, PEER_HALF)
            q_hi, q_lo = _split2(qt_scr[pl.ds(r0, PEER_HALF), :])
            kh = kh_ref[p * PEER_HEADS + hh]
            scores.append(_dot(kh, q_hi) + _dot(kl_ref[p * PEER_HEADS + hh], q_hi) + _dot(kh, q_lo))
        s1, s2 = scores
        v1, rank1 = _top_values(s1, k, with_rank=True)
        v2, rank2 = _top_values(s2, k, with_rank=True)
        sv1 = _stack_rows(v1, k)
        sv2 = _stack_rows(v2, k)
        groups = [v1[0] + sv2]
        for k1 in range(1, 8):
            groups.append(jnp.where(r8 < _CAND_ROWS[k1], v1[k1] + sv2[0:8], -jnp.inf))
        groups.append(sv1[8:16] + v2[0])
        cand = jnp.concatenate(groups, axis=0)
        tau = _top_values(cand, k)[k - 1]
        m0 = v1[0] + v2[0]
        z = jnp.sum(jnp.where(cand >= tau, jnp.exp(cand - m0), 0.0), axis=0, keepdims=True)
        a_ref[hh] = _bf16_pair_bits(jnp.exp(s1 - v1[0]))
        b_ref[hh] = (jnp.exp(s2 - v2[0]) / z).astype(BF16)
        count1 = jnp.zeros_like(s1)
        for k1 in range(k):
            n_k1 = jnp.sum((v1[k1] + sv2 >= tau).astype(F32), axis=0, keepdims=True)
            count1 = jnp.where(rank1 == float(k1), n_k1, count1)
        r_ref[hh] = rank2.astype(BF16)
        l_ref[hh] = _bf16_pair_bits(count1)
        return carry

    lax.fori_loop(0, PEER_HEADS, head_body, 0, unroll=2)


def _router(x, g, wq_t, k_hi, k_lo):
    t = x.shape[0]
    tt = min(t, 512)
    nq = wq_t.shape[0]
    sel = jax.ShapeDtypeStruct((PEER_HEADS, PEER_KEYS, t), jnp.uint32)
    sel_b = jax.ShapeDtypeStruct((PEER_HEADS, PEER_KEYS, t), BF16)
    sel_spec = pl.BlockSpec((PEER_HEADS, PEER_KEYS, tt), lambda i: (0, 0, i))
    return pl.pallas_call(
        _router_kernel,
        grid=(t // tt,),
        in_specs=[pl.BlockSpec((tt, D_MODEL), lambda i: (i, 0)),
                  pl.BlockSpec((1, D_MODEL), lambda i: (0, 0)),
                  pl.BlockSpec((nq, D_MODEL), lambda i: (0, 0)),
                  pl.BlockSpec((2 * PEER_HEADS, PEER_KEYS, PEER_HALF), lambda i: (0, 0, 0)),
                  pl.BlockSpec((2 * PEER_HEADS, PEER_KEYS, PEER_HALF), lambda i: (0, 0, 0))],
        out_specs=[pl.BlockSpec((D_MODEL, tt), lambda i: (0, i)), sel_spec, sel_spec, sel_spec, sel_spec],
        out_shape=[jax.ShapeDtypeStruct((D_MODEL, t), BF16), sel, sel_b, sel_b, sel],
        scratch_shapes=[pltpu.VMEM((nq, tt), F32)],
        compiler_params=_cparams(("parallel",)),
        name="peer_router",
    )(x, g, wq_t, k_hi, k_lo)


def _peer_kernel(x_ref, hb_ref, a_ref, l_ref, b_ref, r_ref, u_ref, vt_ref, gf_ref, o_ref,
                 acc_ref, ga_ref, *, final_norm):
    e = pl.program_id(1)

    @pl.when(e == 0)
    def _():
        acc_ref[...] = jnp.zeros_like(acc_ref)

    u_blk = pltpu.bitcast(u_ref[...], BF16)
    vt_blk = pltpu.bitcast(vt_ref[...], BF16)
    n_i1 = u_blk.shape[0] // PEER_KEYS
    sub = 16
    tt = hb_ref.shape[1]
    shape3 = (PEER_KEYS // sub, sub, tt)
    zero3 = jnp.zeros(shape3, BF16)
    piece = PEER_PIECE_KEYS * PEER_KEYS
    for p in range(n_i1 // PEER_PIECE_KEYS):
        for j in range(p * PEER_PIECE_KEYS, (p + 1) * PEER_PIECE_KEYS):
            gate = zero3
            for hh in range(PEER_HEADS):
                arow = pltpu.bitcast(jnp.broadcast_to(a_ref[hh, j:j + 1, :], (8, tt)), BF16)
                lrow = pltpu.bitcast(jnp.broadcast_to(l_ref[hh, j:j + 1, :], (8, tt)), BF16)
                b3 = b_ref[hh].reshape(shape3)
                r3 = r_ref[hh].reshape(shape3)
                gate = gate + arow[None] * jnp.where(r3 < lrow[None], b3, zero3)
            ga_ref[j * PEER_KEYS:(j + 1) * PEER_KEYS, :] = gate.reshape(PEER_KEYS, tt)
        rows = slice(p * piece, (p + 1) * piece)
        pre = _dot(u_blk[rows, :], hb_ref[...])
        act = pre + pre * lax.erf(pre)
        ga_ref[rows, :] = ga_ref[rows, :] * act.astype(BF16)
    acc_ref[...] += _dot(vt_blk, ga_ref[...])

    @pl.when(e == pl.num_programs(1) - 1)
    def _():
        y = x_ref[...] + acc_ref[...].T
        if final_norm:
            y = _rms(y) * gf_ref[...]
        o_ref[...] = y


def _pack_kernel(w_ref, o_ref, *, scale, transpose):
    w = w_ref[...]
    if transpose:
        w = w.T
    o_ref[...] = pltpu.bitcast((w * scale).astype(BF16), jnp.uint32)


def _pack_table(w_all, layer, scale, transpose):
    _, rows, cols = w_all.shape
    if transpose:
        blk = 512
        in_spec = pl.BlockSpec((None, blk, cols), lambda i: (layer, i, 0))
        out_spec = pl.BlockSpec((cols // 2, blk), lambda i: (0, i))
        out_shape = jax.ShapeDtypeStruct((cols // 2, rows), jnp.uint32)
    else:
        blk = 1024
        in_spec = pl.BlockSpec((None, blk, cols), lambda i: (layer, i, 0))
        out_spec = pl.BlockSpec((blk // 2, cols), lambda i: (i, 0))
        out_shape = jax.ShapeDtypeStruct((rows // 2, cols), jnp.uint32)
    return pl.pallas_call(
        functools.partial(_pack_kernel, scale=scale, transpose=transpose),
        grid=(rows // blk,),
        in_specs=[in_spec],
        out_specs=out_spec,
        out_shape=out_shape,
        compiler_params=_cparams(("parallel",)),
        name="pack_table_t" if transpose else "pack_table",
    )(w_all)


def _peer(x, hb, a, l, b, r, u, vt, gf, final_norm):
    t = x.shape[0]
    tt = min(t, 512)
    eb = 16 * PEER_KEYS
    rows = pl.BlockSpec((tt, D_MODEL), lambda i, e: (i, 0))
    i1_rows = pl.BlockSpec((PEER_HEADS, eb // PEER_KEYS, tt), lambda i, e: (0, e, i))
    all_rows = pl.BlockSpec((PEER_HEADS, PEER_KEYS, tt), lambda i, e: (0, 0, i))
    return pl.pallas_call(
        functools.partial(_peer_kernel, final_norm=final_norm),
        grid=(t // tt, N_EXPERTS // eb),
        in_specs=[rows, pl.BlockSpec((D_MODEL, tt), lambda i, e: (0, i)),
                  i1_rows, i1_rows, all_rows, all_rows,
                  pl.BlockSpec((eb // 2, D_MODEL), lambda i, e: (e, 0)),
                  pl.BlockSpec((D_MODEL // 2, eb), lambda i, e: (0, e)),
                  pl.BlockSpec((1, D_MODEL), lambda i, e: (0, 0))],
        out_specs=rows,
        out_shape=jax.ShapeDtypeStruct((t, D_MODEL), F32),
        scratch_shapes=[pltpu.VMEM((D_MODEL, tt), F32), pltpu.VMEM((eb, tt), BF16)],
        compiler_params=_cparams(("parallel", "arbitrary")),
        name="peer_dense",
    )(x, hb, a, l, b, r, u, vt, gf)


def kernel(x, mem, positions, ln_mix, w_in, lb_param, hgrn_norm, swa_sinks, w_br_hgrn, w_br_ret,
           w_br_swa, w_out, ln_xq, ln_xkv, w_xq, w_xk, w_xv, w_xo, ln_ffn, peer_wq, peer_keys,
           peer_u, peer_v, ln_final):
    b, t, d = x.shape
    assert b == 1 and d == D_MODEL
    depth = w_in.shape[0]
    xs = x.reshape(t, d)
    ms = mem.reshape(mem.shape[1], d)
    cos_r, sin_r, cos_s, sin_s = _rope_tables(positions)
    lb_sm = jax.nn.softmax(lb_param.astype(F32), axis=0)
    lower = jnp.cumsum(lb_sm, axis=0) - lb_sm[0]

    for l in range(depth):
        proj, hf = _proj(xs, ln_mix[l][None, :], _relay_w_in(w_in, l))
        oa = _hgrn(proj, hf, lower[l][None, :])
        ob = _ret(proj, cos_r, sin_r)
        oc = _swa(proj, cos_s, sin_s, swa_sinks[l])
        xs = _merge(xs, proj, oa, ob, oc, hgrn_norm[l][None, :], w_br_hgrn[l].astype(BF16),
                    w_br_ret[l].astype(BF16), w_br_swa[l].astype(BF16), w_out[l].astype(BF16))

        km, vm = _memkv(ms, ln_xkv[l][None, :], w_xk[l].astype(BF16), w_xv[l].astype(BF16))
        xs = _xattn(xs, ln_xq[l][None, :], w_xq[l].astype(BF16), km, vm, w_xo[l].astype(BF16))

        k_hi, k_lo = _split2(peer_keys[l].reshape(2 * PEER_HEADS, PEER_KEYS, PEER_HALF))
        hb, a, bz, r2, cnt = _router(xs, ln_ffn[l][None, :], peer_wq[l].T.astype(BF16), k_hi, k_lo)
        c = 2.0 ** -0.5
        xs = _peer(xs, hb, a, cnt, bz, r2, _pack_table(peer_u, l, c, False), _pack_table(peer_v, l, c, True),
                   ln_final[None, :], final_norm=(l == depth - 1))

    return xs.reshape(b, t, d)
```

```python
import functools
import math

import numpy as np
import jax
import jax.numpy as jnp
from jax import lax
from jax.experimental import pallas as pl
from jax.experimental.pallas import tpu as pltpu

F32 = jnp.float32
BF16 = jnp.bfloat16

D_MODEL = 1024
HEAD_DIM = 128
N_HEADS = 4
SWA_Q_HEADS = 8
SWA_KV_HEADS = 2
SWA_DIM = 64
SWA_BLOCK = 128
ROPE_THETA = 10000.0
X_HEADS = 4
X_DIM = D_MODEL // X_HEADS
PEER_HEADS = 8
PEER_KEYS = 128
PEER_TOPK = 16
PEER_HALF = 128
N_EXPERTS = PEER_KEYS * PEER_KEYS
EPS = 1e-6
NEG_BIG = -1e30
TINY = 1e-30

LANE = 128
HG_CHUNK = 64
RET_CHUNK = 256
PEER_PIECE_KEYS = 4
VMEM_LIMIT = 56 * 1024 * 1024

CB_GATE = 0
CB_HQ, CB_HF, CB_HI, CB_HG = 24, 28, 32, 36
CB_RQ, CB_RK, CB_RV, CB_RG = 40, 44, 48, 52
CB_SQ, CB_SQR, CB_SK, CB_SKR, CB_SVA, CB_SVB = 56, 60, 64, 66, 68, 70
N_CB = 72
PROJ_COLS = N_CB * LANE


def _cparams(sem):
    return pltpu.CompilerParams(dimension_semantics=sem, vmem_limit_bytes=VMEM_LIMIT)


def _dot(a, b):
    return jnp.dot(a, b, preferred_element_type=F32)


def _dot_nt(a, b):
    return lax.dot_general(a, b, (((1,), (1,)), ((), ())), preferred_element_type=F32)


def _dot_tn(a, b):
    return lax.dot_general(a, b, (((0,), (0,)), ((), ())), preferred_element_type=F32)


def _split2(x):
    hi = x.astype(BF16)
    lo = (x - hi.astype(F32)).astype(BF16)
    return hi, lo


def _sigmoid(x):
    return 1.0 / (1.0 + jnp.exp(-x))


def _rms(x):
    return x * lax.rsqrt(jnp.mean(x * x, axis=-1, keepdims=True) + EPS)


def _rope_kernel(pos_ref, f_ref, cr_ref, sr_ref, cs_ref, ss_ref):
    ang = pos_ref[...].astype(F32) * f_ref[...]
    c, s = jnp.cos(ang), jnp.sin(ang)
    lane = lax.broadcasted_iota(jnp.int32, c.shape, 1)
    half, quarter = LANE // 2, LANE // 4

    def retention(x, first):
        return jnp.where(lane < half, first * x, pltpu.roll(x, half, 1))

    def window(x):
        y = pltpu.roll(x, half, 1)
        y = jnp.where(lane < quarter, y, pltpu.roll(y, quarter, 1))
        return jnp.where(lane < half, y, pltpu.roll(y, half, 1))

    cr_ref[...] = retention(c, 1.0)
    sr_ref[...] = retention(s, -1.0)
    cs_ref[...] = window(c)
    ss_ref[...] = window(s)


def _rope_tables(positions):
    t = positions.shape[1]
    tt = min(t, 1024)
    ret_f = ROPE_THETA ** (-jnp.linspace(0.0, 1.0, HEAD_DIM // 2, dtype=F32))
    swa_f = ROPE_THETA ** (-jnp.arange(0, SWA_DIM, 2, dtype=F32) / SWA_DIM)
    freqs = jnp.concatenate([ret_f, swa_f, jnp.zeros((LANE - HEAD_DIM // 2 - SWA_DIM // 2,), F32)])[None, :]
    pos = positions.reshape(t, 1)
    tab = jax.ShapeDtypeStruct((t, LANE), F32)
    row = pl.BlockSpec((tt, LANE), lambda i: (i, 0))
    return pl.pallas_call(
        _rope_kernel,
        grid=(t // tt,),
        in_specs=[pl.BlockSpec((tt, 1), lambda i: (i, 0)), pl.BlockSpec((1, LANE), lambda i: (0, 0))],
        out_specs=[row, row, row, row],
        out_shape=[tab, tab, tab, tab],
        compiler_params=_cparams(("parallel",)),
        name="rope_tables",
    )(pos, freqs)


PROJ_TN = 1536


def _proj_kernel(x_ref, g_ref, w_ref, o_ref, f_ref, h_scr):
    @pl.when(pl.program_id(1) == 0)
    def _():
        h_scr[...] = (_rms(x_ref[...]) * g_ref[...]).astype(BF16)

    y = _dot(h_scr[...], w_ref[...])
    o_ref[...] = y.astype(BF16)

    @pl.when(pl.program_id(1) == CB_HF * LANE // PROJ_TN)
    def _():
        lo = CB_HF * LANE % PROJ_TN
        f_ref[...] = y[:, lo:lo + N_HEADS * HEAD_DIM]


def _proj(x, g, w):
    t = x.shape[0]
    n = w.shape[1]
    tm = min(t, 2048)
    tn = PROJ_TN
    wf = N_HEADS * HEAD_DIM
    return pl.pallas_call(
        _proj_kernel,
        grid=(t // tm, n // tn),
        in_specs=[pl.BlockSpec((tm, D_MODEL), lambda i, j: (i, 0)),
                  pl.BlockSpec((1, D_MODEL), lambda i, j: (0, 0)),
                  pl.BlockSpec((D_MODEL, tn), lambda i, j: (0, j))],
        out_specs=[pl.BlockSpec((tm, tn), lambda i, j: (i, j)),
                   pl.BlockSpec((tm, wf), lambda i, j: (i, 0))],
        out_shape=[jax.ShapeDtypeStruct((t, n), BF16), jax.ShapeDtypeStruct((t, wf), F32)],
        scratch_shapes=[pltpu.VMEM((tm, D_MODEL), BF16)],
        compiler_params=_cparams(("parallel", "arbitrary")),
        name="in_proj",
    )(x, g, w)


def _relay_plan():
    names = (["hq"] * 4 + ["hf"] * 4 + ["hi"] * 4 + ["hg"] * 4 + ["rq"] * 4 + ["rk"] * 4 + ["rv"] * 4
             + ["rg"] * 4 + ["sq"] * 4 + ["sk", "sv"] + ["gate"] * 24)
    first = {}
    for i, n in enumerate(names):
        first.setdefault(n, i)
    plan = [(first["gate"] + i, 0, 0) for i in range(24)]
    for n in ("hq", "hf", "hi", "hg", "rq", "rk", "rv", "rg", "sq"):
        plan += [(first[n] + i, 0, 0) for i in range(4)]
    plan += [(first["sq"] + i, 1, 0) for i in range(4)]
    plan += [(first["sk"], 0, 1), (first["sk"], 0, 2)]
    plan += [(first["sk"], 1, 1), (first["sk"], 1, 2)]
    plan += [(first["sv"], 0, 3), (first["sv"], 0, 4)]
    plan += [(first["sv"], 0, 5), (first["sv"], 0, 6)]
    assert len(plan) == N_CB
    return plan


def _relay_kernel(src_ref, rot_ref, put_ref, w_ref, o_ref):
    j = pl.program_id(0)
    x = w_ref[...]
    lane = lax.broadcasted_iota(jnp.int32, x.shape, 1)
    swapped = pltpu.roll(x, SWA_DIM, 1)
    first_half = lane % SWA_DIM < SWA_DIM // 2
    rotated = jnp.where(first_half, -pltpu.roll(x, LANE - SWA_DIM // 2, 1), pltpu.roll(x, SWA_DIM // 2, 1))
    rotated_swapped = pltpu.roll(rotated, SWA_DIM, 1)
    rot = rot_ref[j] == 1
    y = jnp.where(rot, rotated, x)
    ys = jnp.where(rot, rotated_swapped, swapped)
    low = lane < SWA_DIM
    put = put_ref[j]
    out = y
    out = jnp.where(put == 1, jnp.where(low, y, ys), out)
    out = jnp.where(put == 2, jnp.where(low, ys, y), out)
    out = jnp.where(put == 3, jnp.where(low, y, 0.0), out)
    out = jnp.where(put == 4, jnp.where(low, ys, 0.0), out)
    out = jnp.where(put == 5, jnp.where(low, 0.0, ys), out)
    out = jnp.where(put == 6, jnp.where(low, 0.0, y), out)
    o_ref[...] = out.astype(BF16)


def _relay_w_in(w_in, layer):
    d = w_in.shape[1]
    plan = np.asarray(_relay_plan(), np.int32)
    grid_spec = pltpu.PrefetchScalarGridSpec(
        num_scalar_prefetch=3,
        grid=(N_CB,),
        in_specs=[pl.BlockSpec((None, d, LANE), lambda j, src, rot, put: (layer, 0, src[j]))],
        out_specs=pl.BlockSpec((d, LANE), lambda j, src, rot, put: (0, j)),
    )
    return pl.pallas_call(
        _relay_kernel,
        grid_spec=grid_spec,
        out_shape=jax.ShapeDtypeStruct((d, PROJ_COLS), BF16),
        compiler_params=_cparams(("arbitrary",)),
        name="relay_w_in",
    )(jnp.asarray(plan[:, 0]), jnp.asarray(plan[:, 1]), jnp.asarray(plan[:, 2]), w_in)


def _hgrn_kernel(hq_ref, hf_ref, hi_ref, lb_ref, o_ref, st_ref, *, n_chunks):
    @pl.when(pl.program_id(0) == 0)
    def _():
        st_ref[...] = jnp.zeros_like(st_ref)

    c = HG_CHUNK
    row = lax.broadcasted_iota(jnp.int32, (c, c), 0)
    col = lax.broadcasted_iota(jnp.int32, (c, c), 1)
    diag = row == col
    halves = [c >> (i + 1) for i in range(c.bit_length() - 1)]
    pair_mask = {b: (row // (2 * b) == col // (2 * b)) & (row % (2 * b) >= b) & (col % (2 * b) < b)
                 for b in halves}
    rw = lax.broadcasted_iota(jnp.int32, (c, HEAD_DIM), 0)
    r3 = lax.broadcasted_iota(jnp.int32, (c // 8, 8, HEAD_DIM), 1)
    scale = HEAD_DIM ** -0.5
    log2e = math.log2(math.e)

    def boundary(cum2, b):
        if b >= 8:
            ends = [cum2[s + b - 1:s + b, :] for s in range(0, c, 2 * b)]
            out = ends[-1]
            for i in range(len(ends) - 2, -1, -1):
                out = jnp.where(rw < (i + 1) * 2 * b, ends[i], out)
            return out
        c3 = cum2.reshape(c // 8, 8, HEAD_DIM)
        if b == 4:
            return jnp.broadcast_to(c3[:, 3:4, :], c3.shape).reshape(c, HEAD_DIM)
        if b == 2:
            lo = jnp.broadcast_to(c3[:, 1:2, :], c3.shape)
            hi = jnp.broadcast_to(c3[:, 5:6, :], c3.shape)
            return jnp.where(r3 < 4, lo, hi).reshape(c, HEAD_DIM)
        prev = pltpu.roll(cum2, 1, 0)
        return jnp.where(rw % 2 == 1, prev, cum2)

    def chunk_body(ci, carry):
        r0 = pl.multiple_of(ci * c, c)
        for h in range(N_HEADS):
            lanes = slice(h * HEAD_DIM, (h + 1) * HEAD_DIM)
            xq = hq_ref[pl.ds(r0, c), lanes].astype(F32)
            z = hf_ref[pl.ds(r0, c), lanes]
            v = hi_ref[pl.ds(r0, c), lanes]
            lb = lb_ref[:, lanes]
            omlb = 1.0 - lb
            q = xq * _sigmoid(xq) * scale
            f = lb + omlb * _sigmoid(z)
            logf = jnp.log(jnp.maximum(f, TINY))
            kk = omlb * _sigmoid(-z)
            cum2 = logf * log2e
            step = 1
            while step < c:
                cum2 = cum2 + jnp.where(rw >= step, pltpu.roll(cum2, step, 0), 0.0)
                step *= 2
            scores = jnp.where(diag, jnp.sum(q * kk, axis=-1, keepdims=True), 0.0)
            for b in halves:
                ref = boundary(cum2, b)
                qs = (q * jnp.exp2(jnp.minimum(cum2 - ref, 0.0))).astype(BF16)
                ks = (kk * jnp.exp2(jnp.minimum(ref - cum2, 0.0))).astype(BF16)
                scores = jnp.where(pair_mask[b], _dot_nt(qs, ks), scores)
            st = st_ref[h]
            qe = (q * jnp.exp2(cum2)).astype(BF16)
            o = _dot_nt(qe, st.astype(BF16)) + _dot(scores.astype(BF16), v.astype(BF16))
            o_ref[pl.ds(r0, c), lanes] = o.astype(BF16)
            last = cum2[c - 1:c, :]
            kl = (kk * jnp.exp2(last - cum2)).astype(BF16)
            st_ref[h] = jnp.exp2(last) * st + _dot_tn(v.astype(BF16), kl)
        return carry

    lax.fori_loop(0, n_chunks, chunk_body, 0, unroll=4)


def _hgrn(proj, hf, lower):
    t = proj.shape[0]
    tb = min(t, 512)
    w = N_HEADS * HEAD_DIM

    def cols(cb):
        return pl.BlockSpec((tb, w), lambda i: (i, cb // N_HEADS))

    return pl.pallas_call(
        functools.partial(_hgrn_kernel, n_chunks=tb // HG_CHUNK),
        grid=(t // tb,),
        in_specs=[cols(CB_HQ), pl.BlockSpec((tb, w), lambda i: (i, 0)), cols(CB_HI),
                  pl.BlockSpec((1, w), lambda i: (0, 0))],
        out_specs=pl.BlockSpec((tb, w), lambda i: (i, 0)),
        out_shape=jax.ShapeDtypeStruct((t, w), BF16),
        scratch_shapes=[pltpu.VMEM((N_HEADS, HEAD_DIM, HEAD_DIM), F32)],
        compiler_params=_cparams(("arbitrary",)),
        name="hgrn2",
    )(proj, hf, proj, lower)


def _ret_kernel(q_ref, k_ref, v_ref, cos_ref, sin_ref,
                dec_ref, qd_ref, kd_ref, cd_ref, o_ref, s_ref):
    @pl.when(pl.program_id(0) == 0)
    def _():
        s_ref[...] = jnp.zeros_like(s_ref)

    c = dec_ref.shape[1]
    scale = HEAD_DIM ** -0.5
    for ci in range(q_ref.shape[0] // c):
        rows = slice(ci * c, (ci + 1) * c)
        cos = cos_ref[rows, :]
        sin = sin_ref[rows, :]
        for h in range(N_HEADS):
            lanes = slice(h * HEAD_DIM, (h + 1) * HEAD_DIM)
            q = q_ref[rows, lanes].astype(F32)
            k = k_ref[rows, lanes].astype(F32)
            q = q * cos + pltpu.roll(q, HEAD_DIM // 2, 1) * sin
            k = (k * cos + pltpu.roll(k, HEAD_DIM // 2, 1) * sin) * scale
            v = v_ref[rows, lanes]
            qb = q.astype(BF16)
            s = s_ref[h]
            scores = _dot_nt(qb, k.astype(BF16)) * dec_ref[h]
            o = _dot(qb, s.astype(BF16)) * qd_ref[h] + _dot(scores.astype(BF16), v)
            o_ref[rows, lanes] = o.astype(BF16)
            s_ref[h] = cd_ref[h] * s + _dot_tn((k * kd_ref[h]).astype(BF16), v)


def _ret(proj, cos_r, sin_r):
    t = proj.shape[0]
    c = min(t, RET_CHUNK)
    tb = 4 * c if t % (4 * c) == 0 else c
    w = N_HEADS * HEAD_DIM
    log_gamma = jnp.log(1.0 - 2.0 ** (-5.0 - jnp.arange(N_HEADS, dtype=F32)))
    idx = jnp.arange(c, dtype=F32)
    rel = idx[:, None] - idx[None, :]
    decay = jnp.exp(jnp.where(rel >= 0, log_gamma[:, None, None] * rel, NEG_BIG))
    ones = jnp.ones((1, 1, HEAD_DIM), F32)
    q_decay = jnp.exp(log_gamma[:, None] * (idx + 1.0))[:, :, None] * ones
    k_decay = jnp.exp(log_gamma[:, None] * (c - 1.0 - idx))[:, :, None] * ones
    c_decay = jnp.exp(log_gamma * c)[:, None, None] * ones

    def cols(cb):
        return pl.BlockSpec((tb, w), lambda i: (i, cb // N_HEADS))

    def const(shape):
        return pl.BlockSpec(shape, lambda i: (0, 0, 0))

    tab = pl.BlockSpec((tb, LANE), lambda i: (i, 0))
    return pl.pallas_call(
        _ret_kernel,
        grid=(t // tb,),
        in_specs=[cols(CB_RQ), cols(CB_RK), cols(CB_RV), tab, tab,
                  const((N_HEADS, c, c)), const((N_HEADS, c, HEAD_DIM)),
                  const((N_HEADS, c, HEAD_DIM)), const((N_HEADS, 1, HEAD_DIM))],
        out_specs=pl.BlockSpec((tb, w), lambda i: (i, 0)),
        out_shape=jax.ShapeDtypeStruct((t, w), BF16),
        scratch_shapes=[pltpu.VMEM((N_HEADS, HEAD_DIM, HEAD_DIM), F32)],
        compiler_params=_cparams(("arbitrary",)),
        name="retention",
    )(proj, proj, proj, cos_r, sin_r, decay, q_decay, k_decay, c_decay)


def _swa_kernel(sink_ref, q_ref, qr_ref, kc_ref, krc_ref, vac_ref, vbc_ref,
                kp_ref, krp_ref, vap_ref, vbp_ref, cc_ref, sc_ref, cp_ref, sp_ref, o_ref):
    b = SWA_BLOCK
    nb = q_ref.shape[0] // b
    first = pl.program_id(0) * nb
    cos_c, sin_c = cc_ref[...], sc_ref[...]
    cos_p, sin_p = cp_ref[...], sp_ref[...]
    qi = lax.broadcasted_iota(jnp.int32, (b, 2 * b), 0) + b
    ki = lax.broadcasted_iota(jnp.int32, (b, 2 * b), 1)
    rel = qi - ki
    in_window = (rel >= 0) & (rel < b)
    low = lax.broadcasted_iota(jnp.int32, (b, LANE), 1) < SWA_DIM
    pairs_per_group = SWA_Q_HEADS // SWA_KV_HEADS // 2
    for g in range(SWA_KV_HEADS):
        kl = slice(g * LANE, (g + 1) * LANE)
        k_cur = kc_ref[:, kl].astype(F32) * cos_c + krc_ref[:, kl].astype(F32) * sin_c
        k_prev = kp_ref[:, kl].astype(F32) * cos_p + krp_ref[:, kl].astype(F32) * sin_p
        k_all = jnp.concatenate([k_prev, k_cur], axis=0).astype(BF16)
        v_lo = jnp.concatenate([vap_ref[:, kl], vac_ref[:, kl]], axis=0)
        v_hi = jnp.concatenate([vbp_ref[:, kl], vbc_ref[:, kl]], axis=0)
        for sb in range(nb):
            win = slice(sb * b, (sb + 2) * b)
            rows = slice(sb * b, (sb + 1) * b)
            keep = in_window & ((first + sb > 0) | (ki >= b))
            kw = k_all[win]
            for j in range(pairs_per_group):
                pair = g * pairs_per_group + j
                ql = slice(pair * LANE, (pair + 1) * LANE)
                q = (q_ref[rows, ql].astype(F32) * cos_c[rows] + qr_ref[rows, ql].astype(F32) * sin_c[rows])
                out = None
                for half, vw in ((0, v_lo[win]), (1, v_hi[win])):
                    qh = jnp.where(low if half == 0 else ~low, q, 0.0).astype(BF16)
                    s = _dot_nt(qh, kw) * (SWA_DIM ** -0.5)
                    s = jnp.where(keep, s, NEG_BIG)
                    sink = sink_ref[2 * pair + half]
                    m = jnp.maximum(jnp.max(s, axis=-1, keepdims=True), sink)
                    p = jnp.exp(s - m)
                    denom = jnp.sum(p, axis=-1, keepdims=True) + jnp.exp(sink - m)
                    o = _dot(p.astype(BF16), vw) / denom
                    out = o if out is None else out + o
                o_ref[rows, ql] = out.astype(BF16)


def _swa(proj, cos_s, sin_s, sinks):
    t = proj.shape[0]
    b = SWA_BLOCK
    nb = 4 if t % (4 * b) == 0 else 1
    qw = SWA_Q_HEADS * SWA_DIM
    kw = SWA_KV_HEADS * LANE

    def cur(cb, width):
        return pl.BlockSpec((nb * b, width), lambda i: (i, cb * LANE // width))

    def prev(cb, width):
        return pl.BlockSpec((b, width), lambda i: (jnp.maximum(nb * i - 1, 0), cb * LANE // width))

    tab_c = pl.BlockSpec((nb * b, LANE), lambda i: (i, 0))
    tab_p = pl.BlockSpec((b, LANE), lambda i: (jnp.maximum(nb * i - 1, 0), 0))
    return pl.pallas_call(
        _swa_kernel,
        grid=(t // (nb * b),),
        in_specs=[pl.BlockSpec(memory_space=pltpu.SMEM),
                  cur(CB_SQ, qw), cur(CB_SQR, qw),
                  cur(CB_SK, kw), cur(CB_SKR, kw), cur(CB_SVA, kw), cur(CB_SVB, kw),
                  prev(CB_SK, kw), prev(CB_SKR, kw), prev(CB_SVA, kw), prev(CB_SVB, kw),
                  tab_c, tab_c, tab_p, tab_p],
        out_specs=pl.BlockSpec((nb * b, qw), lambda i: (i, 0)),
        out_shape=jax.ShapeDtypeStruct((t, qw), BF16),
        compiler_params=_cparams(("parallel",)),
        name="swa",
    )(sinks, proj, proj, proj, proj, proj, proj, proj, proj, proj, proj, cos_s, sin_s, cos_s, sin_s)


def _merge_kernel(x_ref, oa_ref, hg_ref, ob_ref, rg_ref, oc_ref, ga_ref, gb_ref, gc_ref,
                  hn_ref, wa_ref, wb_ref, wc_ref, wo_ref, o_ref):
    hg = hg_ref[...].astype(F32)
    a = _rms(oa_ref[...].astype(F32)) * hn_ref[...] * (hg * _sigmoid(hg))
    ya = _dot(a.astype(BF16), wa_ref[...])
    rg = rg_ref[...].astype(F32)
    ob = ob_ref[...].astype(F32)
    parts = []
    for h in range(N_HEADS):
        lanes = slice(h * HEAD_DIM, (h + 1) * HEAD_DIM)
        parts.append(_rms(ob[:, lanes]))
    bn = jnp.concatenate(parts, axis=-1) * (rg * _sigmoid(rg))
    yb = _dot(bn.astype(BF16), wb_ref[...])
    yc = _dot(oc_ref[...], wc_ref[...])
    mixed = (_sigmoid(ga_ref[...].astype(F32)) * ya + _sigmoid(gb_ref[...].astype(F32)) * yb
             + _sigmoid(gc_ref[...].astype(F32)) * yc)
    o_ref[...] = x_ref[...] + _dot(mixed.astype(BF16), wo_ref[...])


def _merge(x, proj, oa, ob, oc, hn, wa, wb, wc, wo):
    t = x.shape[0]
    tt = min(t, 1024)
    w = N_HEADS * HEAD_DIM
    qw = SWA_Q_HEADS * SWA_DIM

    def rows(width, cb=0):
        return pl.BlockSpec((tt, width), lambda i: (i, cb * LANE // width))

    def full(shape):
        return pl.BlockSpec(shape, lambda i: (0, 0))

    return pl.pallas_call(
        _merge_kernel,
        grid=(t // tt,),
        in_specs=[rows(D_MODEL), rows(w), rows(w, CB_HG), rows(w), rows(w, CB_RG), rows(qw),
                  rows(D_MODEL, CB_GATE), rows(D_MODEL, CB_GATE + 8), rows(D_MODEL, CB_GATE + 16),
                  full((1, w)), full((w, D_MODEL)), full((w, D_MODEL)), full((qw, D_MODEL)),
                  full((D_MODEL, D_MODEL))],
        out_specs=rows(D_MODEL),
        out_shape=jax.ShapeDtypeStruct((t, D_MODEL), F32),
        compiler_params=_cparams(("parallel",)),
        name="merge",
    )(x, oa, proj, ob, proj, oc, proj, proj, proj, hn, wa, wb, wc, wo)


def _memkv_kernel(m_ref, g_ref, wk_ref, wv_ref, k_ref, v_ref):
    h = (_rms(m_ref[...]) * g_ref[...]).astype(BF16)
    k_ref[...] = _dot(h, wk_ref[...]).astype(BF16)
    v_ref[...] = _dot(h, wv_ref[...]).astype(BF16)


def _memkv(mem, g, wk, wv):
    nm = mem.shape[0]
    out = jax.ShapeDtypeStruct((nm, D_MODEL), BF16)
    return pl.pallas_call(
        _memkv_kernel,
        out_shape=[out, out],
        compiler_params=pltpu.CompilerParams(vmem_limit_bytes=VMEM_LIMIT),
        name="mem_kv",
    )(mem, g, wk, wv)


def _xattn_kernel(x_ref, g_ref, wq_ref, k_ref, v_ref, wo_ref, o_ref):
    x = x_ref[...]
    h = (_rms(x) * g_ref[...]).astype(BF16)
    q = _dot(h, wq_ref[...])
    outs = []
    for hh in range(X_HEADS):
        lanes = slice(hh * X_DIM, (hh + 1) * X_DIM)
        s = _dot_nt(q[:, lanes].astype(BF16), k_ref[:, lanes]) * (X_DIM ** -0.5)
        m = jnp.max(s, axis=-1, keepdims=True)
        p = jnp.exp(s - m)
        p = p / jnp.sum(p, axis=-1, keepdims=True)
        outs.append(_dot(p.astype(BF16), v_ref[:, lanes]))
    o = jnp.concatenate(outs, axis=-1)
    o_ref[...] = x + _dot(o.astype(BF16), wo_ref[...])


def _xattn(x, g, wq, k, v, wo):
    t = x.shape[0]
    tt = min(t, 1024)
    nm = k.shape[0]

    def full(shape):
        return pl.BlockSpec(shape, lambda i: (0, 0))

    rows = pl.BlockSpec((tt, D_MODEL), lambda i: (i, 0))
    return pl.pallas_call(
        _xattn_kernel,
        grid=(t // tt,),
        in_specs=[rows, full((1, D_MODEL)), full((D_MODEL, D_MODEL)),
                  full((nm, D_MODEL)), full((nm, D_MODEL)), full((D_MODEL, D_MODEL))],
        out_specs=rows,
        out_shape=jax.ShapeDtypeStruct((t, D_MODEL), F32),
        compiler_params=_cparams(("parallel",)),
        name="xattn",
    )(x, g, wq, k, v, wo)


_CAND_ROWS = [PEER_TOPK // (k1 + 1) for k1 in range(8)]
_RANK_BASE = 3.0e38
_RANK_STEP = 3.0e33


def _top_values(s, n, with_rank=False):
    vals = []
    for i in range(n):
        m = jnp.max(s, axis=0, keepdims=True)
        vals.append(m)
        s = jnp.where(s >= m, -(_RANK_BASE + i * _RANK_STEP), s)
    if not with_rank:
        return vals
    rank = jnp.where(s <= -_RANK_BASE, jnp.floor((-s - _RANK_BASE) * (1.0 / _RANK_STEP) + 0.5), float(n))
    return vals, rank


def _bf16_pair_bits(x):
    u = lax.bitcast_convert_type(x.astype(BF16).astype(F32), jnp.uint32)
    return u | (u >> 16)


def _stack_rows(rows, n):
    tt = rows[0].shape[1]
    r = lax.broadcasted_iota(jnp.int32, (n, tt), 0)
    out = jnp.zeros((n, tt), F32)
    for k in range(n):
        out = jnp.where(r == k, rows[k], out)
    return out


def _router_kernel(x_ref, g_ref, wq_ref, kh_ref, kl_ref,
                   hb_ref, a_ref, b_ref, r_ref, l_ref, qt_scr):
    h = _rms(x_ref[...]) * g_ref[...]
    hb_ref[...] = h.T.astype(BF16)
    qt_scr[...] = _dot_nt(wq_ref[...], h.astype(BF16))
    tt = x_ref.shape[0]
    k = PEER_TOPK
    r8 = lax.broadcasted_iota(jnp.int32, (8, tt), 0)

    def head_body(hh, carry):
        scores = []
        for p in range(2):
            r0 = pl.multiple_of(hh * (2 * PEER_HALF) + p * PEER_HALF, PEER_HALF)
            q_hi, q_lo = _split2(qt_scr[pl.ds(r0, PEER_HALF), :])
            kh = kh_ref[p * PEER_HEADS + hh]
            scores.append(_dot(kh, q_hi) + _dot(kl_ref[p * PEER_HEADS + hh], q_hi) + _dot(kh, q_lo))
        s1, s2 = scores
        v1, rank1 = _top_values(s1, k, with_rank=True)
        v2, rank2 = _top_values(s2, k, with_rank=True)
        sv1 = _stack_rows(v1, k)
        sv2 = _stack_rows(v2, k)
        groups = [v1[0] + sv2]
        for k1 in range(1, 8):
            groups.append(jnp.where(r8 < _CAND_ROWS[k1], v1[k1] + sv2[0:8], -jnp.inf))
        groups.append(sv1[8:16] + v2[0])
        cand = jnp.concatenate(groups, axis=0)
        tau = _top_values(cand, k)[k - 1]
        m0 = v1[0] + v2[0]
        z = jnp.sum(jnp.where(cand >= tau, jnp.exp(cand - m0), 0.0), axis=0, keepdims=True)
        a_ref[hh] = _bf16_pair_bits(jnp.exp(s1 - v1[0]))
        b_ref[hh] = (jnp.exp(s2 - v2[0]) / z).astype(BF16)
        count1 = jnp.zeros_like(s1)
        for k1 in range(k):
            n_k1 = jnp.sum((v1[k1] + sv2 >= tau).astype(F32), axis=0, keepdims=True)
            count1 = jnp.where(rank1 == float(k1), n_k1, count1)
        r_ref[hh] = rank2.astype(BF16)
        l_ref[hh] = _bf16_pair_bits(count1)
        return carry

    lax.fori_loop(0, PEER_HEADS, head_body, 0, unroll=4)


def _router(x, g, wq_t, k_hi, k_lo):
    t = x.shape[0]
    tt = min(t, 512)
    nq = wq_t.shape[0]
    sel = jax.ShapeDtypeStruct((PEER_HEADS, PEER_KEYS, t), jnp.uint32)
    sel_b = jax.ShapeDtypeStruct((PEER_HEADS, PEER_KEYS, t), BF16)
    sel_spec = pl.BlockSpec((PEER_HEADS, PEER_KEYS, tt), lambda i: (0, 0, i))
    return pl.pallas_call(
        _router_kernel,
        grid=(t // tt,),
        in_specs=[pl.BlockSpec((tt, D_MODEL), lambda i: (i, 0)),
                  pl.BlockSpec((1, D_MODEL), lambda i: (0, 0)),
                  pl.BlockSpec((nq, D_MODEL), lambda i: (0, 0)),
                  pl.BlockSpec((2 * PEER_HEADS, PEER_KEYS, PEER_HALF), lambda i: (0, 0, 0)),
                  pl.BlockSpec((2 * PEER_HEADS, PEER_KEYS, PEER_HALF), lambda i: (0, 0, 0))],
        out_specs=[pl.BlockSpec((D_MODEL, tt), lambda i: (0, i)), sel_spec, sel_spec, sel_spec, sel_spec],
        out_shape=[jax.ShapeDtypeStruct((D_MODEL, t), BF16), sel, sel_b, sel_b, sel],
        scratch_shapes=[pltpu.VMEM((nq, tt), F32)],
        compiler_params=_cparams(("parallel",)),
        name="peer_router",
    )(x, g, wq_t, k_hi, k_lo)


def _peer_kernel(x_ref, hb_ref, a_ref, l_ref, b_ref, r_ref, u_ref, vt_ref, gf_ref, o_ref,
                 acc_ref, ga_ref, *, final_norm):
    e = pl.program_id(1)

    @pl.when(e == 0)
    def _():
        acc_ref[...] = jnp.zeros_like(acc_ref)

    u_blk = pltpu.bitcast(u_ref[...], BF16)
    vt_blk = pltpu.bitcast(vt_ref[...], BF16)
    n_i1 = u_blk.shape[0] // PEER_KEYS
    sub = 16
    tt = hb_ref.shape[1]
    shape3 = (PEER_KEYS // sub, sub, tt)
    zero3 = jnp.zeros(shape3, BF16)
    piece = PEER_PIECE_KEYS * PEER_KEYS
    for p in range(n_i1 // PEER_PIECE_KEYS):
        for j in range(p * PEER_PIECE_KEYS, (p + 1) * PEER_PIECE_KEYS):
            gate = zero3
            for hh in range(PEER_HEADS):
                arow = pltpu.bitcast(jnp.broadcast_to(a_ref[hh, j:j + 1, :], (8, tt)), BF16)
                lrow = pltpu.bitcast(jnp.broadcast_to(l_ref[hh, j:j + 1, :], (8, tt)), BF16)
                b3 = b_ref[hh].reshape(shape3)
                r3 = r_ref[hh].reshape(shape3)
                gate = gate + arow[None] * jnp.where(r3 < lrow[None], b3, zero3)
            ga_ref[j * PEER_KEYS:(j + 1) * PEER_KEYS, :] = gate.reshape(PEER_KEYS, tt)
        rows = slice(p * piece, (p + 1) * piece)
        pre = _dot(u_blk[rows, :], hb_ref[...])
        act = pre + pre * lax.erf(pre)
        ga_ref[rows, :] = ga_ref[rows, :] * act.astype(BF16)
    acc_ref[...] += _dot(vt_blk, ga_ref[...])

    @pl.when(e == pl.num_programs(1) - 1)
    def _():
        y = x_ref[...] + acc_ref[...].T
        if final_norm:
            y = _rms(y) * gf_ref[...]
        o_ref[...] = y


def _pack_kernel(w_ref, o_ref, *, scale, transpose):
    w = w_ref[...]
    if transpose:
        w = w.T
    o_ref[...] = pltpu.bitcast((w * scale).astype(BF16), jnp.uint32)


def _pack_table(w_all, layer, scale, transpose):
    _, rows, cols = w_all.shape
    if transpose:
        blk = 512
        in_spec = pl.BlockSpec((None, blk, cols), lambda i: (layer, i, 0))
        out_spec = pl.BlockSpec((cols // 2, blk), lambda i: (0, i))
        out_shape = jax.ShapeDtypeStruct((cols // 2, rows), jnp.uint32)
    else:
        blk = 1024
        in_spec = pl.BlockSpec((None, blk, cols), lambda i: (layer, i, 0))
        out_spec = pl.BlockSpec((blk // 2, cols), lambda i: (i, 0))
        out_shape = jax.ShapeDtypeStruct((rows // 2, cols), jnp.uint32)
    return pl.pallas_call(
        functools.partial(_pack_kernel, scale=scale, transpose=transpose),
        grid=(rows // blk,),
        in_specs=[in_spec],
        out_specs=out_spec,
        out_shape=out_shape,
        compiler_params=_cparams(("parallel",)),
        name="pack_table_t" if transpose else "pack_table",
    )(w_all)


def _peer(x, hb, a, l, b, r, u, vt, gf, final_norm):
    t = x.shape[0]
    tt = min(t, 512)
    eb = 16 * PEER_KEYS
    rows = pl.BlockSpec((tt, D_MODEL), lambda i, e: (i, 0))
    i1_rows = pl.BlockSpec((PEER_HEADS, eb // PEER_KEYS, tt), lambda i, e: (0, e, i))
    all_rows = pl.BlockSpec((PEER_HEADS, PEER_KEYS, tt), lambda i, e: (0, 0, i))
    return pl.pallas_call(
        functools.partial(_peer_kernel, final_norm=final_norm),
        grid=(t // tt, N_EXPERTS // eb),
        in_specs=[rows, pl.BlockSpec((D_MODEL, tt), lambda i, e: (0, i)),
                  i1_rows, i1_rows, all_rows, all_rows,
                  pl.BlockSpec((eb // 2, D_MODEL), lambda i, e: (e, 0)),
                  pl.BlockSpec((D_MODEL // 2, eb), lambda i, e: (0, e)),
                  pl.BlockSpec((1, D_MODEL), lambda i, e: (0, 0))],
        out_specs=rows,
        out_shape=jax.ShapeDtypeStruct((t, D_MODEL), F32),
        scratch_shapes=[pltpu.VMEM((D_MODEL, tt), F32), pltpu.VMEM((eb, tt), BF16)],
        compiler_params=_cparams(("parallel", "arbitrary")),
        name="peer_dense",
    )(x, hb, a, l, b, r, u, vt, gf)


def kernel(x, mem, positions, ln_mix, w_in, lb_param, hgrn_norm, swa_sinks, w_br_hgrn, w_br_ret,
           w_br_swa, w_out, ln_xq, ln_xkv, w_xq, w_xk, w_xv, w_xo, ln_ffn, peer_wq, peer_keys,
           peer_u, peer_v, ln_final):
    b, t, d = x.shape
    assert b == 1 and d == D_MODEL
    depth = w_in.shape[0]
    xs = x.reshape(t, d)
    ms = mem.reshape(mem.shape[1], d)
    cos_r, sin_r, cos_s, sin_s = _rope_tables(positions)
    lb_sm = jax.nn.softmax(lb_param.astype(F32), axis=0)
    lower = jnp.cumsum(lb_sm, axis=0) - lb_sm[0]

    for l in range(depth):
        proj, hf = _proj(xs, ln_mix[l][None, :], _relay_w_in(w_in, l))
        oa = _hgrn(proj, hf, lower[l][None, :])
        ob = _ret(proj, cos_r, sin_r)
        oc = _swa(proj, cos_s, sin_s, swa_sinks[l])
        xs = _merge(xs, proj, oa, ob, oc, hgrn_norm[l][None, :], w_br_hgrn[l].astype(BF16),
                    w_br_ret[l].astype(BF16), w_br_swa[l].astype(BF16), w_out[l].astype(BF16))

        km, vm = _memkv(ms, ln_xkv[l][None, :], w_xk[l].astype(BF16), w_xv[l].astype(BF16))
        xs = _xattn(xs, ln_xq[l][None, :], w_xq[l].astype(BF16), km, vm, w_xo[l].astype(BF16))

        k_hi, k_lo = _split2(peer_keys[l].reshape(2 * PEER_HEADS, PEER_KEYS, PEER_HALF))
        hb, a, bz, r2, cnt = _router(xs, ln_ffn[l][None, :], peer_wq[l].T.astype(BF16), k_hi, k_lo)
        c = 2.0 ** -0.5
        xs = _peer(xs, hb, a, cnt, bz, r2, _pack_table(peer_u, l, c, False), _pack_table(peer_v, l, c, True),
                   ln_final[None, :], final_norm=(l == depth - 1))

    return xs.reshape(b, t, d)
```

```python
import functools
import math

import numpy as np
import jax
import jax.numpy as jnp
from jax import lax
from jax.experimental import pallas as pl
from jax.experimental.pallas import tpu as pltpu

F32 = jnp.float32
BF16 = jnp.bfloat16

D_MODEL = 1024
HEAD_DIM = 128
N_HEADS = 4
SWA_Q_HEADS = 8
SWA_KV_HEADS = 2
SWA_DIM = 64
SWA_BLOCK = 128
ROPE_THETA = 10000.0
X_HEADS = 4
X_DIM = D_MODEL // X_HEADS
PEER_HEADS = 8
PEER_KEYS = 128
PEER_TOPK = 16
PEER_HALF = 128
N_EXPERTS = PEER_KEYS * PEER_KEYS
EPS = 1e-6
NEG_BIG = -1e30
TINY = 1e-30

LANE = 128
HG_CHUNK = 64
RET_CHUNK = 256
PEER_PIECE_KEYS = 4
VMEM_LIMIT = 56 * 1024 * 1024

CB_GATE = 0
CB_HQ, CB_HF, CB_HI, CB_HG = 24, 28, 32, 36
CB_RQ, CB_RK, CB_RV, CB_RG = 40, 44, 48, 52
CB_SQ, CB_SQR, CB_SK, CB_SKR, CB_SVA, CB_SVB = 56, 60, 64, 66, 68, 70
N_CB = 72
PROJ_COLS = N_CB * LANE


def _cparams(sem):
    return pltpu.CompilerParams(dimension_semantics=sem, vmem_limit_bytes=VMEM_LIMIT)


def _dot(a, b):
    return jnp.dot(a, b, preferred_element_type=F32)


def _dot_nt(a, b):
    return lax.dot_general(a, b, (((1,), (1,)), ((), ())), preferred_element_type=F32)


def _dot_tn(a, b):
    return lax.dot_general(a, b, (((0,), (0,)), ((), ())), preferred_element_type=F32)


def _split2(x):
    hi = x.astype(BF16)
    lo = (x - hi.astype(F32)).astype(BF16)
    return hi, lo


def _sigmoid(x):
    return 1.0 / (1.0 + jnp.exp(-x))


def _rms(x):
    return x * lax.rsqrt(jnp.mean(x * x, axis=-1, keepdims=True) + EPS)


def _rope_kernel(pos_ref, f_ref, cr_ref, sr_ref, cs_ref, ss_ref):
    ang = pos_ref[...].astype(F32) * f_ref[...]
    c, s = jnp.cos(ang), jnp.sin(ang)
    lane = lax.broadcasted_iota(jnp.int32, c.shape, 1)
    half, quarter = LANE // 2, LANE // 4

    def retention(x, first):
        return jnp.where(lane < half, first * x, pltpu.roll(x, half, 1))

    def window(x):
        y = pltpu.roll(x, half, 1)
        y = jnp.where(lane < quarter, y, pltpu.roll(y, quarter, 1))
        return jnp.where(lane < half, y, pltpu.roll(y, half, 1))

    cr_ref[...] = retention(c, 1.0)
    sr_ref[...] = retention(s, -1.0)
    cs_ref[...] = window(c)
    ss_ref[...] = window(s)


def _rope_tables(positions):
    t = positions.shape[1]
    tt = min(t, 1024)
    ret_f = ROPE_THETA ** (-jnp.linspace(0.0, 1.0, HEAD_DIM // 2, dtype=F32))
    swa_f = ROPE_THETA ** (-jnp.arange(0, SWA_DIM, 2, dtype=F32) / SWA_DIM)
    freqs = jnp.concatenate([ret_f, swa_f, jnp.zeros((LANE - HEAD_DIM // 2 - SWA_DIM // 2,), F32)])[None, :]
    pos = positions.reshape(t, 1)
    tab = jax.ShapeDtypeStruct((t, LANE), F32)
    row = pl.BlockSpec((tt, LANE), lambda i: (i, 0))
    return pl.pallas_call(
        _rope_kernel,
        grid=(t // tt,),
        in_specs=[pl.BlockSpec((tt, 1), lambda i: (i, 0)), pl.BlockSpec((1, LANE), lambda i: (0, 0))],
        out_specs=[row, row, row, row],
        out_shape=[tab, tab, tab, tab],
        compiler_params=_cparams(("parallel",)),
        name="rope_tables",
    )(pos, freqs)


PROJ_TN = 1536


def _proj_kernel(x_ref, g_ref, w_ref, o_ref, f_ref, h_scr):
    @pl.when(pl.program_id(1) == 0)
    def _():
        h_scr[...] = (_rms(x_ref[...]) * g_ref[...]).astype(BF16)

    y = _dot(h_scr[...], w_ref[...])
    o_ref[...] = y.astype(BF16)

    @pl.when(pl.program_id(1) == CB_HF * LANE // PROJ_TN)
    def _():
        lo = CB_HF * LANE % PROJ_TN
        f_ref[...] = y[:, lo:lo + N_HEADS * HEAD_DIM]


def _proj(x, g, w):
    t = x.shape[0]
    n = w.shape[1]
    tm = min(t, 2048)
    tn = PROJ_TN
    wf = N_HEADS * HEAD_DIM
    return pl.pallas_call(
        _proj_kernel,
        grid=(t // tm, n // tn),
        in_specs=[pl.BlockSpec((tm, D_MODEL), lambda i, j: (i, 0)),
                  pl.BlockSpec((1, D_MODEL), lambda i, j: (0, 0)),
                  pl.BlockSpec((D_MODEL, tn), lambda i, j: (0, j))],
        out_specs=[pl.BlockSpec((tm, tn), lambda i, j: (i, j)),
                   pl.BlockSpec((tm, wf), lambda i, j: (i, 0))],
        out_shape=[jax.ShapeDtypeStruct((t, n), BF16), jax.ShapeDtypeStruct((t, wf), F32)],
        scratch_shapes=[pltpu.VMEM((tm, D_MODEL), BF16)],
        compiler_params=_cparams(("parallel", "arbitrary")),
        name="in_proj",
    )(x, g, w)


def _relay_plan():
    names = (["hq"] * 4 + ["hf"] * 4 + ["hi"] * 4 + ["hg"] * 4 + ["rq"] * 4 + ["rk"] * 4 + ["rv"] * 4
             + ["rg"] * 4 + ["sq"] * 4 + ["sk", "sv"] + ["gate"] * 24)
    first = {}
    for i, n in enumerate(names):
        first.setdefault(n, i)
    plan = [(first["gate"] + i, 0, 0) for i in range(24)]
    for n in ("hq", "hf", "hi", "hg", "rq", "rk", "rv", "rg", "sq"):
        plan += [(first[n] + i, 0, 0) for i in range(4)]
    plan += [(first["sq"] + i, 1, 0) for i in range(4)]
    plan += [(first["sk"], 0, 1), (first["sk"], 0, 2)]
    plan += [(first["sk"], 1, 1), (first["sk"], 1, 2)]
    plan += [(first["sv"], 0, 3), (first["sv"], 0, 4)]
    plan += [(first["sv"], 0, 5), (first["sv"], 0, 6)]
    assert len(plan) == N_CB
    return plan


def _relay_kernel(src_ref, rot_ref, put_ref, w_ref, o_ref):
    j = pl.program_id(0)
    x = w_ref[...]
    lane = lax.broadcasted_iota(jnp.int32, x.shape, 1)
    swapped = pltpu.roll(x, SWA_DIM, 1)
    first_half = lane % SWA_DIM < SWA_DIM // 2
    rotated = jnp.where(first_half, -pltpu.roll(x, LANE - SWA_DIM // 2, 1), pltpu.roll(x, SWA_DIM // 2, 1))
    rotated_swapped = pltpu.roll(rotated, SWA_DIM, 1)
    rot = rot_ref[j] == 1
    y = jnp.where(rot, rotated, x)
    ys = jnp.where(rot, rotated_swapped, swapped)
    low = lane < SWA_DIM
    put = put_ref[j]
    out = y
    out = jnp.where(put == 1, jnp.where(low, y, ys), out)
    out = jnp.where(put == 2, jnp.where(low, ys, y), out)
    out = jnp.where(put == 3, jnp.where(low, y, 0.0), out)
    out = jnp.where(put == 4, jnp.where(low, ys, 0.0), out)
    out = jnp.where(put == 5, jnp.where(low, 0.0, ys), out)
    out = jnp.where(put == 6, jnp.where(low, 0.0, y), out)
    o_ref[...] = out.astype(BF16)


def _relay_w_in(w_in, layer):
    d = w_in.shape[1]
    plan = np.asarray(_relay_plan(), np.int32)
    grid_spec = pltpu.PrefetchScalarGridSpec(
        num_scalar_prefetch=3,
        grid=(N_CB,),
        in_specs=[pl.BlockSpec((None, d, LANE), lambda j, src, rot, put: (layer, 0, src[j]))],
        out_specs=pl.BlockSpec((d, LANE), lambda j, src, rot, put: (0, j)),
    )
    return pl.pallas_call(
        _relay_kernel,
        grid_spec=grid_spec,
        out_shape=jax.ShapeDtypeStruct((d, PROJ_COLS), BF16),
        compiler_params=_cparams(("arbitrary",)),
        name="relay_w_in",
    )(jnp.asarray(plan[:, 0]), jnp.asarray(plan[:, 1]), jnp.asarray(plan[:, 2]), w_in)


def _hgrn_kernel(hq_ref, hf_ref, hi_ref, lb_ref, o_ref, st_ref, *, n_chunks):
    @pl.when(pl.program_id(0) == 0)
    def _():
        st_ref[...] = jnp.zeros_like(st_ref)

    c = HG_CHUNK
    row = lax.broadcasted_iota(jnp.int32, (c, c), 0)
    col = lax.broadcasted_iota(jnp.int32, (c, c), 1)
    diag = row == col
    halves = [c >> (i + 1) for i in range(c.bit_length() - 1)]
    pair_mask = {b: (row // (2 * b) == col // (2 * b)) & (row % (2 * b) >= b) & (col % (2 * b) < b)
                 for b in halves}
    rw = lax.broadcasted_iota(jnp.int32, (c, HEAD_DIM), 0)
    r3 = lax.broadcasted_iota(jnp.int32, (c // 8, 8, HEAD_DIM), 1)
    scale = HEAD_DIM ** -0.5
    log2e = math.log2(math.e)

    def boundary(cum2, b):
        if b >= 8:
            ends = [cum2[s + b - 1:s + b, :] for s in range(0, c, 2 * b)]
            out = ends[-1]
            for i in range(len(ends) - 2, -1, -1):
                out = jnp.where(rw < (i + 1) * 2 * b, ends[i], out)
            return out
        c3 = cum2.reshape(c // 8, 8, HEAD_DIM)
        if b == 4:
            return jnp.broadcast_to(c3[:, 3:4, :], c3.shape).reshape(c, HEAD_DIM)
        if b == 2:
            lo = jnp.broadcast_to(c3[:, 1:2, :], c3.shape)
            hi = jnp.broadcast_to(c3[:, 5:6, :], c3.shape)
            return jnp.where(r3 < 4, lo, hi).reshape(c, HEAD_DIM)
        prev = pltpu.roll(cum2, 1, 0)
        return jnp.where(rw % 2 == 1, prev, cum2)

    def chunk_body(ci, carry):
        r0 = pl.multiple_of(ci * c, c)
        for h in range(N_HEADS):
            lanes = slice(h * HEAD_DIM, (h + 1) * HEAD_DIM)
            xq = hq_ref[pl.ds(r0, c), lanes].astype(F32)
            z = hf_ref[pl.ds(r0, c), lanes]
            v = hi_ref[pl.ds(r0, c), lanes]
            lb = lb_ref[:, lanes]
            omlb = 1.0 - lb
            q = xq * _sigmoid(xq) * scale
            f = lb + omlb * _sigmoid(z)
            logf = jnp.log(jnp.maximum(f, TINY))
            kk = omlb * _sigmoid(-z)
            cum2 = logf * log2e
            step = 1
            while step < c:
                cum2 = cum2 + jnp.where(rw >= step, pltpu.roll(cum2, step, 0), 0.0)
                step *= 2
            scores = jnp.where(diag, jnp.sum(q * kk, axis=-1, keepdims=True), 0.0)
            for b in halves:
                ref = boundary(cum2, b)
                qs = (q * jnp.exp2(jnp.minimum(cum2 - ref, 0.0))).astype(BF16)
                ks = (kk * jnp.exp2(jnp.minimum(ref - cum2, 0.0))).astype(BF16)
                scores = jnp.where(pair_mask[b], _dot_nt(qs, ks), scores)
            st = st_ref[h]
            qe = (q * jnp.exp2(cum2)).astype(BF16)
            o = _dot_nt(qe, st.astype(BF16)) + _dot(scores.astype(BF16), v.astype(BF16))
            o_ref[pl.ds(r0, c), lanes] = o.astype(BF16)
            last = cum2[c - 1:c, :]
            kl = (kk * jnp.exp2(last - cum2)).astype(BF16)
            st_ref[h] = jnp.exp2(last) * st + _dot_tn(v.astype(BF16), kl)
        return carry

    lax.fori_loop(0, n_chunks, chunk_body, 0, unroll=4)


def _hgrn(proj, hf, lower):
    t = proj.shape[0]
    tb = min(t, 512)
    w = N_HEADS * HEAD_DIM

    def cols(cb):
        return pl.BlockSpec((tb, w), lambda i: (i, cb // N_HEADS))

    return pl.pallas_call(
        functools.partial(_hgrn_kernel, n_chunks=tb // HG_CHUNK),
        grid=(t // tb,),
        in_specs=[cols(CB_HQ), pl.BlockSpec((tb, w), lambda i: (i, 0)), cols(CB_HI),
                  pl.BlockSpec((1, w), lambda i: (0, 0))],
        out_specs=pl.BlockSpec((tb, w), lambda i: (i, 0)),
        out_shape=jax.ShapeDtypeStruct((t, w), BF16),
        scratch_shapes=[pltpu.VMEM((N_HEADS, HEAD_DIM, HEAD_DIM), F32)],
        compiler_params=_cparams(("arbitrary",)),
        name="hgrn2",
    )(proj, hf, proj, lower)


def _ret_kernel(q_ref, k_ref, v_ref, cos_ref, sin_ref,
                dec_ref, qd_ref, kd_ref, cd_ref, o_ref, s_ref):
    @pl.when(pl.program_id(0) == 0)
    def _():
        s_ref[...] = jnp.zeros_like(s_ref)

    c = dec_ref.shape[1]
    scale = HEAD_DIM ** -0.5
    for ci in range(q_ref.shape[0] // c):
        rows = slice(ci * c, (ci + 1) * c)
        cos = cos_ref[rows, :]
        sin = sin_ref[rows, :]
        for h in range(N_HEADS):
            lanes = slice(h * HEAD_DIM, (h + 1) * HEAD_DIM)
            q = q_ref[rows, lanes].astype(F32)
            k = k_ref[rows, lanes].astype(F32)
            q = q * cos + pltpu.roll(q, HEAD_DIM // 2, 1) * sin
            k = (k * cos + pltpu.roll(k, HEAD_DIM // 2, 1) * sin) * scale
            v = v_ref[rows, lanes]
            qb = q.astype(BF16)
            s = s_ref[h]
            scores = _dot_nt(qb, k.astype(BF16)) * dec_ref[h]
            o = _dot(qb, s.astype(BF16)) * qd_ref[h] + _dot(scores.astype(BF16), v)
            o_ref[rows, lanes] = o.astype(BF16)
            s_ref[h] = cd_ref[h] * s + _dot_tn((k * kd_ref[h]).astype(BF16), v)


def _ret(proj, cos_r, sin_r):
    t = proj.shape[0]
    c = min(t, RET_CHUNK)
    tb = 4 * c if t % (4 * c) == 0 else c
    w = N_HEADS * HEAD_DIM
    log_gamma = jnp.log(1.0 - 2.0 ** (-5.0 - jnp.arange(N_HEADS, dtype=F32)))
    idx = jnp.arange(c, dtype=F32)
    rel = idx[:, None] - idx[None, :]
    decay = jnp.exp(jnp.where(rel >= 0, log_gamma[:, None, None] * rel, NEG_BIG))
    ones = jnp.ones((1, 1, HEAD_DIM), F32)
    q_decay = jnp.exp(log_gamma[:, None] * (idx + 1.0))[:, :, None] * ones
    k_decay = jnp.exp(log_gamma[:, None] * (c - 1.0 - idx))[:, :, None] * ones
    c_decay = jnp.exp(log_gamma * c)[:, None, None] * ones

    def cols(cb):
        return pl.BlockSpec((tb, w), lambda i: (i, cb // N_HEADS))

    def const(shape):
        return pl.BlockSpec(shape, lambda i: (0, 0, 0))

    tab = pl.BlockSpec((tb, LANE), lambda i: (i, 0))
    return pl.pallas_call(
        _ret_kernel,
        grid=(t // tb,),
        in_specs=[cols(CB_RQ), cols(CB_RK), cols(CB_RV), tab, tab,
                  const((N_HEADS, c, c)), const((N_HEADS, c, HEAD_DIM)),
                  const((N_HEADS, c, HEAD_DIM)), const((N_HEADS, 1, HEAD_DIM))],
        out_specs=pl.BlockSpec((tb, w), lambda i: (i, 0)),
        out_shape=jax.ShapeDtypeStruct((t, w), BF16),
        scratch_shapes=[pltpu.VMEM((N_HEADS, HEAD_DIM, HEAD_DIM), F32)],
        compiler_params=_cparams(("arbitrary",)),
        name="retention",
    )(proj, proj, proj, cos_r, sin_r, decay, q_decay, k_decay, c_decay)


def _swa_kernel(sink_ref, q_ref, qr_ref, kc_ref, krc_ref, vac_ref, vbc_ref,
                kp_ref, krp_ref, vap_ref, vbp_ref, cc_ref, sc_ref, cp_ref, sp_ref, o_ref):
    b = SWA_BLOCK
    nb = q_ref.shape[0] // b
    first = pl.program_id(0) * nb
    cos_c, sin_c = cc_ref[...], sc_ref[...]
    cos_p, sin_p = cp_ref[...], sp_ref[...]
    qi = lax.broadcasted_iota(jnp.int32, (b, 2 * b), 0) + b
    ki = lax.broadcasted_iota(jnp.int32, (b, 2 * b), 1)
    rel = qi - ki
    in_window = (rel >= 0) & (rel < b)
    low = lax.broadcasted_iota(jnp.int32, (b, LANE), 1) < SWA_DIM
    pairs_per_group = SWA_Q_HEADS // SWA_KV_HEADS // 2
    for g in range(SWA_KV_HEADS):
        kl = slice(g * LANE, (g + 1) * LANE)
        k_cur = kc_ref[:, kl].astype(F32) * cos_c + krc_ref[:, kl].astype(F32) * sin_c
        k_prev = kp_ref[:, kl].astype(F32) * cos_p + krp_ref[:, kl].astype(F32) * sin_p
        k_all = jnp.concatenate([k_prev, k_cur], axis=0).astype(BF16)
        v_lo = jnp.concatenate([vap_ref[:, kl], vac_ref[:, kl]], axis=0)
        v_hi = jnp.concatenate([vbp_ref[:, kl], vbc_ref[:, kl]], axis=0)
        for sb in range(nb):
            win = slice(sb * b, (sb + 2) * b)
            rows = slice(sb * b, (sb + 1) * b)
            keep = in_window & ((first + sb > 0) | (ki >= b))
            kw = k_all[win]
            for j in range(pairs_per_group):
                pair = g * pairs_per_group + j
                ql = slice(pair * LANE, (pair + 1) * LANE)
                q = (q_ref[rows, ql].astype(F32) * cos_c[rows] + qr_ref[rows, ql].astype(F32) * sin_c[rows])
                out = None
                for half, vw in ((0, v_lo[win]), (1, v_hi[win])):
                    qh = jnp.where(low if half == 0 else ~low, q, 0.0).astype(BF16)
                    s = _dot_nt(qh, kw) * (SWA_DIM ** -0.5)
                    s = jnp.where(keep, s, NEG_BIG)
                    sink = sink_ref[2 * pair + half]
                    m = jnp.maximum(jnp.max(s, axis=-1, keepdims=True), sink)
                    p = jnp.exp(s - m)
                    denom = jnp.sum(p, axis=-1, keepdims=True) + jnp.exp(sink - m)
                    o = _dot(p.astype(BF16), vw) / denom
                    out = o if out is None else out + o
                o_ref[rows, ql] = out.astype(BF16)


def _swa(proj, cos_s, sin_s, sinks):
    t = proj.shape[0]
    b = SWA_BLOCK
    nb = 4 if t % (4 * b) == 0 else 1
    qw = SWA_Q_HEADS * SWA_DIM
    kw = SWA_KV_HEADS * LANE

    def cur(cb, width):
        return pl.BlockSpec((nb * b, width), lambda i: (i, cb * LANE // width))

    def prev(cb, width):
        return pl.BlockSpec((b, width), lambda i: (jnp.maximum(nb * i - 1, 0), cb * LANE // width))

    tab_c = pl.BlockSpec((nb * b, LANE), lambda i: (i, 0))
    tab_p = pl.BlockSpec((b, LANE), lambda i: (jnp.maximum(nb * i - 1, 0), 0))
    return pl.pallas_call(
        _swa_kernel,
        grid=(t // (nb * b),),
        in_specs=[pl.BlockSpec(memory_space=pltpu.SMEM),
                  cur(CB_SQ, qw), cur(CB_SQR, qw),
                  cur(CB_SK, kw), cur(CB_SKR, kw), cur(CB_SVA, kw), cur(CB_SVB, kw),
                  prev(CB_SK, kw), prev(CB_SKR, kw), prev(CB_SVA, kw), prev(CB_SVB, kw),
                  tab_c, tab_c, tab_p, tab_p],
        out_specs=pl.BlockSpec((nb * b, qw), lambda i: (i, 0)),
        out_shape=jax.ShapeDtypeStruct((t, qw), BF16),
        compiler_params=_cparams(("parallel",)),
        name="swa",
    )(sinks, proj, proj, proj, proj, proj, proj, proj, proj, proj, proj, cos_s, sin_s, cos_s, sin_s)


def _merge_kernel(x_ref, oa_ref, hg_ref, ob_ref, rg_ref, oc_ref, ga_ref, gb_ref, gc_ref,
                  hn_ref, wa_ref, wb_ref, wc_ref, wo_ref, o_ref):
    hg = hg_ref[...].astype(F32)
    a = _rms(oa_ref[...].astype(F32)) * hn_ref[...] * (hg * _sigmoid(hg))
    ya = _dot(a.astype(BF16), wa_ref[...])
    rg = rg_ref[...].astype(F32)
    ob = ob_ref[...].astype(F32)
    parts = []
    for h in range(N_HEADS):
        lanes = slice(h * HEAD_DIM, (h + 1) * HEAD_DIM)
        parts.append(_rms(ob[:, lanes]))
    bn = jnp.concatenate(parts, axis=-1) * (rg * _sigmoid(rg))
    yb = _dot(bn.astype(BF16), wb_ref[...])
    yc = _dot(oc_ref[...], wc_ref[...])
    mixed = (_sigmoid(ga_ref[...].astype(F32)) * ya + _sigmoid(gb_ref[...].astype(F32)) * yb
             + _sigmoid(gc_ref[...].astype(F32)) * yc)
    o_ref[...] = x_ref[...] + _dot(mixed.astype(BF16), wo_ref[...])


def _merge(x, proj, oa, ob, oc, hn, wa, wb, wc, wo):
    t = x.shape[0]
    tt = min(t, 1024)
    w = N_HEADS * HEAD_DIM
    qw = SWA_Q_HEADS * SWA_DIM

    def rows(width, cb=0):
        return pl.BlockSpec((tt, width), lambda i: (i, cb * LANE // width))

    def full(shape):
        return pl.BlockSpec(shape, lambda i: (0, 0))

    return pl.pallas_call(
        _merge_kernel,
        grid=(t // tt,),
        in_specs=[rows(D_MODEL), rows(w), rows(w, CB_HG), rows(w), rows(w, CB_RG), rows(qw),
                  rows(D_MODEL, CB_GATE), rows(D_MODEL, CB_GATE + 8), rows(D_MODEL, CB_GATE + 16),
                  full((1, w)), full((w, D_MODEL)), full((w, D_MODEL)), full((qw, D_MODEL)),
                  full((D_MODEL, D_MODEL))],
        out_specs=rows(D_MODEL),
        out_shape=jax.ShapeDtypeStruct((t, D_MODEL), F32),
        compiler_params=_cparams(("parallel",)),
        name="merge",
    )(x, oa, proj, ob, proj, oc, proj, proj, proj, hn, wa, wb, wc, wo)


def _memkv_kernel(m_ref, g_ref, wk_ref, wv_ref, k_ref, v_ref):
    h = (_rms(m_ref[...]) * g_ref[...]).astype(BF16)
    k_ref[...] = _dot(h, wk_ref[...]).astype(BF16)
    v_ref[...] = _dot(h, wv_ref[...]).astype(BF16)


def _memkv(mem, g, wk, wv):
    nm = mem.shape[0]
    out = jax.ShapeDtypeStruct((nm, D_MODEL), BF16)
    return pl.pallas_call(
        _memkv_kernel,
        out_shape=[out, out],
        compiler_params=pltpu.CompilerParams(vmem_limit_bytes=VMEM_LIMIT),
        name="mem_kv",
    )(mem, g, wk, wv)


def _xattn_kernel(x_ref, g_ref, wq_ref, k_ref, v_ref, wo_ref, o_ref):
    x = x_ref[...]
    h = (_rms(x) * g_ref[...]).astype(BF16)
    q = _dot(h, wq_ref[...])
    outs = []
    for hh in range(X_HEADS):
        lanes = slice(hh * X_DIM, (hh + 1) * X_DIM)
        s = _dot_nt(q[:, lanes].astype(BF16), k_ref[:, lanes]) * (X_DIM ** -0.5)
        m = jnp.max(s, axis=-1, keepdims=True)
        p = jnp.exp(s - m)
        p = p / jnp.sum(p, axis=-1, keepdims=True)
        outs.append(_dot(p.astype(BF16), v_ref[:, lanes]))
    o = jnp.concatenate(outs, axis=-1)
    o_ref[...] = x + _dot(o.astype(BF16), wo_ref[...])


def _xattn(x, g, wq, k, v, wo):
    t = x.shape[0]
    tt = min(t, 1024)
    nm = k.shape[0]

    def full(shape):
        return pl.BlockSpec(shape, lambda i: (0, 0))

    rows = pl.BlockSpec((tt, D_MODEL), lambda i: (i, 0))
    return pl.pallas_call(
        _xattn_kernel,
        grid=(t // tt,),
        in_specs=[rows, full((1, D_MODEL)), full((D_MODEL, D_MODEL)),
                  full((nm, D_MODEL)), full((nm, D_MODEL)), full((D_MODEL, D_MODEL))],
        out_specs=rows,
        out_shape=jax.ShapeDtypeStruct((t, D_MODEL), F32),
        compiler_params=_cparams(("parallel",)),
        name="xattn",
    )(x, g, wq, k, v, wo)


_CAND_ROWS = [PEER_TOPK // (k1 + 1) for k1 in range(8)]
_RANK_BASE = 3.0e38
_RANK_STEP = 3.0e33


def _top_values(s, n, with_rank=False):
    vals = []
    for i in range(n):
        m = jnp.max(s, axis=0, keepdims=True)
        vals.append(m)
        s = jnp.where(s >= m, -(_RANK_BASE + i * _RANK_STEP), s)
    if not with_rank:
        return vals
    rank = jnp.where(s <= -_RANK_BASE, jnp.floor((-s - _RANK_BASE) * (1.0 / _RANK_STEP) + 0.5), float(n))
    return vals, rank


def _bf16_pair_bits(x):
    u = lax.bitcast_convert_type(x.astype(BF16).astype(F32), jnp.uint32)
    return u | (u >> 16)


def _stack_rows(rows, n):
    tt = rows[0].shape[1]
    r = lax.broadcasted_iota(jnp.int32, (n, tt), 0)
    out = jnp.zeros((n, tt), F32)
    for k in range(n):
        out = jnp.where(r == k, rows[k], out)
    return out


def _router_kernel(x_ref, g_ref, wq_ref, kh_ref, kl_ref,
                   hb_ref, a_ref, b_ref, r_ref, l_ref, qt_scr):
    h = _rms(x_ref[...]) * g_ref[...]
    hb_ref[...] = h.T.astype(BF16)
    qt_scr[...] = _dot_nt(wq_ref[...], h.astype(BF16))
    tt = x_ref.shape[0]
    k = PEER_TOPK
    r8 = lax.broadcasted_iota(jnp.int32, (8, tt), 0)

    def head_body(hh, carry):
        scores = []
        for p in range(2):
            r0 = pl.multiple_of(hh * (2 * PEER_HALF) + p * PEER_HALF, PEER_HALF)
            q_hi, q_lo = _split2(qt_scr[pl.ds(r0, PEER_HALF), :])
            kh = kh_ref[p * PEER_HEADS + hh]
            scores.append(_dot(kh, q_hi) + _dot(kl_ref[p * PEER_HEADS + hh], q_hi) + _dot(kh, q_lo))
        s1, s2 = scores
        v1, rank1 = _top_values(s1, k, with_rank=True)
        v2, rank2 = _top_values(s2, k, with_rank=True)
        sv1 = _stack_rows(v1, k)
        sv2 = _stack_rows(v2, k)
        groups = [v1[0] + sv2]
        for k1 in range(1, 8):
            groups.append(jnp.where(r8 < _CAND_ROWS[k1], v1[k1] + sv2[0:8], -jnp.inf))
        groups.append(sv1[8:16] + v2[0])
        cand = jnp.concatenate(groups, axis=0)
        tau = _top_values(cand, k)[k - 1]
        m0 = v1[0] + v2[0]
        z = jnp.sum(jnp.where(cand >= tau, jnp.exp(cand - m0), 0.0), axis=0, keepdims=True)
        a_ref[hh] = _bf16_pair_bits(jnp.exp(s1 - v1[0]))
        b_ref[hh] = (jnp.exp(s2 - v2[0]) / z).astype(BF16)
        count1 = jnp.zeros_like(s1)
        for k1 in range(k):
            n_k1 = jnp.sum((v1[k1] + sv2 >= tau).astype(F32), axis=0, keepdims=True)
            count1 = jnp.where(rank1 == float(k1), n_k1, count1)
        r_ref[hh] = rank2.astype(BF16)
        l_ref[hh] = _bf16_pair_bits(count1)
        return carry

    lax.fori_loop(0, PEER_HEADS, head_body, 0, unroll=2)


def _router(x, g, wq_t, k_hi, k_lo):
    t = x.shape[0]
    tt = min(t, 512)
    nq = wq_t.shape[0]
    sel = jax.ShapeDtypeStruct((PEER_HEADS, PEER_KEYS, t), jnp.uint32)
    sel_b = jax.ShapeDtypeStruct((PEER_HEADS, PEER_KEYS, t), BF16)
    sel_spec = pl.BlockSpec((PEER_HEADS, PEER_KEYS, tt), lambda i: (0, 0, i))
    return pl.pallas_call(
        _router_kernel,
        grid=(t // tt,),
        in_specs=[pl.BlockSpec((tt, D_MODEL), lambda i: (i, 0)),
                  pl.BlockSpec((1, D_MODEL), lambda i: (0, 0)),
                  pl.BlockSpec((nq, D_MODEL), lambda i: (0, 0)),
                  pl.BlockSpec((2 * PEER_HEADS, PEER_KEYS, PEER_HALF), lambda i: (0, 0, 0)),
                  pl.BlockSpec((2 * PEER_HEADS, PEER_KEYS, PEER_HALF), lambda i: (0, 0, 0))],
        out_specs=[pl.BlockSpec((D_MODEL, tt), lambda i: (0, i)), sel_spec, sel_spec, sel_spec, sel_spec],
        out_shape=[jax.ShapeDtypeStruct((D_MODEL, t), BF16), sel, sel_b, sel_b, sel],
        scratch_shapes=[pltpu.VMEM((nq, tt), F32)],
        compiler_params=_cparams(("parallel",)),
        name="peer_router",
    )(x, g, wq_t, k_hi, k_lo)


def _peer_kernel(x_ref, hb_ref, a_ref, l_ref, b_ref, r_ref, u_ref, vt_ref, gf_ref, o_ref,
                 acc_ref, ga_ref, *, final_norm):
    e = pl.program_id(1)

    @pl.when(e == 0)
    def _():
        acc_ref[...] = jnp.zeros_like(acc_ref)

    u_blk = pltpu.bitcast(u_ref[...], BF16)
    vt_blk = pltpu.bitcast(vt_ref[...], BF16)
    n_i1 = u_blk.shape[0] // PEER_KEYS
    sub = 16
    tt = hb_ref.shape[1]
    shape3 = (PEER_KEYS // sub, sub, tt)
    zero3 = jnp.zeros(shape3, BF16)
    piece = PEER_PIECE_KEYS * PEER_KEYS
    for p in range(n_i1 // PEER_PIECE_KEYS):
        for j in range(p * PEER_PIECE_KEYS, (p + 1) * PEER_PIECE_KEYS):
            gate = zero3
            for hh in range(PEER_HEADS):
                arow = pltpu.bitcast(jnp.broadcast_to(a_ref[hh, j:j + 1, :], (8, tt)), BF16)
                lrow = pltpu.bitcast(jnp.broadcast_to(l_ref[hh, j:j + 1, :], (8, tt)), BF16)
                b3 = b_ref[hh].reshape(shape3)
                r3 = r_ref[hh].reshape(shape3)
                gate = gate + arow[None] * jnp.where(r3 < lrow[None], b3, zero3)
            ga_ref[j * PEER_KEYS:(j + 1) * PEER_KEYS, :] = gate.reshape(PEER_KEYS, tt)
        rows = slice(p * piece, (p + 1) * piece)
        pre = _dot(u_blk[rows, :], hb_ref[...])
        act = pre + pre * lax.erf(pre)
        ga_ref[rows, :] = ga_ref[rows, :] * act.astype(BF16)
    acc_ref[...] += _dot(vt_blk, ga_ref[...])

    @pl.when(e == pl.num_programs(1) - 1)
    def _():
        y = x_ref[...] + acc_ref[...].T
        if final_norm:
            y = _rms(y) * gf_ref[...]
        o_ref[...] = y


def _pack_kernel(w_ref, o_ref, *, scale, transpose):
    w = w_ref[...]
    if transpose:
        w = w.T
    o_ref[...] = pltpu.bitcast((w * scale).astype(BF16), jnp.uint32)


def _pack_table(w_all, layer, scale, transpose):
    _, rows, cols = w_all.shape
    if transpose:
        blk = 512
        in_spec = pl.BlockSpec((None, blk, cols), lambda i: (layer, i, 0))
        out_spec = pl.BlockSpec((cols // 2, blk), lambda i: (0, i))
        out_shape = jax.ShapeDtypeStruct((cols // 2, rows), jnp.uint32)
    else:
        blk = 1024
        in_spec = pl.BlockSpec((None, blk, cols), lambda i: (layer, i, 0))
        out_spec = pl.BlockSpec((blk // 2, cols), lambda i: (i, 0))
        out_shape = jax.ShapeDtypeStruct((rows // 2, cols), jnp.uint32)
    return pl.pallas_call(
        functools.partial(_pack_kernel, scale=scale, transpose=transpose),
        grid=(rows // blk,),
        in_specs=[in_spec],
        out_specs=out_spec,
        out_shape=out_shape,
        compiler_params=_cparams(("parallel",)),
        name="pack_table_t" if transpose else "pack_table",
    )(w_all)


def _peer(x, hb, a, l, b, r, u, vt, gf, final_norm):
    t = x.shape[0]
    tt = min(t, 512)
    eb = 16 * PEER_KEYS
    rows = pl.BlockSpec((tt, D_MODEL), lambda i, e: (i, 0))
    i1_rows = pl.BlockSpec((PEER_HEADS, eb // PEER_KEYS, tt), lambda i, e: (0, e, i))
    all_rows = pl.BlockSpec((PEER_HEADS, PEER_KEYS, tt), lambda i, e: (0, 0, i))
    return pl.pallas_call(
        functools.partial(_peer_kernel, final_norm=final_norm),
        grid=(t // tt, N_EXPERTS // eb),
        in_specs=[rows, pl.BlockSpec((D_MODEL, tt), lambda i, e: (0, i)),
                  i1_rows, i1_rows, all_rows, all_rows,
                  pl.BlockSpec((eb // 2, D_MODEL), lambda i, e: (e, 0)),
                  pl.BlockSpec((D_MODEL // 2, eb), lambda i, e: (0, e)),
                  pl.BlockSpec((1, D_MODEL), lambda i, e: (0, 0))],
        out_specs=rows,
        out_shape=jax.ShapeDtypeStruct((t, D_MODEL), F32),
        scratch_shapes=[pltpu.VMEM((D_MODEL, tt), F32), pltpu.VMEM((eb, tt), BF16)],
        compiler_params=_cparams(("parallel", "arbitrary")),
        name="peer_dense",
    )(x, hb, a, l, b, r, u, vt, gf)


def kernel(x, mem, positions, ln_mix, w_in, lb_param, hgrn_norm, swa_sinks, w_br_hgrn, w_br_ret,
           w_br_swa, w_out, ln_xq, ln_xkv, w_xq, w_xk, w_xv, w_xo, ln_ffn, peer_wq, peer_keys,
           peer_u, peer_v, ln_final):
    b, t, d = x.shape
    assert b == 1 and d == D_MODEL
    depth = w_in.shape[0]
    xs = x.reshape(t, d)
    ms = mem.reshape(mem.shape[1], d)
    cos_r, sin_r, cos_s, sin_s = _rope_tables(positions)
    lb_sm = jax.nn.softmax(lb_param.astype(F32), axis=0)
    lower = jnp.cumsum(lb_sm, axis=0) - lb_sm[0]

    for l in range(depth):
        proj, hf = _proj(xs, ln_mix[l][None, :], _relay_w_in(w_in, l))
        oa = _hgrn(proj, hf, lower[l][None, :])
        ob = _ret(proj, cos_r, sin_r)
        oc = _swa(proj, cos_s, sin_s, swa_sinks[l])
        xs = _merge(xs, proj, oa, ob, oc, hgrn_norm[l][None, :], w_br_hgrn[l].astype(BF16),
                    w_br_ret[l].astype(BF16), w_br_swa[l].astype(BF16), w_out[l].astype(BF16))

        km, vm = _memkv(ms, ln_xkv[l][None, :], w_xk[l].astype(BF16), w_xv[l].astype(BF16))
        xs = _xattn(xs, ln_xq[l][None, :], w_xq[l].astype(BF16), km, vm, w_xo[l].astype(BF16))

        k_hi, k_lo = _split2(peer_keys[l].reshape(2 * PEER_HEADS, PEER_KEYS, PEER_HALF))
        hb, a, bz, r2, cnt = _router(xs, ln_ffn[l][None, :], peer_wq[l].T.astype(BF16), k_hi, k_lo)
        c = 2.0 ** -0.5
        xs = _peer(xs, hb, a, cnt, bz, r2, _pack_table(peer_u, l, c, False), _pack_table(peer_v, l, c, True),
                   ln_final[None, :], final_norm=(l == depth - 1))

    return xs.reshape(b, t, d)
```

```python
import functools
import math

import numpy as np
import jax
import jax.numpy as jnp
from jax import lax
from jax.experimental import pallas as pl
from jax.experimental.pallas import tpu as pltpu

F32 = jnp.float32
BF16 = jnp.bfloat16

D_MODEL = 1024
HEAD_DIM = 128
N_HEADS = 4
SWA_Q_HEADS = 8
SWA_KV_HEADS = 2
SWA_DIM = 64
SWA_BLOCK = 128
ROPE_THETA = 10000.0
X_HEADS = 4
X_DIM = D_MODEL // X_HEADS
PEER_HEADS = 8
PEER_KEYS = 128
PEER_TOPK = 16
PEER_HALF = 128
N_EXPERTS = PEER_KEYS * PEER_KEYS
EPS = 1e-6
NEG_BIG = -1e30
TINY = 1e-30

LANE = 128
HG_CHUNK = 64
RET_CHUNK = 256
PEER_PIECE_KEYS = 4
VMEM_LIMIT = 56 * 1024 * 1024

CB_GATE = 0
CB_HQ, CB_HF, CB_HI, CB_HG = 24, 28, 32, 36
CB_RQ, CB_RK, CB_RV, CB_RG = 40, 44, 48, 52
CB_SQ, CB_SQR, CB_SK, CB_SKR, CB_SVA, CB_SVB = 56, 60, 64, 66, 68, 70
N_CB = 72
PROJ_COLS = N_CB * LANE


def _cparams(sem):
    return pltpu.CompilerParams(dimension_semantics=sem, vmem_limit_bytes=VMEM_LIMIT)


def _dot(a, b):
    return jnp.dot(a, b, preferred_element_type=F32)


def _dot_nt(a, b):
    return lax.dot_general(a, b, (((1,), (1,)), ((), ())), preferred_element_type=F32)


def _dot_tn(a, b):
    return lax.dot_general(a, b, (((0,), (0,)), ((), ())), preferred_element_type=F32)


def _split2(x):
    hi = x.astype(BF16)
    lo = (x - hi.astype(F32)).astype(BF16)
    return hi, lo


def _sigmoid(x):
    return 1.0 / (1.0 + jnp.exp(-x))


def _rms(x):
    return x * lax.rsqrt(jnp.mean(x * x, axis=-1, keepdims=True) + EPS)


def _rope_kernel(pos_ref, f_ref, cr_ref, sr_ref, cs_ref, ss_ref):
    ang = pos_ref[...].astype(F32) * f_ref[...]
    c, s = jnp.cos(ang), jnp.sin(ang)
    lane = lax.broadcasted_iota(jnp.int32, c.shape, 1)
    half, quarter = LANE // 2, LANE // 4

    def retention(x, first):
        return jnp.where(lane < half, first * x, pltpu.roll(x, half, 1))

    def window(x):
        y = pltpu.roll(x, half, 1)
        y = jnp.where(lane < quarter, y, pltpu.roll(y, quarter, 1))
        return jnp.where(lane < half, y, pltpu.roll(y, half, 1))

    cr_ref[...] = retention(c, 1.0)
    sr_ref[...] = retention(s, -1.0)
    cs_ref[...] = window(c)
    ss_ref[...] = window(s)


def _rope_tables(positions):
    t = positions.shape[1]
    tt = min(t, 1024)
    ret_f = ROPE_THETA ** (-jnp.linspace(0.0, 1.0, HEAD_DIM // 2, dtype=F32))
    swa_f = ROPE_THETA ** (-jnp.arange(0, SWA_DIM, 2, dtype=F32) / SWA_DIM)
    freqs = jnp.concatenate([ret_f, swa_f, jnp.zeros((LANE - HEAD_DIM // 2 - SWA_DIM // 2,), F32)])[None, :]
    pos = positions.reshape(t, 1)
    tab = jax.ShapeDtypeStruct((t, LANE), F32)
    row = pl.BlockSpec((tt, LANE), lambda i: (i, 0))
    return pl.pallas_call(
        _rope_kernel,
        grid=(t // tt,),
        in_specs=[pl.BlockSpec((tt, 1), lambda i: (i, 0)), pl.BlockSpec((1, LANE), lambda i: (0, 0))],
        out_specs=[row, row, row, row],
        out_shape=[tab, tab, tab, tab],
        compiler_params=_cparams(("parallel",)),
        name="rope_tables",
    )(pos, freqs)


PROJ_TN = 1024


def _proj_kernel(x_ref, g_ref, w_ref, o_ref, f_ref, h_scr):
    @pl.when(pl.program_id(1) == 0)
    def _():
        h_scr[...] = (_rms(x_ref[...]) * g_ref[...]).astype(BF16)

    y = _dot(h_scr[...], w_ref[...])
    o_ref[...] = y.astype(BF16)

    @pl.when(pl.program_id(1) == CB_HF * LANE // PROJ_TN)
    def _():
        lo = CB_HF * LANE % PROJ_TN
        f_ref[...] = y[:, lo:lo + N_HEADS * HEAD_DIM]


def _proj(x, g, w):
    t = x.shape[0]
    n = w.shape[1]
    tm = min(t, 2048)
    tn = PROJ_TN
    wf = N_HEADS * HEAD_DIM
    return pl.pallas_call(
        _proj_kernel,
        grid=(t // tm, n // tn),
        in_specs=[pl.BlockSpec((tm, D_MODEL), lambda i, j: (i, 0)),
                  pl.BlockSpec((1, D_MODEL), lambda i, j: (0, 0)),
                  pl.BlockSpec((D_MODEL, tn), lambda i, j: (0, j))],
        out_specs=[pl.BlockSpec((tm, tn), lambda i, j: (i, j)),
                   pl.BlockSpec((tm, wf), lambda i, j: (i, 0))],
        out_shape=[jax.ShapeDtypeStruct((t, n), BF16), jax.ShapeDtypeStruct((t, wf), F32)],
        scratch_shapes=[pltpu.VMEM((tm, D_MODEL), BF16)],
        compiler_params=_cparams(("parallel", "arbitrary")),
        name="in_proj",
    )(x, g, w)


def _relay_plan():
    names = (["hq"] * 4 + ["hf"] * 4 + ["hi"] * 4 + ["hg"] * 4 + ["rq"] * 4 + ["rk"] * 4 + ["rv"] * 4
             + ["rg"] * 4 + ["sq"] * 4 + ["sk", "sv"] + ["gate"] * 24)
    first = {}
    for i, n in enumerate(names):
        first.setdefault(n, i)
    plan = [(first["gate"] + i, 0, 0) for i in range(24)]
    for n in ("hq", "hf", "hi", "hg", "rq", "rk", "rv", "rg", "sq"):
        plan += [(first[n] + i, 0, 0) for i in range(4)]
    plan += [(first["sq"] + i, 1, 0) for i in range(4)]
    plan += [(first["sk"], 0, 1), (first["sk"], 0, 2)]
    plan += [(first["sk"], 1, 1), (first["sk"], 1, 2)]
    plan += [(first["sv"], 0, 3), (first["sv"], 0, 4)]
    plan += [(first["sv"], 0, 5), (first["sv"], 0, 6)]
    assert len(plan) == N_CB
    return plan


def _relay_kernel(src_ref, rot_ref, put_ref, w_ref, o_ref):
    j = pl.program_id(0)
    x = w_ref[...]
    lane = lax.broadcasted_iota(jnp.int32, x.shape, 1)
    swapped = pltpu.roll(x, SWA_DIM, 1)
    first_half = lane % SWA_DIM < SWA_DIM // 2
    rotated = jnp.where(first_half, -pltpu.roll(x, LANE - SWA_DIM // 2, 1), pltpu.roll(x, SWA_DIM // 2, 1))
    rotated_swapped = pltpu.roll(rotated, SWA_DIM, 1)
    rot = rot_ref[j] == 1
    y = jnp.where(rot, rotated, x)
    ys = jnp.where(rot, rotated_swapped, swapped)
    low = lane < SWA_DIM
    put = put_ref[j]
    out = y
    out = jnp.where(put == 1, jnp.where(low, y, ys), out)
    out = jnp.where(put == 2, jnp.where(low, ys, y), out)
    out = jnp.where(put == 3, jnp.where(low, y, 0.0), out)
    out = jnp.where(put == 4, jnp.where(low, ys, 0.0), out)
    out = jnp.where(put == 5, jnp.where(low, 0.0, ys), out)
    out = jnp.where(put == 6, jnp.where(low, 0.0, y), out)
    o_ref[...] = out.astype(BF16)


def _relay_w_in(w_in, layer):
    d = w_in.shape[1]
    plan = np.asarray(_relay_plan(), np.int32)
    grid_spec = pltpu.PrefetchScalarGridSpec(
        num_scalar_prefetch=3,
        grid=(N_CB,),
        in_specs=[pl.BlockSpec((None, d, LANE), lambda j, src, rot, put: (layer, 0, src[j]))],
        out_specs=pl.BlockSpec((d, LANE), lambda j, src, rot, put: (0, j)),
    )
    return pl.pallas_call(
        _relay_kernel,
        grid_spec=grid_spec,
        out_shape=jax.ShapeDtypeStruct((d, PROJ_COLS), BF16),
        compiler_params=_cparams(("arbitrary",)),
        name="relay_w_in",
    )(jnp.asarray(plan[:, 0]), jnp.asarray(plan[:, 1]), jnp.asarray(plan[:, 2]), w_in)


def _hgrn_kernel(hq_ref, hf_ref, hi_ref, lb_ref, o_ref, st_ref, *, n_chunks):
    @pl.when(pl.program_id(0) == 0)
    def _():
        st_ref[...] = jnp.zeros_like(st_ref)

    c = HG_CHUNK
    row = lax.broadcasted_iota(jnp.int32, (c, c), 0)
    col = lax.broadcasted_iota(jnp.int32, (c, c), 1)
    diag = row == col
    halves = [c >> (i + 1) for i in range(c.bit_length() - 1)]
    pair_mask = {b: (row // (2 * b) == col // (2 * b)) & (row % (2 * b) >= b) & (col % (2 * b) < b)
                 for b in halves}
    rw = lax.broadcasted_iota(jnp.int32, (c, HEAD_DIM), 0)
    r3 = lax.broadcasted_iota(jnp.int32, (c // 8, 8, HEAD_DIM), 1)
    scale = HEAD_DIM ** -0.5
    log2e = math.log2(math.e)

    def boundary(cum2, b):
        if b >= 8:
            ends = [cum2[s + b - 1:s + b, :] for s in range(0, c, 2 * b)]
            out = ends[-1]
            for i in range(len(ends) - 2, -1, -1):
                out = jnp.where(rw < (i + 1) * 2 * b, ends[i], out)
            return out
        c3 = cum2.reshape(c // 8, 8, HEAD_DIM)
        if b == 4:
            return jnp.broadcast_to(c3[:, 3:4, :], c3.shape).reshape(c, HEAD_DIM)
        if b == 2:
            lo = jnp.broadcast_to(c3[:, 1:2, :], c3.shape)
            hi = jnp.broadcast_to(c3[:, 5:6, :], c3.shape)
            return jnp.where(r3 < 4, lo, hi).reshape(c, HEAD_DIM)
        prev = pltpu.roll(cum2, 1, 0)
        return jnp.where(rw % 2 == 1, prev, cum2)

    def chunk_body(ci, carry):
        r0 = pl.multiple_of(ci * c, c)
        for h in range(N_HEADS):
            lanes = slice(h * HEAD_DIM, (h + 1) * HEAD_DIM)
            xq = hq_ref[pl.ds(r0, c), lanes].astype(F32)
            z = hf_ref[pl.ds(r0, c), lanes]
            v = hi_ref[pl.ds(r0, c), lanes]
            lb = lb_ref[:, lanes]
            omlb = 1.0 - lb
            q = xq * _sigmoid(xq) * scale
            f = lb + omlb * _sigmoid(z)
            logf = jnp.log(jnp.maximum(f, TINY))
            kk = omlb * _sigmoid(-z)
            cum2 = logf * log2e
            step = 1
            while step < c:
                cum2 = cum2 + jnp.where(rw >= step, pltpu.roll(cum2, step, 0), 0.0)
                step *= 2
            scores = jnp.where(diag, jnp.sum(q * kk, axis=-1, keepdims=True), 0.0)
            for b in halves:
                ref = boundary(cum2, b)
                qs = (q * jnp.exp2(jnp.minimum(cum2 - ref, 0.0))).astype(BF16)
                ks = (kk * jnp.exp2(jnp.minimum(ref - cum2, 0.0))).astype(BF16)
                scores = jnp.where(pair_mask[b], _dot_nt(qs, ks), scores)
            st = st_ref[h]
            qe = (q * jnp.exp2(cum2)).astype(BF16)
            o = _dot_nt(qe, st.astype(BF16)) + _dot(scores.astype(BF16), v.astype(BF16))
            o_ref[pl.ds(r0, c), lanes] = o.astype(BF16)
            last = cum2[c - 1:c, :]
            kl = (kk * jnp.exp2(last - cum2)).astype(BF16)
            st_ref[h] = jnp.exp2(last) * st + _dot_tn(v.astype(BF16), kl)
        return carry

    lax.fori_loop(0, n_chunks, chunk_body, 0, unroll=4)


def _hgrn(proj, hf, lower):
    t = proj.shape[0]
    tb = min(t, 512)
    w = N_HEADS * HEAD_DIM

    def cols(cb):
        return pl.BlockSpec((tb, w), lambda i: (i, cb // N_HEADS))

    return pl.pallas_call(
        functools.partial(_hgrn_kernel, n_chunks=tb // HG_CHUNK),
        grid=(t // tb,),
        in_specs=[cols(CB_HQ), pl.BlockSpec((tb, w), lambda i: (i, 0)), cols(CB_HI),
                  pl.BlockSpec((1, w), lambda i: (0, 0))],
        out_specs=pl.BlockSpec((tb, w), lambda i: (i, 0)),
        out_shape=jax.ShapeDtypeStruct((t, w), BF16),
        scratch_shapes=[pltpu.VMEM((N_HEADS, HEAD_DIM, HEAD_DIM), F32)],
        compiler_params=_cparams(("arbitrary",)),
        name="hgrn2",
    )(proj, hf, proj, lower)


def _ret_kernel(q_ref, k_ref, v_ref, cos_ref, sin_ref,
                dec_ref, qd_ref, kd_ref, cd_ref, o_ref, s_ref):
    @pl.when(pl.program_id(0) == 0)
    def _():
        s_ref[...] = jnp.zeros_like(s_ref)

    c = dec_ref.shape[1]
    scale = HEAD_DIM ** -0.5
    for ci in range(q_ref.shape[0] // c):
        rows = slice(ci * c, (ci + 1) * c)
        cos = cos_ref[rows, :]
        sin = sin_ref[rows, :]
        for h in range(N_HEADS):
            lanes = slice(h * HEAD_DIM, (h + 1) * HEAD_DIM)
            q = q_ref[rows, lanes].astype(F32)
            k = k_ref[rows, lanes].astype(F32)
            q = q * cos + pltpu.roll(q, HEAD_DIM // 2, 1) * sin
            k = (k * cos + pltpu.roll(k, HEAD_DIM // 2, 1) * sin) * scale
            v = v_ref[rows, lanes]
            qb = q.astype(BF16)
            s = s_ref[h]
            scores = _dot_nt(qb, k.astype(BF16)) * dec_ref[h]
            o = _dot(qb, s.astype(BF16)) * qd_ref[h] + _dot(scores.astype(BF16), v)
            o_ref[rows, lanes] = o.astype(BF16)
            s_ref[h] = cd_ref[h] * s + _dot_tn((k * kd_ref[h]).astype(BF16), v)


def _ret(proj, cos_r, sin_r):
    t = proj.shape[0]
    c = min(t, RET_CHUNK)
    tb = 4 * c if t % (4 * c) == 0 else c
    w = N_HEADS * HEAD_DIM
    log_gamma = jnp.log(1.0 - 2.0 ** (-5.0 - jnp.arange(N_HEADS, dtype=F32)))
    idx = jnp.arange(c, dtype=F32)
    rel = idx[:, None] - idx[None, :]
    decay = jnp.exp(jnp.where(rel >= 0, log_gamma[:, None, None] * rel, NEG_BIG))
    ones = jnp.ones((1, 1, HEAD_DIM), F32)
    q_decay = jnp.exp(log_gamma[:, None] * (idx + 1.0))[:, :, None] * ones
    k_decay = jnp.exp(log_gamma[:, None] * (c - 1.0 - idx))[:, :, None] * ones
    c_decay = jnp.exp(log_gamma * c)[:, None, None] * ones

    def cols(cb):
        return pl.BlockSpec((tb, w), lambda i: (i, cb // N_HEADS))

    def const(shape):
        return pl.BlockSpec(shape, lambda i: (0, 0, 0))

    tab = pl.BlockSpec((tb, LANE), lambda i: (i, 0))
    return pl.pallas_call(
        _ret_kernel,
        grid=(t // tb,),
        in_specs=[cols(CB_RQ), cols(CB_RK), cols(CB_RV), tab, tab,
                  const((N_HEADS, c, c)), const((N_HEADS, c, HEAD_DIM)),
                  const((N_HEADS, c, HEAD_DIM)), const((N_HEADS, 1, HEAD_DIM))],
        out_specs=pl.BlockSpec((tb, w), lambda i: (i, 0)),
        out_shape=jax.ShapeDtypeStruct((t, w), BF16),
        scratch_shapes=[pltpu.VMEM((N_HEADS, HEAD_DIM, HEAD_DIM), F32)],
        compiler_params=_cparams(("arbitrary",)),
        name="retention",
    )(proj, proj, proj, cos_r, sin_r, decay, q_decay, k_decay, c_decay)


def _swa_kernel(sink_ref, q_ref, qr_ref, kc_ref, krc_ref, vac_ref, vbc_ref,
                kp_ref, krp_ref, vap_ref, vbp_ref, cc_ref, sc_ref, cp_ref, sp_ref, o_ref):
    b = SWA_BLOCK
    nb = q_ref.shape[0] // b
    first = pl.program_id(0) * nb
    cos_c, sin_c = cc_ref[...], sc_ref[...]
    cos_p, sin_p = cp_ref[...], sp_ref[...]
    qi = lax.broadcasted_iota(jnp.int32, (b, 2 * b), 0) + b
    ki = lax.broadcasted_iota(jnp.int32, (b, 2 * b), 1)
    rel = qi - ki
    in_window = (rel >= 0) & (rel < b)
    low = lax.broadcasted_iota(jnp.int32, (b, LANE), 1) < SWA_DIM
    pairs_per_group = SWA_Q_HEADS // SWA_KV_HEADS // 2
    for g in range(SWA_KV_HEADS):
        kl = slice(g * LANE, (g + 1) * LANE)
        k_cur = kc_ref[:, kl].astype(F32) * cos_c + krc_ref[:, kl].astype(F32) * sin_c
        k_prev = kp_ref[:, kl].astype(F32) * cos_p + krp_ref[:, kl].astype(F32) * sin_p
        k_all = jnp.concatenate([k_prev, k_cur], axis=0).astype(BF16)
        v_lo = jnp.concatenate([vap_ref[:, kl], vac_ref[:, kl]], axis=0)
        v_hi = jnp.concatenate([vbp_ref[:, kl], vbc_ref[:, kl]], axis=0)
        for sb in range(nb):
            win = slice(sb * b, (sb + 2) * b)
            rows = slice(sb * b, (sb + 1) * b)
            keep = in_window & ((first + sb > 0) | (ki >= b))
            kw = k_all[win]
            for j in range(pairs_per_group):
                pair = g * pairs_per_group + j
                ql = slice(pair * LANE, (pair + 1) * LANE)
                q = (q_ref[rows, ql].astype(F32) * cos_c[rows] + qr_ref[rows, ql].astype(F32) * sin_c[rows])
                out = None
                for half, vw in ((0, v_lo[win]), (1, v_hi[win])):
                    qh = jnp.where(low if half == 0 else ~low, q, 0.0).astype(BF16)
                    s = _dot_nt(qh, kw) * (SWA_DIM ** -0.5)
                    s = jnp.where(keep, s, NEG_BIG)
                    sink = sink_ref[2 * pair + half]
                    m = jnp.maximum(jnp.max(s, axis=-1, keepdims=True), sink)
                    p = jnp.exp(s - m)
                    denom = jnp.sum(p, axis=-1, keepdims=True) + jnp.exp(sink - m)
                    o = _dot(p.astype(BF16), vw) / denom
                    out = o if out is None else out + o
                o_ref[rows, ql] = out.astype(BF16)


def _swa(proj, cos_s, sin_s, sinks):
    t = proj.shape[0]
    b = SWA_BLOCK
    nb = 4 if t % (4 * b) == 0 else 1
    qw = SWA_Q_HEADS * SWA_DIM
    kw = SWA_KV_HEADS * LANE

    def cur(cb, width):
        return pl.BlockSpec((nb * b, width), lambda i: (i, cb * LANE // width))

    def prev(cb, width):
        return pl.BlockSpec((b, width), lambda i: (jnp.maximum(nb * i - 1, 0), cb * LANE // width))

    tab_c = pl.BlockSpec((nb * b, LANE), lambda i: (i, 0))
    tab_p = pl.BlockSpec((b, LANE), lambda i: (jnp.maximum(nb * i - 1, 0), 0))
    return pl.pallas_call(
        _swa_kernel,
        grid=(t // (nb * b),),
        in_specs=[pl.BlockSpec(memory_space=pltpu.SMEM),
                  cur(CB_SQ, qw), cur(CB_SQR, qw),
                  cur(CB_SK, kw), cur(CB_SKR, kw), cur(CB_SVA, kw), cur(CB_SVB, kw),
                  prev(CB_SK, kw), prev(CB_SKR, kw), prev(CB_SVA, kw), prev(CB_SVB, kw),
                  tab_c, tab_c, tab_p, tab_p],
        out_specs=pl.BlockSpec((nb * b, qw), lambda i: (i, 0)),
        out_shape=jax.ShapeDtypeStruct((t, qw), BF16),
        compiler_params=_cparams(("parallel",)),
        name="swa",
    )(sinks, proj, proj, proj, proj, proj, proj, proj, proj, proj, proj, cos_s, sin_s, cos_s, sin_s)


def _merge_kernel(x_ref, oa_ref, hg_ref, ob_ref, rg_ref, oc_ref, ga_ref, gb_ref, gc_ref,
                  hn_ref, wa_ref, wb_ref, wc_ref, wo_ref, o_ref):
    hg = hg_ref[...].astype(F32)
    a = _rms(oa_ref[...].astype(F32)) * hn_ref[...] * (hg * _sigmoid(hg))
    ya = _dot(a.astype(BF16), wa_ref[...])
    rg = rg_ref[...].astype(F32)
    ob = ob_ref[...].astype(F32)
    parts = []
    for h in range(N_HEADS):
        lanes = slice(h * HEAD_DIM, (h + 1) * HEAD_DIM)
        parts.append(_rms(ob[:, lanes]))
    bn = jnp.concatenate(parts, axis=-1) * (rg * _sigmoid(rg))
    yb = _dot(bn.astype(BF16), wb_ref[...])
    yc = _dot(oc_ref[...], wc_ref[...])
    mixed = (_sigmoid(ga_ref[...].astype(F32)) * ya + _sigmoid(gb_ref[...].astype(F32)) * yb
             + _sigmoid(gc_ref[...].astype(F32)) * yc)
    o_ref[...] = x_ref[...] + _dot(mixed.astype(BF16), wo_ref[...])


def _merge(x, proj, oa, ob, oc, hn, wa, wb, wc, wo):
    t = x.shape[0]
    tt = min(t, 1024)
    w = N_HEADS * HEAD_DIM
    qw = SWA_Q_HEADS * SWA_DIM

    def rows(width, cb=0):
        return pl.BlockSpec((tt, width), lambda i: (i, cb * LANE // width))

    def full(shape):
        return pl.BlockSpec(shape, lambda i: (0, 0))

    return pl.pallas_call(
        _merge_kernel,
        grid=(t // tt,),
        in_specs=[rows(D_MODEL), rows(w), rows(w, CB_HG), rows(w), rows(w, CB_RG), rows(qw),
                  rows(D_MODEL, CB_GATE), rows(D_MODEL, CB_GATE + 8), rows(D_MODEL, CB_GATE + 16),
                  full((1, w)), full((w, D_MODEL)), full((w, D_MODEL)), full((qw, D_MODEL)),
                  full((D_MODEL, D_MODEL))],
        out_specs=rows(D_MODEL),
        out_shape=jax.ShapeDtypeStruct((t, D_MODEL), F32),
        compiler_params=_cparams(("parallel",)),
        name="merge",
    )(x, oa, proj, ob, proj, oc, proj, proj, proj, hn, wa, wb, wc, wo)


def _memkv_kernel(m_ref, g_ref, wk_ref, wv_ref, k_ref, v_ref):
    h = (_rms(m_ref[...]) * g_ref[...]).astype(BF16)
    k_ref[...] = _dot(h, wk_ref[...]).astype(BF16)
    v_ref[...] = _dot(h, wv_ref[...]).astype(BF16)


def _memkv(mem, g, wk, wv):
    nm = mem.shape[0]
    out = jax.ShapeDtypeStruct((nm, D_MODEL), BF16)
    return pl.pallas_call(
        _memkv_kernel,
        out_shape=[out, out],
        compiler_params=pltpu.CompilerParams(vmem_limit_bytes=VMEM_LIMIT),
        name="mem_kv",
    )(mem, g, wk, wv)


def _merge_xattn_kernel(x_ref, oa_ref, hg_ref, ob_ref, rg_ref, oc_ref, ga_ref, gb_ref, gc_ref,
                        hn_ref, wa_ref, wb_ref, wc_ref, wo_ref,
                        g_ref, wq_ref, k_ref, v_ref, wxo_ref, o_ref, mid_ref):
    _merge_kernel(x_ref, oa_ref, hg_ref, ob_ref, rg_ref, oc_ref, ga_ref, gb_ref, gc_ref,
                  hn_ref, wa_ref, wb_ref, wc_ref, wo_ref, mid_ref)
    _xattn_kernel(mid_ref, g_ref, wq_ref, k_ref, v_ref, wxo_ref, o_ref)


def _merge_xattn(x, proj, oa, ob, oc, hn, wa, wb, wc, wo, g, wq, k, v, wxo):
    t = x.shape[0]
    tt = min(t, 512)
    w = N_HEADS * HEAD_DIM
    qw = SWA_Q_HEADS * SWA_DIM
    nm = k.shape[0]

    def rows(width, cb=0):
        return pl.BlockSpec((tt, width), lambda i: (i, cb * LANE // width))

    def full(shape):
        return pl.BlockSpec(shape, lambda i: (0, 0))

    return pl.pallas_call(
        _merge_xattn_kernel,
        grid=(t // tt,),
        in_specs=[rows(D_MODEL), rows(w), rows(w, CB_HG), rows(w), rows(w, CB_RG), rows(qw),
                  rows(D_MODEL, CB_GATE), rows(D_MODEL, CB_GATE + 8), rows(D_MODEL, CB_GATE + 16),
                  full((1, w)), full((w, D_MODEL)), full((w, D_MODEL)), full((qw, D_MODEL)),
                  full((D_MODEL, D_MODEL)),
                  full((1, D_MODEL)), full((D_MODEL, D_MODEL)), full((nm, D_MODEL)), full((nm, D_MODEL)),
                  full((D_MODEL, D_MODEL))],
        out_specs=rows(D_MODEL),
        out_shape=jax.ShapeDtypeStruct((t, D_MODEL), F32),
        scratch_shapes=[pltpu.VMEM((tt, D_MODEL), F32)],
        compiler_params=_cparams(("parallel",)),
        name="merge_xattn",
    )(x, oa, proj, ob, proj, oc, proj, proj, proj, hn, wa, wb, wc, wo, g, wq, k, v, wxo)


def _xattn_kernel(x_ref, g_ref, wq_ref, k_ref, v_ref, wo_ref, o_ref):
    x = x_ref[...]
    h = (_rms(x) * g_ref[...]).astype(BF16)
    q = _dot(h, wq_ref[...])
    outs = []
    for hh in range(X_HEADS):
        lanes = slice(hh * X_DIM, (hh + 1) * X_DIM)
        s = _dot_nt(q[:, lanes].astype(BF16), k_ref[:, lanes]) * (X_DIM ** -0.5)
        m = jnp.max(s, axis=-1, keepdims=True)
        p = jnp.exp(s - m)
        p = p / jnp.sum(p, axis=-1, keepdims=True)
        outs.append(_dot(p.astype(BF16), v_ref[:, lanes]))
    o = jnp.concatenate(outs, axis=-1)
    o_ref[...] = x + _dot(o.astype(BF16), wo_ref[...])


def _xattn(x, g, wq, k, v, wo):
    t = x.shape[0]
    tt = min(t, 1024)
    nm = k.shape[0]

    def full(shape):
        return pl.BlockSpec(shape, lambda i: (0, 0))

    rows = pl.BlockSpec((tt, D_MODEL), lambda i: (i, 0))
    return pl.pallas_call(
        _xattn_kernel,
        grid=(t // tt,),
        in_specs=[rows, full((1, D_MODEL)), full((D_MODEL, D_MODEL)),
                  full((nm, D_MODEL)), full((nm, D_MODEL)), full((D_MODEL, D_MODEL))],
        out_specs=rows,
        out_shape=jax.ShapeDtypeStruct((t, D_MODEL), F32),
        compiler_params=_cparams(("parallel",)),
        name="xattn",
    )(x, g, wq, k, v, wo)


_CAND_ROWS = [PEER_TOPK // (k1 + 1) for k1 in range(8)]
_RANK_BASE = 3.0e38
_RANK_STEP = 3.0e33


def _top_values(s, n, with_rank=False):
    vals = []
    for i in range(n):
        m = jnp.max(s, axis=0, keepdims=True)
        vals.append(m)
        s = jnp.where(s >= m, -(_RANK_BASE + i * _RANK_STEP), s)
    if not with_rank:
        return vals
    rank = jnp.where(s <= -_RANK_BASE, jnp.floor((-s - _RANK_BASE) * (1.0 / _RANK_STEP) + 0.5), float(n))
    return vals, rank


def _bf16_pair_bits(x):
    u = lax.bitcast_convert_type(x.astype(BF16).astype(F32), jnp.uint32)
    return u | (u >> 16)


def _stack_rows(rows, n):
    tt = rows[0].shape[1]
    r = lax.broadcasted_iota(jnp.int32, (n, tt), 0)
    out = jnp.zeros((n, tt), F32)
    for k in range(n):
        out = jnp.where(r == k, rows[k], out)
    return out


def _router_kernel(x_ref, g_ref, wq_ref, kh_ref, kl_ref,
                   hb_ref, a_ref, b_ref, r_ref, l_ref, qt_scr):
    h = _rms(x_ref[...]) * g_ref[...]
    hb_ref[...] = h.T.astype(BF16)
    qt_scr[...] = _dot_nt(wq_ref[...], h.astype(BF16))
    tt = x_ref.shape[0]
    k = PEER_TOPK
    r8 = lax.broadcasted_iota(jnp.int32, (8, tt), 0)

    def head_body(hh, carry):
        scores = []
        for p in range(2):
            r0 = pl.multiple_of(hh * (2 * PEER_HALF) + p * PEER_HALF, PEER_HALF)
            q_hi, q_lo = _split2(qt_scr[pl.ds(r0, PEER_HALF), :])
            kh = kh_ref[p * PEER_HEADS + hh]
            scores.append(_dot(kh, q_hi) + _dot(kl_ref[p * PEER_HEADS + hh], q_hi) + _dot(kh, q_lo))
        s1, s2 = scores
        v1, rank1 = _top_values(s1, k, with_rank=True)
        v2, rank2 = _top_values(s2, k, with_rank=True)
        sv1 = _stack_rows(v1, k)
        sv2 = _stack_rows(v2, k)
        groups = [v1[0] + sv2]
        for k1 in range(1, 8):
            groups.append(jnp.where(r8 < _CAND_ROWS[k1], v1[k1] + sv2[0:8], -jnp.inf))
        groups.append(sv1[8:16] + v2[0])
        cand = jnp.concatenate(groups, axis=0)
        tau = _top_values(cand, k)[k - 1]
        m0 = v1[0] + v2[0]
        z = jnp.sum(jnp.where(cand >= tau, jnp.exp(cand - m0), 0.0), axis=0, keepdims=True)
        a_ref[hh] = _bf16_pair_bits(jnp.exp(s1 - v1[0]))
        b_ref[hh] = (jnp.exp(s2 - v2[0]) / z).astype(BF16)
        count1 = jnp.zeros_like(s1)
        for k1 in range(k):
            n_k1 = jnp.sum((v1[k1] + sv2 >= tau).astype(F32), axis=0, keepdims=True)
            count1 = jnp.where(rank1 == float(k1), n_k1, count1)
        r_ref[hh] = rank2.astype(BF16)
        l_ref[hh] = _bf16_pair_bits(count1)
        return carry

    lax.fori_loop(0, PEER_HEADS, head_body, 0, unroll=2)


def _router(x, g, wq_t, k_hi, k_lo):
    t = x.shape[0]
    tt = min(t, 512)
    nq = wq_t.shape[0]
    sel = jax.ShapeDtypeStruct((PEER_HEADS, PEER_KEYS, t), jnp.uint32)
    sel_b = jax.ShapeDtypeStruct((PEER_HEADS, PEER_KEYS, t), BF16)
    sel_spec = pl.BlockSpec((PEER_HEADS, PEER_KEYS, tt), lambda i: (0, 0, i))
    return pl.pallas_call(
        _router_kernel,
        grid=(t // tt,),
        in_specs=[pl.BlockSpec((tt, D_MODEL), lambda i: (i, 0)),
                  pl.BlockSpec((1, D_MODEL), lambda i: (0, 0)),
                  pl.BlockSpec((nq, D_MODEL), lambda i: (0, 0)),
                  pl.BlockSpec((2 * PEER_HEADS, PEER_KEYS, PEER_HALF), lambda i: (0, 0, 0)),
                  pl.BlockSpec((2 * PEER_HEADS, PEER_KEYS, PEER_HALF), lambda i: (0, 0, 0))],
        out_specs=[pl.BlockSpec((D_MODEL, tt), lambda i: (0, i)), sel_spec, sel_spec, sel_spec, sel_spec],
        out_shape=[jax.ShapeDtypeStruct((D_MODEL, t), BF16), sel, sel_b, sel_b, sel],
        scratch_shapes=[pltpu.VMEM((nq, tt), F32)],
        compiler_params=_cparams(("parallel",)),
        name="peer_router",
    )(x, g, wq_t, k_hi, k_lo)


def _peer_kernel(x_ref, hb_ref, a_ref, l_ref, b_ref, r_ref, u_ref, vt_ref, gf_ref, o_ref,
                 acc_ref, ga_ref, *, final_norm):
    e = pl.program_id(1)

    @pl.when(e == 0)
    def _():
        acc_ref[...] = jnp.zeros_like(acc_ref)

    u_blk = pltpu.bitcast(u_ref[...], BF16)
    vt_blk = pltpu.bitcast(vt_ref[...], BF16)
    n_i1 = u_blk.shape[0] // PEER_KEYS
    sub = 16
    tt = hb_ref.shape[1]
    shape3 = (PEER_KEYS // sub, sub, tt)
    zero3 = jnp.zeros(shape3, BF16)
    piece = PEER_PIECE_KEYS * PEER_KEYS
    for p in range(n_i1 // PEER_PIECE_KEYS):
        for j in range(p * PEER_PIECE_KEYS, (p + 1) * PEER_PIECE_KEYS):
            gate = zero3
            for hh in range(PEER_HEADS):
                arow = pltpu.bitcast(jnp.broadcast_to(a_ref[hh, j:j + 1, :], (8, tt)), BF16)
                lrow = pltpu.bitcast(jnp.broadcast_to(l_ref[hh, j:j + 1, :], (8, tt)), BF16)
                b3 = b_ref[hh].reshape(shape3)
                r3 = r_ref[hh].reshape(shape3)
                gate = gate + arow[None] * jnp.where(r3 < lrow[None], b3, zero3)
            ga_ref[j * PEER_KEYS:(j + 1) * PEER_KEYS, :] = gate.reshape(PEER_KEYS, tt)
        rows = slice(p * piece, (p + 1) * piece)
        pre = _dot(u_blk[rows, :], hb_ref[...])
        act = pre + pre * lax.erf(pre)
        ga_ref[rows, :] = ga_ref[rows, :] * act.astype(BF16)
    acc_ref[...] += _dot(vt_blk, ga_ref[...])

    @pl.when(e == pl.num_programs(1) - 1)
    def _():
        y = x_ref[...] + acc_ref[...].T
        if final_norm:
            y = _rms(y) * gf_ref[...]
        o_ref[...] = y


def _pack_kernel(w_ref, o_ref, *, scale, transpose):
    w = w_ref[...]
    if transpose:
        w = w.T
    o_ref[...] = pltpu.bitcast((w * scale).astype(BF16), jnp.uint32)


def _pack_table(w_all, layer, scale, transpose):
    _, rows, cols = w_all.shape
    if transpose:
        blk = 512
        in_spec = pl.BlockSpec((None, blk, cols), lambda i: (layer, i, 0))
        out_spec = pl.BlockSpec((cols // 2, blk), lambda i: (0, i))
        out_shape = jax.ShapeDtypeStruct((cols // 2, rows), jnp.uint32)
    else:
        blk = 1024
        in_spec = pl.BlockSpec((None, blk, cols), lambda i: (layer, i, 0))
        out_spec = pl.BlockSpec((blk // 2, cols), lambda i: (i, 0))
        out_shape = jax.ShapeDtypeStruct((rows // 2, cols), jnp.uint32)
    return pl.pallas_call(
        functools.partial(_pack_kernel, scale=scale, transpose=transpose),
        grid=(rows // blk,),
        in_specs=[in_spec],
        out_specs=out_spec,
        out_shape=out_shape,
        compiler_params=_cparams(("parallel",)),
        name="pack_table_t" if transpose else "pack_table",
    )(w_all)


def _peer(x, hb, a, l, b, r, u, vt, gf, final_norm):
    t = x.shape[0]
    tt = min(t, 512)
    eb = 16 * PEER_KEYS
    rows = pl.BlockSpec((tt, D_MODEL), lambda i, e: (i, 0))
    i1_rows = pl.BlockSpec((PEER_HEADS, eb // PEER_KEYS, tt), lambda i, e: (0, e, i))
    all_rows = pl.BlockSpec((PEER_HEADS, PEER_KEYS, tt), lambda i, e: (0, 0, i))
    return pl.pallas_call(
        functools.partial(_peer_kernel, final_norm=final_norm),
        grid=(t // tt, N_EXPERTS // eb),
        in_specs=[rows, pl.BlockSpec((D_MODEL, tt), lambda i, e: (0, i)),
                  i1_rows, i1_rows, all_rows, all_rows,
                  pl.BlockSpec((eb // 2, D_MODEL), lambda i, e: (e, 0)),
                  pl.BlockSpec((D_MODEL // 2, eb), lambda i, e: (0, e)),
                  pl.BlockSpec((1, D_MODEL), lambda i, e: (0, 0))],
        out_specs=rows,
        out_shape=jax.ShapeDtypeStruct((t, D_MODEL), F32),
        scratch_shapes=[pltpu.VMEM((D_MODEL, tt), F32), pltpu.VMEM((eb, tt), BF16)],
        compiler_params=_cparams(("parallel", "arbitrary")),
        name="peer_dense",
    )(x, hb, a, l, b, r, u, vt, gf)


def kernel(x, mem, positions, ln_mix, w_in, lb_param, hgrn_norm, swa_sinks, w_br_hgrn, w_br_ret,
           w_br_swa, w_out, ln_xq, ln_xkv, w_xq, w_xk, w_xv, w_xo, ln_ffn, peer_wq, peer_keys,
           peer_u, peer_v, ln_final):
    b, t, d = x.shape
    assert b == 1 and d == D_MODEL
    depth = w_in.shape[0]
    xs = x.reshape(t, d)
    ms = mem.reshape(mem.shape[1], d)
    cos_r, sin_r, cos_s, sin_s = _rope_tables(positions)
    lb_sm = jax.nn.softmax(lb_param.astype(F32), axis=0)
    lower = jnp.cumsum(lb_sm, axis=0) - lb_sm[0]

    for l in range(depth):
        proj, hf = _proj(xs, ln_mix[l][None, :], _relay_w_in(w_in, l))
        oa = _hgrn(proj, hf, lower[l][None, :])
        ob = _ret(proj, cos_r, sin_r)
        oc = _swa(proj, cos_s, sin_s, swa_sinks[l])
        km, vm = _memkv(ms, ln_xkv[l][None, :], w_xk[l].astype(BF16), w_xv[l].astype(BF16))
        xs = _merge_xattn(xs, proj, oa, ob, oc, hgrn_norm[l][None, :], w_br_hgrn[l].astype(BF16),
                          w_br_ret[l].astype(BF16), w_br_swa[l].astype(BF16), w_out[l].astype(BF16),
                          ln_xq[l][None, :], w_xq[l].astype(BF16), km, vm, w_xo[l].astype(BF16))

        k_hi, k_lo = _split2(peer_keys[l].reshape(2 * PEER_HEADS, PEER_KEYS, PEER_HALF))
        hb, a, bz, r2, cnt = _router(xs, ln_ffn[l][None, :], peer_wq[l].T.astype(BF16), k_hi, k_lo)
        c = 2.0 ** -0.5
        xs = _peer(xs, hb, a, cnt, bz, r2, _pack_table(peer_u, l, c, False), _pack_table(peer_v, l, c, True),
                   ln_final[None, :], final_norm=(l == depth - 1))

    return xs.reshape(b, t, d)
```
